```python
import jax
import jax.numpy as jnp
from jax import lax
import numpy as np

D_MODEL = 1024
BATCH = 8
SEQ = 2048
DEPTH = 2
DEC_BATCH = 128
DEC_SEQ = 4
PAST_LEN = 16384
PAGE_SIZE = 128

D_MIX = 2 * D_MODEL
D_CONV = D_MIX // 2
D_SSD = D_MIX - D_CONV
SSD_HEAD_DIM = 64
SSD_HEADS = D_SSD // SSD_HEAD_DIM
SSD_GROUPS = 2
SSD_HEADS_PER_GROUP = SSD_HEADS // SSD_GROUPS
SSD_STATE = 128
SSD_CONV = 4
SSD_CHUNK = 128
SHORT_CONV = 3
D_XBC = D_SSD + 2 * SSD_GROUPS * SSD_STATE
D_IN_PROJ = 3 * D_CONV + D_SSD + D_XBC + SSD_HEADS
D_FF = 2816
N_EXPERTS = 8
TOP_K = 2
D_FF_EXPERT = D_FF // TOP_K
N_DENSE = (DEPTH + 1) // 2
N_MOE = DEPTH // 2
EPS = 1e-6

kernel_name = 'hymba_style_shortconv_ssd_hybrid_step'


def _rmsnorm(x, g):
    xf = x.astype(jnp.float32)
    y = xf * lax.rsqrt(jnp.mean(xf * xf, axis=-1, keepdims=True) + EPS)
    return (y * g.astype(jnp.float32)).astype(x.dtype)


def _causal_dwconv(u, state, w):
    L = u.shape[1]
    k = w.shape[0]
    full = jnp.concatenate([state.astype(u.dtype), u], axis=1)
    out = full[:, 0:L] * w[0]
    for j in range(1, k):
        out = out + full[:, j:j + L] * w[j]
    return out, full[:, L:]


def _ssd_chunked(x, dt, a, bm, cm, s0):
    bsz, L, H, P = x.shape
    N = bm.shape[-1]
    G, HG = SSD_GROUPS, SSD_HEADS_PER_GROUP
    f32 = jnp.float32
    q = min(SSD_CHUNK, L)
    nc = -(-L // q)
    pad = nc * q - L
    x = x.astype(f32)
    dt = dt.astype(f32)
    bm = bm.astype(f32)
    cm = cm.astype(f32)
    if pad:
        x = jnp.pad(x, ((0, 0), (0, pad), (0, 0), (0, 0)))
        dt = jnp.pad(dt, ((0, 0), (0, pad), (0, 0)))
        bm = jnp.pad(bm, ((0, 0), (0, pad), (0, 0), (0, 0)))
        cm = jnp.pad(cm, ((0, 0), (0, pad), (0, 0), (0, 0)))
    x = x.reshape(bsz, nc, q, G, HG, P)
    dt = dt.reshape(bsz, nc, q, G, HG)
    bm = bm.reshape(bsz, nc, q, G, N)
    cm = cm.reshape(bsz, nc, q, G, N)
    cum = jnp.cumsum(dt * a.astype(f32).reshape(G, HG), axis=2)
    xdt = x * dt[..., None]
    causal = jnp.tril(jnp.ones((q, q), dtype=bool))[None, None, :, :, None, None]
    seg = cum[:, :, :, None] - cum[:, :, None, :]
    decay = jnp.exp(jnp.where(causal, seg, -jnp.inf))
    cb = jnp.einsum('bcqgn,bcsgn->bcqsg', cm, bm)
    y_diag = jnp.einsum('bcqsgh,bcsghp->bcqghp', cb[..., None] * decay, xdt)
    to_end = jnp.exp(cum[:, :, -1:] - cum)
    chunk_states = jnp.einsum('bcsgn,bcsghp->bcghpn', bm, xdt * to_end[..., None])
    chunk_decay = jnp.exp(cum[:, :, -1])

    def step(s, inp):
        st, dec = inp
        return s * dec[..., None, None] + st, s

    s_init = s0.astype(f32).reshape(bsz, G, HG, P, N)
    s_fin, s_prev = lax.scan(step, s_init, (jnp.moveaxis(chunk_states, 1, 0), jnp.moveaxis(chunk_decay, 1, 0)))
    s_prev = jnp.moveaxis(s_prev, 0, 1)
    y_off = jnp.einsum('bcqgn,bcghpn->bcqghp', cm, s_prev) * jnp.exp(cum)[..., None]
    y = (y_diag + y_off).reshape(bsz, nc * q, H, P)[:, :L]
    return y, s_fin.reshape(bsz, H, P, N)


def _token_mixer(h, conva_state, convb_state, ssm_state, w_in, w_out, conva_w, convb_w, convb_b,
                 dt_bias, a_log, d_skip, ssd_norm):
    bsz, L, _ = h.shape
    proj = jnp.einsum('bld,de->ble', h, w_in)
    sizes = [D_CONV, D_CONV, D_CONV, D_SSD, D_XBC, SSD_HEADS]
    gb, gc, hv, z, xbc, dt_raw = jnp.split(proj, list(np.cumsum(sizes)[:-1]), axis=-1)
    u = gc * hv
    v, new_conva = _causal_dwconv(u, conva_state, conva_w)
    ya = gb * v
    xbc_c, new_convb = _causal_dwconv(xbc, convb_state, convb_w)
    xbc_c = jax.nn.silu(xbc_c + convb_b)
    gn = SSD_GROUPS * SSD_STATE
    xs = xbc_c[..., :D_SSD].reshape(bsz, L, SSD_HEADS, SSD_HEAD_DIM)
    bm = xbc_c[..., D_SSD:D_SSD + gn].reshape(bsz, L, SSD_GROUPS, SSD_STATE)
    cm = xbc_c[..., D_SSD + gn:].reshape(bsz, L, SSD_GROUPS, SSD_STATE)
    dt = jax.nn.softplus(dt_raw.astype(jnp.float32) + dt_bias.astype(jnp.float32))
    a = -jnp.exp(a_log.astype(jnp.float32))
    y, new_ssm = _ssd_chunked(xs, dt, a, bm, cm, ssm_state)
    y = y + d_skip.astype(jnp.float32)[:, None] * xs.astype(jnp.float32)
    y = y.reshape(bsz, L, D_SSD) * jax.nn.silu(z.astype(jnp.float32))
    yg = y.reshape(bsz, L, SSD_GROUPS, D_SSD // SSD_GROUPS)
    yg = yg * lax.rsqrt(jnp.mean(yg * yg, axis=-1, keepdims=True) + EPS)
    yb = (yg.reshape(bsz, L, D_SSD) * ssd_norm.astype(jnp.float32)).astype(h.dtype)
    out = jnp.einsum('ble,ed->bld', jnp.concatenate([ya, yb], axis=-1), w_out)
    return out, new_conva, new_convb, new_ssm.astype(ssm_state.dtype)


def _swiglu(h, wg, wu, wd):
    g = jnp.einsum('bld,df->blf', h, wg)
    up = jnp.einsum('bld,df->blf', h, wu)
    return jnp.einsum('blf,fd->bld', jax.nn.silu(g) * up, wd)


def _moe_swiglu(h, router, wg, wu, wd):
    logits = jnp.einsum('bld,de->ble', h, router).astype(jnp.float32)
    probs = jax.nn.softmax(logits, axis=-1)
    top_p, top_i = lax.top_k(probs, TOP_K)
    top_p = top_p / jnp.sum(top_p, axis=-1, keepdims=True)
    combine = jnp.sum(jax.nn.one_hot(top_i, N_EXPERTS, dtype=jnp.float32) * top_p[..., None], axis=-2)
    out = jnp.zeros(h.shape, jnp.float32)
    for e in range(N_EXPERTS):
        ye = _swiglu(h, wg[e], wu[e], wd[e]).astype(jnp.float32)
        out = out + combine[..., e:e + 1] * ye
    return out.astype(h.dtype)


def setup_inputs(seed: int = 0) -> dict:
    key = jax.random.key(seed)
    ks = jax.random.split(key, 32)
    f32 = jnp.float32
    nrm = lambda k, shape, s: jax.random.normal(k, shape, f32) * s
    gain = lambda k, shape: 1.0 + 0.02 * jax.random.normal(k, shape, f32)
    dt0 = jnp.exp(jax.random.uniform(ks[20], (DEPTH, SSD_HEADS), f32) * (np.log(0.1) - np.log(0.001)) + np.log(0.001))
    dt_bias = dt0 + jnp.log(-jnp.expm1(-dt0))
    return {
        'x_prompt': nrm(ks[0], (BATCH, SEQ, D_MODEL), 1.0),
        'x_sample': nrm(ks[1], (DEC_BATCH, DEC_SEQ, D_MODEL), 1.0),
        'c_prompt': nrm(ks[2], (BATCH, D_MODEL), 1.0),
        'c_sample': nrm(ks[3], (DEC_BATCH, D_MODEL), 1.0),
        'state_conva': nrm(ks[4], (DEPTH, DEC_BATCH, SHORT_CONV - 1, D_CONV), 1.0),
        'state_convb': nrm(ks[5], (DEPTH, DEC_BATCH, SSD_CONV - 1, D_XBC), 1.0),
        'state_ssm': nrm(ks[6], (DEPTH, DEC_BATCH, SSD_HEADS, SSD_HEAD_DIM, SSD_STATE), 0.3),
        'ada_w': nrm(ks[7], (DEPTH, D_MODEL, 6 * D_MODEL), 0.5 * D_MODEL ** -0.5),
        'ada_b': nrm(ks[8], (DEPTH, 6 * D_MODEL), 0.02),
        'norm_pre_mix': gain(ks[9], (DEPTH, D_MODEL)),
        'norm_post_mix': gain(ks[10], (DEPTH, D_MODEL)),
        'norm_pre_ffn': gain(ks[11], (DEPTH, D_MODEL)),
        'norm_post_ffn': gain(ks[12], (DEPTH, D_MODEL)),
        'w_in': nrm(ks[13], (DEPTH, D_MODEL, D_IN_PROJ), D_MODEL ** -0.5),
        'w_out': nrm(ks[14], (DEPTH, D_MIX, D_MODEL), D_MIX ** -0.5),
        'conva_w': nrm(ks[15], (DEPTH, SHORT_CONV, D_CONV), SHORT_CONV ** -0.5),
        'convb_w': nrm(ks[16], (DEPTH, SSD_CONV, D_XBC), SSD_CONV ** -0.5),
        'convb_b': nrm(ks[17], (DEPTH, D_XBC), 0.02),
        'dt_bias': dt_bias,
        'a_log': jnp.log(jax.random.uniform(ks[18], (DEPTH, SSD_HEADS), f32, 1.0, 16.0)),
        'd_skip': gain(ks[19], (DEPTH, SSD_HEADS)),
        'ssd_norm': gain(ks[21], (DEPTH, D_SSD)),
        'ffd_w_gate': nrm(ks[22], (N_DENSE, D_MODEL, D_FF), D_MODEL ** -0.5),
        'ffd_w_up': nrm(ks[23], (N_DENSE, D_MODEL, D_FF), D_MODEL ** -0.5),
        'ffd_w_down': nrm(ks[24], (N_DENSE, D_FF, D_MODEL), D_FF ** -0.5),
        'moe_router': nrm(ks[25], (N_MOE, D_MODEL, N_EXPERTS), D_MODEL ** -0.5),
        'moe_w_gate': nrm(ks[26], (N_MOE, N_EXPERTS, D_MODEL, D_FF_EXPERT), D_MODEL ** -0.5),
        'moe_w_up': nrm(ks[27], (N_MOE, N_EXPERTS, D_MODEL, D_FF_EXPERT), D_MODEL ** -0.5),
        'moe_w_down': nrm(ks[28], (N_MOE, N_EXPERTS, D_FF_EXPERT, D_MODEL), D_FF_EXPERT ** -0.5),
    }


def reference(x_prompt, x_sample, c_prompt, c_sample, state_conva, state_convb, state_ssm,
              ada_w, ada_b, norm_pre_mix, norm_post_mix, norm_pre_ffn, norm_post_ffn,
              w_in, w_out, conva_w, convb_w, convb_b, dt_bias, a_log, d_skip, ssd_norm,
              ffd_w_gate, ffd_w_up, ffd_w_down, moe_router, moe_w_gate, moe_w_up, moe_w_down):

    def run_group(x, c, conva, convb, ssm):
        new_a, new_b, new_s = [], [], []
        for i in range(DEPTH):
            mod = jnp.einsum('bd,de->be', jax.nn.silu(c), ada_w[i]) + ada_b[i]
            sh1, sc1, g1, sh2, sc2, g2 = [m[:, None, :] for m in jnp.split(mod, 6, axis=-1)]
            h = _rmsnorm(x, norm_pre_mix[i]) * (1.0 + sc1) + sh1
            mix, na, nb, ns = _token_mixer(h, conva[i], convb[i], ssm[i], w_in[i], w_out[i], conva_w[i],
                                           convb_w[i], convb_b[i], dt_bias[i], a_log[i], d_skip[i], ssd_norm[i])
            x = x + g1 * _rmsnorm(mix, norm_post_mix[i])
            h = _rmsnorm(x, norm_pre_ffn[i]) * (1.0 + sc2) + sh2
            j = i // 2
            if i % 2 == 0:
                f = _swiglu(h, ffd_w_gate[j], ffd_w_up[j], ffd_w_down[j])
            else:
                f = _moe_swiglu(h, moe_router[j], moe_w_gate[j], moe_w_up[j], moe_w_down[j])
            x = x + g2 * _rmsnorm(f, norm_post_ffn[i])
            new_a.append(na)
            new_b.append(nb)
            new_s.append(ns)
        return x, jnp.stack(new_a), jnp.stack(new_b), jnp.stack(new_s)

    bp = x_prompt.shape[0]
    zero_a = jnp.zeros((DEPTH, bp, SHORT_CONV - 1, D_CONV), x_prompt.dtype)
    zero_b = jnp.zeros((DEPTH, bp, SSD_CONV - 1, D_XBC), x_prompt.dtype)
    zero_s = jnp.zeros((DEPTH, bp, SSD_HEADS, SSD_HEAD_DIM, SSD_STATE), state_ssm.dtype)
    y_prompt, a_p, b_p, s_p = run_group(x_prompt, c_prompt, zero_a, zero_b, zero_s)
    y_sample, a_s, b_s, s_s = run_group(x_sample, c_sample, state_conva, state_convb, state_ssm)
    return (y_prompt, y_sample, a_p, b_p, s_p, a_s, b_s, s_s)
```

```python
import functools

import jax
import jax.numpy as jnp
from jax import lax
from jax.experimental import pallas as pl
from jax.experimental.pallas import tpu as pltpu

EPS = 1e-6
SSD_GROUPS = 2
SSD_CHUNK = 128
TOP_K = 2
LANES = 128
SUBLANES = 8
VMEM_LIMIT_BYTES = 56 * 1024 * 1024

F32 = jnp.float32
BF16 = jnp.bfloat16


def _cparams(*sem):
    return pltpu.CompilerParams(dimension_semantics=sem, vmem_limit_bytes=VMEM_LIMIT_BYTES)


def _resident(shape):
    return pl.BlockSpec(shape, lambda *_: (0,) * len(shape), pipeline_mode=pl.Buffered(1))


def _silu(x):
    return x * (1.0 / (1.0 + jnp.exp(-x)))


def _softplus(x):
    return jnp.maximum(x, 0.0) + jnp.log1p(jnp.exp(-jnp.abs(x)))


def _rms(x, g):
    return x * lax.rsqrt(jnp.mean(x * x, axis=-1, keepdims=True) + EPS) * g


def _dot(a, b):
    return jnp.dot(a, b, preferred_element_type=F32)


def _adaln_kernel(c_ref, w_ref, b_ref, o_ref):
    s = _silu(c_ref[...]).astype(BF16)
    o_ref[...] = _dot(s, w_ref[...]) + b_ref[...]


def _adaln(c_all, ada_w, ada_b):
    depth, d, d6 = ada_w.shape
    rows = c_all.shape[0]
    tn = d6 // 4
    return pl.pallas_call(
        _adaln_kernel,
        grid=(depth, d6 // tn),
        in_specs=[
            pl.BlockSpec((rows, d), lambda l, j: (0, 0)),
            pl.BlockSpec((None, d, tn), lambda l, j: (l, 0, j)),
            pl.BlockSpec((None, 1, tn), lambda l, j: (l, 0, j)),
        ],
        out_specs=pl.BlockSpec((None, rows, tn), lambda l, j: (l, 0, j)),
        out_shape=jax.ShapeDtypeStruct((depth, rows, d6), F32),
        compiler_params=_cparams("arbitrary", "arbitrary"),
        name="adaln",
    )(c_all, ada_w, ada_b.reshape(depth, 1, d6))


def _mod_spec(mod, k, d, tiles_per_seq):
    if tiles_per_seq is None:
        return pl.BlockSpec((mod.shape[0], d), lambda i, *_: (0, k))
    return pl.BlockSpec((None, 1, d), lambda i, *_: (i // tiles_per_seq, 0, k))


def _inproj_kernel(x_ref, sh_ref, sc_ref, g_ref, w_ref, gb_ref, u_ref, z_ref, xbc_ref, dt_ref, *, dc, ds, dx):
    h = (_rms(x_ref[...], g_ref[...]) * (1.0 + sc_ref[...]) + sh_ref[...]).astype(BF16)
    step = 512 if dc % 512 == 0 else dc
    for a in range(0, dc, step):
        gb_ref[:, a:a + step] = _dot(h, w_ref[:, a:a + step])
        u_ref[:, a:a + step] = _dot(h, w_ref[:, dc + a:dc + a + step]) * _dot(h, w_ref[:, 2 * dc + a:2 * dc + a + step])
    zstep = 512 if ds % 512 == 0 else ds
    for a in range(0, ds, zstep):
        z_ref[:, a:a + zstep] = _dot(h, w_ref[:, 3 * dc + a:3 * dc + a + zstep])
    o = 3 * dc + ds
    xstep = 512 if dx % 512 == 0 else dx
    for a in range(0, dx, xstep):
        xbc_ref[:, a:a + xstep] = _dot(h, w_ref[:, o + a:o + a + xstep])
    dt_ref[...] = _dot(h, w_ref[:, o + dx:o + dx + LANES])


def _inproj(x2d, mod, tiles_per_seq, tm, norm_w, w_in, dc, ds, dx):
    t, d = x2d.shape
    wcols = w_in.shape[1]
    row = lambda w: pl.BlockSpec((tm, w), lambda i: (i, 0))
    return pl.pallas_call(
        functools.partial(_inproj_kernel, dc=dc, ds=ds, dx=dx),
        grid=(t // tm,),
        in_specs=[
            row(d),
            _mod_spec(mod, 0, d, tiles_per_seq),
            _mod_spec(mod, 1, d, tiles_per_seq),
            pl.BlockSpec((1, d), lambda i: (0, 0)),
            _resident((d, wcols)),
        ],
        out_specs=[row(dc), row(dc), row(ds), row(dx), row(LANES)],
        out_shape=[
            jax.ShapeDtypeStruct((t, dc), F32),
            jax.ShapeDtypeStruct((t, dc), F32),
            jax.ShapeDtypeStruct((t, ds), F32),
            jax.ShapeDtypeStruct((t, dx), F32),
            jax.ShapeDtypeStruct((t, LANES), F32),
        ],
        compiler_params=_cparams("arbitrary"),
        name="inproj",
    )(x2d, mod, mod, norm_w.reshape(1, d), w_in)


def _split_hi_lo(v):
    hi = v.astype(BF16)
    lo = (v - hi.astype(F32)).astype(BF16)
    return jnp.concatenate([hi, lo], axis=1)


def _conv_taps(ext_ref, w_ref, rows, base):
    k = w_ref.shape[0]
    acc = w_ref[k - 1:k, :] * ext_ref[base:base + rows, :]
    for j in range(k - 1):
        off = base - (k - 1) + j
        acc = acc + w_ref[j:j + 1, :] * ext_ref[off:off + rows, :]
    return acc


def _ssd_prompt_kernel(gb_ref, u_ref, z_ref, xbc_ref, dt_ref, caw_ref, cbw_ref, cbb_ref, dtb_ref, alog_ref,
                       dsk_ref, snorm_ref, e_ref, cat_ref, nca_ref, ncb_ref, nss_ref, uext, xext, st,
                       *, dc, ds, n, p, hg):
    c = pl.program_id(1)
    q = SSD_CHUNK
    g_cnt = SSD_GROUPS
    gw = hg * p

    @pl.when(c == 0)
    def _():
        uext[0:SUBLANES, :] = jnp.zeros((SUBLANES, dc), F32)
        xext[0:SUBLANES, :] = jnp.zeros((SUBLANES, xext.shape[1]), F32)
        st[...] = jnp.zeros(st.shape, F32)

    base = SUBLANES
    uext[base:base + q, :] = u_ref[...]
    xext[base:base + q, :] = xbc_ref[...]

    v = _conv_taps(uext, caw_ref, q, base)
    cat_ref[:, 0:dc] = (gb_ref[...] * v).astype(BF16)

    xc = _silu(_conv_taps(xext, cbw_ref, q, base) + cbb_ref[...])
    xs = xc[:, 0:ds]

    dt = _softplus(dt_ref[...] + dtb_ref[...])
    da = dt * (-jnp.exp(alog_ref[...]))
    row = lax.broadcasted_iota(jnp.int32, (q, LANES), 0)
    cum = da
    k = 1
    while k < q:
        cum = cum + jnp.where(row >= k, pltpu.roll(cum, k, axis=0), 0.0)
        k *= 2
    cum_last = cum[q - 1:q, :]
    w_exp = _dot(_split_hi_lo(dt * jnp.exp(cum_last - cum)), e_ref[...])
    ecum_exp = _dot(_split_hi_lo(jnp.exp(cum)), e_ref[...])
    cum_t = cum.T
    dt_t = dt.T

    tril = lax.broadcasted_iota(jnp.int32, (q, q), 0) >= lax.broadcasted_iota(jnp.int32, (q, q), 1)
    lane = lax.broadcasted_iota(jnp.int32, (q, LANES), 1)
    hpl = LANES // p
    ys = []
    for g in range(g_cnt):
        bm = xc[:, ds + g * n:ds + (g + 1) * n]
        cm = xc[:, ds + g_cnt * n + g * n:ds + g_cnt * n + (g + 1) * n].astype(BF16)
        bm_t = bm.T.astype(BF16)
        cb = _dot(cm, bm_t)
        parts = []
        for slab in range(gw // LANES):
            lhs, rhs = [], []
            xslab = xs[:, g * gw + slab * LANES:g * gw + (slab + 1) * LANES]
            for j in range(hpl):
                h = g * hg + slab * hpl + j
                seg = cum[:, h:h + 1] - cum_t[h:h + 1, :]
                m = cb * jnp.where(tril, jnp.exp(seg), 0.0) * dt_t[h:h + 1, :]
                lhs.append(m.astype(BF16))
                rhs.append(jnp.where((lane >= j * p) & (lane < (j + 1) * p), xslab, 0.0).astype(BF16))
            parts.append(_dot(jnp.concatenate(lhs, axis=1), jnp.concatenate(rhs, axis=0)))
        gsl = slice(g * gw, (g + 1) * gw)
        s_prev = st[g]
        y_off = _dot(cm, s_prev.astype(BF16)) * ecum_exp[:, gsl]
        ys.append(jnp.concatenate(parts, axis=1) + y_off)
        xw = (xs[:, gsl] * w_exp[:, gsl]).astype(BF16)
        st[g] = s_prev * ecum_exp[q - 1:q, gsl] + _dot(bm_t, xw)

    for g in range(g_cnt):
        gsl = slice(g * gw, (g + 1) * gw)
        y = ys[g] + dsk_ref[:, gsl] * xs[:, gsl]
        y = y * _silu(z_ref[:, gsl])
        y = y * lax.rsqrt(jnp.mean(y * y, axis=-1, keepdims=True) + EPS) * snorm_ref[:, gsl]
        cat_ref[:, dc + g * gw:dc + (g + 1) * gw] = y.astype(BF16)

    uext[0:SUBLANES, :] = uext[q:q + SUBLANES, :]
    xext[0:SUBLANES, :] = xext[q:q + SUBLANES, :]

    @pl.when(c == pl.num_programs(1) - 1)
    def _():
        ka = nca_ref.shape[0]
        kb = ncb_ref.shape[0]
        nca_ref[...] = uext[SUBLANES - ka:SUBLANES, :]
        ncb_ref[...] = xext[SUBLANES - kb:SUBLANES, :]
        for g in range(g_cnt):
            for slab in range(gw // LANES):
                t_blk = st[g, :, slab * LANES:(slab + 1) * LANES].T
                r0 = g * gw + slab * LANES
                nss_ref[r0:r0 + LANES, :] = t_blk


def _ssd_prompt(gb, u, z, xbc, dtr, bsz, seq, caw, cbw, cbb, dtb, alog, dsk, snorm, emat, n, p, hg):
    dc = u.shape[1]
    ds = z.shape[1]
    dx = xbc.shape[1]
    q = SSD_CHUNK
    nc = seq // q
    ka, kb = caw.shape[0] - 1, cbw.shape[0] - 1
    row = lambda w: pl.BlockSpec((q, w), lambda b, c: (b * nc + c, 0))
    full = lambda a: pl.BlockSpec(a.shape, lambda b, c: (0,) * a.ndim)
    params = [caw, cbw, cbb, dtb, alog, dsk, snorm, emat]
    return pl.pallas_call(
        functools.partial(_ssd_prompt_kernel, dc=dc, ds=ds, n=n, p=p, hg=hg),
        grid=(bsz, nc),
        in_specs=[row(dc), row(dc), row(ds), row(dx), row(LANES)] + [full(a) for a in params],
        out_specs=[
            pl.BlockSpec((q, dc + ds), lambda b, c: (b * nc + c, 0)),
            pl.BlockSpec((None, ka, dc), lambda b, c: (b, 0, 0)),
            pl.BlockSpec((None, kb, dx), lambda b, c: (b, 0, 0)),
            pl.BlockSpec((None, ds, n), lambda b, c: (b, 0, 0)),
        ],
        out_shape=[
            jax.ShapeDtypeStruct((bsz * seq, dc + ds), BF16),
            jax.ShapeDtypeStruct((bsz, ka, dc), F32),
            jax.ShapeDtypeStruct((bsz, kb, dx), F32),
            jax.ShapeDtypeStruct((bsz, ds, n), F32),
        ],
        scratch_shapes=[
            pltpu.VMEM((q + SUBLANES, dc), F32),
            pltpu.VMEM((q + SUBLANES, dx), F32),
            pltpu.VMEM((SSD_GROUPS, n, hg * p), F32),
        ],
        compiler_params=_cparams("arbitrary", "arbitrary"),
        name="ssd_prompt",
    )(gb, u, z, xbc, dtr, *params)


def _route(logits, n_exp):
    lane = lax.broadcasted_iota(jnp.int32, logits.shape, 1).astype(F32)
    valid = lane < n_exp
    logits = jnp.where(valid, logits, -jnp.inf)
    e = jnp.exp(logits - jnp.max(logits, axis=-1, keepdims=True))
    prob = jnp.where(valid, e / jnp.sum(e, axis=-1, keepdims=True), -1.0)
    big = float(LANES)
    m1 = jnp.max(prob, axis=-1, keepdims=True)
    i1 = jnp.min(jnp.where(prob == m1, lane, big), axis=-1, keepdims=True)
    rest = jnp.where(lane == i1, -1.0, prob)
    m2 = jnp.max(rest, axis=-1, keepdims=True)
    i2 = jnp.min(jnp.where(rest == m2, lane, big), axis=-1, keepdims=True)
    den = m1 + m2
    return jnp.where(lane == i1, m1 / den, 0.0) + jnp.where(lane == i2, m2 / den, 0.0)


def _outproj_kernel(cat_ref, w_ref, x_ref, g1_ref, npost_ref, npre_ref, sh_ref, sc_ref, *rest, n_exp):
    if n_exp:
        r_ref, x1_ref, h2_ref, comb_ref = rest
    else:
        x1_ref, h2_ref = rest
    mix = _dot(cat_ref[...], w_ref[...])
    x1 = x_ref[...] + g1_ref[...] * _rms(mix, npost_ref[...])
    x1_ref[...] = x1
    h2 = _rms(x1, npre_ref[...]) * (1.0 + sc_ref[...]) + sh_ref[...]
    h2_ref[...] = h2.astype(BF16)
    if n_exp:
        logits = jnp.dot(h2, r_ref[...], preferred_element_type=F32, precision=lax.Precision.HIGHEST)
        comb_ref[...] = _route(logits, n_exp)


def _outproj(cat, w_out, x2d, mod, tiles_per_seq, tm, npost, npre, router, n_exp):
    t, d = x2d.shape
    dm = cat.shape[1]
    row = lambda w: pl.BlockSpec((tm, w), lambda i: (i, 0))
    vec = pl.BlockSpec((1, d), lambda i: (0, 0))
    in_specs = [row(dm), _resident((dm, d)), row(d), _mod_spec(mod, 2, d, tiles_per_seq),
                vec, vec, _mod_spec(mod, 3, d, tiles_per_seq), _mod_spec(mod, 4, d, tiles_per_seq)]
    args = [cat, w_out, x2d, mod, npost.reshape(1, d), npre.reshape(1, d), mod, mod]
    out_specs = [row(d), row(d)]
    out_shape = [jax.ShapeDtypeStruct((t, d), F32), jax.ShapeDtypeStruct((t, d), BF16)]
    if n_exp:
        in_specs.append(pl.BlockSpec((d, LANES), lambda i: (0, 0)))
        args.append(router)
        out_specs.append(row(LANES))
        out_shape.append(jax.ShapeDtypeStruct((t, LANES), F32))
    return pl.pallas_call(
        functools.partial(_outproj_kernel, n_exp=n_exp),
        grid=(t // tm,),
        in_specs=in_specs,
        out_specs=out_specs,
        out_shape=out_shape,
        compiler_params=_cparams("arbitrary"),
        name="outproj",
    )(*args)


def _ffn_kernel(h_ref, wg_ref, wu_ref, wd_ref, x1_ref, g2_ref, npost_ref, o_ref, *, fstep):
    h = h_ref[...]
    f = wg_ref.shape[1]
    acc = None
    for a in range(0, f, fstep):
        act = (_silu(_dot(h, wg_ref[:, a:a + fstep])) * _dot(h, wu_ref[:, a:a + fstep])).astype(BF16)
        part = _dot(act, wd_ref[a:a + fstep, :])
        acc = part if acc is None else acc + part
    o_ref[...] = x1_ref[...] + g2_ref[...] * _rms(acc, npost_ref[...])


def _ffn_dense(h2, wg, wu, wd, x1, mod, tiles_per_seq, tm, npost):
    t, d = x1.shape
    f = wg.shape[1]
    fstep = next(c for c in (f // 4, f // 2, f) if c % LANES == 0)
    row = lambda w: pl.BlockSpec((tm, w), lambda i: (i, 0))
    const = lambda a: _resident(a.shape)
    return pl.pallas_call(
        functools.partial(_ffn_kernel, fstep=fstep),
        grid=(t // tm,),
        in_specs=[row(d), const(wg), const(wu), const(wd), row(d), _mod_spec(mod, 5, d, tiles_per_seq),
                  pl.BlockSpec((1, d), lambda i: (0, 0))],
        out_specs=row(d),
        out_shape=jax.ShapeDtypeStruct((t, d), F32),
        compiler_params=_cparams("arbitrary"),
        name="ffn_dense",
    )(h2, wg, wu, wd, x1, mod, npost.reshape(1, d))


def _moe_kernel(h_ref, comb_ref, wg_ref, wu_ref, wd_ref, x1_ref, g2_ref, npost_ref, o_ref, acc_ref):
    e = pl.program_id(1)
    h = h_ref[...]
    act = (_silu(_dot(h, wg_ref[...])) * _dot(h, wu_ref[...])).astype(BF16)
    ye = _dot(act, wd_ref[...])
    lane = lax.broadcasted_iota(jnp.int32, comb_ref.shape, 1)
    ce = jnp.sum(jnp.where(lane == e, comb_ref[...], 0.0), axis=-1, keepdims=True)
    contrib = ce * ye

    @pl.when(e == 0)
    def _():
        acc_ref[...] = contrib

    @pl.when(e > 0)
    def _():
        acc_ref[...] += contrib

    @pl.when(e == pl.num_programs(1) - 1)
    def _():
        o_ref[...] = x1_ref[...] + g2_ref[...] * _rms(acc_ref[...], npost_ref[...])


def _ffn_moe(h2, comb, wg, wu, wd, x1, mod, tiles_per_seq, tm, npost):
    t, d = x1.shape
    n_exp, _, fe = wg.shape
    row = lambda w: pl.BlockSpec((tm, w), lambda i, e: (i, 0))
    return pl.pallas_call(
        _moe_kernel,
        grid=(t // tm, n_exp),
        in_specs=[row(d), row(LANES),
                  pl.BlockSpec((None, d, fe), lambda i, e: (e, 0, 0)),
                  pl.BlockSpec((None, d, fe), lambda i, e: (e, 0, 0)),
                  pl.BlockSpec((None, fe, d), lambda i, e: (e, 0, 0)),
                  row(d), _mod_spec(mod, 5, d, tiles_per_seq), pl.BlockSpec((1, d), lambda i, e: (0, 0))],
        out_specs=row(d),
        out_shape=jax.ShapeDtypeStruct((t, d), F32),
        scratch_shapes=[pltpu.VMEM((tm, d), F32)],
        compiler_params=_cparams("arbitrary", "arbitrary"),
        name="ffn_moe",
    )(h2, comb, wg, wu, wd, x1, mod, npost.reshape(1, d))


def _ssd_sample_kernel(gb_ref, u_ref, z_ref, xbc_ref, dt_ref, sa_ref, sb_ref, ss_ref, caw_ref, cbw_ref, cbb_ref,
                       dtb_ref, alog_ref, dsk_ref, snorm_ref, e_ref, cat_ref, nca_ref, ncb_ref, nss_ref,
                       *, dc, ds, dx, n, p, hg, steps, bb):
    g_cnt = SSD_GROUPS
    gw = hg * p
    rows = steps * bb
    ka = caw_ref.shape[0] - 1
    kb = cbw_ref.shape[0] - 1

    def conv(cur_ref, st_ref, w_ref, kprev, width):
        hist = [st_ref[:, j * width:(j + 1) * width] for j in range(kprev)] + [cur_ref[t] for t in range(steps)]
        outs = []
        for t in range(steps):
            acc = w_ref[kprev:kprev + 1, :] * hist[t + kprev]
            for j in range(kprev):
                acc = acc + w_ref[j:j + 1, :] * hist[t + j]
            outs.append(acc)
        return outs, hist[len(hist) - kprev:]

    v, new_a = conv(u_ref, sa_ref, caw_ref, ka, dc)
    for t in range(steps):
        cat_ref[t, :, 0:dc] = (gb_ref[t] * v[t]).astype(BF16)
    for j in range(ka):
        nca_ref[:, j * dc:(j + 1) * dc] = new_a[j]
    xcs, new_b = conv(xbc_ref, sb_ref, cbw_ref, kb, dx)
    for j in range(kb):
        ncb_ref[:, j * dx:(j + 1) * dx] = new_b[j]
    xc = _silu(jnp.concatenate(xcs, axis=0) + cbb_ref[...])
    xs = xc[:, 0:ds]

    dt = _softplus(jnp.concatenate([dt_ref[t] for t in range(steps)], axis=0) + dtb_ref[...])
    da = dt * (-jnp.exp(alog_ref[...]))
    cums = [da[0:bb]]
    for t in range(1, steps):
        cums.append(cums[-1] + da[t * bb:(t + 1) * bb])
    cum = jnp.concatenate(cums, axis=0)
    cum_last = jnp.concatenate([cums[-1]] * steps, axis=0)
    w_exp = _dot(_split_hi_lo(dt * jnp.exp(cum_last - cum)), e_ref[...])
    ecum_exp = _dot(_split_hi_lo(jnp.exp(cum)), e_ref[...])
    pad = LANES - rows

    def pad_t(a):
        return jnp.concatenate([a, jnp.zeros((pad, LANES), F32)], axis=0).T[:, 0:rows] if pad else a.T

    cum_t = pad_t(cum)
    dt_t = pad_t(dt)

    ri = lax.broadcasted_iota(jnp.int32, (rows, rows), 0)
    ci = lax.broadcasted_iota(jnp.int32, (rows, rows), 1)
    same = ((ri % bb) == (ci % bb)) & (ri >= ci)
    lane = lax.broadcasted_iota(jnp.int32, (rows, LANES), 1)
    rowid = lax.broadcasted_iota(jnp.int32, (rows, 1), 0) % bb
    seqlane = lax.broadcasted_iota(jnp.int32, (LANES, LANES), 1)
    hpl = LANES // p
    nslab = gw // LANES
    ys = []
    for g in range(g_cnt):
        gsl = slice(g * gw, (g + 1) * gw)
        bm = xc[:, ds + g * n:ds + (g + 1) * n].astype(BF16)
        cm = xc[:, ds + g_cnt * n + g * n:ds + g_cnt * n + (g + 1) * n].astype(BF16)
        cb = lax.dot_general(cm, bm, (((1,), (1,)), ((), ())), preferred_element_type=F32)
        parts = []
        for slab in range(nslab):
            lhs, rhs = [], []
            xslab = xs[:, g * gw + slab * LANES:g * gw + (slab + 1) * LANES]
            for j in range(hpl):
                h = g * hg + slab * hpl + j
                seg = cum[:, h:h + 1] - cum_t[h:h + 1, :]
                m = cb * jnp.where(same, jnp.exp(seg), 0.0) * dt_t[h:h + 1, :]
                lhs.append(m.astype(BF16))
                rhs.append(jnp.where((lane >= j * p) & (lane < (j + 1) * p), xslab, 0.0).astype(BF16))
            parts.append(_dot(jnp.concatenate(lhs, axis=1), jnp.concatenate(rhs, axis=0)))
        y_diag = jnp.concatenate(parts, axis=1)
        xw = (xs[:, gsl] * w_exp[:, gsl]).astype(BF16)
        dec = ecum_exp[(steps - 1) * bb:steps * bb, gsl]
        dec = jnp.concatenate([dec, jnp.zeros((LANES - bb, gw), F32)], axis=0)
        dec_t = [dec[:, s * LANES:(s + 1) * LANES].T for s in range(nslab)]

        def per_seq(b, y_off, g=g, cm=cm, bm=bm, xw=xw, dec_t=dec_t):
            r0 = g * gw
            s0 = ss_ref[b, r0:r0 + gw, :]
            r = lax.dot_general(cm, s0.astype(BF16), (((1,), (1,)), ((), ())), preferred_element_type=F32)
            y_off = jnp.where(rowid == b, r, y_off)
            xw_b = jnp.where(rowid == b, xw, jnp.zeros_like(xw))
            upd = lax.dot_general(xw_b, bm, (((0,), (0,)), ((), ())), preferred_element_type=F32)
            for s in range(nslab):
                dcol = jnp.sum(jnp.where(seqlane == b, dec_t[s], 0.0), axis=1, keepdims=True)
                nss_ref[b, r0 + s * LANES:r0 + (s + 1) * LANES, :] = (
                    s0[s * LANES:(s + 1) * LANES, :] * dcol + upd[s * LANES:(s + 1) * LANES, :])
            return y_off

        y_off = lax.fori_loop(0, bb, per_seq, jnp.zeros((rows, gw), F32))
        ys.append(y_diag + y_off * ecum_exp[:, gsl])

    for g in range(g_cnt):
        gsl = slice(g * gw, (g + 1) * gw)
        y = ys[g] + dsk_ref[:, gsl] * xs[:, gsl]
        zg = jnp.concatenate([z_ref[t, :, gsl] for t in range(steps)], axis=0)
        y = y * _silu(zg)
        y = (y * lax.rsqrt(jnp.mean(y * y, axis=-1, keepdims=True) + EPS) * snorm_ref[:, gsl]).astype(BF16)
        for t in range(steps):
            cat_ref[t, :, dc + g * gw:dc + (g + 1) * gw] = y[t * bb:(t + 1) * bb]


def _ssd_sample(gb, u, z, xbc, dtr, sa, sb, ss, caw, cbw, cbb, dtb, alog, dsk, snorm, emat, n, p, hg, bb):
    steps, bsz, dc = u.shape
    ds = z.shape[2]
    dx = xbc.shape[2]
    ka, kb = caw.shape[0] - 1, cbw.shape[0] - 1
    tok = lambda w: pl.BlockSpec((steps, bb, w), lambda i: (0, i, 0))
    full = lambda a: pl.BlockSpec(a.shape, lambda i: (0,) * a.ndim)
    params = [caw, cbw, cbb, dtb, alog, dsk, snorm, emat]
    return pl.pallas_call(
        functools.partial(_ssd_sample_kernel, dc=dc, ds=ds, dx=dx, n=n, p=p, hg=hg, steps=steps, bb=bb),
        grid=(bsz // bb,),
        in_specs=[tok(dc), tok(dc), tok(ds), tok(dx), tok(LANES),
                  pl.BlockSpec((bb, ka * dc), lambda i: (i, 0)),
                  pl.BlockSpec((bb, kb * dx), lambda i: (i, 0)),
                  pl.BlockSpec((bb, ds, n), lambda i: (i, 0, 0))] + [full(a) for a in params],
        out_specs=[tok(dc + ds),
                   pl.BlockSpec((bb, ka * dc), lambda i: (i, 0)),
                   pl.BlockSpec((bb, kb * dx), lambda i: (i, 0)),
                   pl.BlockSpec((bb, ds, n), lambda i: (i, 0, 0))],
        out_shape=[jax.ShapeDtypeStruct((steps, bsz, dc + ds), BF16),
                   jax.ShapeDtypeStruct((bsz, ka * dc), F32),
                   jax.ShapeDtypeStruct((bsz, kb * dx), F32),
                   jax.ShapeDtypeStruct((bsz, ds, n), F32)],
        compiler_params=_cparams("arbitrary"),
        name="ssd_sample",
    )(gb, u, z, xbc, dtr, sa, sb, ss, *params)


def _row_tile(t, want):
    tm = min(want, t)
    while t % tm:
        tm //= 2
    return tm


def kernel(x_prompt, x_sample, c_prompt, c_sample, state_conva, state_convb, state_ssm, ada_w, ada_b, norm_pre_mix,
           norm_post_mix, norm_pre_ffn, norm_post_ffn, w_in, w_out, conva_w, convb_w, convb_b, dt_bias, a_log, d_skip,
           ssd_norm, ffd_w_gate, ffd_w_up, ffd_w_down, moe_router, moe_w_gate, moe_w_up, moe_w_down):
    bp, seq, d = x_prompt.shape
    bs, steps, _ = x_sample.shape
    depth = w_in.shape[0]
    dc = conva_w.shape[-1]
    dx = convb_w.shape[-1]
    ds = ssd_norm.shape[-1]
    heads = dt_bias.shape[-1]
    p = ds // heads
    n = (dx - ds) // (2 * SSD_GROUPS)
    hg = heads // SSD_GROUPS
    n_exp = moe_router.shape[-1]
    assert seq % SSD_CHUNK == 0 and LANES % p == 0 and (hg * p) % LANES == 0 and n == LANES
    assert heads <= LANES and n_exp <= LANES and dc % LANES == 0 and dx % LANES == 0 and d % LANES == 0
    assert w_in.shape[-1] == 3 * dc + ds + dx + heads

    w_in_p = jnp.pad(w_in, ((0, 0), (0, 0), (0, LANES - heads))).astype(BF16)
    w_out_b = w_out.astype(BF16)
    ada_w_b = ada_w.astype(BF16)
    ffd_g, ffd_u, ffd_d = ffd_w_gate.astype(BF16), ffd_w_up.astype(BF16), ffd_w_down.astype(BF16)
    moe_g, moe_u, moe_d = moe_w_gate.astype(BF16), moe_w_up.astype(BF16), moe_w_down.astype(BF16)
    router_p = jnp.pad(moe_router, ((0, 0), (0, 0), (0, LANES - n_exp)))
    padh = lambda a: jnp.pad(a, ((0, 0), (0, LANES - heads))).reshape(depth, 1, LANES)
    dtb_p, alog_p = padh(dt_bias), padh(a_log)
    dsk_e = jnp.repeat(d_skip, p, axis=-1).reshape(depth, 1, ds)
    hot = (jnp.arange(LANES)[:, None] == (jnp.arange(ds)[None, :] // p)).astype(BF16)
    emat = jnp.concatenate([hot, hot], axis=0)

    mod = _adaln(jnp.concatenate([c_prompt, c_sample], axis=0), ada_w_b, ada_b)

    tm_p = _row_tile(seq, 512)
    tps = seq // tm_p
    bb = _row_tile(bs, 16)

    def layer_tail(i, cat, xg, modg, tiles_per_seq, tm):
        j = i // 2
        moe = i % 2 == 1
        res = _outproj(cat, w_out_b[i], xg, modg, tiles_per_seq, tm, norm_post_mix[i], norm_pre_ffn[i],
                       router_p[j] if moe else None, n_exp if moe else 0)
        if moe:
            x1, h2, comb = res
            return _ffn_moe(h2, comb, moe_g[j], moe_u[j], moe_d[j], x1, modg, tiles_per_seq, tm, norm_post_ffn[i])
        x1, h2 = res
        return _ffn_dense(h2, ffd_g[j], ffd_u[j], ffd_d[j], x1, modg, tiles_per_seq, tm, norm_post_ffn[i])

    xp = x_prompt.reshape(bp * seq, d)
    pa, pb, ps = [], [], []
    for i in range(depth):
        modg = mod[i, :bp].reshape(bp, 1, 6 * d)
        gb, u, z, xbc, dtr = _inproj(xp, modg, tps, tm_p, norm_pre_mix[i], w_in_p[i], dc, ds, dx)
        cat, na, nb, ns = _ssd_prompt(gb, u, z, xbc, dtr, bp, seq, conva_w[i], convb_w[i], convb_b[i].reshape(1, dx),
                                      dtb_p[i], alog_p[i], dsk_e[i], ssd_norm[i].reshape(1, ds), emat, n, p, hg)
        xp = layer_tail(i, cat, xp, modg, tps, tm_p)
        pa.append(na)
        pb.append(nb)
        ps.append(ns.reshape(bp, heads, p, n))

    xs_ = x_sample.transpose(1, 0, 2).reshape(steps * bs, d)
    sa_l, sb_l, ss_l = [], [], []
    ka, kb = conva_w.shape[1] - 1, convb_w.shape[1] - 1
    for i in range(depth):
        modg = mod[i, bp:]
        gb, u, z, xbc, dtr = _inproj(xs_, modg, None, bs, norm_pre_mix[i], w_in_p[i], dc, ds, dx)
        r3 = lambda a: a.reshape(steps, bs, a.shape[-1])
        cat, na, nb, ns = _ssd_sample(r3(gb), r3(u), r3(z), r3(xbc), r3(dtr),
                                      state_conva[i].reshape(bs, ka * dc), state_convb[i].reshape(bs, kb * dx),
                                      state_ssm[i].reshape(bs, ds, n), conva_w[i], convb_w[i],
                                      convb_b[i].reshape(1, dx), dtb_p[i], alog_p[i], dsk_e[i],
                                      ssd_norm[i].reshape(1, ds), emat, n, p, hg, bb)
        xs_ = layer_tail(i, cat.reshape(steps * bs, dc + ds), xs_, modg, None, bs)
        sa_l.append(na.reshape(bs, ka, dc))
        sb_l.append(nb.reshape(bs, kb, dx))
        ss_l.append(ns.reshape(bs, heads, p, n))

    y_prompt = xp.reshape(bp, seq, d)
    y_sample = xs_.reshape(steps, bs, d).transpose(1, 0, 2)
    return (y_prompt, y_sample, jnp.stack(pa), jnp.stack(pb), jnp.stack(ps),
            jnp.stack(sa_l), jnp.stack(sb_l), jnp.stack(ss_l))
```

```python
import functools

import jax
import jax.numpy as jnp
from jax import lax
from jax.experimental import pallas as pl
from jax.experimental.pallas import tpu as pltpu

EPS = 1e-6
SSD_GROUPS = 2
SSD_CHUNK = 128
TOP_K = 2
LANES = 128
SUBLANES = 8
VMEM_LIMIT_BYTES = 56 * 1024 * 1024
MOE_TILE = 512

F32 = jnp.float32
BF16 = jnp.bfloat16


def _cparams(*sem):
    return pltpu.CompilerParams(dimension_semantics=sem, vmem_limit_bytes=VMEM_LIMIT_BYTES)


def _resident(shape):
    return pl.BlockSpec(shape, lambda *_: (0,) * len(shape), pipeline_mode=pl.Buffered(1))


def _silu(x):
    return x * (1.0 / (1.0 + jnp.exp(-x)))


def _softplus(x):
    return jnp.maximum(x, 0.0) + jnp.log1p(jnp.exp(-jnp.abs(x)))


def _rms(x, g):
    return x * lax.rsqrt(jnp.mean(x * x, axis=-1, keepdims=True) + EPS) * g


def _dot(a, b):
    return jnp.dot(a, b, preferred_element_type=F32)


def _dot_nt(a, b):
    return lax.dot_general(a, b, (((1,), (1,)), ((), ())), preferred_element_type=F32)


def _dot_tn(a, b):
    return lax.dot_general(a, b, (((0,), (0,)), ((), ())), preferred_element_type=F32)


def _chunk(*widths):
    return next(c for c in (512, 256, LANES) if all(w % c == 0 for w in widths))


def _adaln_kernel(c_ref, w_ref, b_ref, o_ref):
    s = _silu(c_ref[...]).astype(BF16)
    o_ref[...] = _dot(s, w_ref[...].astype(BF16)) + b_ref[...]


def _adaln(c_all, ada_w, ada_b):
    depth, d, d6 = ada_w.shape
    rows = c_all.shape[0]
    tn = _chunk(d6 // 6) * 2
    return pl.pallas_call(
        _adaln_kernel,
        grid=(depth, d6 // tn),
        in_specs=[
            pl.BlockSpec((rows, d), lambda l, j: (0, 0)),
            pl.BlockSpec((None, d, tn), lambda l, j: (l, 0, j)),
            pl.BlockSpec((None, 1, tn), lambda l, j: (l, 0, j)),
        ],
        out_specs=pl.BlockSpec((None, rows, tn), lambda l, j: (l, 0, j)),
        out_shape=jax.ShapeDtypeStruct((depth, rows, d6), F32),
        compiler_params=_cparams("arbitrary", "arbitrary"),
        name="adaln",
    )(c_all, ada_w, ada_b.reshape(depth, 1, d6))


class _Rows:
    def __init__(self, tm, nt, tps, reps, d, pro):
        self.tm, self.nt, self.tps, self.reps, self.d, self.pro = tm, nt, tps, reps, d, pro

    def tile(self, i):
        return jnp.maximum(i - self.pro, 0)

    def rows(self, w):
        return pl.BlockSpec((self.tm, w), lambda i, *_: (self.tile(i), 0))

    def prompt_rows(self, w):
        return pl.BlockSpec((self.tm, w), lambda i, *_: (jnp.minimum(self.tile(i), self.nt - 2), 0))

    def sample_rows(self, w):
        return pl.BlockSpec((self.tm, w), lambda i, *_: (0, 0))

    def seq_mod(self, k):
        return pl.BlockSpec((None, 1, self.d), lambda i, *_: (self.tile(i) // self.tps, 0, k))

    def row_mod(self, k, rows):
        return pl.BlockSpec((rows, self.d), lambda i, *_: (0, k))


def _mod(is_sample, seq_ref, row_ref, reps):
    rowm = jnp.concatenate([row_ref[...]] * reps, axis=0)
    return seq_ref[...] + jnp.where(is_sample, rowm, 0.0)


def _inproj_kernel(*refs, geo, dc, ds, dx, cw, split_x):
    if split_x:
        xp_ref, xs_ref, *refs = refs
    else:
        xp_ref, *refs = refs
    (qsh_ref, qsc_ref, rsh_ref, rsc_ref, g_ref, wch_ref, wdt_ref,
     gb_ref, u_ref, z_ref, xbc_ref, dt_ref, wbf) = refs
    i = pl.program_id(0)

    @pl.when(i < geo.pro)
    def _():
        wbf[i] = wch_ref[...].astype(BF16)

    @pl.when(i >= geo.pro)
    def _():
        is_s = i - geo.pro == geo.nt - 1
        x = jnp.where(is_s, xs_ref[...], xp_ref[...]) if split_x else xp_ref[...]
        sc = _mod(is_s, qsc_ref, rsc_ref, geo.reps)
        sh = _mod(is_s, qsh_ref, rsh_ref, geo.reps)
        h = (_rms(x, g_ref[...]) * (1.0 + sc) + sh).astype(BF16)
        nc = dc // cw
        for a in range(nc):
            gb_ref[:, a * cw:(a + 1) * cw] = _dot(h, wbf[a])
            u_ref[:, a * cw:(a + 1) * cw] = _dot(h, wbf[nc + a]) * _dot(h, wbf[2 * nc + a])
        for a in range(ds // cw):
            z_ref[:, a * cw:(a + 1) * cw] = _dot(h, wbf[3 * nc + a])
        for a in range(dx // cw):
            xbc_ref[:, a * cw:(a + 1) * cw] = _dot(h, wbf[3 * nc + ds // cw + a])
        dt_ref[...] = _dot(h, wdt_ref[...])


def _inproj(x_p, x_s, seq_mod, row_mod, geo_args, norm_w, w_in, layer, w_dt, dc, ds, dx):
    tm, nt, tps, reps, d = geo_args
    cw = _chunk(dc, ds, dx)
    npro = (3 * dc + ds + dx) // cw
    geo = _Rows(tm, nt, tps, reps, d, npro)
    t_all = tm * nt
    split_x = x_s is not None
    xin = [geo.prompt_rows(d), geo.sample_rows(d)] if split_x else [geo.rows(d)]
    xargs = [x_p, x_s] if split_x else [x_p]
    nrow = row_mod.shape[0]
    return pl.pallas_call(
        functools.partial(_inproj_kernel, geo=geo, dc=dc, ds=ds, dx=dx, cw=cw, split_x=split_x),
        grid=(npro + nt,),
        in_specs=xin + [
            geo.seq_mod(0), geo.seq_mod(1), geo.row_mod(0, nrow), geo.row_mod(1, nrow),
            pl.BlockSpec((1, d), lambda i: (0, 0)),
            pl.BlockSpec((None, d, cw), lambda i: (layer, 0, jnp.minimum(i, npro - 1))),
            _resident((d, LANES)),
        ],
        out_specs=[geo.rows(dc), geo.rows(dc), geo.rows(ds), geo.rows(dx), geo.rows(LANES)],
        out_shape=[
            jax.ShapeDtypeStruct((t_all, dc), F32),
            jax.ShapeDtypeStruct((t_all, dc), F32),
            jax.ShapeDtypeStruct((t_all, ds), F32),
            jax.ShapeDtypeStruct((t_all, dx), F32),
            jax.ShapeDtypeStruct((t_all, LANES), F32),
        ],
        scratch_shapes=[pltpu.VMEM((npro, d, cw), BF16)],
        compiler_params=_cparams("arbitrary"),
        name="inproj",
    )(*xargs, seq_mod, seq_mod, row_mod, row_mod, norm_w.reshape(1, d), w_in, w_dt)


def _split_hi_lo(v):
    hi = v.astype(BF16)
    lo = (v - hi.astype(F32)).astype(BF16)
    return jnp.concatenate([hi, lo], axis=1)


def _conv_taps(ext_ref, w_ref, rows, base):
    k = w_ref.shape[0]
    acc = w_ref[k - 1:k, :] * ext_ref[base:base + rows, :]
    for j in range(k - 1):
        off = base - (k - 1) + j
        acc = acc + w_ref[j:j + 1, :] * ext_ref[off:off + rows, :]
    return acc


def _diag_block(cb, mask, cum, cum_t, dt_t, xs, lane, g, hg, gw, p):
    hpl = LANES // p
    parts = []
    for slab in range(gw // LANES):
        lhs, rhs = [], []
        xslab = xs[:, g * gw + slab * LANES:g * gw + (slab + 1) * LANES]
        for j in range(hpl):
            h = g * hg + slab * hpl + j
            seg = cum[:, h:h + 1] - cum_t[h:h + 1, :]
            m = cb * jnp.where(mask, jnp.exp(seg), 0.0) * dt_t[h:h + 1, :]
            lhs.append(m.astype(BF16))
            rhs.append(jnp.where((lane >= j * p) & (lane < (j + 1) * p), xslab, 0.0).astype(BF16))
        parts.append(_dot(jnp.concatenate(lhs, axis=1), jnp.concatenate(rhs, axis=0)))
    return jnp.concatenate(parts, axis=1)


def _gated_group_norm(y, xs_g, z_g, dsk_g, snorm_g):
    y = (y + dsk_g * xs_g) * _silu(z_g)
    return (y * lax.rsqrt(jnp.mean(y * y, axis=-1, keepdims=True) + EPS) * snorm_g).astype(BF16)


def _ssd_prompt_kernel(gb_ref, u_ref, z_ref, xbc_ref, dt_ref, caw_ref, cbw_ref, cbb_ref, dtb_ref, alog_ref,
                       dsk_ref, snorm_ref, e_ref, cat_ref, nca_ref, ncb_ref, nss_ref, uext, xext, st,
                       *, dc, ds, n, p, hg):
    c = pl.program_id(1)
    q = SSD_CHUNK
    gw = hg * p

    @pl.when(c == 0)
    def _():
        uext[0:SUBLANES, :] = jnp.zeros((SUBLANES, dc), F32)
        xext[0:SUBLANES, :] = jnp.zeros((SUBLANES, xext.shape[1]), F32)
        st[...] = jnp.zeros(st.shape, F32)

    base = SUBLANES
    uext[base:base + q, :] = u_ref[...]
    xext[base:base + q, :] = xbc_ref[...]

    cat_ref[:, 0:dc] = (gb_ref[...] * _conv_taps(uext, caw_ref, q, base)).astype(BF16)

    xc = _silu(_conv_taps(xext, cbw_ref, q, base) + cbb_ref[...])
    xs = xc[:, 0:ds]

    dt = _softplus(dt_ref[...] + dtb_ref[...])
    da = dt * (-jnp.exp(alog_ref[...]))
    row = lax.broadcasted_iota(jnp.int32, (q, LANES), 0)
    cum = da
    k = 1
    while k < q:
        cum = cum + jnp.where(row >= k, pltpu.roll(cum, k, axis=0), 0.0)
        k *= 2
    cum_last = cum[q - 1:q, :]
    w_exp = _dot(_split_hi_lo(dt * jnp.exp(cum_last - cum)), e_ref[...])
    ecum_exp = _dot(_split_hi_lo(jnp.exp(cum)), e_ref[...])
    cum_t = cum.T
    dt_t = dt.T

    tril = lax.broadcasted_iota(jnp.int32, (q, q), 0) >= lax.broadcasted_iota(jnp.int32, (q, q), 1)
    lane = lax.broadcasted_iota(jnp.int32, (q, LANES), 1)
    for g in range(SSD_GROUPS):
        gsl = slice(g * gw, (g + 1) * gw)
        bm = xc[:, ds + g * n:ds + (g + 1) * n]
        cm = xc[:, ds + SSD_GROUPS * n + g * n:ds + SSD_GROUPS * n + (g + 1) * n].astype(BF16)
        bm_t = bm.T.astype(BF16)
        y = _diag_block(_dot(cm, bm_t), tril, cum, cum_t, dt_t, xs, lane, g, hg, gw, p)
        s_prev = st[g]
        y = y + _dot(cm, s_prev.astype(BF16)) * ecum_exp[:, gsl]
        xw = (xs[:, gsl] * w_exp[:, gsl]).astype(BF16)
        st[g] = s_prev * ecum_exp[q - 1:q, gsl] + _dot(bm_t, xw)
        cat_ref[:, dc + g * gw:dc + (g + 1) * gw] = _gated_group_norm(
            y, xs[:, gsl], z_ref[:, gsl], dsk_ref[:, gsl], snorm_ref[:, gsl])

    uext[0:SUBLANES, :] = uext[q:q + SUBLANES, :]
    xext[0:SUBLANES, :] = xext[q:q + SUBLANES, :]

    @pl.when(c == pl.num_programs(1) - 1)
    def _():
        ka = nca_ref.shape[0]
        kb = ncb_ref.shape[0]
        nca_ref[...] = uext[SUBLANES - ka:SUBLANES, :]
        ncb_ref[...] = xext[SUBLANES - kb:SUBLANES, :]
        for g in range(SSD_GROUPS):
            for slab in range(gw // LANES):
                r0 = g * gw + slab * LANES
                nss_ref[r0:r0 + LANES, :] = st[g, :, slab * LANES:(slab + 1) * LANES].T


def _ssd_prompt(gb, u, z, xbc, dtr, bsz, seq, params, n, p, hg):
    dc = u.shape[1]
    ds = z.shape[1]
    dx = xbc.shape[1]
    q = SSD_CHUNK
    nc = seq // q
    ka, kb = params[0].shape[0] - 1, params[1].shape[0] - 1
    row = lambda w: pl.BlockSpec((q, w), lambda b, c: (b * nc + c, 0))
    return pl.pallas_call(
        functools.partial(_ssd_prompt_kernel, dc=dc, ds=ds, n=n, p=p, hg=hg),
        grid=(bsz, nc),
        in_specs=[row(dc), row(dc), row(ds), row(dx), row(LANES)] + [_resident(a.shape) for a in params],
        out_specs=[
            pl.BlockSpec((q, dc + ds), lambda b, c: (b * nc + c, 0)),
            pl.BlockSpec((None, ka, dc), lambda b, c: (b, 0, 0)),
            pl.BlockSpec((None, kb, dx), lambda b, c: (b, 0, 0)),
            pl.BlockSpec((None, ds, n), lambda b, c: (b, 0, 0)),
        ],
        out_shape=[
            jax.ShapeDtypeStruct((bsz * seq, dc + ds), BF16),
            jax.ShapeDtypeStruct((bsz, ka, dc), F32),
            jax.ShapeDtypeStruct((bsz, kb, dx), F32),
            jax.ShapeDtypeStruct((bsz, ds, n), F32),
        ],
        scratch_shapes=[
            pltpu.VMEM((q + SUBLANES, dc), F32),
            pltpu.VMEM((q + SUBLANES, dx), F32),
            pltpu.VMEM((SSD_GROUPS, n, hg * p), F32),
        ],
        compiler_params=_cparams("arbitrary", "arbitrary"),
        name="ssd_prompt",
    )(gb, u, z, xbc, dtr, *params)


def _ssd_sample_kernel(gb_ref, u_ref, z_ref, xbc_ref, dt_ref, sa_ref, sb_ref, ss_ref, caw_ref, cbw_ref, cbb_ref,
                       dtb_ref, alog_ref, dsk_ref, snorm_ref, e_ref, cat_ref, nca_ref, ncb_ref, nss_ref,
                       *, dc, ds, dx, n, p, hg, steps, bb):
    gw = hg * p
    rows = steps * bb
    ka = caw_ref.shape[0] - 1
    kb = cbw_ref.shape[0] - 1

    def conv(cur_ref, st_ref, w_ref, kprev, width):
        hist = [st_ref[:, j * width:(j + 1) * width] for j in range(kprev)] + [cur_ref[t] for t in range(steps)]
        outs = []
        for t in range(steps):
            acc = w_ref[kprev:kprev + 1, :] * hist[t + kprev]
            for j in range(kprev):
                acc = acc + w_ref[j:j + 1, :] * hist[t + j]
            outs.append(acc)
        return outs, hist[len(hist) - kprev:]

    v, new_a = conv(u_ref, sa_ref, caw_ref, ka, dc)
    for t in range(steps):
        cat_ref[t, :, 0:dc] = (gb_ref[t] * v[t]).astype(BF16)
    for j in range(ka):
        nca_ref[:, j * dc:(j + 1) * dc] = new_a[j]
    xcs, new_b = conv(xbc_ref, sb_ref, cbw_ref, kb, dx)
    for j in range(kb):
        ncb_ref[:, j * dx:(j + 1) * dx] = new_b[j]
    xc = _silu(jnp.concatenate(xcs, axis=0) + cbb_ref[...])
    xs = xc[:, 0:ds]

    dt = _softplus(jnp.concatenate([dt_ref[t] for t in range(steps)], axis=0) + dtb_ref[...])
    da = dt * (-jnp.exp(alog_ref[...]))
    cums = [da[0:bb]]
    for t in range(1, steps):
        cums.append(cums[-1] + da[t * bb:(t + 1) * bb])
    cum = jnp.concatenate(cums, axis=0)
    cum_last = jnp.concatenate([cums[-1]] * steps, axis=0)
    w_exp = _dot(_split_hi_lo(dt * jnp.exp(cum_last - cum)), e_ref[...])
    ecum_exp = _dot(_split_hi_lo(jnp.exp(cum)), e_ref[...])

    def pad_t(a):
        a = jnp.concatenate([a, jnp.zeros((LANES - rows, LANES), F32)], axis=0) if rows < LANES else a
        return a.T[:, 0:rows]

    cum_t = pad_t(cum)
    dt_t = pad_t(dt)

    ri = lax.broadcasted_iota(jnp.int32, (rows, rows), 0)
    ci = lax.broadcasted_iota(jnp.int32, (rows, rows), 1)
    same = ((ri % bb) == (ci % bb)) & (ri >= ci)
    lane = lax.broadcasted_iota(jnp.int32, (rows, LANES), 1)
    rowid = lax.broadcasted_iota(jnp.int32, (rows, 1), 0) % bb
    seqlane = lax.broadcasted_iota(jnp.int32, (LANES, LANES), 1)
    nslab = gw // LANES
    for g in range(SSD_GROUPS):
        gsl = slice(g * gw, (g + 1) * gw)
        bm = xc[:, ds + g * n:ds + (g + 1) * n].astype(BF16)
        cm = xc[:, ds + SSD_GROUPS * n + g * n:ds + SSD_GROUPS * n + (g + 1) * n].astype(BF16)
        y_diag = _diag_block(_dot_nt(cm, bm), same, cum, cum_t, dt_t, xs, lane, g, hg, gw, p)
        xw = (xs[:, gsl] * w_exp[:, gsl]).astype(BF16)
        dec = ecum_exp[(steps - 1) * bb:steps * bb, gsl]
        dec = jnp.concatenate([dec, jnp.zeros((LANES - bb, gw), F32)], axis=0)
        dec_t = [dec[:, s * LANES:(s + 1) * LANES].T for s in range(nslab)]

        def per_seq(b, y_off, g=g, cm=cm, bm=bm, xw=xw, dec_t=dec_t):
            r0 = g * gw
            s0 = ss_ref[b, r0:r0 + gw, :]
            y_off = jnp.where(rowid == b, _dot_nt(cm, s0.astype(BF16)), y_off)
            upd = _dot_tn(jnp.where(rowid == b, xw, jnp.zeros_like(xw)), bm)
            for s in range(nslab):
                dcol = jnp.sum(jnp.where(seqlane == b, dec_t[s], 0.0), axis=1, keepdims=True)
                nss_ref[b, r0 + s * LANES:r0 + (s + 1) * LANES, :] = (
                    s0[s * LANES:(s + 1) * LANES, :] * dcol + upd[s * LANES:(s + 1) * LANES, :])
            return y_off

        y_off = lax.fori_loop(0, bb, per_seq, jnp.zeros((rows, gw), F32))
        zg = jnp.concatenate([z_ref[t, :, gsl] for t in range(steps)], axis=0)
        yn = _gated_group_norm(y_diag + y_off * ecum_exp[:, gsl], xs[:, gsl], zg, dsk_ref[:, gsl], snorm_ref[:, gsl])
        for t in range(steps):
            cat_ref[t, :, dc + g * gw:dc + (g + 1) * gw] = yn[t * bb:(t + 1) * bb]


def _ssd_sample(gb, u, z, xbc, dtr, blk0, steps, bs, sa, sb, ss, params, n, p, hg, bb):
    dc, ds, dx = u.shape[2], z.shape[2], xbc.shape[2]
    ka, kb = params[0].shape[0] - 1, params[1].shape[0] - 1
    tok = lambda w: pl.BlockSpec((steps, bb, w), lambda i: (blk0, i, 0))
    otok = lambda w: pl.BlockSpec((steps, bb, w), lambda i: (0, i, 0))
    return pl.pallas_call(
        functools.partial(_ssd_sample_kernel, dc=dc, ds=ds, dx=dx, n=n, p=p, hg=hg, steps=steps, bb=bb),
        grid=(bs // bb,),
        in_specs=[tok(dc), tok(dc), tok(ds), tok(dx), tok(LANES),
                  pl.BlockSpec((bb, ka * dc), lambda i: (i, 0)),
                  pl.BlockSpec((bb, kb * dx), lambda i: (i, 0)),
                  pl.BlockSpec((bb, ds, n), lambda i: (i, 0, 0))] + [_resident(a.shape) for a in params],
        out_specs=[otok(dc + ds),
                   pl.BlockSpec((bb, ka * dc), lambda i: (i, 0)),
                   pl.BlockSpec((bb, kb * dx), lambda i: (i, 0)),
                   pl.BlockSpec((bb, ds, n), lambda i: (i, 0, 0))],
        out_shape=[jax.ShapeDtypeStruct((steps, bs, dc + ds), BF16),
                   jax.ShapeDtypeStruct((bs, ka * dc), F32),
                   jax.ShapeDtypeStruct((bs, kb * dx), F32),
                   jax.ShapeDtypeStruct((bs, ds, n), F32)],
        compiler_params=_cparams("arbitrary"),
        name="ssd_sample",
    )(gb, u, z, xbc, dtr, sa, sb, ss, *params)


def _route(logits, n_exp):
    lane = lax.broadcasted_iota(jnp.int32, logits.shape, 1).astype(F32)
    valid = lane < n_exp
    logits = jnp.where(valid, logits, -jnp.inf)
    e = jnp.exp(logits - jnp.max(logits, axis=-1, keepdims=True))
    prob = jnp.where(valid, e / jnp.sum(e, axis=-1, keepdims=True), -1.0)
    big = float(LANES)
    m1 = jnp.max(prob, axis=-1, keepdims=True)
    i1 = jnp.min(jnp.where(prob == m1, lane, big), axis=-1, keepdims=True)
    rest = jnp.where(lane == i1, -1.0, prob)
    m2 = jnp.max(rest, axis=-1, keepdims=True)
    i2 = jnp.min(jnp.where(rest == m2, lane, big), axis=-1, keepdims=True)
    den = m1 + m2
    return jnp.where(lane == 0.0, i1, jnp.where(lane == 1.0, i2, jnp.where(lane == 2.0, m1 / den,
                     jnp.where(lane == 3.0, m2 / den, 0.0))))


def _outproj_kernel(catp_ref, cats_ref, wch_ref, *rest, geo, n_exp, kc, split_x):
    if split_x:
        xp_ref, xs_ref, *rest = rest
    else:
        xp_ref, *rest = rest
    qg_ref, qsh_ref, qsc_ref, rg_ref, rsh_ref, rsc_ref, npost_ref, npre_ref, *rest = rest
    if n_exp:
        r_ref, x1_ref, h2_ref, sel_ref, wbf = rest
    else:
        x1_ref, h2_ref, wbf = rest
    i = pl.program_id(0)

    @pl.when(i < geo.pro)
    def _():
        wbf[pl.ds(pl.multiple_of(i * kc, kc), kc), :] = wch_ref[...].astype(BF16)

    @pl.when(i >= geo.pro)
    def _():
        is_s = i - geo.pro == geo.nt - 1
        cat = jnp.where(is_s, cats_ref[...], catp_ref[...])
        mix = _dot(cat, wbf[...])
        x = jnp.where(is_s, xs_ref[...], xp_ref[...]) if split_x else xp_ref[...]
        x1 = x + _mod(is_s, qg_ref, rg_ref, geo.reps) * _rms(mix, npost_ref[...])
        x1_ref[...] = x1
        h2 = (_rms(x1, npre_ref[...]) * (1.0 + _mod(is_s, qsc_ref, rsc_ref, geo.reps))
              + _mod(is_s, qsh_ref, rsh_ref, geo.reps))
        h2_ref[...] = h2.astype(h2_ref.dtype)
        if n_exp:
            h_hi = h2.astype(BF16)
            h_lo = (h2 - h_hi.astype(F32)).astype(BF16)
            r = r_ref[...]
            r_hi = r.astype(BF16)
            r_lo = (r - r_hi.astype(F32)).astype(BF16)
            logits = _dot(h_hi, r_hi) + (_dot(h_lo, r_hi) + _dot(h_hi, r_lo))
            sel_ref[...] = _route(logits, n_exp)


def _outproj(cat_p, cat_s, w_out, layer, x_p, x_s, seq_mod, row_mod, geo_args, npost, npre, router, n_exp):
    tm, nt, tps, reps, d = geo_args
    dm = cat_p.shape[1]
    kc = _chunk(dm) // 2
    npro = dm // kc
    geo = _Rows(tm, nt, tps, reps, d, npro)
    t_all = tm * nt
    nrow = row_mod.shape[0]
    vec = pl.BlockSpec((1, d), lambda i: (0, 0))
    split_x = x_s is not None
    in_specs = [geo.prompt_rows(dm), geo.sample_rows(dm),
                pl.BlockSpec((None, kc, d), lambda i: (layer, jnp.minimum(i, npro - 1), 0))]
    in_specs += [geo.prompt_rows(d), geo.sample_rows(d)] if split_x else [geo.rows(d)]
    in_specs += [geo.seq_mod(2), geo.seq_mod(3), geo.seq_mod(4),
                 geo.row_mod(2, nrow), geo.row_mod(3, nrow), geo.row_mod(4, nrow), vec, vec]
    args = [cat_p, cat_s, w_out] + ([x_p, x_s] if split_x else [x_p])
    args += [seq_mod, seq_mod, seq_mod, row_mod, row_mod, row_mod, npost.reshape(1, d), npre.reshape(1, d)]
    out_specs = [geo.rows(d), geo.rows(d)]
    out_shape = [jax.ShapeDtypeStruct((t_all, d), F32), jax.ShapeDtypeStruct((t_all, d), F32 if n_exp else BF16)]
    if n_exp:
        in_specs.append(_resident((d, LANES)))
        args.append(router)
        out_specs.append(geo.rows(LANES))
        out_shape.append(jax.ShapeDtypeStruct((t_all, LANES), F32))
    return pl.pallas_call(
        functools.partial(_outproj_kernel, geo=geo, n_exp=n_exp, kc=kc, split_x=split_x),
        grid=(npro + nt,),
        in_specs=in_specs,
        out_specs=out_specs,
        out_shape=out_shape,
        scratch_shapes=[pltpu.VMEM((dm, d), BF16)],
        compiler_params=_cparams("arbitrary"),
        name="outproj",
    )(*args)


def _write_split(is_s, val, outp_ref, outs_ref):
    @pl.when(jnp.logical_not(is_s))
    def _():
        outp_ref[...] = val

    @pl.when(is_s)
    def _():
        outs_ref[...] = val


def _ffn_kernel(h_ref, wg_ref, wu_ref, wd_ref, x1_ref, qg_ref, rg_ref, npost_ref, *rest, geo, cf, split_out):
    outs, (wgb, wub, wdb, act) = rest[:-4], rest[-4:]
    i = pl.program_id(0)

    @pl.when(i < geo.pro)
    def _():
        wgb[i] = wg_ref[...].astype(BF16)
        wub[i] = wu_ref[...].astype(BF16)
        wdb[pl.ds(pl.multiple_of(i * cf, cf), cf), :] = wd_ref[...].astype(BF16)

    @pl.when(i >= geo.pro)
    def _():
        is_s = i - geo.pro == geo.nt - 1
        h = h_ref[...]
        for c in range(geo.pro):
            act[:, c * cf:(c + 1) * cf] = (_silu(_dot(h, wgb[c])) * _dot(h, wub[c])).astype(BF16)
        f = _dot(act[...], wdb[...])
        val = x1_ref[...] + _mod(is_s, qg_ref, rg_ref, geo.reps) * _rms(f, npost_ref[...])
        if split_out:
            _write_split(is_s, val, *outs)
        else:
            outs[0][...] = val


def _out_rows(geo, d, split_out):
    t_all = geo.tm * geo.nt
    if split_out:
        return ([geo.prompt_rows(d), geo.sample_rows(d)],
                [jax.ShapeDtypeStruct((t_all - geo.tm, d), F32), jax.ShapeDtypeStruct((geo.tm, d), F32)])
    return [geo.rows(d)], [jax.ShapeDtypeStruct((t_all, d), F32)]


def _ffn_dense(h2, wg, wu, wd, j, x1, seq_mod, row_mod, geo_args, npost, split_out):
    tm, nt, tps, reps, d = geo_args
    f = wg.shape[2]
    cf = 256 if f % 256 == 0 else LANES
    npro = f // cf
    geo = _Rows(tm, nt, tps, reps, d, npro)
    out_specs, out_shape = _out_rows(geo, d, split_out)
    chunk = lambda i: jnp.minimum(i, npro - 1)
    return pl.pallas_call(
        functools.partial(_ffn_kernel, geo=geo, cf=cf, split_out=split_out),
        grid=(npro + nt,),
        in_specs=[geo.rows(d),
                  pl.BlockSpec((None, d, cf), lambda i: (j, 0, chunk(i))),
                  pl.BlockSpec((None, d, cf), lambda i: (j, 0, chunk(i))),
                  pl.BlockSpec((None, cf, d), lambda i: (j, chunk(i), 0)),
                  geo.rows(d), geo.seq_mod(5), geo.row_mod(5, row_mod.shape[0]),
                  pl.BlockSpec((1, d), lambda i: (0, 0))],
        out_specs=out_specs,
        out_shape=out_shape,
        scratch_shapes=[pltpu.VMEM((npro, d, cf), BF16), pltpu.VMEM((npro, d, cf), BF16),
                        pltpu.VMEM((f, d), BF16), pltpu.VMEM((tm, f), BF16)],
        compiler_params=_cparams("arbitrary"),
        name="ffn_dense",
    )(h2, wg, wu, wd, x1, seq_mod, row_mod, npost.reshape(1, d))


def _start_row_gather(idx_ref, src_hbm, dst, sem, count):
    def body(r, carry):
        pltpu.make_async_copy(src_hbm.at[pl.ds(idx_ref[0, r], 1)], dst.at[pl.ds(r, 1)], sem).start()
        return carry
    lax.fori_loop(0, count, body, 0, unroll=8)


def _wait_row_gather(src_hbm, dst, sem):
    pltpu.make_async_copy(src_hbm.at[pl.ds(0, dst.shape[0])], dst, sem).wait()


def _moe_expert_kernel(texp_ref, nvalid_ref, tok0_ref, toknext_ref, h_hbm, wg_ref, wu_ref, wd_ref, o_ref,
                       xbuf, sems, *, tm):
    i = pl.program_id(0)
    nvalid = nvalid_ref[0]
    slot = i % 2

    @pl.when(i == 0)
    def _():
        _start_row_gather(tok0_ref, h_hbm, xbuf.at[0], sems.at[0], tm)

    @pl.when(i + 1 < nvalid)
    def _():
        _start_row_gather(toknext_ref, h_hbm, xbuf.at[1 - slot], sems.at[1 - slot], tm)

    @pl.when(i < nvalid)
    def _():
        _wait_row_gather(h_hbm, xbuf.at[slot], sems.at[slot])
        h = xbuf[slot].astype(BF16)
        act = (_silu(_dot(h, wg_ref[...])) * _dot(h, wu_ref[...])).astype(BF16)
        o_ref[...] = _dot(act, wd_ref[...])

    @pl.when(i >= nvalid)
    def _():
        o_ref[...] = jnp.zeros(o_ref.shape, F32)


def _moe_experts(h2, tile_expert, n_valid, slot_token, wg, wu, wd, tm):
    n_tiles = tile_expert.shape[0]
    d = h2.shape[1]
    n_exp, _, fe = wg.shape
    tok = slot_token.reshape(n_tiles, 1, tm)
    smem_blk = lambda fn: pl.BlockSpec((None, 1, tm), fn, memory_space=pltpu.SMEM)
    grid_spec = pltpu.PrefetchScalarGridSpec(
        num_scalar_prefetch=2,
        grid=(n_tiles,),
        in_specs=[
            smem_blk(lambda i, te, nv: (0, 0, 0)),
            smem_blk(lambda i, te, nv: (jnp.minimum(i + 1, n_tiles - 1), 0, 0)),
            pl.BlockSpec(memory_space=pl.ANY),
            pl.BlockSpec((None, d, fe), lambda i, te, nv: (te[i], 0, 0)),
            pl.BlockSpec((None, d, fe), lambda i, te, nv: (te[i], 0, 0)),
            pl.BlockSpec((None, fe, d), lambda i, te, nv: (te[i], 0, 0)),
        ],
        out_specs=pl.BlockSpec((tm, d), lambda i, te, nv: (i, 0)),
        scratch_shapes=[pltpu.VMEM((2, tm, d), F32), pltpu.SemaphoreType.DMA((2,))],
    )
    return pl.pallas_call(
        functools.partial(_moe_expert_kernel, tm=tm),
        grid_spec=grid_spec,
        out_shape=jax.ShapeDtypeStruct((n_tiles * tm, d), F32),
        compiler_params=_cparams("arbitrary"),
        name="moe_experts",
    )(tile_expert, n_valid, tok, tok, h2, wg, wu, wd)


def _moe_combine_kernel(pos0_ref, posnext_ref, ye_hbm, sel_ref, x1_ref, qg_ref, rg_ref, npost_ref, *rest,
                        geo, split_out):
    outs, (ybuf, sems) = rest[:-2], rest[-2:]
    i = pl.program_id(0)
    tm = geo.tm
    slot = i % 2

    @pl.when(i == 0)
    def _():
        _start_row_gather(pos0_ref, ye_hbm, ybuf.at[0], sems.at[0], 2 * tm)

    @pl.when(i + 1 < geo.nt)
    def _():
        _start_row_gather(posnext_ref, ye_hbm, ybuf.at[1 - slot], sems.at[1 - slot], 2 * tm)

    _wait_row_gather(ye_hbm, ybuf.at[slot], sems.at[slot])
    is_s = i == geo.nt - 1
    sel = sel_ref[...]
    lane = lax.broadcasted_iota(jnp.int32, sel.shape, 1)
    w1 = jnp.sum(jnp.where(lane == 2, sel, 0.0), axis=-1, keepdims=True)
    w2 = jnp.sum(jnp.where(lane == 3, sel, 0.0), axis=-1, keepdims=True)
    f = w1 * ybuf[slot, 0:tm, :] + w2 * ybuf[slot, tm:2 * tm, :]
    val = x1_ref[...] + _mod(is_s, qg_ref, rg_ref, geo.reps) * _rms(f, npost_ref[...])
    if split_out:
        _write_split(is_s, val, *outs)
    else:
        outs[0][...] = val


def _moe_combine(ye, pos, sel, x1, seq_mod, row_mod, geo_args, npost, split_out):
    tm, nt, tps, reps, d = geo_args
    geo = _Rows(tm, nt, tps, reps, d, 0)
    out_specs, out_shape = _out_rows(geo, d, split_out)
    pos3 = pos.reshape(nt, 1, 2 * tm)
    smem_blk = lambda fn: pl.BlockSpec((None, 1, 2 * tm), fn, memory_space=pltpu.SMEM)
    return pl.pallas_call(
        functools.partial(_moe_combine_kernel, geo=geo, split_out=split_out),
        grid=(nt,),
        in_specs=[smem_blk(lambda i: (0, 0, 0)),
                  smem_blk(lambda i: (jnp.minimum(i + 1, nt - 1), 0, 0)),
                  pl.BlockSpec(memory_space=pl.ANY),
                  geo.rows(LANES), geo.rows(d), geo.seq_mod(5), geo.row_mod(5, row_mod.shape[0]),
                  pl.BlockSpec((1, d), lambda i: (0, 0))],
        out_specs=out_specs,
        out_shape=out_shape,
        scratch_shapes=[pltpu.VMEM((2, 2 * tm, d), F32), pltpu.SemaphoreType.DMA((2,))],
        compiler_params=_cparams("arbitrary"),
        name="moe_combine",
    )(pos3, pos3, ye, sel, x1, seq_mod, row_mod, npost.reshape(1, d))


def _moe_tables(sel, n_exp, tm_tok, tm_slot):
    t = sel.shape[0]
    e_flat = jnp.concatenate([sel[:, 0], sel[:, 1]]).astype(jnp.int32)
    a_cnt = TOP_K * t
    onehot = (e_flat[:, None] == jnp.arange(n_exp, dtype=jnp.int32)[None, :]).astype(jnp.int32)
    csum = jnp.cumsum(onehot, axis=0)
    rank = jnp.sum(csum * onehot, axis=1) - 1
    counts = csum[-1]
    tiles_e = (counts + tm_slot - 1) // tm_slot
    tile_end = jnp.cumsum(tiles_e)
    slot_start = (tile_end - tiles_e) * tm_slot
    pos = jnp.sum(onehot * slot_start[None, :], axis=1) + rank
    n_tiles = (a_cnt + n_exp * (tm_slot - 1)) // tm_slot
    tile_expert = jnp.minimum(
        jnp.sum((jnp.arange(n_tiles, dtype=jnp.int32)[:, None] >= tile_end[None, :]).astype(jnp.int32), axis=1),
        n_exp - 1)
    token = jnp.arange(a_cnt, dtype=jnp.int32) % t
    slot_token = jnp.zeros((n_tiles * tm_slot,), jnp.int32).at[pos].set(token)
    pos_tiles = jnp.concatenate([pos[:t].reshape(-1, tm_tok), pos[t:].reshape(-1, tm_tok)], axis=1)
    return tile_expert, tile_end[-1:].astype(jnp.int32), slot_token, pos_tiles


def _largest_tile(t, want):
    tm = min(want, t)
    while t % tm:
        tm //= 2
    return tm


def kernel(x_prompt, x_sample, c_prompt, c_sample, state_conva, state_convb, state_ssm, ada_w, ada_b, norm_pre_mix,
           norm_post_mix, norm_pre_ffn, norm_post_ffn, w_in, w_out, conva_w, convb_w, convb_b, dt_bias, a_log, d_skip,
           ssd_norm, ffd_w_gate, ffd_w_up, ffd_w_down, moe_router, moe_w_gate, moe_w_up, moe_w_down):
    bp, seq, d = x_prompt.shape
    bs, steps, _ = x_sample.shape
    depth = w_in.shape[0]
    dc = conva_w.shape[-1]
    dx = convb_w.shape[-1]
    ds = ssd_norm.shape[-1]
    heads = dt_bias.shape[-1]
    p = ds // heads
    n = (dx - ds) // (2 * SSD_GROUPS)
    hg = heads // SSD_GROUPS
    n_exp = moe_router.shape[-1]
    ka, kb = conva_w.shape[1] - 1, convb_w.shape[1] - 1
    tm = steps * bs
    t_p = bp * seq
    assert seq % SSD_CHUNK == 0 and LANES % p == 0 and (hg * p) % LANES == 0 and n == LANES
    assert heads <= LANES and n_exp <= LANES and dc % LANES == 0 and dx % LANES == 0 and d % LANES == 0
    assert w_in.shape[-1] == 3 * dc + ds + dx + heads and seq % tm == 0 and tm % SUBLANES == 0
    assert bs % SUBLANES == 0 and (t_p // bs) % steps == 0
    nt = t_p // tm + 1
    geo_args = (tm, nt, seq // tm, steps, d)

    w_dt = jnp.pad(w_in[:, :, 3 * dc + ds + dx:], ((0, 0), (0, 0), (0, LANES - heads))).astype(BF16)
    moe_g, moe_u, moe_d = moe_w_gate.astype(BF16), moe_w_up.astype(BF16), moe_w_down.astype(BF16)
    router_p = jnp.pad(moe_router, ((0, 0), (0, 0), (0, LANES - n_exp)))
    padh = lambda a: jnp.pad(a, ((0, 0), (0, LANES - heads))).reshape(depth, 1, LANES)
    dtb_p, alog_p = padh(dt_bias), padh(a_log)
    dsk_e = jnp.repeat(d_skip, p, axis=-1).reshape(depth, 1, ds)
    hot = (jnp.arange(LANES)[:, None] == (jnp.arange(ds)[None, :] // p)).astype(BF16)
    emat = jnp.concatenate([hot, hot], axis=0)

    mod = _adaln(jnp.concatenate([c_prompt, c_sample], axis=0), ada_w, ada_b)
    seq_mod = jnp.pad(mod[:, :bp], ((0, 0), (0, 1), (0, 0))).reshape(depth, bp + 1, 1, 6 * d)
    row_mod = mod[:, bp:]

    bb = _largest_tile(bs, 16)
    x_all = None
    xs_tm = x_sample.transpose(1, 0, 2).reshape(tm, d)
    xp2d = x_prompt.reshape(t_p, d)
    pa, pb, ps, sa_l, sb_l, ss_l = [], [], [], [], [], []
    for i in range(depth):
        j = i // 2
        last = i == depth - 1
        if x_all is None:
            proj = _inproj(xp2d, xs_tm, seq_mod[i], row_mod[i], geo_args, norm_pre_mix[i], w_in, i, w_dt[i], dc, ds, dx)
        else:
            proj = _inproj(x_all, None, seq_mod[i], row_mod[i], geo_args, norm_pre_mix[i], w_in, i, w_dt[i], dc, ds, dx)
        params = [conva_w[i], convb_w[i], convb_b[i].reshape(1, dx), dtb_p[i], alog_p[i], dsk_e[i],
                  ssd_norm[i].reshape(1, ds), emat]
        cat_p, na, nb, ns = _ssd_prompt(*proj, bp, seq, params, n, p, hg)
        r3 = lambda a: a.reshape(a.shape[0] // bs, bs, a.shape[-1])
        cat_s, sna, snb, sns = _ssd_sample(*[r3(a) for a in proj], t_p // (bs * steps), steps, bs,
                                           state_conva[i].reshape(bs, ka * dc), state_convb[i].reshape(bs, kb * dx),
                                           state_ssm[i].reshape(bs, ds, n), params, n, p, hg, bb)
        pa.append(na)
        pb.append(nb)
        ps.append(ns.reshape(bp, heads, p, n))
        sa_l.append(sna.reshape(bs, ka, dc))
        sb_l.append(snb.reshape(bs, kb, dx))
        ss_l.append(sns.reshape(bs, heads, p, n))

        moe = i % 2 == 1
        x_in = (xp2d, xs_tm) if x_all is None else (x_all, None)
        res = _outproj(cat_p, cat_s.reshape(tm, dc + ds), w_out, i, *x_in, seq_mod[i], row_mod[i], geo_args,
                       norm_post_mix[i], norm_pre_ffn[i], router_p[j] if moe else None, n_exp if moe else 0)
        if moe:
            x1, h2, sel = res
            tile_expert, n_valid, slot_token, pos_tiles = _moe_tables(sel, n_exp, tm, MOE_TILE)
            ye = _moe_experts(h2, tile_expert, n_valid, slot_token, moe_g[j], moe_u[j], moe_d[j], MOE_TILE)
            out = _moe_combine(ye, pos_tiles, sel, x1, seq_mod[i], row_mod[i], geo_args, norm_post_ffn[i], last)
        else:
            x1, h2 = res
            out = _ffn_dense(h2, ffd_w_gate, ffd_w_up, ffd_w_down, j, x1, seq_mod[i], row_mod[i], geo_args,
                             norm_post_ffn[i], last)
        if last:
            y_p, y_s = out
        else:
            x_all = out[0]

    y_prompt = y_p.reshape(bp, seq, d)
    y_sample = y_s.reshape(steps, bs, d).transpose(1, 0, 2)
    return (y_prompt, y_sample, jnp.stack(pa), jnp.stack(pb), jnp.stack(ps),
            jnp.stack(sa_l), jnp.stack(sb_l), jnp.stack(ss_l))
```

```python
import functools

import jax
import jax.numpy as jnp
from jax import lax
from jax.experimental import pallas as pl
from jax.experimental.pallas import tpu as pltpu

EPS = 1e-6
SSD_GROUPS = 2
SSD_CHUNK = 128
TOP_K = 2
LANES = 128
SUBLANES = 8
VMEM_LIMIT_BYTES = 56 * 1024 * 1024
MOE_TILE = 512

F32 = jnp.float32
BF16 = jnp.bfloat16


def _cparams(*sem):
    return pltpu.CompilerParams(dimension_semantics=sem, vmem_limit_bytes=VMEM_LIMIT_BYTES)


def _resident(shape):
    return pl.BlockSpec(shape, lambda *_: (0,) * len(shape), pipeline_mode=pl.Buffered(1))


def _silu(x):
    return x * (1.0 / (1.0 + jnp.exp(-x)))


def _softplus(x):
    return jnp.maximum(x, 0.0) + jnp.log1p(jnp.exp(-jnp.abs(x)))


def _rms(x, g):
    return x * lax.rsqrt(jnp.mean(x * x, axis=-1, keepdims=True) + EPS) * g


def _dot(a, b):
    return jnp.dot(a, b, preferred_element_type=F32)


def _dot_nt(a, b):
    return lax.dot_general(a, b, (((1,), (1,)), ((), ())), preferred_element_type=F32)


def _dot_tn(a, b):
    return lax.dot_general(a, b, (((0,), (0,)), ((), ())), preferred_element_type=F32)


def _chunk(*widths):
    return next(c for c in (512, 256, LANES) if all(w % c == 0 for w in widths))


def _adaln_kernel(c_ref, w_ref, b_ref, o_ref):
    s = _silu(c_ref[...]).astype(BF16)
    o_ref[...] = _dot(s, w_ref[...].astype(BF16)) + b_ref[...]


def _adaln(c_all, ada_w, ada_b):
    depth, d, d6 = ada_w.shape
    rows = c_all.shape[0]
    tn = _chunk(d6 // 6) * 2
    return pl.pallas_call(
        _adaln_kernel,
        grid=(depth, d6 // tn),
        in_specs=[
            pl.BlockSpec((rows, d), lambda l, j: (0, 0)),
            pl.BlockSpec((None, d, tn), lambda l, j: (l, 0, j)),
            pl.BlockSpec((None, 1, tn), lambda l, j: (l, 0, j)),
        ],
        out_specs=pl.BlockSpec((None, rows, tn), lambda l, j: (l, 0, j)),
        out_shape=jax.ShapeDtypeStruct((depth, rows, d6), F32),
        compiler_params=_cparams("arbitrary", "arbitrary"),
        name="adaln",
    )(c_all, ada_w, ada_b.reshape(depth, 1, d6))


class _Rows:
    def __init__(self, tm, nt, tps, reps, d, pro):
        self.tm, self.nt, self.tps, self.reps, self.d, self.pro = tm, nt, tps, reps, d, pro

    def tile(self, i):
        return jnp.maximum(i - self.pro, 0)

    def rows(self, w):
        return pl.BlockSpec((self.tm, w), lambda i, *_: (self.tile(i), 0))

    def prompt_rows(self, w):
        return pl.BlockSpec((self.tm, w), lambda i, *_: (jnp.minimum(self.tile(i), self.nt - 2), 0))

    def sample_rows(self, w):
        return pl.BlockSpec((self.tm, w), lambda i, *_: (0, 0))

    def seq_mod(self, k):
        return pl.BlockSpec((None, 1, self.d), lambda i, *_: (self.tile(i) // self.tps, 0, k))

    def row_mod(self, k, rows):
        return pl.BlockSpec((rows, self.d), lambda i, *_: (0, k))


def _mod(is_sample, seq_ref, row_ref, reps):
    rowm = jnp.concatenate([row_ref[...]] * reps, axis=0)
    return seq_ref[...] + jnp.where(is_sample, rowm, 0.0)


def _inproj_kernel(*refs, geo, dc, ds, dx, cw, split_x):
    if split_x:
        xp_ref, xs_ref, *refs = refs
    else:
        xp_ref, *refs = refs
    (qsh_ref, qsc_ref, rsh_ref, rsc_ref, g_ref, wch_ref, wdt_ref,
     gb_ref, u_ref, z_ref, xbc_ref, dt_ref, wbf) = refs
    i = pl.program_id(0)

    @pl.when(i < geo.pro)
    def _():
        wbf[i] = wch_ref[...].astype(BF16)

    @pl.when(i >= geo.pro)
    def _():
        is_s = i - geo.pro == geo.nt - 1
        x = jnp.where(is_s, xs_ref[...], xp_ref[...]) if split_x else xp_ref[...]
        sc = _mod(is_s, qsc_ref, rsc_ref, geo.reps)
        sh = _mod(is_s, qsh_ref, rsh_ref, geo.reps)
        h = (_rms(x, g_ref[...]) * (1.0 + sc) + sh).astype(BF16)
        nc = dc // cw
        for a in range(nc):
            gb_ref[:, a * cw:(a + 1) * cw] = _dot(h, wbf[a])
            u_ref[:, a * cw:(a + 1) * cw] = _dot(h, wbf[nc + a]) * _dot(h, wbf[2 * nc + a])
        for a in range(ds // cw):
            z_ref[:, a * cw:(a + 1) * cw] = _dot(h, wbf[3 * nc + a])
        for a in range(dx // cw):
            xbc_ref[:, a * cw:(a + 1) * cw] = _dot(h, wbf[3 * nc + ds // cw + a])
        dt_ref[...] = _dot(h, wdt_ref[...])


def _inproj(x_p, x_s, seq_mod, row_mod, geo_args, norm_w, w_in, layer, w_dt, dc, ds, dx):
    tm, nt, tps, reps, d = geo_args
    cw = _chunk(dc, ds, dx)
    npro = (3 * dc + ds + dx) // cw
    geo = _Rows(tm, nt, tps, reps, d, npro)
    t_all = tm * nt
    split_x = x_s is not None
    xin = [geo.prompt_rows(d), geo.sample_rows(d)] if split_x else [geo.rows(d)]
    xargs = [x_p, x_s] if split_x else [x_p]
    nrow = row_mod.shape[0]
    return pl.pallas_call(
        functools.partial(_inproj_kernel, geo=geo, dc=dc, ds=ds, dx=dx, cw=cw, split_x=split_x),
        grid=(npro + nt,),
        in_specs=xin + [
            geo.seq_mod(0), geo.seq_mod(1), geo.row_mod(0, nrow), geo.row_mod(1, nrow),
            pl.BlockSpec((1, d), lambda i: (0, 0)),
            pl.BlockSpec((None, d, cw), lambda i: (layer, 0, jnp.minimum(i, npro - 1))),
            _resident((d, LANES)),
        ],
        out_specs=[geo.rows(dc), geo.rows(dc), geo.rows(ds), geo.rows(dx), geo.rows(LANES)],
        out_shape=[
            jax.ShapeDtypeStruct((t_all, dc), F32),
            jax.ShapeDtypeStruct((t_all, dc), F32),
            jax.ShapeDtypeStruct((t_all, ds), F32),
            jax.ShapeDtypeStruct((t_all, dx), F32),
            jax.ShapeDtypeStruct((t_all, LANES), F32),
        ],
        scratch_shapes=[pltpu.VMEM((npro, d, cw), BF16)],
        compiler_params=_cparams("arbitrary"),
        name="inproj",
    )(*xargs, seq_mod, seq_mod, row_mod, row_mod, norm_w.reshape(1, d), w_in, w_dt)


def _split_hi_lo(v):
    hi = v.astype(BF16)
    lo = (v - hi.astype(F32)).astype(BF16)
    return jnp.concatenate([hi, lo], axis=1)


def _conv_taps(ext_ref, w_ref, rows, base):
    k = w_ref.shape[0]
    acc = w_ref[k - 1:k, :] * ext_ref[base:base + rows, :]
    for j in range(k - 1):
        off = base - (k - 1) + j
        acc = acc + w_ref[j:j + 1, :] * ext_ref[off:off + rows, :]
    return acc


def _diag_block(cb, mask, cum, cum_t, dt_t, xs, lane, g, hg, gw, p):
    hpl = LANES // p
    parts = []
    for slab in range(gw // LANES):
        lhs, rhs = [], []
        xslab = xs[:, g * gw + slab * LANES:g * gw + (slab + 1) * LANES]
        for j in range(hpl):
            h = g * hg + slab * hpl + j
            seg = cum[:, h:h + 1] - cum_t[h:h + 1, :]
            m = cb * jnp.where(mask, jnp.exp(seg), 0.0) * dt_t[h:h + 1, :]
            lhs.append(m.astype(BF16))
            rhs.append(jnp.where((lane >= j * p) & (lane < (j + 1) * p), xslab, 0.0).astype(BF16))
        parts.append(_dot(jnp.concatenate(lhs, axis=1), jnp.concatenate(rhs, axis=0)))
    return jnp.concatenate(parts, axis=1)


def _gated_group_norm(y, xs_g, z_g, dsk_g, snorm_g):
    y = (y + dsk_g * xs_g) * _silu(z_g)
    return (y * lax.rsqrt(jnp.mean(y * y, axis=-1, keepdims=True) + EPS) * snorm_g).astype(BF16)


def _ssd_prompt_kernel(gb_ref, u_ref, z_ref, xbc_ref, dt_ref, caw_ref, cbw_ref, cbb_ref, dtb_ref, alog_ref,
                       dsk_ref, snorm_ref, e_ref, cat_ref, nca_ref, ncb_ref, nss_ref, uext, xext, st,
                       *, dc, ds, n, p, hg):
    c = pl.program_id(1)
    q = SSD_CHUNK
    gw = hg * p

    @pl.when(c == 0)
    def _():
        uext[0:SUBLANES, :] = jnp.zeros((SUBLANES, dc), F32)
        xext[0:SUBLANES, :] = jnp.zeros((SUBLANES, xext.shape[1]), F32)
        st[...] = jnp.zeros(st.shape, F32)

    base = SUBLANES
    uext[base:base + q, :] = u_ref[...]
    xext[base:base + q, :] = xbc_ref[...]

    cat_ref[:, 0:dc] = (gb_ref[...] * _conv_taps(uext, caw_ref, q, base)).astype(BF16)

    xc = _silu(_conv_taps(xext, cbw_ref, q, base) + cbb_ref[...])
    xs = xc[:, 0:ds]

    dt = _softplus(dt_ref[...] + dtb_ref[...])
    da = dt * (-jnp.exp(alog_ref[...]))
    row = lax.broadcasted_iota(jnp.int32, (q, LANES), 0)
    cum = da
    k = 1
    while k < q:
        cum = cum + jnp.where(row >= k, pltpu.roll(cum, k, axis=0), 0.0)
        k *= 2
    cum_last = cum[q - 1:q, :]
    w_exp = _dot(_split_hi_lo(dt * jnp.exp(cum_last - cum)), e_ref[...])
    ecum_exp = _dot(_split_hi_lo(jnp.exp(cum)), e_ref[...])
    cum_t = cum.T
    dt_t = dt.T

    tril = lax.broadcasted_iota(jnp.int32, (q, q), 0) >= lax.broadcasted_iota(jnp.int32, (q, q), 1)
    lane = lax.broadcasted_iota(jnp.int32, (q, LANES), 1)
    for g in range(SSD_GROUPS):
        gsl = slice(g * gw, (g + 1) * gw)
        bm = xc[:, ds + g * n:ds + (g + 1) * n]
        cm = xc[:, ds + SSD_GROUPS * n + g * n:ds + SSD_GROUPS * n + (g + 1) * n].astype(BF16)
        bm_t = bm.T.astype(BF16)
        y = _diag_block(_dot(cm, bm_t), tril, cum, cum_t, dt_t, xs, lane, g, hg, gw, p)
        s_prev = st[g]
        y = y + _dot(cm, s_prev.astype(BF16)) * ecum_exp[:, gsl]
        xw = (xs[:, gsl] * w_exp[:, gsl]).astype(BF16)
        st[g] = s_prev * ecum_exp[q - 1:q, gsl] + _dot(bm_t, xw)
        cat_ref[:, dc + g * gw:dc + (g + 1) * gw] = _gated_group_norm(
            y, xs[:, gsl], z_ref[:, gsl], dsk_ref[:, gsl], snorm_ref[:, gsl])

    uext[0:SUBLANES, :] = uext[q:q + SUBLANES, :]
    xext[0:SUBLANES, :] = xext[q:q + SUBLANES, :]

    @pl.when(c == pl.num_programs(1) - 1)
    def _():
        ka = nca_ref.shape[0]
        kb = ncb_ref.shape[0]
        nca_ref[...] = uext[SUBLANES - ka:SUBLANES, :]
        ncb_ref[...] = xext[SUBLANES - kb:SUBLANES, :]
        for g in range(SSD_GROUPS):
            for slab in range(gw // LANES):
                r0 = g * gw + slab * LANES
                nss_ref[r0:r0 + LANES, :] = st[g, :, slab * LANES:(slab + 1) * LANES].T


def _ssd_prompt(gb, u, z, xbc, dtr, bsz, seq, params, n, p, hg):
    dc = u.shape[1]
    ds = z.shape[1]
    dx = xbc.shape[1]
    q = SSD_CHUNK
    nc = seq // q
    ka, kb = params[0].shape[0] - 1, params[1].shape[0] - 1
    row = lambda w: pl.BlockSpec((q, w), lambda b, c: (b * nc + c, 0))
    return pl.pallas_call(
        functools.partial(_ssd_prompt_kernel, dc=dc, ds=ds, n=n, p=p, hg=hg),
        grid=(bsz, nc),
        in_specs=[row(dc), row(dc), row(ds), row(dx), row(LANES)] + [_resident(a.shape) for a in params],
        out_specs=[
            pl.BlockSpec((q, dc + ds), lambda b, c: (b * nc + c, 0)),
            pl.BlockSpec((None, ka, dc), lambda b, c: (b, 0, 0)),
            pl.BlockSpec((None, kb, dx), lambda b, c: (b, 0, 0)),
            pl.BlockSpec((None, ds, n), lambda b, c: (b, 0, 0)),
        ],
        out_shape=[
            jax.ShapeDtypeStruct((bsz * seq, dc + ds), BF16),
            jax.ShapeDtypeStruct((bsz, ka, dc), F32),
            jax.ShapeDtypeStruct((bsz, kb, dx), F32),
            jax.ShapeDtypeStruct((bsz, ds, n), F32),
        ],
        scratch_shapes=[
            pltpu.VMEM((q + SUBLANES, dc), F32),
            pltpu.VMEM((q + SUBLANES, dx), F32),
            pltpu.VMEM((SSD_GROUPS, n, hg * p), F32),
        ],
        compiler_params=_cparams("arbitrary", "arbitrary"),
        name="ssd_prompt",
    )(gb, u, z, xbc, dtr, *params)


def _ssd_sample_kernel(gb_ref, u_ref, z_ref, xbc_ref, dt_ref, sa_ref, sb_ref, ss_ref, caw_ref, cbw_ref, cbb_ref,
                       dtb_ref, alog_ref, dsk_ref, snorm_ref, e_ref, cat_ref, nca_ref, ncb_ref, nss_ref,
                       *, dc, ds, dx, n, p, hg, steps, bb):
    gw = hg * p
    rows = steps * bb
    ka = caw_ref.shape[0] - 1
    kb = cbw_ref.shape[0] - 1

    def conv(cur_ref, st_ref, w_ref, kprev, width):
        hist = [st_ref[:, j * width:(j + 1) * width] for j in range(kprev)] + [cur_ref[t] for t in range(steps)]
        outs = []
        for t in range(steps):
            acc = w_ref[kprev:kprev + 1, :] * hist[t + kprev]
            for j in range(kprev):
                acc = acc + w_ref[j:j + 1, :] * hist[t + j]
            outs.append(acc)
        return outs, hist[len(hist) - kprev:]

    v, new_a = conv(u_ref, sa_ref, caw_ref, ka, dc)
    for t in range(steps):
        cat_ref[t, :, 0:dc] = (gb_ref[t] * v[t]).astype(BF16)
    for j in range(ka):
        nca_ref[:, j * dc:(j + 1) * dc] = new_a[j]
    xcs, new_b = conv(xbc_ref, sb_ref, cbw_ref, kb, dx)
    for j in range(kb):
        ncb_ref[:, j * dx:(j + 1) * dx] = new_b[j]
    xc = _silu(jnp.concatenate(xcs, axis=0) + cbb_ref[...])
    xs = xc[:, 0:ds]

    dt = _softplus(jnp.concatenate([dt_ref[t] for t in range(steps)], axis=0) + dtb_ref[...])
    da = dt * (-jnp.exp(alog_ref[...]))
    cums = [da[0:bb]]
    for t in range(1, steps):
        cums.append(cums[-1] + da[t * bb:(t + 1) * bb])
    cum = jnp.concatenate(cums, axis=0)
    cum_last = jnp.concatenate([cums[-1]] * steps, axis=0)
    w_exp = _dot(_split_hi_lo(dt * jnp.exp(cum_last - cum)), e_ref[...])
    ecum_exp = _dot(_split_hi_lo(jnp.exp(cum)), e_ref[...])

    def pad_t(a):
        a = jnp.concatenate([a, jnp.zeros((LANES - rows, LANES), F32)], axis=0) if rows < LANES else a
        return a.T[:, 0:rows]

    cum_t = pad_t(cum)
    dt_t = pad_t(dt)

    ri = lax.broadcasted_iota(jnp.int32, (rows, rows), 0)
    ci = lax.broadcasted_iota(jnp.int32, (rows, rows), 1)
    same = ((ri % bb) == (ci % bb)) & (ri >= ci)
    lane = lax.broadcasted_iota(jnp.int32, (rows, LANES), 1)
    rowid = lax.broadcasted_iota(jnp.int32, (rows, 1), 0) % bb
    seqlane = lax.broadcasted_iota(jnp.int32, (LANES, LANES), 1)
    nslab = gw // LANES
    for g in range(SSD_GROUPS):
        gsl = slice(g * gw, (g + 1) * gw)
        bm = xc[:, ds + g * n:ds + (g + 1) * n].astype(BF16)
        cm = xc[:, ds + SSD_GROUPS * n + g * n:ds + SSD_GROUPS * n + (g + 1) * n].astype(BF16)
        y_diag = _diag_block(_dot_nt(cm, bm), same, cum, cum_t, dt_t, xs, lane, g, hg, gw, p)
        xw = (xs[:, gsl] * w_exp[:, gsl]).astype(BF16)
        dec = ecum_exp[(steps - 1) * bb:steps * bb, gsl]
        dec = jnp.concatenate([dec, jnp.zeros((LANES - bb, gw), F32)], axis=0)
        dec_t = [dec[:, s * LANES:(s + 1) * LANES].T for s in range(nslab)]

        def per_seq(b, y_off, g=g, cm=cm, bm=bm, xw=xw, dec_t=dec_t):
            r0 = g * gw
            s0 = ss_ref[b, r0:r0 + gw, :]
            y_off = jnp.where(rowid == b, _dot_nt(cm, s0.astype(BF16)), y_off)
            upd = _dot_tn(jnp.where(rowid == b, xw, jnp.zeros_like(xw)), bm)
            for s in range(nslab):
                dcol = jnp.sum(jnp.where(seqlane == b, dec_t[s], 0.0), axis=1, keepdims=True)
                nss_ref[b, r0 + s * LANES:r0 + (s + 1) * LANES, :] = (
                    s0[s * LANES:(s + 1) * LANES, :] * dcol + upd[s * LANES:(s + 1) * LANES, :])
            return y_off

        y_off = lax.fori_loop(0, bb, per_seq, jnp.zeros((rows, gw), F32))
        zg = jnp.concatenate([z_ref[t, :, gsl] for t in range(steps)], axis=0)
        yn = _gated_group_norm(y_diag + y_off * ecum_exp[:, gsl], xs[:, gsl], zg, dsk_ref[:, gsl], snorm_ref[:, gsl])
        for t in range(steps):
            cat_ref[t, :, dc + g * gw:dc + (g + 1) * gw] = yn[t * bb:(t + 1) * bb]


def _ssd_sample_aliased_kernel(*refs, **kw):
    _ssd_sample_kernel(*refs[1:], **kw)


def _ssd_sample(gb, u, z, xbc, dtr, blk0, steps, bs, layer, sa, sb, ss, prev_ss, params, n, p, hg, bb):
    dc, ds, dx = u.shape[2], z.shape[2], xbc.shape[2]
    depth = ss.shape[0]
    ka, kb = params[0].shape[0] - 1, params[1].shape[0] - 1
    tok = lambda w: pl.BlockSpec((steps, bb, w), lambda i: (blk0, i, 0))
    otok = lambda w: pl.BlockSpec((steps, bb, w), lambda i: (0, i, 0))
    in_specs = [tok(dc), tok(dc), tok(ds), tok(dx), tok(LANES),
                pl.BlockSpec((None, bb, ka * dc), lambda i: (layer, i, 0)),
                pl.BlockSpec((None, bb, kb * dx), lambda i: (layer, i, 0)),
                pl.BlockSpec((None, bb, ds, n), lambda i: (layer, i, 0, 0))] + [_resident(a.shape) for a in params]
    args = [gb, u, z, xbc, dtr, sa, sb, ss, *params]
    kw = dict(dc=dc, ds=ds, dx=dx, n=n, p=p, hg=hg, steps=steps, bb=bb)
    body, aliases = functools.partial(_ssd_sample_kernel, **kw), {}
    if prev_ss is not None:
        body, aliases = functools.partial(_ssd_sample_aliased_kernel, **kw), {0: 3}
        in_specs = [pl.BlockSpec(memory_space=pl.ANY)] + in_specs
        args = [prev_ss] + args
    return pl.pallas_call(
        body,
        grid=(bs // bb,),
        in_specs=in_specs,
        out_specs=[otok(dc + ds),
                   pl.BlockSpec((bb, ka * dc), lambda i: (i, 0)),
                   pl.BlockSpec((bb, kb * dx), lambda i: (i, 0)),
                   pl.BlockSpec((None, bb, ds, n), lambda i: (layer, i, 0, 0))],
        out_shape=[jax.ShapeDtypeStruct((steps, bs, dc + ds), BF16),
                   jax.ShapeDtypeStruct((bs, ka * dc), F32),
                   jax.ShapeDtypeStruct((bs, kb * dx), F32),
                   jax.ShapeDtypeStruct((depth, bs, ds, n), F32)],
        input_output_aliases=aliases,
        compiler_params=_cparams("arbitrary"),
        name="ssd_sample",
    )(*args)


def _route(logits, n_exp, cnt_ref):
    rows = logits.shape[0]
    lane = lax.broadcasted_iota(jnp.int32, logits.shape, 1).astype(F32)
    valid = lane < n_exp
    logits = jnp.where(valid, logits, -jnp.inf)
    e = jnp.exp(logits - jnp.max(logits, axis=-1, keepdims=True))
    prob = jnp.where(valid, e / jnp.sum(e, axis=-1, keepdims=True), -1.0)
    big = float(LANES)
    m1 = jnp.max(prob, axis=-1, keepdims=True)
    i1 = jnp.min(jnp.where(prob == m1, lane, big), axis=-1, keepdims=True)
    rest = jnp.where(lane == i1, -1.0, prob)
    m2 = jnp.max(rest, axis=-1, keepdims=True)
    i2 = jnp.min(jnp.where(rest == m2, lane, big), axis=-1, keepdims=True)
    den = m1 + m2
    chosen = jnp.where((lane == i1) | (lane == i2), 1.0, 0.0)
    before = (lax.broadcasted_iota(jnp.int32, (rows, rows), 0) > lax.broadcasted_iota(jnp.int32, (rows, rows), 1))
    rank = _dot(jnp.where(before, 1.0, 0.0).astype(BF16), chosen.astype(BF16)) + cnt_ref[0:1, :]
    r1 = jnp.sum(jnp.where(lane == i1, rank, 0.0), axis=-1, keepdims=True)
    r2 = jnp.sum(jnp.where(lane == i2, rank, 0.0), axis=-1, keepdims=True)
    cnt_ref[0:1, :] = cnt_ref[0:1, :] + jnp.sum(chosen, axis=0, keepdims=True)
    cols = [i1, i2, m1 / den, m2 / den, r1, r2]
    out = jnp.zeros(logits.shape, F32)
    for k, col in enumerate(cols):
        out = jnp.where(lane == float(k), col, out)
    return out


def _outproj_kernel(catp_ref, cats_ref, wch_ref, *rest, geo, n_exp, kc, split_x):
    if split_x:
        xp_ref, xs_ref, *rest = rest
    else:
        xp_ref, *rest = rest
    qg_ref, qsh_ref, qsc_ref, rg_ref, rsh_ref, rsc_ref, npost_ref, npre_ref, *rest = rest
    if n_exp:
        r_ref, x1_ref, h2_ref, sel_ref, cnt_ref, wbf = rest
    else:
        x1_ref, h2_ref, wbf = rest
    i = pl.program_id(0)

    @pl.when(i < geo.pro)
    def _():
        wbf[pl.ds(pl.multiple_of(i * kc, kc), kc), :] = wch_ref[...].astype(BF16)
        if n_exp:
            cnt_ref[...] = jnp.zeros(cnt_ref.shape, F32)

    @pl.when(i >= geo.pro)
    def _():
        is_s = i - geo.pro == geo.nt - 1
        cat = jnp.where(is_s, cats_ref[...], catp_ref[...])
        mix = _dot(cat, wbf[...])
        x = jnp.where(is_s, xs_ref[...], xp_ref[...]) if split_x else xp_ref[...]
        x1 = x + _mod(is_s, qg_ref, rg_ref, geo.reps) * _rms(mix, npost_ref[...])
        x1_ref[...] = x1
        h2 = (_rms(x1, npre_ref[...]) * (1.0 + _mod(is_s, qsc_ref, rsc_ref, geo.reps))
              + _mod(is_s, qsh_ref, rsh_ref, geo.reps))
        h2_ref[...] = h2.astype(h2_ref.dtype)
        if n_exp:
            h_hi = h2.astype(BF16)
            h_lo = (h2 - h_hi.astype(F32)).astype(BF16)
            r = r_ref[...]
            r_hi = r.astype(BF16)
            r_lo = (r - r_hi.astype(F32)).astype(BF16)
            logits = _dot(h_hi, r_hi) + (_dot(h_lo, r_hi) + _dot(h_hi, r_lo))
            sel_ref[...] = _route(logits, n_exp, cnt_ref)


def _outproj(cat_p, cat_s, w_out, layer, x_p, x_s, seq_mod, row_mod, geo_args, npost, npre, router, n_exp):
    tm, nt, tps, reps, d = geo_args
    dm = cat_p.shape[1]
    kc = _chunk(dm) // 2
    npro = dm // kc
    geo = _Rows(tm, nt, tps, reps, d, npro)
    t_all = tm * nt
    nrow = row_mod.shape[0]
    vec = pl.BlockSpec((1, d), lambda i: (0, 0))
    split_x = x_s is not None
    in_specs = [geo.prompt_rows(dm), geo.sample_rows(dm),
                pl.BlockSpec((None, kc, d), lambda i: (layer, jnp.minimum(i, npro - 1), 0))]
    in_specs += [geo.prompt_rows(d), geo.sample_rows(d)] if split_x else [geo.rows(d)]
    in_specs += [geo.seq_mod(2), geo.seq_mod(3), geo.seq_mod(4),
                 geo.row_mod(2, nrow), geo.row_mod(3, nrow), geo.row_mod(4, nrow), vec, vec]
    args = [cat_p, cat_s, w_out] + ([x_p, x_s] if split_x else [x_p])
    args += [seq_mod, seq_mod, seq_mod, row_mod, row_mod, row_mod, npost.reshape(1, d), npre.reshape(1, d)]
    out_specs = [geo.rows(d), geo.rows(d)]
    out_shape = [jax.ShapeDtypeStruct((t_all, d), F32), jax.ShapeDtypeStruct((t_all, d), F32 if n_exp else BF16)]
    if n_exp:
        in_specs.append(_resident((d, LANES)))
        args.append(router)
        out_specs += [geo.rows(LANES), pl.BlockSpec((SUBLANES, LANES), lambda i: (0, 0))]
        out_shape += [jax.ShapeDtypeStruct((t_all, LANES), F32), jax.ShapeDtypeStruct((SUBLANES, LANES), F32)]
    return pl.pallas_call(
        functools.partial(_outproj_kernel, geo=geo, n_exp=n_exp, kc=kc, split_x=split_x),
        grid=(npro + nt,),
        in_specs=in_specs,
        out_specs=out_specs,
        out_shape=out_shape,
        scratch_shapes=[pltpu.VMEM((dm, d), BF16)],
        compiler_params=_cparams("arbitrary"),
        name="outproj",
    )(*args)


def _write_split(is_s, val, outp_ref, outs_ref):
    @pl.when(jnp.logical_not(is_s))
    def _():
        outp_ref[...] = val

    @pl.when(is_s)
    def _():
        outs_ref[...] = val


def _ffn_kernel(h_ref, wg_ref, wu_ref, wd_ref, x1_ref, qg_ref, rg_ref, npost_ref, *rest, geo, cf, split_out):
    outs, (wgb, wub, wdb, act) = rest[:-4], rest[-4:]
    i = pl.program_id(0)

    @pl.when(i < geo.pro)
    def _():
        wgb[i] = wg_ref[...].astype(BF16)
        wub[i] = wu_ref[...].astype(BF16)
        wdb[pl.ds(pl.multiple_of(i * cf, cf), cf), :] = wd_ref[...].astype(BF16)

    @pl.when(i >= geo.pro)
    def _():
        is_s = i - geo.pro == geo.nt - 1
        h = h_ref[...]
        for c in range(geo.pro):
            act[:, c * cf:(c + 1) * cf] = (_silu(_dot(h, wgb[c])) * _dot(h, wub[c])).astype(BF16)
        f = _dot(act[...], wdb[...])
        val = x1_ref[...] + _mod(is_s, qg_ref, rg_ref, geo.reps) * _rms(f, npost_ref[...])
        if split_out:
            _write_split(is_s, val, *outs)
        else:
            outs[0][...] = val


def _out_rows(geo, d, split_out):
    t_all = geo.tm * geo.nt
    if split_out:
        return ([geo.prompt_rows(d), geo.sample_rows(d)],
                [jax.ShapeDtypeStruct((t_all - geo.tm, d), F32), jax.ShapeDtypeStruct((geo.tm, d), F32)])
    return [geo.rows(d)], [jax.ShapeDtypeStruct((t_all, d), F32)]


def _ffn_dense(h2, wg, wu, wd, j, x1, seq_mod, row_mod, geo_args, npost, split_out):
    tm, nt, tps, reps, d = geo_args
    f = wg.shape[2]
    cf = 256 if f % 256 == 0 else LANES
    npro = f // cf
    geo = _Rows(tm, nt, tps, reps, d, npro)
    out_specs, out_shape = _out_rows(geo, d, split_out)
    chunk = lambda i: jnp.minimum(i, npro - 1)
    return pl.pallas_call(
        functools.partial(_ffn_kernel, geo=geo, cf=cf, split_out=split_out),
        grid=(npro + nt,),
        in_specs=[geo.rows(d),
                  pl.BlockSpec((None, d, cf), lambda i: (j, 0, chunk(i))),
                  pl.BlockSpec((None, d, cf), lambda i: (j, 0, chunk(i))),
                  pl.BlockSpec((None, cf, d), lambda i: (j, chunk(i), 0)),
                  geo.rows(d), geo.seq_mod(5), geo.row_mod(5, row_mod.shape[0]),
                  pl.BlockSpec((1, d), lambda i: (0, 0))],
        out_specs=out_specs,
        out_shape=out_shape,
        scratch_shapes=[pltpu.VMEM((npro, d, cf), BF16), pltpu.VMEM((npro, d, cf), BF16),
                        pltpu.VMEM((f, d), BF16), pltpu.VMEM((tm, f), BF16)],
        compiler_params=_cparams("arbitrary"),
        name="ffn_dense",
    )(h2, wg, wu, wd, x1, seq_mod, row_mod, npost.reshape(1, d))


def _row_copy(src, src_row, dst, dst_row, sem):
    return pltpu.make_async_copy(src.at[pl.ds(src_row, 1)], dst.at[pl.ds(dst_row, 1)], sem)


def _moe_dispatch_kernel(pad_start_ref, pad_len_ref, nvalid_ref, pos_ref, h_ref, xs_hbm, zbuf, sems,
                         *, tm, n_exp):
    i = pl.program_id(0)
    ts = zbuf.shape[0]

    @pl.when(i == 0)
    def _():
        zbuf[...] = jnp.zeros(zbuf.shape, F32)
        for e in range(n_exp):
            def start(k, c, e=e):
                _row_copy(zbuf, 0, xs_hbm, pad_start_ref[e] + k, sems.at[1]).start()
                return c

            def wait(k, c, e=e):
                _row_copy(zbuf, 0, xs_hbm, pad_start_ref[e] + k, sems.at[1]).wait()
                return c

            lax.fori_loop(0, pad_len_ref[e], start, 0)
            lax.fori_loop(0, pad_len_ref[e], wait, 0)

        def tile_copy(j):
            return pltpu.make_async_copy(zbuf, xs_hbm.at[pl.ds(pl.multiple_of(j * ts, ts), ts)], sems.at[1])

        def start_tile(j, c):
            tile_copy(j).start()
            return c

        def wait_tile(j, c):
            tile_copy(j).wait()
            return c

        lax.fori_loop(nvalid_ref[0], xs_hbm.shape[0] // ts, start_tile, 0)
        lax.fori_loop(nvalid_ref[0], xs_hbm.shape[0] // ts, wait_tile, 0)

    for r in range(tm):
        _row_copy(h_ref, r, xs_hbm, pos_ref[0, r], sems.at[0]).start()
        _row_copy(h_ref, r, xs_hbm, pos_ref[0, tm + r], sems.at[0]).start()
    for _ in range(TOP_K):
        pltpu.make_async_copy(h_ref, xs_hbm.at[pl.ds(0, tm)], sems.at[0]).wait()


def _moe_dispatch(h2, pos_tiles, pad_start, pad_len, n_valid, n_slots, tm, tm_slot):
    t, d = h2.shape
    nt = t // tm
    n_exp = pad_start.shape[0]
    grid_spec = pltpu.PrefetchScalarGridSpec(
        num_scalar_prefetch=3,
        grid=(nt,),
        in_specs=[pl.BlockSpec((None, 1, TOP_K * tm), lambda i, *_: (i, 0, 0), memory_space=pltpu.SMEM),
                  pl.BlockSpec((tm, d), lambda i, *_: (i, 0))],
        out_specs=pl.BlockSpec(memory_space=pl.ANY),
        scratch_shapes=[pltpu.VMEM((tm_slot, d), F32), pltpu.SemaphoreType.DMA((2,))],
    )
    return pl.pallas_call(
        functools.partial(_moe_dispatch_kernel, tm=tm, n_exp=n_exp),
        grid_spec=grid_spec,
        out_shape=jax.ShapeDtypeStruct((n_slots, d), F32),
        compiler_params=_cparams("arbitrary"),
        name="moe_dispatch",
    )(pad_start, pad_len, n_valid, pos_tiles.reshape(nt, 1, TOP_K * tm), h2)


def _moe_expert_kernel(texp_ref, nvalid_ref, x_ref, wg_ref, wu_ref, wd_ref, o_ref):
    i = pl.program_id(0)

    @pl.when(i < nvalid_ref[0])
    def _():
        h = x_ref[...].astype(BF16)
        act = (_silu(_dot(h, wg_ref[...])) * _dot(h, wu_ref[...])).astype(BF16)
        o_ref[...] = _dot(act, wd_ref[...])

    @pl.when(i >= nvalid_ref[0])
    def _():
        o_ref[...] = jnp.zeros(o_ref.shape, F32)


def _moe_experts(x_sorted, tile_expert, n_valid, wg, wu, wd, tm):
    n_tiles = tile_expert.shape[0]
    d = x_sorted.shape[1]
    n_exp, _, fe = wg.shape
    grid_spec = pltpu.PrefetchScalarGridSpec(
        num_scalar_prefetch=2,
        grid=(n_tiles,),
        in_specs=[
            pl.BlockSpec((tm, d), lambda i, te, nv: (jnp.minimum(i, nv[0] - 1), 0)),
            pl.BlockSpec((None, d, fe), lambda i, te, nv: (te[i], 0, 0)),
            pl.BlockSpec((None, d, fe), lambda i, te, nv: (te[i], 0, 0)),
            pl.BlockSpec((None, fe, d), lambda i, te, nv: (te[i], 0, 0)),
        ],
        out_specs=pl.BlockSpec((tm, d), lambda i, te, nv: (i, 0)),
    )
    return pl.pallas_call(
        _moe_expert_kernel,
        grid_spec=grid_spec,
        out_shape=jax.ShapeDtypeStruct((n_tiles * tm, d), F32),
        compiler_params=_cparams("arbitrary"),
        name="moe_experts",
    )(tile_expert, n_valid, x_sorted, wg, wu, wd)


def _start_row_gather(idx_ref, src_hbm, dst, sem, count):
    for r in range(count):
        _row_copy(src_hbm, idx_ref[0, r], dst, r, sem).start()


def _wait_row_gather(src_hbm, dst, sem):
    pltpu.make_async_copy(src_hbm.at[pl.ds(0, dst.shape[0])], dst, sem).wait()


def _moe_combine_kernel(pos0_ref, posnext_ref, ye_hbm, sel_ref, x1_ref, qg_ref, rg_ref, npost_ref, *rest,
                        geo, split_out):
    outs, (ybuf, sems) = rest[:-2], rest[-2:]
    i = pl.program_id(0)
    tm = geo.tm
    slot = i % 2

    @pl.when(i == 0)
    def _():
        _start_row_gather(pos0_ref, ye_hbm, ybuf.at[0], sems.at[0], 2 * tm)

    @pl.when(i + 1 < geo.nt)
    def _():
        _start_row_gather(posnext_ref, ye_hbm, ybuf.at[1 - slot], sems.at[1 - slot], 2 * tm)

    _wait_row_gather(ye_hbm, ybuf.at[slot], sems.at[slot])
    is_s = i == geo.nt - 1
    sel = sel_ref[...]
    lane = lax.broadcasted_iota(jnp.int32, sel.shape, 1)
    w1 = jnp.sum(jnp.where(lane == 2, sel, 0.0), axis=-1, keepdims=True)
    w2 = jnp.sum(jnp.where(lane == 3, sel, 0.0), axis=-1, keepdims=True)
    f = w1 * ybuf[slot, 0:tm, :] + w2 * ybuf[slot, tm:2 * tm, :]
    val = x1_ref[...] + _mod(is_s, qg_ref, rg_ref, geo.reps) * _rms(f, npost_ref[...])
    if split_out:
        _write_split(is_s, val, *outs)
    else:
        outs[0][...] = val


def _moe_combine(ye, pos, sel, x1, seq_mod, row_mod, geo_args, npost, split_out):
    tm, nt, tps, reps, d = geo_args
    geo = _Rows(tm, nt, tps, reps, d, 0)
    out_specs, out_shape = _out_rows(geo, d, split_out)
    pos3 = pos.reshape(nt, 1, 2 * tm)
    smem_blk = lambda fn: pl.BlockSpec((None, 1, 2 * tm), fn, memory_space=pltpu.SMEM)
    return pl.pallas_call(
        functools.partial(_moe_combine_kernel, geo=geo, split_out=split_out),
        grid=(nt,),
        in_specs=[smem_blk(lambda i: (0, 0, 0)),
                  smem_blk(lambda i: (jnp.minimum(i + 1, nt - 1), 0, 0)),
                  pl.BlockSpec(memory_space=pl.ANY),
                  geo.rows(LANES), geo.rows(d), geo.seq_mod(5), geo.row_mod(5, row_mod.shape[0]),
                  pl.BlockSpec((1, d), lambda i: (0, 0))],
        out_specs=out_specs,
        out_shape=out_shape,
        scratch_shapes=[pltpu.VMEM((2, 2 * tm, d), F32), pltpu.SemaphoreType.DMA((2,))],
        compiler_params=_cparams("arbitrary"),
        name="moe_combine",
    )(pos3, pos3, ye, sel, x1, seq_mod, row_mod, npost.reshape(1, d))


def _moe_tables(sel, counts, n_exp, tm_tok, tm_slot):
    t = sel.shape[0]
    counts = counts[0, :n_exp].astype(jnp.int32)
    tiles_e = (counts + tm_slot - 1) // tm_slot
    tile_end = jnp.cumsum(tiles_e)
    slot_start = (tile_end - tiles_e) * tm_slot
    choice = sel[:, 0:TOP_K].astype(jnp.int32)
    rank = sel[:, 2 * TOP_K:3 * TOP_K].astype(jnp.int32)
    onehot = choice[:, :, None] == jnp.arange(n_exp, dtype=jnp.int32)[None, None, :]
    pos = jnp.sum(jnp.where(onehot, slot_start[None, None, :], 0), axis=-1) + rank
    n_tiles = (TOP_K * t + n_exp * (tm_slot - 1)) // tm_slot
    tile_expert = jnp.minimum(
        jnp.sum((jnp.arange(n_tiles, dtype=jnp.int32)[:, None] >= tile_end[None, :]).astype(jnp.int32), axis=1),
        n_exp - 1)
    pos_tiles = jnp.concatenate([pos[:, k].reshape(-1, tm_tok) for k in range(TOP_K)], axis=1)
    pad_start = slot_start + counts
    pad_len = tiles_e * tm_slot - counts
    return tile_expert, tile_end[-1:].astype(jnp.int32), pos_tiles, pad_start, pad_len, n_tiles * tm_slot


def _largest_tile(t, want):
    tm = min(want, t)
    while t % tm:
        tm //= 2
    return tm


def kernel(x_prompt, x_sample, c_prompt, c_sample, state_conva, state_convb, state_ssm, ada_w, ada_b, norm_pre_mix,
           norm_post_mix, norm_pre_ffn, norm_post_ffn, w_in, w_out, conva_w, convb_w, convb_b, dt_bias, a_log, d_skip,
           ssd_norm, ffd_w_gate, ffd_w_up, ffd_w_down, moe_router, moe_w_gate, moe_w_up, moe_w_down):
    bp, seq, d = x_prompt.shape
    bs, steps, _ = x_sample.shape
    depth = w_in.shape[0]
    dc = conva_w.shape[-1]
    dx = convb_w.shape[-1]
    ds = ssd_norm.shape[-1]
    heads = dt_bias.shape[-1]
    p = ds // heads
    n = (dx - ds) // (2 * SSD_GROUPS)
    hg = heads // SSD_GROUPS
    n_exp = moe_router.shape[-1]
    ka, kb = conva_w.shape[1] - 1, convb_w.shape[1] - 1
    tm = steps * bs
    t_p = bp * seq
    assert seq % SSD_CHUNK == 0 and LANES % p == 0 and (hg * p) % LANES == 0 and n == LANES
    assert heads <= LANES and n_exp <= LANES and dc % LANES == 0 and dx % LANES == 0 and d % LANES == 0
    assert w_in.shape[-1] == 3 * dc + ds + dx + heads and seq % tm == 0 and tm % SUBLANES == 0
    assert bs % SUBLANES == 0 and (t_p // bs) % steps == 0
    nt = t_p // tm + 1
    geo_args = (tm, nt, seq // tm, steps, d)

    w_dt = jnp.pad(w_in[:, :, 3 * dc + ds + dx:], ((0, 0), (0, 0), (0, LANES - heads))).astype(BF16)
    moe_g, moe_u, moe_d = moe_w_gate.astype(BF16), moe_w_up.astype(BF16), moe_w_down.astype(BF16)
    router_p = jnp.pad(moe_router, ((0, 0), (0, 0), (0, LANES - n_exp)))
    padh = lambda a: jnp.pad(a, ((0, 0), (0, LANES - heads))).reshape(depth, 1, LANES)
    dtb_p, alog_p = padh(dt_bias), padh(a_log)
    dsk_e = jnp.repeat(d_skip, p, axis=-1).reshape(depth, 1, ds)
    hot = (jnp.arange(LANES)[:, None] == (jnp.arange(ds)[None, :] // p)).astype(BF16)
    emat = jnp.concatenate([hot, hot], axis=0)

    mod = _adaln(jnp.concatenate([c_prompt, c_sample], axis=0), ada_w, ada_b)
    seq_mod = jnp.pad(mod[:, :bp], ((0, 0), (0, 1), (0, 0))).reshape(depth, bp + 1, 1, 6 * d)
    row_mod = mod[:, bp:]

    bb = _largest_tile(bs, 16)
    x_all = None
    xs_tm = x_sample.transpose(1, 0, 2).reshape(tm, d)
    xp2d = x_prompt.reshape(t_p, d)
    pa, pb, ps, sa_l, sb_l = [], [], [], [], []
    ss_new = jnp.zeros((depth, bs, ds, n), F32)
    for i in range(depth):
        j = i // 2
        last = i == depth - 1
        if x_all is None:
            proj = _inproj(xp2d, xs_tm, seq_mod[i], row_mod[i], geo_args, norm_pre_mix[i], w_in, i, w_dt[i], dc, ds, dx)
        else:
            proj = _inproj(x_all, None, seq_mod[i], row_mod[i], geo_args, norm_pre_mix[i], w_in, i, w_dt[i], dc, ds, dx)
        params = [conva_w[i], convb_w[i], convb_b[i].reshape(1, dx), dtb_p[i], alog_p[i], dsk_e[i],
                  ssd_norm[i].reshape(1, ds), emat]
        cat_p, na, nb, ns = _ssd_prompt(*proj, bp, seq, params, n, p, hg)
        r3 = lambda a: a.reshape(a.shape[0] // bs, bs, a.shape[-1])
        cat_s, sna, snb, ss_new = _ssd_sample(*[r3(a) for a in proj], t_p // (bs * steps), steps, bs, i,
                                              state_conva.reshape(depth, bs, ka * dc),
                                              state_convb.reshape(depth, bs, kb * dx),
                                              state_ssm.reshape(depth, bs, ds, n), ss_new, params, n, p, hg, bb)
        pa.append(na)
        pb.append(nb)
        ps.append(ns.reshape(bp, heads, p, n))
        sa_l.append(sna.reshape(bs, ka, dc))
        sb_l.append(snb.reshape(bs, kb, dx))

        moe = i % 2 == 1
        x_in = (xp2d, xs_tm) if x_all is None else (x_all, None)
        res = _outproj(cat_p, cat_s.reshape(tm, dc + ds), w_out, i, *x_in, seq_mod[i], row_mod[i], geo_args,
                       norm_post_mix[i], norm_pre_ffn[i], router_p[j] if moe else None, n_exp if moe else 0)
        if moe:
            x1, h2, sel, counts = res
            tile_expert, n_valid, pos_tiles, pad_start, pad_len, n_slots = _moe_tables(sel, counts, n_exp, tm, MOE_TILE)
            x_sorted = _moe_dispatch(h2, pos_tiles, pad_start, pad_len, n_valid, n_slots, tm, MOE_TILE)
            ye = _moe_experts(x_sorted, tile_expert, n_valid, moe_g[j], moe_u[j], moe_d[j], MOE_TILE)
            out = _moe_combine(ye, pos_tiles, sel, x1, seq_mod[i], row_mod[i], geo_args, norm_post_ffn[i], last)
        else:
            x1, h2 = res
            out = _ffn_dense(h2, ffd_w_gate, ffd_w_up, ffd_w_down, j, x1, seq_mod[i], row_mod[i], geo_args,
                             norm_post_ffn[i], last)
        if last:
            y_p, y_s = out
        else:
            x_all = out[0]

    y_prompt = y_p.reshape(bp, seq, d)
    y_sample = y_s.reshape(steps, bs, d).transpose(1, 0, 2)
    return (y_prompt, y_sample, jnp.stack(pa), jnp.stack(pb), jnp.stack(ps),
            jnp.stack(sa_l), jnp.stack(sb_l), ss_new.reshape(depth, bs, heads, p, n))
```

```python
import functools

import jax
import jax.numpy as jnp
from jax import lax
from jax.experimental import pallas as pl
from jax.experimental.pallas import tpu as pltpu

EPS = 1e-6
SSD_GROUPS = 2
SSD_CHUNK = 128
TOP_K = 2
LOG2E = 1.4426950408889634
LANES = 128
SUBLANES = 8
VMEM_LIMIT_BYTES = 56 * 1024 * 1024
MOE_TILE = 512

F32 = jnp.float32
BF16 = jnp.bfloat16


def _cparams(*sem):
    return pltpu.CompilerParams(dimension_semantics=sem, vmem_limit_bytes=VMEM_LIMIT_BYTES)


def _resident(shape):
    return pl.BlockSpec(shape, lambda *_: (0,) * len(shape), pipeline_mode=pl.Buffered(1))


def _silu(x):
    return x * (1.0 / (1.0 + jnp.exp(-x)))


def _softplus(x):
    return jnp.maximum(x, 0.0) + jnp.log1p(jnp.exp(-jnp.abs(x)))


def _rms(x, g):
    return x * lax.rsqrt(jnp.mean(x * x, axis=-1, keepdims=True) + EPS) * g


def _dot(a, b):
    return jnp.dot(a, b, preferred_element_type=F32)


def _dot_nt(a, b):
    return lax.dot_general(a, b, (((1,), (1,)), ((), ())), preferred_element_type=F32)


def _dot_tn(a, b):
    return lax.dot_general(a, b, (((0,), (0,)), ((), ())), preferred_element_type=F32)


def _chunk(*widths):
    return next(c for c in (512, 256, LANES) if all(w % c == 0 for w in widths))


def _adaln_kernel(c_ref, w_ref, b_ref, o_ref):
    s = _silu(c_ref[...]).astype(BF16)
    o_ref[...] = _dot(s, w_ref[...].astype(BF16)) + b_ref[...]


def _adaln(c_all, ada_w, ada_b):
    depth, d, d6 = ada_w.shape
    rows = c_all.shape[0]
    tn = _chunk(d6 // 6) * 2
    return pl.pallas_call(
        _adaln_kernel,
        grid=(depth, d6 // tn),
        in_specs=[
            pl.BlockSpec((rows, d), lambda l, j: (0, 0)),
            pl.BlockSpec((None, d, tn), lambda l, j: (l, 0, j)),
            pl.BlockSpec((None, 1, tn), lambda l, j: (l, 0, j)),
        ],
        out_specs=pl.BlockSpec((None, rows, tn), lambda l, j: (l, 0, j)),
        out_shape=jax.ShapeDtypeStruct((depth, rows, d6), F32),
        compiler_params=_cparams("arbitrary", "arbitrary"),
        name="adaln",
    )(c_all, ada_w, ada_b.reshape(depth, 1, d6))


class _Rows:
    def __init__(self, tm, nt, tps, reps, d, pro):
        self.tm, self.nt, self.tps, self.reps, self.d, self.pro = tm, nt, tps, reps, d, pro

    def tile(self, i):
        return jnp.maximum(i - self.pro, 0)

    def rows(self, w):
        return pl.BlockSpec((self.tm, w), lambda i, *_: (self.tile(i), 0))

    def prompt_rows(self, w):
        return pl.BlockSpec((self.tm, w), lambda i, *_: (jnp.minimum(self.tile(i), self.nt - 2), 0))

    def sample_rows(self, w):
        return pl.BlockSpec((self.tm, w), lambda i, *_: (0, 0))

    def seq_mod(self, k):
        return pl.BlockSpec((None, 1, self.d), lambda i, *_: (self.tile(i) // self.tps, 0, k))

    def row_mod(self, k, rows):
        return pl.BlockSpec((rows, self.d), lambda i, *_: (0, k))


def _mod(is_sample, seq_ref, row_ref, reps):
    rowm = jnp.concatenate([row_ref[...]] * reps, axis=0)
    return seq_ref[...] + jnp.where(is_sample, rowm, 0.0)


def _causal_conv_cols(cur, c0, w_ref, tail, ext, nst_ref, keep_state, first, emit):
    tm, cw = cur.shape
    kp = w_ref.shape[0] - 1
    wcol = lambda j: w_ref[j:j + 1, c0:c0 + cw]
    ext[0:SUBLANES, :] = jnp.where(first, 0.0, tail[:, c0:c0 + cw])
    ext[SUBLANES:SUBLANES + tm, :] = cur
    acc = wcol(kp) * cur
    for j in range(kp):
        off = SUBLANES - (kp - j)
        acc = acc + wcol(j) * ext[off:off + tm, :]
    tail[:, c0:c0 + cw] = ext[tm:tm + SUBLANES, :]
    nst_ref[:, c0:c0 + cw] = jnp.where(keep_state, nst_ref[:, c0:c0 + cw], ext[SUBLANES + tm - kp:SUBLANES + tm, :])
    emit(acc)


def _inproj_kernel(*refs, geo, dc, ds, dx, cw, split_x):
    if split_x:
        xp_ref, xs_ref, *refs = refs
    else:
        xp_ref, *refs = refs
    (qsh_ref, qsc_ref, rsh_ref, rsc_ref, g_ref, wch_ref, wdt_ref, caw_ref, cbw_ref, cbb_ref,
     ya_ref, z_ref, xc_ref, dt_ref, ncap_ref, ncbp_ref, gbs_ref, us_ref, xbcs_ref, wbf, ext, tail_u, tail_x) = refs
    i = pl.program_id(0)

    @pl.when(i < geo.pro)
    def _():
        wbf[i] = wch_ref[...].astype(BF16)

    @pl.when(i >= geo.pro)
    def _():
        r = i - geo.pro
        is_s = r == geo.nt - 1
        first = r % geo.tps == 0
        x = jnp.where(is_s, xs_ref[...], xp_ref[...]) if split_x else xp_ref[...]
        sc = _mod(is_s, qsc_ref, rsc_ref, geo.reps)
        sh = _mod(is_s, qsh_ref, rsh_ref, geo.reps)
        h = (_rms(x, g_ref[...]) * (1.0 + sc) + sh).astype(BF16)
        nc = dc // cw
        for a in range(nc):
            c0 = a * cw
            gate_b = _dot(h, wbf[a])
            u = _dot(h, wbf[nc + a]) * _dot(h, wbf[2 * nc + a])
            gbs_ref[:, c0:c0 + cw] = gate_b
            us_ref[:, c0:c0 + cw] = u

            def emit_a(v, c0=c0, gate_b=gate_b):
                ya_ref[:, c0:c0 + cw] = (gate_b * v).astype(BF16)

            _causal_conv_cols(u, c0, caw_ref, tail_u, ext, ncap_ref, is_s, first, emit_a)
        for a in range(ds // cw):
            z_ref[:, a * cw:(a + 1) * cw] = _dot(h, wbf[3 * nc + a])
        for a in range(dx // cw):
            c0 = a * cw
            xbc = _dot(h, wbf[3 * nc + ds // cw + a])
            xbcs_ref[:, c0:c0 + cw] = xbc

            def emit_b(v, c0=c0):
                xc_ref[:, c0:c0 + cw] = _silu(v + cbb_ref[:, c0:c0 + cw])

            _causal_conv_cols(xbc, c0, cbw_ref, tail_x, ext, ncbp_ref, is_s, first, emit_b)
        dt_ref[...] = _dot(h, wdt_ref[...])


def _inproj(x_p, x_s, seq_mod, row_mod, geo_args, norm_w, w_in, layer, w_dt, caw, cbw, cbb, bp, ds):
    tm, nt, tps, reps, d = geo_args
    dc, dx = caw.shape[1], cbw.shape[1]
    ka, kb = caw.shape[0] - 1, cbw.shape[0] - 1
    cw = _chunk(dc, ds, dx)
    npro = (3 * dc + ds + dx) // cw
    geo = _Rows(tm, nt, tps, reps, d, npro)
    t_all = tm * nt
    split_x = x_s is not None
    xin = [geo.prompt_rows(d), geo.sample_rows(d)] if split_x else [geo.rows(d)]
    xargs = [x_p, x_s] if split_x else [x_p]
    nrow = row_mod.shape[0]
    seq_blk = lambda k, w: pl.BlockSpec((None, k, w), lambda i: (jnp.minimum(geo.tile(i) // tps, bp - 1), 0, 0))
    return pl.pallas_call(
        functools.partial(_inproj_kernel, geo=geo, dc=dc, ds=ds, dx=dx, cw=cw, split_x=split_x),
        grid=(npro + nt,),
        in_specs=xin + [
            geo.seq_mod(0), geo.seq_mod(1), geo.row_mod(0, nrow), geo.row_mod(1, nrow),
            pl.BlockSpec((1, d), lambda i: (0, 0)),
            pl.BlockSpec((None, d, cw), lambda i: (layer, 0, jnp.minimum(i, npro - 1))),
            _resident((d, LANES)), _resident(caw.shape), _resident(cbw.shape), _resident(cbb.shape),
        ],
        out_specs=[geo.rows(dc), geo.rows(ds), geo.rows(dx), geo.rows(LANES), seq_blk(ka, dc), seq_blk(kb, dx),
                   geo.sample_rows(dc), geo.sample_rows(dc), geo.sample_rows(dx)],
        out_shape=[
            jax.ShapeDtypeStruct((t_all, dc), BF16),
            jax.ShapeDtypeStruct((t_all, ds), F32),
            jax.ShapeDtypeStruct((t_all, dx), F32),
            jax.ShapeDtypeStruct((t_all, LANES), F32),
            jax.ShapeDtypeStruct((bp, ka, dc), F32),
            jax.ShapeDtypeStruct((bp, kb, dx), F32),
            jax.ShapeDtypeStruct((tm, dc), F32),
            jax.ShapeDtypeStruct((tm, dc), F32),
            jax.ShapeDtypeStruct((tm, dx), F32),
        ],
        scratch_shapes=[pltpu.VMEM((npro, d, cw), BF16), pltpu.VMEM((SUBLANES + tm, cw), F32),
                        pltpu.VMEM((SUBLANES, dc), F32), pltpu.VMEM((SUBLANES, dx), F32)],
        compiler_params=_cparams("arbitrary"),
        name="inproj",
    )(*xargs, seq_mod, seq_mod, row_mod, row_mod, norm_w.reshape(1, d), w_in, w_dt, caw, cbw, cbb)


def _split_hi_lo(v):
    hi = v.astype(BF16)
    lo = (v - hi.astype(F32)).astype(BF16)
    return jnp.concatenate([hi, lo], axis=1)


def _diag_block(cb, mask, cum, cum_t, dt_t, xs, lane, g, hg, gw, p):
    hpl = LANES // p
    parts = []
    for slab in range(gw // LANES):
        lhs, rhs = [], []
        xslab = xs[:, g * gw + slab * LANES:g * gw + (slab + 1) * LANES]
        for j in range(hpl):
            h = g * hg + slab * hpl + j
            seg = cum[:, h:h + 1] - cum_t[h:h + 1, :]
            m = cb * jnp.where(mask, jnp.exp2(seg), 0.0) * dt_t[h:h + 1, :]
            lhs.append(m.astype(BF16))
            rhs.append(jnp.where((lane >= j * p) & (lane < (j + 1) * p), xslab, 0.0).astype(BF16))
        parts.append(_dot(jnp.concatenate(lhs, axis=1), jnp.concatenate(rhs, axis=0)))
    return jnp.concatenate(parts, axis=1)


def _gated_group_norm(y, xs_g, z_g, dsk_g, snorm_g):
    y = (y + dsk_g * xs_g) * _silu(z_g)
    return (y * lax.rsqrt(jnp.mean(y * y, axis=-1, keepdims=True) + EPS) * snorm_g).astype(BF16)


def _ssd_prompt_kernel(z_ref, xc_ref, dt_ref, dtb_ref, alog_ref, dsk_ref, snorm_ref, e_ref, yb_ref, nss_ref, st,
                       *, ds, n, p, hg):
    c = pl.program_id(1)
    q = SSD_CHUNK
    gw = hg * p

    @pl.when(c == 0)
    def _():
        st[...] = jnp.zeros(st.shape, F32)

    xs = xc_ref[:, 0:ds]
    dt = _softplus(dt_ref[...] + dtb_ref[...])
    da = dt * (-jnp.exp(alog_ref[...]) * LOG2E)
    row = lax.broadcasted_iota(jnp.int32, (q, LANES), 0)
    cum = da
    k = 1
    while k < q:
        cum = cum + jnp.where(row >= k, pltpu.roll(cum, k, axis=0), 0.0)
        k *= 2
    cum_last = cum[q - 1:q, :]
    w_exp = _dot(_split_hi_lo(dt * jnp.exp2(cum_last - cum)), e_ref[...])
    ecum_exp = _dot(_split_hi_lo(jnp.exp2(cum)), e_ref[...])
    cum_t = cum.T
    dt_t = dt.T

    tril = lax.broadcasted_iota(jnp.int32, (q, q), 0) >= lax.broadcasted_iota(jnp.int32, (q, q), 1)
    lane = lax.broadcasted_iota(jnp.int32, (q, LANES), 1)
    for g in range(SSD_GROUPS):
        gsl = slice(g * gw, (g + 1) * gw)
        bm = xc_ref[:, ds + g * n:ds + (g + 1) * n]
        cm = xc_ref[:, ds + SSD_GROUPS * n + g * n:ds + SSD_GROUPS * n + (g + 1) * n].astype(BF16)
        bm_t = bm.T.astype(BF16)
        y = _diag_block(_dot(cm, bm_t), tril, cum, cum_t, dt_t, xs, lane, g, hg, gw, p)
        s_prev = st[g]
        y = y + _dot(cm, s_prev.astype(BF16)) * ecum_exp[:, gsl]
        xw = (xs[:, gsl] * w_exp[:, gsl]).astype(BF16)
        st[g] = s_prev * ecum_exp[q - 1:q, gsl] + _dot(bm_t, xw)
        yb_ref[:, gsl] = _gated_group_norm(y, xs[:, gsl], z_ref[:, gsl], dsk_ref[:, gsl], snorm_ref[:, gsl])

    @pl.when(c == pl.num_programs(1) - 1)
    def _():
        for g in range(SSD_GROUPS):
            for slab in range(gw // LANES):
                r0 = g * gw + slab * LANES
                nss_ref[r0:r0 + LANES, :] = st[g, :, slab * LANES:(slab + 1) * LANES].T


def _ssd_prompt(z, xc, dtr, bsz, seq, params, n, p, hg):
    ds = z.shape[1]
    dx = xc.shape[1]
    q = SSD_CHUNK
    nc = seq // q
    row = lambda w: pl.BlockSpec((q, w), lambda b, c: (b * nc + c, 0))
    return pl.pallas_call(
        functools.partial(_ssd_prompt_kernel, ds=ds, n=n, p=p, hg=hg),
        grid=(bsz, nc),
        in_specs=[row(ds), row(dx), row(LANES)] + [_resident(a.shape) for a in params],
        out_specs=[row(ds), pl.BlockSpec((None, ds, n), lambda b, c: (b, 0, 0))],
        out_shape=[jax.ShapeDtypeStruct((bsz * seq, ds), BF16), jax.ShapeDtypeStruct((bsz, ds, n), F32)],
        scratch_shapes=[pltpu.VMEM((SSD_GROUPS, n, hg * p), F32)],
        compiler_params=_cparams("arbitrary", "arbitrary"),
        name="ssd_prompt",
    )(z, xc, dtr, *params)


def _ssd_sample_kernel(gb_ref, u_ref, xbc_ref, z_ref, dt_ref, sa_ref, sb_ref, ss_ref, caw_ref, cbw_ref, cbb_ref,
                       dtb_ref, alog_ref, dsk_ref, snorm_ref, e_ref, ya_ref, yb_ref, nca_ref, ncb_ref, nss_ref,
                       *, dc, ds, dx, n, p, hg, steps, bb):
    gw = hg * p
    rows = steps * bb
    ka = caw_ref.shape[0] - 1
    kb = cbw_ref.shape[0] - 1

    def conv(cur_ref, st_ref, w_ref, kprev, width):
        hist = [st_ref[:, j * width:(j + 1) * width] for j in range(kprev)] + [cur_ref[t] for t in range(steps)]
        outs = []
        for t in range(steps):
            acc = w_ref[kprev:kprev + 1, :] * hist[t + kprev]
            for j in range(kprev):
                acc = acc + w_ref[j:j + 1, :] * hist[t + j]
            outs.append(acc)
        return outs, hist[len(hist) - kprev:]

    v, new_a = conv(u_ref, sa_ref, caw_ref, ka, dc)
    for t in range(steps):
        ya_ref[t] = (gb_ref[t] * v[t]).astype(BF16)
    for j in range(ka):
        nca_ref[:, j * dc:(j + 1) * dc] = new_a[j]
    xcs, new_b = conv(xbc_ref, sb_ref, cbw_ref, kb, dx)
    for j in range(kb):
        ncb_ref[:, j * dx:(j + 1) * dx] = new_b[j]
    xc = _silu(jnp.concatenate(xcs, axis=0) + cbb_ref[...])
    xs = xc[:, 0:ds]

    dt = _softplus(jnp.concatenate([dt_ref[t] for t in range(steps)], axis=0) + dtb_ref[...])
    da = dt * (-jnp.exp(alog_ref[...]) * LOG2E)
    cums = [da[0:bb]]
    for t in range(1, steps):
        cums.append(cums[-1] + da[t * bb:(t + 1) * bb])
    cum = jnp.concatenate(cums, axis=0)
    cum_last = jnp.concatenate([cums[-1]] * steps, axis=0)
    w_exp = _dot(_split_hi_lo(dt * jnp.exp2(cum_last - cum)), e_ref[...])
    ecum_exp = _dot(_split_hi_lo(jnp.exp2(cum)), e_ref[...])

    def pad_t(a):
        a = jnp.concatenate([a, jnp.zeros((LANES - rows, LANES), F32)], axis=0) if rows < LANES else a
        return a.T[:, 0:rows]

    cum_t = pad_t(cum)
    dt_t = pad_t(dt)

    ri = lax.broadcasted_iota(jnp.int32, (rows, rows), 0)
    ci = lax.broadcasted_iota(jnp.int32, (rows, rows), 1)
    same = ((ri % bb) == (ci % bb)) & (ri >= ci)
    lane = lax.broadcasted_iota(jnp.int32, (rows, LANES), 1)
    rowid = lax.broadcasted_iota(jnp.int32, (rows, 1), 0) % bb
    seqlane = lax.broadcasted_iota(jnp.int32, (LANES, LANES), 1)
    nslab = gw // LANES
    for g in range(SSD_GROUPS):
        gsl = slice(g * gw, (g + 1) * gw)
        bm = xc[:, ds + g * n:ds + (g + 1) * n].astype(BF16)
        cm = xc[:, ds + SSD_GROUPS * n + g * n:ds + SSD_GROUPS * n + (g + 1) * n].astype(BF16)
        y_diag = _diag_block(_dot_nt(cm, bm), same, cum, cum_t, dt_t, xs, lane, g, hg, gw, p)
        xw = (xs[:, gsl] * w_exp[:, gsl]).astype(BF16)
        dec = ecum_exp[(steps - 1) * bb:steps * bb, gsl]
        dec = jnp.concatenate([dec, jnp.zeros((LANES - bb, gw), F32)], axis=0)
        dec_t = [dec[:, s * LANES:(s + 1) * LANES].T for s in range(nslab)]

        def per_seq(b, y_off, g=g, cm=cm, bm=bm, xw=xw, dec_t=dec_t):
            r0 = g * gw
            s0 = ss_ref[b, r0:r0 + gw, :]
            y_off = jnp.where(rowid == b, _dot_nt(cm, s0.astype(BF16)), y_off)
            upd = _dot_tn(jnp.where(rowid == b, xw, jnp.zeros_like(xw)), bm)
            for s in range(nslab):
                dcol = jnp.sum(jnp.where(seqlane == b, dec_t[s], 0.0), axis=1, keepdims=True)
                nss_ref[b, r0 + s * LANES:r0 + (s + 1) * LANES, :] = (
                    s0[s * LANES:(s + 1) * LANES, :] * dcol + upd[s * LANES:(s + 1) * LANES, :])
            return y_off

        y_off = lax.fori_loop(0, bb, per_seq, jnp.zeros((rows, gw), F32))
        zg = jnp.concatenate([z_ref[t, :, gsl] for t in range(steps)], axis=0)
        yn = _gated_group_norm(y_diag + y_off * ecum_exp[:, gsl], xs[:, gsl], zg, dsk_ref[:, gsl], snorm_ref[:, gsl])
        for t in range(steps):
            yb_ref[t, :, gsl] = yn[t * bb:(t + 1) * bb]


def _ssd_sample_aliased_kernel(*refs, **kw):
    _ssd_sample_kernel(*refs[1:], **kw)


def _ssd_sample(gb, u, xbc, z, dtr, blk0, steps, bs, layer, sa, sb, ss, prev_ss, params, n, p, hg, bb):
    dc, ds, dx = u.shape[2], z.shape[2], xbc.shape[2]
    ka, kb = params[0].shape[0] - 1, params[1].shape[0] - 1
    own = lambda w: pl.BlockSpec((steps, bb, w), lambda i: (0, i, 0))
    tok = lambda w: pl.BlockSpec((steps, bb, w), lambda i: (blk0, i, 0))
    state = pl.BlockSpec((None, bb, ds, n), lambda i: (layer, i, 0, 0))
    return pl.pallas_call(
        functools.partial(_ssd_sample_aliased_kernel, dc=dc, ds=ds, dx=dx, n=n, p=p, hg=hg, steps=steps, bb=bb),
        grid=(bs // bb,),
        in_specs=[pl.BlockSpec(memory_space=pl.ANY), own(dc), own(dc), own(dx), tok(ds), tok(LANES),
                  pl.BlockSpec((None, bb, ka * dc), lambda i: (layer, i, 0)),
                  pl.BlockSpec((None, bb, kb * dx), lambda i: (layer, i, 0)), state]
        + [_resident(a.shape) for a in params],
        out_specs=[own(dc), own(ds), pl.BlockSpec((bb, ka * dc), lambda i: (i, 0)),
                   pl.BlockSpec((bb, kb * dx), lambda i: (i, 0)), state],
        out_shape=[jax.ShapeDtypeStruct((steps, bs, dc), BF16), jax.ShapeDtypeStruct((steps, bs, ds), BF16),
                   jax.ShapeDtypeStruct((bs, ka * dc), F32), jax.ShapeDtypeStruct((bs, kb * dx), F32),
                   jax.ShapeDtypeStruct(prev_ss.shape, F32)],
        input_output_aliases={0: 4},
        compiler_params=_cparams("arbitrary"),
        name="ssd_sample",
    )(prev_ss, gb, u, xbc, z, dtr, sa, sb, ss, *params)


def _route(logits, n_exp, cnt_ref):
    rows = logits.shape[0]
    lane = lax.broadcasted_iota(jnp.int32, logits.shape, 1).astype(F32)
    valid = lane < n_exp
    logits = jnp.where(valid, logits, -jnp.inf)
    e = jnp.exp(logits - jnp.max(logits, axis=-1, keepdims=True))
    prob = jnp.where(valid, e / jnp.sum(e, axis=-1, keepdims=True), -1.0)
    big = float(LANES)
    m1 = jnp.max(prob, axis=-1, keepdims=True)
    i1 = jnp.min(jnp.where(prob == m1, lane, big), axis=-1, keepdims=True)
    rest = jnp.where(lane == i1, -1.0, prob)
    m2 = jnp.max(rest, axis=-1, keepdims=True)
    i2 = jnp.min(jnp.where(rest == m2, lane, big), axis=-1, keepdims=True)
    den = m1 + m2
    chosen = jnp.where((lane == i1) | (lane == i2), 1.0, 0.0)
    before = (lax.broadcasted_iota(jnp.int32, (rows, rows), 0) > lax.broadcasted_iota(jnp.int32, (rows, rows), 1))
    rank = _dot(jnp.where(before, 1.0, 0.0).astype(BF16), chosen.astype(BF16)) + cnt_ref[0:1, :]
    r1 = jnp.sum(jnp.where(lane == i1, rank, 0.0), axis=-1, keepdims=True)
    r2 = jnp.sum(jnp.where(lane == i2, rank, 0.0), axis=-1, keepdims=True)
    cnt_ref[0:1, :] = cnt_ref[0:1, :] + jnp.sum(chosen, axis=0, keepdims=True)
    cols = [i1, i2, m1 / den, m2 / den, r1, r2]
    out = jnp.zeros(logits.shape, F32)
    for k, col in enumerate(cols):
        out = jnp.where(lane == float(k), col, out)
    return out


def _outproj_kernel(yap_ref, yas_ref, ybp_ref, ybs_ref, wch_ref, *rest, geo, n_exp, kc, split_x):
    if split_x:
        xp_ref, xs_ref, *rest = rest
    else:
        xp_ref, *rest = rest
    qg_ref, qsh_ref, qsc_ref, rg_ref, rsh_ref, rsc_ref, npost_ref, npre_ref, *rest = rest
    if n_exp:
        r_ref, x1_ref, h2_ref, sel_ref, cnt_ref, wbf = rest
    else:
        x1_ref, h2_ref, wbf = rest
    i = pl.program_id(0)

    @pl.when(i < geo.pro)
    def _():
        wbf[pl.ds(pl.multiple_of(i * kc, kc), kc), :] = wch_ref[...].astype(BF16)
        if n_exp:
            cnt_ref[...] = jnp.zeros(cnt_ref.shape, F32)

    @pl.when(i >= geo.pro)
    def _():
        is_s = i - geo.pro == geo.nt - 1
        da = yap_ref.shape[1]
        mix = (_dot(jnp.where(is_s, yas_ref[...], yap_ref[...]), wbf[0:da, :])
               + _dot(jnp.where(is_s, ybs_ref[...], ybp_ref[...]), wbf[da:, :]))
        x = jnp.where(is_s, xs_ref[...], xp_ref[...]) if split_x else xp_ref[...]
        x1 = x + _mod(is_s, qg_ref, rg_ref, geo.reps) * _rms(mix, npost_ref[...])
        x1_ref[...] = x1
        h2 = (_rms(x1, npre_ref[...]) * (1.0 + _mod(is_s, qsc_ref, rsc_ref, geo.reps))
              + _mod(is_s, qsh_ref, rsh_ref, geo.reps))
        h2_ref[...] = h2.astype(h2_ref.dtype)
        if n_exp:
            h_hi = h2.astype(BF16)
            h_lo = (h2 - h_hi.astype(F32)).astype(BF16)
            r = r_ref[...]
            r_hi = r.astype(BF16)
            r_lo = (r - r_hi.astype(F32)).astype(BF16)
            logits = _dot(h_hi, r_hi) + (_dot(h_lo, r_hi) + _dot(h_hi, r_lo))
            sel_ref[...] = _route(logits, n_exp, cnt_ref)


def _outproj(ya_p, ya_s, yb_p, yb_s, w_out, layer, x_p, x_s, seq_mod, row_mod, geo_args, npost, npre, router,
             n_exp):
    tm, nt, tps, reps, d = geo_args
    da, db = ya_p.shape[1], yb_p.shape[1]
    dm = da + db
    kc = _chunk(dm) // 2
    npro = dm // kc
    geo = _Rows(tm, nt, tps, reps, d, npro)
    t_all = tm * nt
    nrow = row_mod.shape[0]
    vec = pl.BlockSpec((1, d), lambda i: (0, 0))
    split_x = x_s is not None
    in_specs = [geo.prompt_rows(da), geo.sample_rows(da), geo.prompt_rows(db), geo.sample_rows(db),
                pl.BlockSpec((None, kc, d), lambda i: (layer, jnp.minimum(i, npro - 1), 0))]
    in_specs += [geo.prompt_rows(d), geo.sample_rows(d)] if split_x else [geo.rows(d)]
    in_specs += [geo.seq_mod(2), geo.seq_mod(3), geo.seq_mod(4),
                 geo.row_mod(2, nrow), geo.row_mod(3, nrow), geo.row_mod(4, nrow), vec, vec]
    args = [ya_p, ya_s, yb_p, yb_s, w_out] + ([x_p, x_s] if split_x else [x_p])
    args += [seq_mod, seq_mod, seq_mod, row_mod, row_mod, row_mod, npost.reshape(1, d), npre.reshape(1, d)]
    out_specs = [geo.rows(d), geo.rows(d)]
    out_shape = [jax.ShapeDtypeStruct((t_all, d), F32), jax.ShapeDtypeStruct((t_all, d), F32 if n_exp else BF16)]
    if n_exp:
        in_specs.append(_resident((d, LANES)))
        args.append(router)
        out_specs += [geo.rows(LANES), pl.BlockSpec((SUBLANES, LANES), lambda i: (0, 0))]
        out_shape += [jax.ShapeDtypeStruct((t_all, LANES), F32), jax.ShapeDtypeStruct((SUBLANES, LANES), F32)]
    return pl.pallas_call(
        functools.partial(_outproj_kernel, geo=geo, n_exp=n_exp, kc=kc, split_x=split_x),
        grid=(npro + nt,),
        in_specs=in_specs,
        out_specs=out_specs,
        out_shape=out_shape,
        scratch_shapes=[pltpu.VMEM((dm, d), BF16)],
        compiler_params=_cparams("arbitrary"),
        name="outproj",
    )(*args)


def _write_split(is_s, val, outp_ref, outs_ref):
    @pl.when(jnp.logical_not(is_s))
    def _():
        outp_ref[...] = val

    @pl.when(is_s)
    def _():
        outs_ref[...] = val


def _ffn_kernel(h_ref, wg_ref, wu_ref, wd_ref, x1_ref, qg_ref, rg_ref, npost_ref, *rest, geo, cf, split_out):
    outs, (wgb, wub, wdb, act) = rest[:-4], rest[-4:]
    i = pl.program_id(0)

    @pl.when(i < geo.pro)
    def _():
        wgb[i] = wg_ref[...].astype(BF16)
        wub[i] = wu_ref[...].astype(BF16)
        wdb[pl.ds(pl.multiple_of(i * cf, cf), cf), :] = wd_ref[...].astype(BF16)

    @pl.when(i >= geo.pro)
    def _():
        is_s = i - geo.pro == geo.nt - 1
        h = h_ref[...]
        for c in range(geo.pro):
            act[:, c * cf:(c + 1) * cf] = (_silu(_dot(h, wgb[c])) * _dot(h, wub[c])).astype(BF16)
        f = _dot(act[...], wdb[...])
        val = x1_ref[...] + _mod(is_s, qg_ref, rg_ref, geo.reps) * _rms(f, npost_ref[...])
        if split_out:
            _write_split(is_s, val, *outs)
        else:
            outs[0][...] = val


def _out_rows(geo, d, split_out):
    t_all = geo.tm * geo.nt
    if split_out:
        return ([geo.prompt_rows(d), geo.sample_rows(d)],
                [jax.ShapeDtypeStruct((t_all - geo.tm, d), F32), jax.ShapeDtypeStruct((geo.tm, d), F32)])
    return [geo.rows(d)], [jax.ShapeDtypeStruct((t_all, d), F32)]


def _ffn_dense(h2, wg, wu, wd, j, x1, seq_mod, row_mod, geo_args, npost, split_out):
    tm, nt, tps, reps, d = geo_args
    f = wg.shape[2]
    cf = 256 if f % 256 == 0 else LANES
    npro = f // cf
    geo = _Rows(tm, nt, tps, reps, d, npro)
    out_specs, out_shape = _out_rows(geo, d, split_out)
    chunk = lambda i: jnp.minimum(i, npro - 1)
    return pl.pallas_call(
        functools.partial(_ffn_kernel, geo=geo, cf=cf, split_out=split_out),
        grid=(npro + nt,),
        in_specs=[geo.rows(d),
                  pl.BlockSpec((None, d, cf), lambda i: (j, 0, chunk(i))),
                  pl.BlockSpec((None, d, cf), lambda i: (j, 0, chunk(i))),
                  pl.BlockSpec((None, cf, d), lambda i: (j, chunk(i), 0)),
                  geo.rows(d), geo.seq_mod(5), geo.row_mod(5, row_mod.shape[0]),
                  pl.BlockSpec((1, d), lambda i: (0, 0))],
        out_specs=out_specs,
        out_shape=out_shape,
        scratch_shapes=[pltpu.VMEM((npro, d, cf), BF16), pltpu.VMEM((npro, d, cf), BF16),
                        pltpu.VMEM((f, d), BF16), pltpu.VMEM((tm, f), BF16)],
        compiler_params=_cparams("arbitrary"),
        name="ffn_dense",
    )(h2, wg, wu, wd, x1, seq_mod, row_mod, npost.reshape(1, d))


def _row_copy(src, src_row, dst, dst_row, sem):
    return pltpu.make_async_copy(src.at[pl.ds(src_row, 1)], dst.at[pl.ds(dst_row, 1)], sem)


def _moe_dispatch_kernel(pad_start_ref, pad_len_ref, nvalid_ref, pos_ref, h_ref, xs_hbm, zbuf, sems,
                         *, tm, n_exp):
    i = pl.program_id(0)
    ts = zbuf.shape[0]

    @pl.when(i == 0)
    def _():
        zbuf[...] = jnp.zeros(zbuf.shape, F32)
        for e in range(n_exp):
            def start(k, c, e=e):
                _row_copy(zbuf, 0, xs_hbm, pad_start_ref[e] + k, sems.at[1]).start()
                return c

            def wait(k, c, e=e):
                _row_copy(zbuf, 0, xs_hbm, pad_start_ref[e] + k, sems.at[1]).wait()
                return c

            lax.fori_loop(0, pad_len_ref[e], start, 0)
            lax.fori_loop(0, pad_len_ref[e], wait, 0)

        def tile_copy(j):
            return pltpu.make_async_copy(zbuf, xs_hbm.at[pl.ds(pl.multiple_of(j * ts, ts), ts)], sems.at[1])

        def start_tile(j, c):
            tile_copy(j).start()
            return c

        def wait_tile(j, c):
            tile_copy(j).wait()
            return c

        lax.fori_loop(nvalid_ref[0], xs_hbm.shape[0] // ts, start_tile, 0)
        lax.fori_loop(nvalid_ref[0], xs_hbm.shape[0] // ts, wait_tile, 0)

    for r in range(tm):
        _row_copy(h_ref, r, xs_hbm, pos_ref[0, r], sems.at[0]).start()
        _row_copy(h_ref, r, xs_hbm, pos_ref[0, tm + r], sems.at[0]).start()
    for _ in range(TOP_K):
        pltpu.make_async_copy(h_ref, xs_hbm.at[pl.ds(0, tm)], sems.at[0]).wait()


def _moe_dispatch(h2, pos_tiles, pad_start, pad_len, n_valid, n_slots, tm, tm_slot):
    t, d = h2.shape
    nt = t // tm
    n_exp = pad_start.shape[0]
    grid_spec = pltpu.PrefetchScalarGridSpec(
        num_scalar_prefetch=3,
        grid=(nt,),
        in_specs=[pl.BlockSpec((None, 1, TOP_K * tm), lambda i, *_: (i, 0, 0), memory_space=pltpu.SMEM),
                  pl.BlockSpec((tm, d), lambda i, *_: (i, 0))],
        out_specs=pl.BlockSpec(memory_space=pl.ANY),
        scratch_shapes=[pltpu.VMEM((tm_slot, d), F32), pltpu.SemaphoreType.DMA((2,))],
    )
    return pl.pallas_call(
        functools.partial(_moe_dispatch_kernel, tm=tm, n_exp=n_exp),
        grid_spec=grid_spec,
        out_shape=jax.ShapeDtypeStruct((n_slots, d), F32),
        compiler_params=_cparams("arbitrary"),
        name="moe_dispatch",
    )(pad_start, pad_len, n_valid, pos_tiles.reshape(nt, 1, TOP_K * tm), h2)


def _moe_expert_kernel(texp_ref, nvalid_ref, x_ref, wg_ref, wu_ref, wd_ref, o_ref):
    i = pl.program_id(0)

    @pl.when(i < nvalid_ref[0])
    def _():
        h = x_ref[...].astype(BF16)
        act = (_silu(_dot(h, wg_ref[...])) * _dot(h, wu_ref[...])).astype(BF16)
        o_ref[...] = _dot(act, wd_ref[...])

    @pl.when(i >= nvalid_ref[0])
    def _():
        o_ref[...] = jnp.zeros(o_ref.shape, F32)


def _moe_experts(x_sorted, tile_expert, n_valid, wg, wu, wd, tm):
    n_tiles = tile_expert.shape[0]
    d = x_sorted.shape[1]
    n_exp, _, fe = wg.shape
    grid_spec = pltpu.PrefetchScalarGridSpec(
        num_scalar_prefetch=2,
        grid=(n_tiles,),
        in_specs=[
            pl.BlockSpec((tm, d), lambda i, te, nv: (jnp.minimum(i, nv[0] - 1), 0)),
            pl.BlockSpec((None, d, fe), lambda i, te, nv: (te[i], 0, 0)),
            pl.BlockSpec((None, d, fe), lambda i, te, nv: (te[i], 0, 0)),
            pl.BlockSpec((None, fe, d), lambda i, te, nv: (te[i], 0, 0)),
        ],
        out_specs=pl.BlockSpec((tm, d), lambda i, te, nv: (i, 0)),
    )
    return pl.pallas_call(
        _moe_expert_kernel,
        grid_spec=grid_spec,
        out_shape=jax.ShapeDtypeStruct((n_tiles * tm, d), F32),
        compiler_params=_cparams("arbitrary"),
        name="moe_experts",
    )(tile_expert, n_valid, x_sorted, wg, wu, wd)


def _start_row_gather(idx_ref, src_hbm, dst, sem, count):
    for r in range(count):
        _row_copy(src_hbm, idx_ref[0, r], dst, r, sem).start()


def _wait_row_gather(src_hbm, dst, sem):
    pltpu.make_async_copy(src_hbm.at[pl.ds(0, dst.shape[0])], dst, sem).wait()


def _moe_combine_kernel(pos0_ref, posnext_ref, ye_hbm, sel_ref, x1_ref, qg_ref, rg_ref, npost_ref, *rest,
                        geo, split_out):
    outs, (ybuf, sems) = rest[:-2], rest[-2:]
    i = pl.program_id(0)
    tm = geo.tm
    slot = i % 2

    @pl.when(i == 0)
    def _():
        _start_row_gather(pos0_ref, ye_hbm, ybuf.at[0], sems.at[0], 2 * tm)

    @pl.when(i + 1 < geo.nt)
    def _():
        _start_row_gather(posnext_ref, ye_hbm, ybuf.at[1 - slot], sems.at[1 - slot], 2 * tm)

    _wait_row_gather(ye_hbm, ybuf.at[slot], sems.at[slot])
    is_s = i == geo.nt - 1
    sel = sel_ref[...]
    lane = lax.broadcasted_iota(jnp.int32, sel.shape, 1)
    w1 = jnp.sum(jnp.where(lane == 2, sel, 0.0), axis=-1, keepdims=True)
    w2 = jnp.sum(jnp.where(lane == 3, sel, 0.0), axis=-1, keepdims=True)
    f = w1 * ybuf[slot, 0:tm, :] + w2 * ybuf[slot, tm:2 * tm, :]
    val = x1_ref[...] + _mod(is_s, qg_ref, rg_ref, geo.reps) * _rms(f, npost_ref[...])
    if split_out:
        _write_split(is_s, val, *outs)
    else:
        outs[0][...] = val


def _moe_combine(ye, pos, sel, x1, seq_mod, row_mod, geo_args, npost, split_out):
    tm, nt, tps, reps, d = geo_args
    geo = _Rows(tm, nt, tps, reps, d, 0)
    out_specs, out_shape = _out_rows(geo, d, split_out)
    pos3 = pos.reshape(nt, 1, 2 * tm)
    smem_blk = lambda fn: pl.BlockSpec((None, 1, 2 * tm), fn, memory_space=pltpu.SMEM)
    return pl.pallas_call(
        functools.partial(_moe_combine_kernel, geo=geo, split_out=split_out),
        grid=(nt,),
        in_specs=[smem_blk(lambda i: (0, 0, 0)),
                  smem_blk(lambda i: (jnp.minimum(i + 1, nt - 1), 0, 0)),
                  pl.BlockSpec(memory_space=pl.ANY),
                  geo.rows(LANES), geo.rows(d), geo.seq_mod(5), geo.row_mod(5, row_mod.shape[0]),
                  pl.BlockSpec((1, d), lambda i: (0, 0))],
        out_specs=out_specs,
        out_shape=out_shape,
        scratch_shapes=[pltpu.VMEM((2, 2 * tm, d), F32), pltpu.SemaphoreType.DMA((2,))],
        compiler_params=_cparams("arbitrary"),
        name="moe_combine",
    )(pos3, pos3, ye, sel, x1, seq_mod, row_mod, npost.reshape(1, d))


def _moe_tables(sel, counts, n_exp, tm_tok, tm_slot):
    t = sel.shape[0]
    counts = counts[0, :n_exp].astype(jnp.int32)
    tiles_e = (counts + tm_slot - 1) // tm_slot
    tile_end = jnp.cumsum(tiles_e)
    slot_start = (tile_end - tiles_e) * tm_slot
    choice = sel[:, 0:TOP_K].astype(jnp.int32)
    rank = sel[:, 2 * TOP_K:3 * TOP_K].astype(jnp.int32)
    onehot = choice[:, :, None] == jnp.arange(n_exp, dtype=jnp.int32)[None, None, :]
    pos = jnp.sum(jnp.where(onehot, slot_start[None, None, :], 0), axis=-1) + rank
    n_tiles = (TOP_K * t + n_exp * (tm_slot - 1)) // tm_slot
    tile_expert = jnp.minimum(
        jnp.sum((jnp.arange(n_tiles, dtype=jnp.int32)[:, None] >= tile_end[None, :]).astype(jnp.int32), axis=1),
        n_exp - 1)
    pos_tiles = jnp.concatenate([pos[:, k].reshape(-1, tm_tok) for k in range(TOP_K)], axis=1)
    pad_start = slot_start + counts
    pad_len = tiles_e * tm_slot - counts
    return tile_expert, tile_end[-1:].astype(jnp.int32), pos_tiles, pad_start, pad_len, n_tiles * tm_slot


def _largest_tile(t, want):
    tm = min(want, t)
    while t % tm:
        tm //= 2
    return tm


def kernel(x_prompt, x_sample, c_prompt, c_sample, state_conva, state_convb, state_ssm, ada_w, ada_b, norm_pre_mix,
           norm_post_mix, norm_pre_ffn, norm_post_ffn, w_in, w_out, conva_w, convb_w, convb_b, dt_bias, a_log, d_skip,
           ssd_norm, ffd_w_gate, ffd_w_up, ffd_w_down, moe_router, moe_w_gate, moe_w_up, moe_w_down):
    bp, seq, d = x_prompt.shape
    bs, steps, _ = x_sample.shape
    depth = w_in.shape[0]
    dc = conva_w.shape[-1]
    dx = convb_w.shape[-1]
    ds = ssd_norm.shape[-1]
    heads = dt_bias.shape[-1]
    p = ds // heads
    n = (dx - ds) // (2 * SSD_GROUPS)
    hg = heads // SSD_GROUPS
    n_exp = moe_router.shape[-1]
    ka, kb = conva_w.shape[1] - 1, convb_w.shape[1] - 1
    tm = steps * bs
    t_p = bp * seq
    assert seq % SSD_CHUNK == 0 and LANES % p == 0 and (hg * p) % LANES == 0 and n == LANES
    assert heads <= LANES and n_exp <= LANES and dc % LANES == 0 and dx % LANES == 0 and d % LANES == 0
    assert w_in.shape[-1] == 3 * dc + ds + dx + heads and seq % tm == 0 and tm % SUBLANES == 0
    assert bs % SUBLANES == 0 and (t_p // bs) % steps == 0
    nt = t_p // tm + 1
    geo_args = (tm, nt, seq // tm, steps, d)

    w_dt = jnp.pad(w_in[:, :, 3 * dc + ds + dx:], ((0, 0), (0, 0), (0, LANES - heads))).astype(BF16)
    moe_g, moe_u, moe_d = moe_w_gate.astype(BF16), moe_w_up.astype(BF16), moe_w_down.astype(BF16)
    router_p = jnp.pad(moe_router, ((0, 0), (0, 0), (0, LANES - n_exp)))
    padh = lambda a: jnp.pad(a, ((0, 0), (0, LANES - heads))).reshape(depth, 1, LANES)
    dtb_p, alog_p = padh(dt_bias), padh(a_log)
    dsk_e = jnp.repeat(d_skip, p, axis=-1).reshape(depth, 1, ds)
    hot = (jnp.arange(LANES)[:, None] == (jnp.arange(ds)[None, :] // p)).astype(BF16)
    emat = jnp.concatenate([hot, hot], axis=0)

    mod = _adaln(jnp.concatenate([c_prompt, c_sample], axis=0), ada_w, ada_b)
    seq_mod = jnp.pad(mod[:, :bp], ((0, 0), (0, 1), (0, 0))).reshape(depth, bp + 1, 1, 6 * d)
    row_mod = mod[:, bp:]

    bb = _largest_tile(bs, 16)
    x_all = None
    xs_tm = x_sample.transpose(1, 0, 2).reshape(tm, d)
    xp2d = x_prompt.reshape(t_p, d)
    sa_all = state_conva.reshape(depth, bs, ka * dc)
    sb_all = state_convb.reshape(depth, bs, kb * dx)
    ss_all = state_ssm.reshape(depth, bs, ds, n)
    pa, pb, ps, sa_l, sb_l = [], [], [], [], []
    ss_new = jnp.zeros((depth, bs, ds, n), F32)
    for i in range(depth):
        j = i // 2
        last = i == depth - 1
        x_in = (xp2d, xs_tm) if x_all is None else (x_all, None)
        conv_params = [conva_w[i], convb_w[i], convb_b[i].reshape(1, dx)]
        ya, z, xc, dtr, na, nb, gb_s, u_s, xbc_s = _inproj(
            *x_in, seq_mod[i], row_mod[i], geo_args, norm_pre_mix[i], w_in, i, w_dt[i], *conv_params, bp, ds)
        params = [dtb_p[i], alog_p[i], dsk_e[i], ssd_norm[i].reshape(1, ds), emat]
        yb_p, ns = _ssd_prompt(z, xc, dtr, bp, seq, params, n, p, hg)
        r3 = lambda a: a.reshape(a.shape[0] // bs, bs, a.shape[-1])
        ya_s, yb_s, sna, snb, ss_new = _ssd_sample(
            r3(gb_s), r3(u_s), r3(xbc_s), r3(z), r3(dtr), t_p // (bs * steps), steps, bs, i, sa_all, sb_all, ss_all,
            ss_new, conv_params + params, n, p, hg, bb)
        pa.append(na)
        pb.append(nb)
        ps.append(ns.reshape(bp, heads, p, n))
        sa_l.append(sna.reshape(bs, ka, dc))
        sb_l.append(snb.reshape(bs, kb, dx))

        moe = i % 2 == 1
        res = _outproj(ya, ya_s.reshape(tm, dc), yb_p, yb_s.reshape(tm, ds), w_out, i, *x_in, seq_mod[i], row_mod[i],
                       geo_args,
                       norm_post_mix[i], norm_pre_ffn[i], router_p[j] if moe else None, n_exp if moe else 0)
        if moe:
            x1, h2, sel, counts = res
            tile_expert, n_valid, pos_tiles, pad_start, pad_len, n_slots = _moe_tables(sel, counts, n_exp, tm, MOE_TILE)
            x_sorted = _moe_dispatch(h2, pos_tiles, pad_start, pad_len, n_valid, n_slots, tm, MOE_TILE)
            ye = _moe_experts(x_sorted, tile_expert, n_valid, moe_g[j], moe_u[j], moe_d[j], MOE_TILE)
            out = _moe_combine(ye, pos_tiles, sel, x1, seq_mod[i], row_mod[i], geo_args, norm_post_ffn[i], last)
        else:
            x1, h2 = res
            out = _ffn_dense(h2, ffd_w_gate, ffd_w_up, ffd_w_down, j, x1, seq_mod[i], row_mod[i], geo_args,
                             norm_post_ffn[i], last)
        if last:
            y_p, y_s = out
        else:
            x_all = out[0]

    y_prompt = y_p.reshape(bp, seq, d)
    y_sample = y_s.reshape(steps, bs, d).transpose(1, 0, 2)
    return (y_prompt, y_sample, jnp.stack(pa), jnp.stack(pb), jnp.stack(ps),
            jnp.stack(sa_l), jnp.stack(sb_l), ss_new.reshape(depth, bs, heads, p, n))
```

```python
import functools

import jax
import jax.numpy as jnp
from jax import lax
from jax.experimental import pallas as pl
from jax.experimental.pallas import tpu as pltpu

EPS = 1e-6
SSD_GROUPS = 2
SSD_CHUNK = 128
TOP_K = 2
LOG2E = 1.4426950408889634
LANES = 128
SUBLANES = 8
VMEM_LIMIT_BYTES = 56 * 1024 * 1024
MOE_TILE = 512

F32 = jnp.float32
BF16 = jnp.bfloat16


def _cparams(*sem):
    return pltpu.CompilerParams(dimension_semantics=sem, vmem_limit_bytes=VMEM_LIMIT_BYTES)


def _resident(shape):
    return pl.BlockSpec(shape, lambda *_: (0,) * len(shape), pipeline_mode=pl.Buffered(1))


def _silu(x):
    return x * (1.0 / (1.0 + jnp.exp(-x)))


def _softplus(x):
    return jnp.maximum(x, 0.0) + jnp.log1p(jnp.exp(-jnp.abs(x)))


def _rms(x, g):
    return x * lax.rsqrt(jnp.mean(x * x, axis=-1, keepdims=True) + EPS) * g


def _dot(a, b):
    return jnp.dot(a, b, preferred_element_type=F32)


def _dot_nt(a, b):
    return lax.dot_general(a, b, (((1,), (1,)), ((), ())), preferred_element_type=F32)


def _dot_tn(a, b):
    return lax.dot_general(a, b, (((0,), (0,)), ((), ())), preferred_element_type=F32)


def _chunk(*widths):
    return next(c for c in (512, 256, LANES) if all(w % c == 0 for w in widths))


def _adaln_kernel(c_ref, w_ref, b_ref, o_ref):
    s = _silu(c_ref[...]).astype(BF16)
    o_ref[...] = _dot(s, w_ref[...].astype(BF16)) + b_ref[...]


def _adaln(c_all, ada_w, ada_b):
    depth, d, d6 = ada_w.shape
    rows = c_all.shape[0]
    tn = _chunk(d6 // 6) * 2
    return pl.pallas_call(
        _adaln_kernel,
        grid=(depth, d6 // tn),
        in_specs=[
            pl.BlockSpec((rows, d), lambda l, j: (0, 0)),
            pl.BlockSpec((None, d, tn), lambda l, j: (l, 0, j)),
            pl.BlockSpec((None, 1, tn), lambda l, j: (l, 0, j)),
        ],
        out_specs=pl.BlockSpec((None, rows, tn), lambda l, j: (l, 0, j)),
        out_shape=jax.ShapeDtypeStruct((depth, rows, d6), F32),
        compiler_params=_cparams("arbitrary", "arbitrary"),
        name="adaln",
    )(c_all, ada_w, ada_b.reshape(depth, 1, d6))


class _Rows:
    def __init__(self, tm, nt, tps, reps, d, pro):
        self.tm, self.nt, self.tps, self.reps, self.d, self.pro = tm, nt, tps, reps, d, pro

    def tile(self, i):
        return jnp.maximum(i - self.pro, 0)

    def rows(self, w):
        return pl.BlockSpec((self.tm, w), lambda i, *_: (self.tile(i), 0))

    def prompt_rows(self, w):
        return pl.BlockSpec((self.tm, w), lambda i, *_: (jnp.minimum(self.tile(i), self.nt - 2), 0))

    def sample_rows(self, w):
        return pl.BlockSpec((self.tm, w), lambda i, *_: (0, 0))

    def seq_mod(self, k):
        return pl.BlockSpec((None, 1, self.d), lambda i, *_: (self.tile(i) // self.tps, 0, k))

    def row_mod(self, k, rows):
        return pl.BlockSpec((rows, self.d), lambda i, *_: (0, k))


def _mod(is_sample, seq_ref, row_ref, reps):
    rowm = jnp.concatenate([row_ref[...]] * reps, axis=0)
    return seq_ref[...] + jnp.where(is_sample, rowm, 0.0)


def _causal_conv_cols(cur, c0, w_ref, tail, ext, nst_ref, keep_state, first, emit):
    tm, cw = cur.shape
    kp = w_ref.shape[0] - 1
    wcol = lambda j: w_ref[j:j + 1, c0:c0 + cw]
    ext[0:SUBLANES, :] = jnp.where(first, 0.0, tail[:, c0:c0 + cw])
    ext[SUBLANES:SUBLANES + tm, :] = cur
    acc = wcol(kp) * cur
    for j in range(kp):
        off = SUBLANES - (kp - j)
        acc = acc + wcol(j) * ext[off:off + tm, :]
    tail[:, c0:c0 + cw] = ext[tm:tm + SUBLANES, :]
    nst_ref[:, c0:c0 + cw] = jnp.where(keep_state, nst_ref[:, c0:c0 + cw], ext[SUBLANES + tm - kp:SUBLANES + tm, :])
    emit(acc)


def _inproj_kernel(*refs, geo, dc, ds, dx, cw, split_x):
    if split_x:
        xp_ref, xs_ref, *refs = refs
    else:
        xp_ref, *refs = refs
    (qsh_ref, qsc_ref, rsh_ref, rsc_ref, g_ref, wch_ref, wdt_ref, caw_ref, cbw_ref, cbb_ref,
     ya_ref, z_ref, xc_ref, dt_ref, ncap_ref, ncbp_ref, gbs_ref, us_ref, xbcs_ref, wbf, ext, tail_u, tail_x) = refs
    i = pl.program_id(0)

    @pl.when(i < geo.pro)
    def _():
        wbf[i] = wch_ref[...].astype(BF16)

    @pl.when(i >= geo.pro)
    def _():
        r = i - geo.pro
        is_s = r == geo.nt - 1
        first = r % geo.tps == 0
        x = jnp.where(is_s, xs_ref[...], xp_ref[...]) if split_x else xp_ref[...]
        sc = _mod(is_s, qsc_ref, rsc_ref, geo.reps)
        sh = _mod(is_s, qsh_ref, rsh_ref, geo.reps)
        h = (_rms(x, g_ref[...]) * (1.0 + sc) + sh).astype(BF16)
        nc = dc // cw
        for a in range(nc):
            c0 = a * cw
            gate_b = _dot(h, wbf[a])
            u = _dot(h, wbf[nc + a]) * _dot(h, wbf[2 * nc + a])
            gbs_ref[:, c0:c0 + cw] = gate_b
            us_ref[:, c0:c0 + cw] = u

            def emit_a(v, c0=c0, gate_b=gate_b):
                ya_ref[:, c0:c0 + cw] = (gate_b * v).astype(BF16)

            _causal_conv_cols(u, c0, caw_ref, tail_u, ext, ncap_ref, is_s, first, emit_a)
        for a in range(ds // cw):
            z_ref[:, a * cw:(a + 1) * cw] = _dot(h, wbf[3 * nc + a])
        for a in range(dx // cw):
            c0 = a * cw
            xbc = _dot(h, wbf[3 * nc + ds // cw + a])
            xbcs_ref[:, c0:c0 + cw] = xbc

            def emit_b(v, c0=c0):
                xc_ref[:, c0:c0 + cw] = _silu(v + cbb_ref[:, c0:c0 + cw])

            _causal_conv_cols(xbc, c0, cbw_ref, tail_x, ext, ncbp_ref, is_s, first, emit_b)
        dt_ref[...] = _dot(h, wdt_ref[...])


def _inproj(x_p, x_s, seq_mod, row_mod, geo_args, norm_w, w_in, layer, w_dt, caw, cbw, cbb, bp, ds):
    tm, nt, tps, reps, d = geo_args
    dc, dx = caw.shape[1], cbw.shape[1]
    ka, kb = caw.shape[0] - 1, cbw.shape[0] - 1
    cw = _chunk(dc, ds, dx)
    npro = (3 * dc + ds + dx) // cw
    geo = _Rows(tm, nt, tps, reps, d, npro)
    t_all = tm * nt
    split_x = x_s is not None
    xin = [geo.prompt_rows(d), geo.sample_rows(d)] if split_x else [geo.rows(d)]
    xargs = [x_p, x_s] if split_x else [x_p]
    nrow = row_mod.shape[0]
    seq_blk = lambda k, w: pl.BlockSpec((None, k, w), lambda i: (jnp.minimum(geo.tile(i) // tps, bp - 1), 0, 0))
    return pl.pallas_call(
        functools.partial(_inproj_kernel, geo=geo, dc=dc, ds=ds, dx=dx, cw=cw, split_x=split_x),
        grid=(npro + nt,),
        in_specs=xin + [
            geo.seq_mod(0), geo.seq_mod(1), geo.row_mod(0, nrow), geo.row_mod(1, nrow),
            pl.BlockSpec((1, d), lambda i: (0, 0)),
            pl.BlockSpec((None, d, cw), lambda i: (layer, 0, jnp.minimum(i, npro - 1))),
            _resident((d, LANES)), _resident(caw.shape), _resident(cbw.shape), _resident(cbb.shape),
        ],
        out_specs=[geo.rows(dc), geo.rows(ds), geo.rows(dx), geo.rows(LANES), seq_blk(ka, dc), seq_blk(kb, dx),
                   geo.sample_rows(dc), geo.sample_rows(dc), geo.sample_rows(dx)],
        out_shape=[
            jax.ShapeDtypeStruct((t_all, dc), BF16),
            jax.ShapeDtypeStruct((t_all, ds), F32),
            jax.ShapeDtypeStruct((t_all, dx), F32),
            jax.ShapeDtypeStruct((t_all, LANES), F32),
            jax.ShapeDtypeStruct((bp, ka, dc), F32),
            jax.ShapeDtypeStruct((bp, kb, dx), F32),
            jax.ShapeDtypeStruct((tm, dc), F32),
            jax.ShapeDtypeStruct((tm, dc), F32),
            jax.ShapeDtypeStruct((tm, dx), F32),
        ],
        scratch_shapes=[pltpu.VMEM((npro, d, cw), BF16), pltpu.VMEM((SUBLANES + tm, cw), F32),
                        pltpu.VMEM((SUBLANES, dc), F32), pltpu.VMEM((SUBLANES, dx), F32)],
        compiler_params=_cparams("arbitrary"),
        name="inproj",
    )(*xargs, seq_mod, seq_mod, row_mod, row_mod, norm_w.reshape(1, d), w_in, w_dt, caw, cbw, cbb)


def _split_hi_lo(v):
    hi = v.astype(BF16)
    lo = (v - hi.astype(F32)).astype(BF16)
    return jnp.concatenate([hi, lo], axis=1)


def _diag_block(cb, mask, cum, cum_t, dt_t, xs, lane, g, hg, gw, p):
    hpl = LANES // p
    parts = []
    for slab in range(gw // LANES):
        lhs, rhs = [], []
        xslab = xs[:, g * gw + slab * LANES:g * gw + (slab + 1) * LANES]
        for j in range(hpl):
            h = g * hg + slab * hpl + j
            seg = cum[:, h:h + 1] - cum_t[h:h + 1, :]
            m = cb * jnp.where(mask, jnp.exp2(seg), 0.0) * dt_t[h:h + 1, :]
            lhs.append(m.astype(BF16))
            rhs.append(jnp.where((lane >= j * p) & (lane < (j + 1) * p), xslab, 0.0).astype(BF16))
        parts.append(_dot(jnp.concatenate(lhs, axis=1), jnp.concatenate(rhs, axis=0)))
    return jnp.concatenate(parts, axis=1)


def _gated_group_norm(y, xs_g, z_g, dsk_g, snorm_g):
    y = (y + dsk_g * xs_g) * _silu(z_g)
    return (y * lax.rsqrt(jnp.mean(y * y, axis=-1, keepdims=True) + EPS) * snorm_g).astype(BF16)


def _ssd_prompt_kernel(z_ref, xc_ref, dt_ref, dtb_ref, alog_ref, dsk_ref, snorm_ref, e_ref, yb_ref, nss_ref, st,
                       *, ds, n, p, hg, sub):
    c = pl.program_id(1)
    q = SSD_CHUNK
    gw = hg * p

    @pl.when(c == 0)
    def _():
        st[...] = jnp.zeros(st.shape, F32)

    row = lax.broadcasted_iota(jnp.int32, (q, LANES), 0)
    tril = lax.broadcasted_iota(jnp.int32, (q, q), 0) >= lax.broadcasted_iota(jnp.int32, (q, q), 1)
    lane = lax.broadcasted_iota(jnp.int32, (q, LANES), 1)
    neg_a = -jnp.exp(alog_ref[...]) * LOG2E
    for k in range(sub):
        rs = slice(k * q, (k + 1) * q)
        xs = xc_ref[rs, 0:ds]
        dt = _softplus(dt_ref[rs, :] + dtb_ref[...])
        cum = dt * neg_a
        step = 1
        while step < q:
            cum = cum + jnp.where(row >= step, pltpu.roll(cum, step, axis=0), 0.0)
            step *= 2
        cum_last = cum[q - 1:q, :]
        w_exp = _dot(_split_hi_lo(dt * jnp.exp2(cum_last - cum)), e_ref[...])
        ecum_exp = _dot(_split_hi_lo(jnp.exp2(cum)), e_ref[...])
        cum_t = cum.T
        dt_t = dt.T
        for g in range(SSD_GROUPS):
            gsl = slice(g * gw, (g + 1) * gw)
            bm = xc_ref[rs, ds + g * n:ds + (g + 1) * n]
            cm = xc_ref[rs, ds + SSD_GROUPS * n + g * n:ds + SSD_GROUPS * n + (g + 1) * n].astype(BF16)
            bm_t = bm.T.astype(BF16)
            y = _diag_block(_dot(cm, bm_t), tril, cum, cum_t, dt_t, xs, lane, g, hg, gw, p)
            s_prev = st[g]
            y = y + _dot(cm, s_prev.astype(BF16)) * ecum_exp[:, gsl]
            xw = (xs[:, gsl] * w_exp[:, gsl]).astype(BF16)
            st[g] = s_prev * ecum_exp[q - 1:q, gsl] + _dot(bm_t, xw)
            yb_ref[rs, gsl] = _gated_group_norm(y, xs[:, gsl], z_ref[rs, gsl], dsk_ref[:, gsl], snorm_ref[:, gsl])

    @pl.when(c == pl.num_programs(1) - 1)
    def _():
        for g in range(SSD_GROUPS):
            for slab in range(gw // LANES):
                r0 = g * gw + slab * LANES
                nss_ref[r0:r0 + LANES, :] = st[g, :, slab * LANES:(slab + 1) * LANES].T


def _ssd_prompt(z, xc, dtr, bsz, seq, params, n, p, hg):
    ds = z.shape[1]
    dx = xc.shape[1]
    sub = next(k for k in (4, 2, 1) if seq % (k * SSD_CHUNK) == 0)
    rows = sub * SSD_CHUNK
    nc = seq // rows
    row = lambda w: pl.BlockSpec((rows, w), lambda b, c: (b * nc + c, 0))
    return pl.pallas_call(
        functools.partial(_ssd_prompt_kernel, ds=ds, n=n, p=p, hg=hg, sub=sub),
        grid=(bsz, nc),
        in_specs=[row(ds), row(dx), row(LANES)] + [_resident(a.shape) for a in params],
        out_specs=[row(ds), pl.BlockSpec((None, ds, n), lambda b, c: (b, 0, 0))],
        out_shape=[jax.ShapeDtypeStruct((bsz * seq, ds), BF16), jax.ShapeDtypeStruct((bsz, ds, n), F32)],
        scratch_shapes=[pltpu.VMEM((SSD_GROUPS, n, hg * p), F32)],
        compiler_params=_cparams("arbitrary", "arbitrary"),
        name="ssd_prompt",
    )(z, xc, dtr, *params)


def _ssd_sample_kernel(gb_ref, u_ref, xbc_ref, z_ref, dt_ref, sa_ref, sb_ref, ss_ref, caw_ref, cbw_ref, cbb_ref,
                       dtb_ref, alog_ref, dsk_ref, snorm_ref, e_ref, ya_ref, yb_ref, nca_ref, ncb_ref, nss_ref,
                       *, dc, ds, dx, n, p, hg, steps, bb):
    gw = hg * p
    rows = steps * bb
    ka = caw_ref.shape[0] - 1
    kb = cbw_ref.shape[0] - 1

    def conv(cur_ref, st_ref, w_ref, kprev, width):
        hist = [st_ref[:, j * width:(j + 1) * width] for j in range(kprev)] + [cur_ref[t] for t in range(steps)]
        outs = []
        for t in range(steps):
            acc = w_ref[kprev:kprev + 1, :] * hist[t + kprev]
            for j in range(kprev):
                acc = acc + w_ref[j:j + 1, :] * hist[t + j]
            outs.append(acc)
        return outs, hist[len(hist) - kprev:]

    v, new_a = conv(u_ref, sa_ref, caw_ref, ka, dc)
    for t in range(steps):
        ya_ref[t] = (gb_ref[t] * v[t]).astype(BF16)
    for j in range(ka):
        nca_ref[:, j * dc:(j + 1) * dc] = new_a[j]
    xcs, new_b = conv(xbc_ref, sb_ref, cbw_ref, kb, dx)
    for j in range(kb):
        ncb_ref[:, j * dx:(j + 1) * dx] = new_b[j]
    xc = _silu(jnp.concatenate(xcs, axis=0) + cbb_ref[...])
    xs = xc[:, 0:ds]

    dt = _softplus(jnp.concatenate([dt_ref[t] for t in range(steps)], axis=0) + dtb_ref[...])
    da = dt * (-jnp.exp(alog_ref[...]) * LOG2E)
    cums = [da[0:bb]]
    for t in range(1, steps):
        cums.append(cums[-1] + da[t * bb:(t + 1) * bb])
    cum = jnp.concatenate(cums, axis=0)
    cum_last = jnp.concatenate([cums[-1]] * steps, axis=0)
    w_exp = _dot(_split_hi_lo(dt * jnp.exp2(cum_last - cum)), e_ref[...])
    ecum_exp = _dot(_split_hi_lo(jnp.exp2(cum)), e_ref[...])

    def pad_t(a):
        a = jnp.concatenate([a, jnp.zeros((LANES - rows, LANES), F32)], axis=0) if rows < LANES else a
        return a.T[:, 0:rows]

    cum_t = pad_t(cum)
    dt_t = pad_t(dt)

    ri = lax.broadcasted_iota(jnp.int32, (rows, rows), 0)
    ci = lax.broadcasted_iota(jnp.int32, (rows, rows), 1)
    same = ((ri % bb) == (ci % bb)) & (ri >= ci)
    lane = lax.broadcasted_iota(jnp.int32, (rows, LANES), 1)
    rowid = lax.broadcasted_iota(jnp.int32, (rows, 1), 0) % bb
    seqlane = lax.broadcasted_iota(jnp.int32, (LANES, LANES), 1)
    nslab = gw // LANES
    for g in range(SSD_GROUPS):
        gsl = slice(g * gw, (g + 1) * gw)
        bm = xc[:, ds + g * n:ds + (g + 1) * n].astype(BF16)
        cm = xc[:, ds + SSD_GROUPS * n + g * n:ds + SSD_GROUPS * n + (g + 1) * n].astype(BF16)
        y_diag = _diag_block(_dot_nt(cm, bm), same, cum, cum_t, dt_t, xs, lane, g, hg, gw, p)
        xw = (xs[:, gsl] * w_exp[:, gsl]).astype(BF16)
        dec = ecum_exp[(steps - 1) * bb:steps * bb, gsl]
        dec = jnp.concatenate([dec, jnp.zeros((LANES - bb, gw), F32)], axis=0)
        dec_t = [dec[:, s * LANES:(s + 1) * LANES].T for s in range(nslab)]

        def per_seq(b, y_off, g=g, cm=cm, bm=bm, xw=xw, dec_t=dec_t):
            r0 = g * gw
            s0 = ss_ref[b, r0:r0 + gw, :]
            y_off = jnp.where(rowid == b, _dot_nt(cm, s0.astype(BF16)), y_off)
            upd = _dot_tn(jnp.where(rowid == b, xw, jnp.zeros_like(xw)), bm)
            for s in range(nslab):
                dcol = jnp.sum(jnp.where(seqlane == b, dec_t[s], 0.0), axis=1, keepdims=True)
                nss_ref[b, r0 + s * LANES:r0 + (s + 1) * LANES, :] = (
                    s0[s * LANES:(s + 1) * LANES, :] * dcol + upd[s * LANES:(s + 1) * LANES, :])
            return y_off

        y_off = lax.fori_loop(0, bb, per_seq, jnp.zeros((rows, gw), F32), unroll=4)
        zg = jnp.concatenate([z_ref[t, :, gsl] for t in range(steps)], axis=0)
        yn = _gated_group_norm(y_diag + y_off * ecum_exp[:, gsl], xs[:, gsl], zg, dsk_ref[:, gsl], snorm_ref[:, gsl])
        for t in range(steps):
            yb_ref[t, :, gsl] = yn[t * bb:(t + 1) * bb]


def _ssd_sample_aliased_kernel(*refs, **kw):
    _ssd_sample_kernel(*refs[1:], **kw)


def _ssd_sample(gb, u, xbc, z, dtr, blk0, steps, bs, layer, sa, sb, ss, prev_ss, params, n, p, hg, bb):
    dc, ds, dx = u.shape[2], z.shape[2], xbc.shape[2]
    ka, kb = params[0].shape[0] - 1, params[1].shape[0] - 1
    own = lambda w: pl.BlockSpec((steps, bb, w), lambda i: (0, i, 0))
    tok = lambda w: pl.BlockSpec((steps, bb, w), lambda i: (blk0, i, 0))
    state = pl.BlockSpec((None, bb, ds, n), lambda i: (layer, i, 0, 0))
    return pl.pallas_call(
        functools.partial(_ssd_sample_aliased_kernel, dc=dc, ds=ds, dx=dx, n=n, p=p, hg=hg, steps=steps, bb=bb),
        grid=(bs // bb,),
        in_specs=[pl.BlockSpec(memory_space=pl.ANY), own(dc), own(dc), own(dx), tok(ds), tok(LANES),
                  pl.BlockSpec((None, bb, ka * dc), lambda i: (layer, i, 0)),
                  pl.BlockSpec((None, bb, kb * dx), lambda i: (layer, i, 0)), state]
        + [_resident(a.shape) for a in params],
        out_specs=[own(dc), own(ds), pl.BlockSpec((bb, ka * dc), lambda i: (i, 0)),
                   pl.BlockSpec((bb, kb * dx), lambda i: (i, 0)), state],
        out_shape=[jax.ShapeDtypeStruct((steps, bs, dc), BF16), jax.ShapeDtypeStruct((steps, bs, ds), BF16),
                   jax.ShapeDtypeStruct((bs, ka * dc), F32), jax.ShapeDtypeStruct((bs, kb * dx), F32),
                   jax.ShapeDtypeStruct(prev_ss.shape, F32)],
        input_output_aliases={0: 4},
        compiler_params=_cparams("arbitrary"),
        name="ssd_sample",
    )(prev_ss, gb, u, xbc, z, dtr, sa, sb, ss, *params)


def _route(logits, n_exp, cnt_ref):
    rows = logits.shape[0]
    lane = lax.broadcasted_iota(jnp.int32, logits.shape, 1).astype(F32)
    valid = lane < n_exp
    logits = jnp.where(valid, logits, -jnp.inf)
    e = jnp.exp(logits - jnp.max(logits, axis=-1, keepdims=True))
    prob = jnp.where(valid, e / jnp.sum(e, axis=-1, keepdims=True), -1.0)
    big = float(LANES)
    m1 = jnp.max(prob, axis=-1, keepdims=True)
    i1 = jnp.min(jnp.where(prob == m1, lane, big), axis=-1, keepdims=True)
    rest = jnp.where(lane == i1, -1.0, prob)
    m2 = jnp.max(rest, axis=-1, keepdims=True)
    i2 = jnp.min(jnp.where(rest == m2, lane, big), axis=-1, keepdims=True)
    den = m1 + m2
    chosen = jnp.where((lane == i1) | (lane == i2), 1.0, 0.0)
    before = (lax.broadcasted_iota(jnp.int32, (rows, rows), 0) > lax.broadcasted_iota(jnp.int32, (rows, rows), 1))
    rank = _dot(jnp.where(before, 1.0, 0.0).astype(BF16), chosen.astype(BF16)) + cnt_ref[0:1, :]
    r1 = jnp.sum(jnp.where(lane == i1, rank, 0.0), axis=-1, keepdims=True)
    r2 = jnp.sum(jnp.where(lane == i2, rank, 0.0), axis=-1, keepdims=True)
    cnt_ref[0:1, :] = cnt_ref[0:1, :] + jnp.sum(chosen, axis=0, keepdims=True)
    cols = [i1, i2, m1 / den, m2 / den, r1, r2]
    out = jnp.zeros(logits.shape, F32)
    for k, col in enumerate(cols):
        out = jnp.where(lane == float(k), col, out)
    return out


def _outproj_kernel(yap_ref, yas_ref, ybp_ref, ybs_ref, wch_ref, *rest, geo, n_exp, kc, split_x):
    if split_x:
        xp_ref, xs_ref, *rest = rest
    else:
        xp_ref, *rest = rest
    qg_ref, qsh_ref, qsc_ref, rg_ref, rsh_ref, rsc_ref, npost_ref, npre_ref, *rest = rest
    if n_exp:
        r_ref, x1_ref, h2_ref, sel_ref, cnt_ref, wbf = rest
    else:
        x1_ref, h2_ref, wbf = rest
    i = pl.program_id(0)

    @pl.when(i < geo.pro)
    def _():
        wbf[pl.ds(pl.multiple_of(i * kc, kc), kc), :] = wch_ref[...].astype(BF16)
        if n_exp:
            cnt_ref[...] = jnp.zeros(cnt_ref.shape, F32)

    @pl.when(i >= geo.pro)
    def _():
        is_s = i - geo.pro == geo.nt - 1
        da = yap_ref.shape[1]
        mix = (_dot(jnp.where(is_s, yas_ref[...], yap_ref[...]), wbf[0:da, :])
               + _dot(jnp.where(is_s, ybs_ref[...], ybp_ref[...]), wbf[da:, :]))
        x = jnp.where(is_s, xs_ref[...], xp_ref[...]) if split_x else xp_ref[...]
        x1 = x + _mod(is_s, qg_ref, rg_ref, geo.reps) * _rms(mix, npost_ref[...])
        x1_ref[...] = x1
        h2 = (_rms(x1, npre_ref[...]) * (1.0 + _mod(is_s, qsc_ref, rsc_ref, geo.reps))
              + _mod(is_s, qsh_ref, rsh_ref, geo.reps))
        h2_ref[...] = h2.astype(h2_ref.dtype)
        if n_exp:
            h_hi = h2.astype(BF16)
            h_lo = (h2 - h_hi.astype(F32)).astype(BF16)
            r = r_ref[...]
            r_hi = r.astype(BF16)
            r_lo = (r - r_hi.astype(F32)).astype(BF16)
            logits = _dot(h_hi, r_hi) + (_dot(h_lo, r_hi) + _dot(h_hi, r_lo))
            sel_ref[...] = _route(logits, n_exp, cnt_ref)


def _outproj(ya_p, ya_s, yb_p, yb_s, w_out, layer, x_p, x_s, seq_mod, row_mod, geo_args, npost, npre, router,
             n_exp):
    tm, nt, tps, reps, d = geo_args
    da, db = ya_p.shape[1], yb_p.shape[1]
    dm = da + db
    kc = _chunk(dm) // 2
    npro = dm // kc
    geo = _Rows(tm, nt, tps, reps, d, npro)
    t_all = tm * nt
    nrow = row_mod.shape[0]
    vec = pl.BlockSpec((1, d), lambda i: (0, 0))
    split_x = x_s is not None
    in_specs = [geo.prompt_rows(da), geo.sample_rows(da), geo.prompt_rows(db), geo.sample_rows(db),
                pl.BlockSpec((None, kc, d), lambda i: (layer, jnp.minimum(i, npro - 1), 0))]
    in_specs += [geo.prompt_rows(d), geo.sample_rows(d)] if split_x else [geo.rows(d)]
    in_specs += [geo.seq_mod(2), geo.seq_mod(3), geo.seq_mod(4),
                 geo.row_mod(2, nrow), geo.row_mod(3, nrow), geo.row_mod(4, nrow), vec, vec]
    args = [ya_p, ya_s, yb_p, yb_s, w_out] + ([x_p, x_s] if split_x else [x_p])
    args += [seq_mod, seq_mod, seq_mod, row_mod, row_mod, row_mod, npost.reshape(1, d), npre.reshape(1, d)]
    out_specs = [geo.rows(d), geo.rows(d)]
    out_shape = [jax.ShapeDtypeStruct((t_all, d), F32), jax.ShapeDtypeStruct((t_all, d), F32 if n_exp else BF16)]
    if n_exp:
        in_specs.append(_resident((d, LANES)))
        args.append(router)
        out_specs += [geo.rows(LANES), pl.BlockSpec((SUBLANES, LANES), lambda i: (0, 0))]
        out_shape += [jax.ShapeDtypeStruct((t_all, LANES), F32), jax.ShapeDtypeStruct((SUBLANES, LANES), F32)]
    return pl.pallas_call(
        functools.partial(_outproj_kernel, geo=geo, n_exp=n_exp, kc=kc, split_x=split_x),
        grid=(npro + nt,),
        in_specs=in_specs,
        out_specs=out_specs,
        out_shape=out_shape,
        scratch_shapes=[pltpu.VMEM((dm, d), BF16)],
        compiler_params=_cparams("arbitrary"),
        name="outproj",
    )(*args)


def _write_split(is_s, val, outp_ref, outs_ref):
    @pl.when(jnp.logical_not(is_s))
    def _():
        outp_ref[...] = val

    @pl.when(is_s)
    def _():
        outs_ref[...] = val


def _ffn_kernel(h_ref, wg_ref, wu_ref, wd_ref, x1_ref, qg_ref, rg_ref, npost_ref, *rest, geo, cf, split_out):
    outs, (wgb, wub, wdb, act) = rest[:-4], rest[-4:]
    i = pl.program_id(0)

    @pl.when(i < geo.pro)
    def _():
        wgb[i] = wg_ref[...].astype(BF16)
        wub[i] = wu_ref[...].astype(BF16)
        wdb[pl.ds(pl.multiple_of(i * cf, cf), cf), :] = wd_ref[...].astype(BF16)

    @pl.when(i >= geo.pro)
    def _():
        is_s = i - geo.pro == geo.nt - 1
        h = h_ref[...]
        for c in range(geo.pro):
            act[:, c * cf:(c + 1) * cf] = (_silu(_dot(h, wgb[c])) * _dot(h, wub[c])).astype(BF16)
        f = _dot(act[...], wdb[...])
        val = x1_ref[...] + _mod(is_s, qg_ref, rg_ref, geo.reps) * _rms(f, npost_ref[...])
        if split_out:
            _write_split(is_s, val, *outs)
        else:
            outs[0][...] = val


def _out_rows(geo, d, split_out):
    t_all = geo.tm * geo.nt
    if split_out:
        return ([geo.prompt_rows(d), geo.sample_rows(d)],
                [jax.ShapeDtypeStruct((t_all - geo.tm, d), F32), jax.ShapeDtypeStruct((geo.tm, d), F32)])
    return [geo.rows(d)], [jax.ShapeDtypeStruct((t_all, d), F32)]


def _ffn_dense(h2, wg, wu, wd, j, x1, seq_mod, row_mod, geo_args, npost, split_out):
    tm, nt, tps, reps, d = geo_args
    f = wg.shape[2]
    cf = 256 if f % 256 == 0 else LANES
    npro = f // cf
    geo = _Rows(tm, nt, tps, reps, d, npro)
    out_specs, out_shape = _out_rows(geo, d, split_out)
    chunk = lambda i: jnp.minimum(i, npro - 1)
    return pl.pallas_call(
        functools.partial(_ffn_kernel, geo=geo, cf=cf, split_out=split_out),
        grid=(npro + nt,),
        in_specs=[geo.rows(d),
                  pl.BlockSpec((None, d, cf), lambda i: (j, 0, chunk(i))),
                  pl.BlockSpec((None, d, cf), lambda i: (j, 0, chunk(i))),
                  pl.BlockSpec((None, cf, d), lambda i: (j, chunk(i), 0)),
                  geo.rows(d), geo.seq_mod(5), geo.row_mod(5, row_mod.shape[0]),
                  pl.BlockSpec((1, d), lambda i: (0, 0))],
        out_specs=out_specs,
        out_shape=out_shape,
        scratch_shapes=[pltpu.VMEM((npro, d, cf), BF16), pltpu.VMEM((npro, d, cf), BF16),
                        pltpu.VMEM((f, d), BF16), pltpu.VMEM((tm, f), BF16)],
        compiler_params=_cparams("arbitrary"),
        name="ffn_dense",
    )(h2, wg, wu, wd, x1, seq_mod, row_mod, npost.reshape(1, d))


def _row_copy(src, src_row, dst, dst_row, sem):
    return pltpu.make_async_copy(src.at[pl.ds(src_row, 1)], dst.at[pl.ds(dst_row, 1)], sem)


def _moe_dispatch_kernel(pad_start_ref, pad_len_ref, nvalid_ref, pos_ref, h_ref, wg_ref, wu_ref, wd_ref,
                         xs_hbm, wgb_ref, wub_ref, wdb_ref, zbuf, sems, *, tm, n_exp):
    i = pl.program_id(0)
    ts = zbuf.shape[0]
    wgb_ref[...] = wg_ref[...].astype(BF16)
    wub_ref[...] = wu_ref[...].astype(BF16)
    wdb_ref[...] = wd_ref[...].astype(BF16)

    @pl.when(i == 0)
    def _():
        zbuf[...] = jnp.zeros(zbuf.shape, F32)
        for e in range(n_exp):
            def start(k, c, e=e):
                _row_copy(zbuf, 0, xs_hbm, pad_start_ref[e] + k, sems.at[1]).start()
                return c

            def wait(k, c, e=e):
                _row_copy(zbuf, 0, xs_hbm, pad_start_ref[e] + k, sems.at[1]).wait()
                return c

            lax.fori_loop(0, pad_len_ref[e], start, 0)
            lax.fori_loop(0, pad_len_ref[e], wait, 0)

        def tile_copy(j):
            return pltpu.make_async_copy(zbuf, xs_hbm.at[pl.ds(pl.multiple_of(j * ts, ts), ts)], sems.at[1])

        def start_tile(j, c):
            tile_copy(j).start()
            return c

        def wait_tile(j, c):
            tile_copy(j).wait()
            return c

        lax.fori_loop(nvalid_ref[0], xs_hbm.shape[0] // ts, start_tile, 0)
        lax.fori_loop(nvalid_ref[0], xs_hbm.shape[0] // ts, wait_tile, 0)

    for r in range(tm):
        _row_copy(h_ref, r, xs_hbm, pos_ref[0, r], sems.at[0]).start(priority=0)
        _row_copy(h_ref, r, xs_hbm, pos_ref[0, tm + r], sems.at[0]).start(priority=1)
    for _ in range(TOP_K):
        pltpu.make_async_copy(h_ref, xs_hbm.at[pl.ds(0, tm)], sems.at[0]).wait()


def _moe_dispatch(h2, pos_tiles, pad_start, pad_len, n_valid, n_slots, tm, tm_slot, wg, wu, wd):
    t, d = h2.shape
    nt = t // tm
    n_exp = pad_start.shape[0]
    pack = 2 * SUBLANES
    ncast = next(k for k in range(nt, 0, -1)
                 if wg.shape[0] % (k * pack) == 0 and wd.shape[0] % (k * pack) == 0)
    wrows = lambda a: pl.BlockSpec((a.shape[0] // ncast, a.shape[1]), lambda i, *_: (jnp.minimum(i, ncast - 1), 0))
    grid_spec = pltpu.PrefetchScalarGridSpec(
        num_scalar_prefetch=3,
        grid=(nt,),
        in_specs=[pl.BlockSpec((None, 1, TOP_K * tm), lambda i, *_: (i, 0, 0), memory_space=pltpu.SMEM),
                  pl.BlockSpec((tm, d), lambda i, *_: (i, 0)), wrows(wg), wrows(wu), wrows(wd)],
        out_specs=[pl.BlockSpec(memory_space=pl.ANY), wrows(wg), wrows(wu), wrows(wd)],
        scratch_shapes=[pltpu.VMEM((tm_slot, d), F32), pltpu.SemaphoreType.DMA((2,))],
    )
    return pl.pallas_call(
        functools.partial(_moe_dispatch_kernel, tm=tm, n_exp=n_exp),
        grid_spec=grid_spec,
        out_shape=[jax.ShapeDtypeStruct((n_slots, d), F32)] + [jax.ShapeDtypeStruct(a.shape, BF16) for a in (wg, wu, wd)],
        compiler_params=_cparams("arbitrary"),
        name="moe_dispatch",
    )(pad_start, pad_len, n_valid, pos_tiles.reshape(nt, 1, TOP_K * tm), h2, wg, wu, wd)


def _moe_expert_kernel(texp_ref, nvalid_ref, x_ref, wg_ref, wu_ref, wd_ref, o_ref):
    i = pl.program_id(0)

    @pl.when(i < nvalid_ref[0])
    def _():
        h = x_ref[...].astype(BF16)
        act = (_silu(_dot(h, wg_ref[...])) * _dot(h, wu_ref[...])).astype(BF16)
        o_ref[...] = _dot(act, wd_ref[...])

    @pl.when(i >= nvalid_ref[0])
    def _():
        o_ref[...] = jnp.zeros(o_ref.shape, F32)


def _moe_experts(x_sorted, tile_expert, n_valid, wg, wu, wd, tm):
    n_tiles = tile_expert.shape[0]
    d = x_sorted.shape[1]
    n_exp, _, fe = wg.shape
    grid_spec = pltpu.PrefetchScalarGridSpec(
        num_scalar_prefetch=2,
        grid=(n_tiles,),
        in_specs=[
            pl.BlockSpec((tm, d), lambda i, te, nv: (jnp.minimum(i, nv[0] - 1), 0)),
            pl.BlockSpec((None, d, fe), lambda i, te, nv: (te[i], 0, 0)),
            pl.BlockSpec((None, d, fe), lambda i, te, nv: (te[i], 0, 0)),
            pl.BlockSpec((None, fe, d), lambda i, te, nv: (te[i], 0, 0)),
        ],
        out_specs=pl.BlockSpec((tm, d), lambda i, te, nv: (i, 0)),
    )
    return pl.pallas_call(
        _moe_expert_kernel,
        grid_spec=grid_spec,
        out_shape=jax.ShapeDtypeStruct((n_tiles * tm, d), F32),
        compiler_params=_cparams("arbitrary"),
        name="moe_experts",
    )(tile_expert, n_valid, x_sorted, wg, wu, wd)


def _start_row_gather(idx_ref, src_hbm, dst, sem, count):
    for r in range(count):
        _row_copy(src_hbm, idx_ref[0, r], dst, r, sem).start(priority=r % 2)


def _wait_row_gather(src_hbm, dst, sem):
    pltpu.make_async_copy(src_hbm.at[pl.ds(0, dst.shape[0])], dst, sem).wait()


def _moe_combine_kernel(pos0_ref, posnext_ref, ye_hbm, sel_ref, x1_ref, qg_ref, rg_ref, npost_ref, *rest,
                        geo, split_out):
    outs, (ybuf, sems) = rest[:-2], rest[-2:]
    i = pl.program_id(0)
    tm = geo.tm
    slot = i % 2

    @pl.when(i == 0)
    def _():
        _start_row_gather(pos0_ref, ye_hbm, ybuf.at[0], sems.at[0], 2 * tm)

    @pl.when(i + 1 < geo.nt)
    def _():
        _start_row_gather(posnext_ref, ye_hbm, ybuf.at[1 - slot], sems.at[1 - slot], 2 * tm)

    _wait_row_gather(ye_hbm, ybuf.at[slot], sems.at[slot])
    is_s = i == geo.nt - 1
    sel = sel_ref[...]
    lane = lax.broadcasted_iota(jnp.int32, sel.shape, 1)
    w1 = jnp.sum(jnp.where(lane == 2, sel, 0.0), axis=-1, keepdims=True)
    w2 = jnp.sum(jnp.where(lane == 3, sel, 0.0), axis=-1, keepdims=True)
    f = w1 * ybuf[slot, 0:tm, :] + w2 * ybuf[slot, tm:2 * tm, :]
    val = x1_ref[...] + _mod(is_s, qg_ref, rg_ref, geo.reps) * _rms(f, npost_ref[...])
    if split_out:
        _write_split(is_s, val, *outs)
    else:
        outs[0][...] = val


def _moe_combine(ye, pos, sel, x1, seq_mod, row_mod, geo_args, npost, split_out):
    tm, nt, tps, reps, d = geo_args
    geo = _Rows(tm, nt, tps, reps, d, 0)
    out_specs, out_shape = _out_rows(geo, d, split_out)
    pos3 = pos.reshape(nt, 1, 2 * tm)
    smem_blk = lambda fn: pl.BlockSpec((None, 1, 2 * tm), fn, memory_space=pltpu.SMEM)
    return pl.pallas_call(
        functools.partial(_moe_combine_kernel, geo=geo, split_out=split_out),
        grid=(nt,),
        in_specs=[smem_blk(lambda i: (0, 0, 0)),
                  smem_blk(lambda i: (jnp.minimum(i + 1, nt - 1), 0, 0)),
                  pl.BlockSpec(memory_space=pl.ANY),
                  geo.rows(LANES), geo.rows(d), geo.seq_mod(5), geo.row_mod(5, row_mod.shape[0]),
                  pl.BlockSpec((1, d), lambda i: (0, 0))],
        out_specs=out_specs,
        out_shape=out_shape,
        scratch_shapes=[pltpu.VMEM((2, 2 * tm, d), F32), pltpu.SemaphoreType.DMA((2,))],
        compiler_params=_cparams("arbitrary"),
        name="moe_combine",
    )(pos3, pos3, ye, sel, x1, seq_mod, row_mod, npost.reshape(1, d))


def _moe_tables(sel, counts, n_exp, tm_tok, tm_slot):
    t = sel.shape[0]
    counts = counts[0, :n_exp].astype(jnp.int32)
    tiles_e = (counts + tm_slot - 1) // tm_slot
    tile_end = jnp.cumsum(tiles_e)
    slot_start = (tile_end - tiles_e) * tm_slot
    choice = sel[:, 0:TOP_K].astype(jnp.int32)
    rank = sel[:, 2 * TOP_K:3 * TOP_K].astype(jnp.int32)
    onehot = choice[:, :, None] == jnp.arange(n_exp, dtype=jnp.int32)[None, None, :]
    pos = jnp.sum(jnp.where(onehot, slot_start[None, None, :], 0), axis=-1) + rank
    n_tiles = (TOP_K * t + n_exp * (tm_slot - 1)) // tm_slot
    tile_expert = jnp.minimum(
        jnp.sum((jnp.arange(n_tiles, dtype=jnp.int32)[:, None] >= tile_end[None, :]).astype(jnp.int32), axis=1),
        n_exp - 1)
    pos_tiles = jnp.concatenate([pos[:, k].reshape(-1, tm_tok) for k in range(TOP_K)], axis=1)
    pad_start = slot_start + counts
    pad_len = tiles_e * tm_slot - counts
    return tile_expert, tile_end[-1:].astype(jnp.int32), pos_tiles, pad_start, pad_len, n_tiles * tm_slot


def _largest_tile(t, want):
    tm = min(want, t)
    while t % tm:
        tm //= 2
    return tm


def kernel(x_prompt, x_sample, c_prompt, c_sample, state_conva, state_convb, state_ssm, ada_w, ada_b, norm_pre_mix,
           norm_post_mix, norm_pre_ffn, norm_post_ffn, w_in, w_out, conva_w, convb_w, convb_b, dt_bias, a_log, d_skip,
           ssd_norm, ffd_w_gate, ffd_w_up, ffd_w_down, moe_router, moe_w_gate, moe_w_up, moe_w_down):
    bp, seq, d = x_prompt.shape
    bs, steps, _ = x_sample.shape
    depth = w_in.shape[0]
    dc = conva_w.shape[-1]
    dx = convb_w.shape[-1]
    ds = ssd_norm.shape[-1]
    heads = dt_bias.shape[-1]
    p = ds // heads
    n = (dx - ds) // (2 * SSD_GROUPS)
    hg = heads // SSD_GROUPS
    n_exp = moe_router.shape[-1]
    ka, kb = conva_w.shape[1] - 1, convb_w.shape[1] - 1
    tm = steps * bs
    t_p = bp * seq
    assert seq % SSD_CHUNK == 0 and LANES % p == 0 and (hg * p) % LANES == 0 and n == LANES
    assert heads <= LANES and n_exp <= LANES and dc % LANES == 0 and dx % LANES == 0 and d % LANES == 0
    assert w_in.shape[-1] == 3 * dc + ds + dx + heads and seq % tm == 0 and tm % SUBLANES == 0
    assert bs % SUBLANES == 0 and (t_p // bs) % steps == 0
    nt = t_p // tm + 1
    geo_args = (tm, nt, seq // tm, steps, d)

    w_dt = jnp.pad(w_in[:, :, 3 * dc + ds + dx:], ((0, 0), (0, 0), (0, LANES - heads))).astype(BF16)
    router_p = jnp.pad(moe_router, ((0, 0), (0, 0), (0, LANES - n_exp)))
    padh = lambda a: jnp.pad(a, ((0, 0), (0, LANES - heads))).reshape(depth, 1, LANES)
    dtb_p, alog_p = padh(dt_bias), padh(a_log)
    dsk_e = jnp.repeat(d_skip, p, axis=-1).reshape(depth, 1, ds)
    hot = (jnp.arange(LANES)[:, None] == (jnp.arange(ds)[None, :] // p)).astype(BF16)
    emat = jnp.concatenate([hot, hot], axis=0)

    mod = _adaln(jnp.concatenate([c_prompt, c_sample], axis=0), ada_w, ada_b)
    seq_mod = jnp.pad(mod[:, :bp], ((0, 0), (0, 1), (0, 0))).reshape(depth, bp + 1, 1, 6 * d)
    row_mod = mod[:, bp:]

    bb = _largest_tile(bs, 16)
    x_all = None
    xs_tm = x_sample.transpose(1, 0, 2).reshape(tm, d)
    xp2d = x_prompt.reshape(t_p, d)
    sa_all = state_conva.reshape(depth, bs, ka * dc)
    sb_all = state_convb.reshape(depth, bs, kb * dx)
    ss_all = state_ssm.reshape(depth, bs, ds, n)
    pa, pb, ps, sa_l, sb_l = [], [], [], [], []
    ss_new = jnp.zeros((depth, bs, ds, n), F32)
    for i in range(depth):
        j = i // 2
        last = i == depth - 1
        x_in = (xp2d, xs_tm) if x_all is None else (x_all, None)
        conv_params = [conva_w[i], convb_w[i], convb_b[i].reshape(1, dx)]
        ya, z, xc, dtr, na, nb, gb_s, u_s, xbc_s = _inproj(
            *x_in, seq_mod[i], row_mod[i], geo_args, norm_pre_mix[i], w_in, i, w_dt[i], *conv_params, bp, ds)
        params = [dtb_p[i], alog_p[i], dsk_e[i], ssd_norm[i].reshape(1, ds), emat]
        yb_p, ns = _ssd_prompt(z, xc, dtr, bp, seq, params, n, p, hg)
        r3 = lambda a: a.reshape(a.shape[0] // bs, bs, a.shape[-1])
        ya_s, yb_s, sna, snb, ss_new = _ssd_sample(
            r3(gb_s), r3(u_s), r3(xbc_s), r3(z), r3(dtr), t_p // (bs * steps), steps, bs, i, sa_all, sb_all, ss_all,
            ss_new, conv_params + params, n, p, hg, bb)
        pa.append(na)
        pb.append(nb)
        ps.append(ns.reshape(bp, heads, p, n))
        sa_l.append(sna.reshape(bs, ka, dc))
        sb_l.append(snb.reshape(bs, kb, dx))

        moe = i % 2 == 1
        res = _outproj(ya, ya_s.reshape(tm, dc), yb_p, yb_s.reshape(tm, ds), w_out, i, *x_in, seq_mod[i], row_mod[i],
                       geo_args,
                       norm_post_mix[i], norm_pre_ffn[i], router_p[j] if moe else None, n_exp if moe else 0)
        if moe:
            x1, h2, sel, counts = res
            tile_expert, n_valid, pos_tiles, pad_start, pad_len, n_slots = _moe_tables(sel, counts, n_exp, tm, MOE_TILE)
            fe = moe_w_gate.shape[-1]
            x_sorted, wg_b, wu_b, wd_b = _moe_dispatch(
                h2, pos_tiles, pad_start, pad_len, n_valid, n_slots, tm, MOE_TILE, moe_w_gate[j].reshape(n_exp * d, fe),
                moe_w_up[j].reshape(n_exp * d, fe), moe_w_down[j].reshape(n_exp * fe, d))
            ye = _moe_experts(x_sorted, tile_expert, n_valid, wg_b.reshape(n_exp, d, fe), wu_b.reshape(n_exp, d, fe),
                              wd_b.reshape(n_exp, fe, d), MOE_TILE)
            out = _moe_combine(ye, pos_tiles, sel, x1, seq_mod[i], row_mod[i], geo_args, norm_post_ffn[i], last)
        else:
            x1, h2 = res
            out = _ffn_dense(h2, ffd_w_gate, ffd_w_up, ffd_w_down, j, x1, seq_mod[i], row_mod[i], geo_args,
                             norm_post_ffn[i], last)
        if last:
            y_p, y_s = out
        else:
            x_all = out[0]

    y_prompt = y_p.reshape(bp, seq, d)
    y_sample = y_s.reshape(steps, bs, d).transpose(1, 0, 2)
    return (y_prompt, y_sample, jnp.stack(pa), jnp.stack(pb), jnp.stack(ps),
            jnp.stack(sa_l), jnp.stack(sb_l), ss_new.reshape(depth, bs, heads, p, n))
```

```python
import functools

import jax
import jax.numpy as jnp
from jax import lax
from jax.experimental import pallas as pl
from jax.experimental.pallas import tpu as pltpu

EPS = 1e-6
SSD_GROUPS = 2
SSD_CHUNK = 128
TOP_K = 2
LOG2E = 1.4426950408889634
LANES = 128
SUBLANES = 8
VMEM_LIMIT_BYTES = 56 * 1024 * 1024
MOE_TILE = 512

F32 = jnp.float32
BF16 = jnp.bfloat16


def _cparams(*sem):
    return pltpu.CompilerParams(dimension_semantics=sem, vmem_limit_bytes=VMEM_LIMIT_BYTES)


def _resident(shape):
    return pl.BlockSpec(shape, lambda *_: (0,) * len(shape), pipeline_mode=pl.Buffered(1))


def _silu(x):
    return x * (1.0 / (1.0 + jnp.exp(-x)))


def _softplus(x):
    return jnp.maximum(x, 0.0) + jnp.log1p(jnp.exp(-jnp.abs(x)))


def _rms(x, g):
    return x * lax.rsqrt(jnp.mean(x * x, axis=-1, keepdims=True) + EPS) * g


def _dot(a, b):
    return jnp.dot(a, b, preferred_element_type=F32)


def _dot_nt(a, b):
    return lax.dot_general(a, b, (((1,), (1,)), ((), ())), preferred_element_type=F32)


def _dot_tn(a, b):
    return lax.dot_general(a, b, (((0,), (0,)), ((), ())), preferred_element_type=F32)


def _chunk(*widths):
    return next(c for c in (512, 256, LANES) if all(w % c == 0 for w in widths))


def _adaln_kernel(c_ref, w_ref, b_ref, o_ref):
    s = _silu(c_ref[...]).astype(BF16)
    o_ref[...] = _dot(s, w_ref[...].astype(BF16)) + b_ref[...]


def _adaln(c_all, ada_w, ada_b):
    depth, d, d6 = ada_w.shape
    rows = c_all.shape[0]
    tn = _chunk(d6 // 6) * 2
    return pl.pallas_call(
        _adaln_kernel,
        grid=(depth, d6 // tn),
        in_specs=[
            pl.BlockSpec((rows, d), lambda l, j: (0, 0)),
            pl.BlockSpec((None, d, tn), lambda l, j: (l, 0, j)),
            pl.BlockSpec((None, 1, tn), lambda l, j: (l, 0, j)),
        ],
        out_specs=pl.BlockSpec((None, rows, tn), lambda l, j: (l, 0, j)),
        out_shape=jax.ShapeDtypeStruct((depth, rows, d6), F32),
        compiler_params=_cparams("arbitrary", "arbitrary"),
        name="adaln",
    )(c_all, ada_w, ada_b.reshape(depth, 1, d6))


class _Rows:
    def __init__(self, tm, nt, tps, reps, d, pro):
        self.tm, self.nt, self.tps, self.reps, self.d, self.pro = tm, nt, tps, reps, d, pro

    def tile(self, i):
        return jnp.maximum(i - self.pro, 0)

    def rows(self, w):
        return pl.BlockSpec((self.tm, w), lambda i, *_: (self.tile(i), 0))

    def prompt_rows(self, w):
        return pl.BlockSpec((self.tm, w), lambda i, *_: (jnp.minimum(self.tile(i), self.nt - 2), 0))

    def sample_rows(self, w):
        return pl.BlockSpec((self.tm, w), lambda i, *_: (0, 0))

    def seq_mod(self, k):
        return pl.BlockSpec((None, 1, self.d), lambda i, *_: (self.tile(i) // self.tps, 0, k))

    def row_mod(self, k, rows):
        return pl.BlockSpec((rows, self.d), lambda i, *_: (0, k))


def _mod(is_sample, seq_ref, row_ref, reps):
    rowm = jnp.concatenate([row_ref[...]] * reps, axis=0)
    return seq_ref[...] + jnp.where(is_sample, rowm, 0.0)


def _causal_conv_cols(cur, c0, w_ref, tail, ext, nst_ref, keep_state, first, emit):
    tm, cw = cur.shape
    kp = w_ref.shape[0] - 1
    wcol = lambda j: w_ref[j:j + 1, c0:c0 + cw]
    ext[0:SUBLANES, :] = jnp.where(first, 0.0, tail[:, c0:c0 + cw])
    ext[SUBLANES:SUBLANES + tm, :] = cur
    acc = wcol(kp) * cur
    for j in range(kp):
        off = SUBLANES - (kp - j)
        acc = acc + wcol(j) * ext[off:off + tm, :]
    tail[:, c0:c0 + cw] = ext[tm:tm + SUBLANES, :]
    nst_ref[:, c0:c0 + cw] = jnp.where(keep_state, nst_ref[:, c0:c0 + cw], ext[SUBLANES + tm - kp:SUBLANES + tm, :])
    emit(acc)


def _inproj_kernel(*refs, geo, dc, ds, dx, cw, split_x):
    if split_x:
        xp_ref, xs_ref, *refs = refs
    else:
        xp_ref, *refs = refs
    (qsh_ref, qsc_ref, rsh_ref, rsc_ref, g_ref, wch_ref, wdt_ref, caw_ref, cbw_ref, cbb_ref,
     ya_ref, z_ref, xc_ref, dt_ref, ncap_ref, ncbp_ref, gbs_ref, us_ref, xbcs_ref, wbf, ext, tail_u, tail_x) = refs
    i = pl.program_id(0)

    @pl.when(i < geo.pro)
    def _():
        wbf[i] = wch_ref[...].astype(BF16)

    @pl.when(i >= geo.pro)
    def _():
        r = i - geo.pro
        is_s = r == geo.nt - 1
        first = r % geo.tps == 0
        x = jnp.where(is_s, xs_ref[...], xp_ref[...]) if split_x else xp_ref[...]
        sc = _mod(is_s, qsc_ref, rsc_ref, geo.reps)
        sh = _mod(is_s, qsh_ref, rsh_ref, geo.reps)
        h = (_rms(x, g_ref[...]) * (1.0 + sc) + sh).astype(BF16)
        nc = dc // cw
        for a in range(nc):
            c0 = a * cw
            gate_b = _dot(h, wbf[a])
            u = _dot(h, wbf[nc + a]) * _dot(h, wbf[2 * nc + a])
            gbs_ref[:, c0:c0 + cw] = gate_b
            us_ref[:, c0:c0 + cw] = u

            def emit_a(v, c0=c0, gate_b=gate_b):
                ya_ref[:, c0:c0 + cw] = (gate_b * v).astype(BF16)

            _causal_conv_cols(u, c0, caw_ref, tail_u, ext, ncap_ref, is_s, first, emit_a)
        for a in range(ds // cw):
            z_ref[:, a * cw:(a + 1) * cw] = _dot(h, wbf[3 * nc + a])
        for a in range(dx // cw):
            c0 = a * cw
            xbc = _dot(h, wbf[3 * nc + ds // cw + a])
            xbcs_ref[:, c0:c0 + cw] = xbc

            def emit_b(v, c0=c0):
                xc_ref[:, c0:c0 + cw] = _silu(v + cbb_ref[:, c0:c0 + cw])

            _causal_conv_cols(xbc, c0, cbw_ref, tail_x, ext, ncbp_ref, is_s, first, emit_b)
        dt_ref[...] = _dot(h, wdt_ref[...])


def _inproj(x_p, x_s, seq_mod, row_mod, geo_args, norm_w, w_in, layer, w_dt, caw, cbw, cbb, bp, ds):
    tm, nt, tps, reps, d = geo_args
    dc, dx = caw.shape[1], cbw.shape[1]
    ka, kb = caw.shape[0] - 1, cbw.shape[0] - 1
    cw = _chunk(dc, ds, dx)
    npro = (3 * dc + ds + dx) // cw
    geo = _Rows(tm, nt, tps, reps, d, npro)
    t_all = tm * nt
    split_x = x_s is not None
    xin = [geo.prompt_rows(d), geo.sample_rows(d)] if split_x else [geo.rows(d)]
    xargs = [x_p, x_s] if split_x else [x_p]
    nrow = row_mod.shape[0]
    seq_blk = lambda k, w: pl.BlockSpec((None, k, w), lambda i: (jnp.minimum(geo.tile(i) // tps, bp - 1), 0, 0))
    return pl.pallas_call(
        functools.partial(_inproj_kernel, geo=geo, dc=dc, ds=ds, dx=dx, cw=cw, split_x=split_x),
        grid=(npro + nt,),
        in_specs=xin + [
            geo.seq_mod(0), geo.seq_mod(1), geo.row_mod(0, nrow), geo.row_mod(1, nrow),
            pl.BlockSpec((1, d), lambda i: (0, 0)),
            pl.BlockSpec((None, d, cw), lambda i: (layer, 0, jnp.minimum(i, npro - 1))),
            _resident((d, LANES)), _resident(caw.shape), _resident(cbw.shape), _resident(cbb.shape),
        ],
        out_specs=[geo.rows(dc), geo.rows(ds), geo.rows(dx), geo.rows(LANES), seq_blk(ka, dc), seq_blk(kb, dx),
                   geo.sample_rows(dc), geo.sample_rows(dc), geo.sample_rows(dx)],
        out_shape=[
            jax.ShapeDtypeStruct((t_all, dc), BF16),
            jax.ShapeDtypeStruct((t_all, ds), F32),
            jax.ShapeDtypeStruct((t_all, dx), F32),
            jax.ShapeDtypeStruct((t_all, LANES), F32),
            jax.ShapeDtypeStruct((bp, ka, dc), F32),
            jax.ShapeDtypeStruct((bp, kb, dx), F32),
            jax.ShapeDtypeStruct((tm, dc), F32),
            jax.ShapeDtypeStruct((tm, dc), F32),
            jax.ShapeDtypeStruct((tm, dx), F32),
        ],
        scratch_shapes=[pltpu.VMEM((npro, d, cw), BF16), pltpu.VMEM((SUBLANES + tm, cw), F32),
                        pltpu.VMEM((SUBLANES, dc), F32), pltpu.VMEM((SUBLANES, dx), F32)],
        compiler_params=_cparams("arbitrary"),
        name="inproj",
    )(*xargs, seq_mod, seq_mod, row_mod, row_mod, norm_w.reshape(1, d), w_in, w_dt, caw, cbw, cbb)


def _split_hi_lo(v):
    hi = v.astype(BF16)
    lo = (v - hi.astype(F32)).astype(BF16)
    return jnp.concatenate([hi, lo], axis=1)


def _diag_block(cb, mask, cum, cum_t, dt_t, xs, lane, g, hg, gw, p):
    hpl = LANES // p
    parts = []
    for slab in range(gw // LANES):
        lhs, rhs = [], []
        xslab = xs[:, g * gw + slab * LANES:g * gw + (slab + 1) * LANES]
        for j in range(hpl):
            h = g * hg + slab * hpl + j
            seg = cum[:, h:h + 1] - cum_t[h:h + 1, :]
            m = cb * jnp.where(mask, jnp.exp2(seg), 0.0) * dt_t[h:h + 1, :]
            lhs.append(m.astype(BF16))
            rhs.append(jnp.where((lane >= j * p) & (lane < (j + 1) * p), xslab, 0.0).astype(BF16))
        parts.append(_dot(jnp.concatenate(lhs, axis=1), jnp.concatenate(rhs, axis=0)))
    return jnp.concatenate(parts, axis=1)


def _gated_group_norm(y, xs_g, z_g, dsk_g, snorm_g):
    y = (y + dsk_g * xs_g) * _silu(z_g)
    return (y * lax.rsqrt(jnp.mean(y * y, axis=-1, keepdims=True) + EPS) * snorm_g).astype(BF16)


def _ssd_prompt_kernel(z_ref, xc_ref, dt_ref, dtb_ref, alog_ref, dsk_ref, snorm_ref, e_ref, yb_ref, nss_ref, st,
                       *, ds, n, p, hg, sub):
    c = pl.program_id(1)
    q = SSD_CHUNK
    gw = hg * p

    @pl.when(c == 0)
    def _():
        st[...] = jnp.zeros(st.shape, F32)

    row = lax.broadcasted_iota(jnp.int32, (q, LANES), 0)
    tril = lax.broadcasted_iota(jnp.int32, (q, q), 0) >= lax.broadcasted_iota(jnp.int32, (q, q), 1)
    lane = lax.broadcasted_iota(jnp.int32, (q, LANES), 1)
    neg_a = -jnp.exp(alog_ref[...]) * LOG2E
    for k in range(sub):
        rs = slice(k * q, (k + 1) * q)
        xs = xc_ref[rs, 0:ds]
        dt = _softplus(dt_ref[rs, :] + dtb_ref[...])
        cum = dt * neg_a
        step = 1
        while step < q:
            cum = cum + jnp.where(row >= step, pltpu.roll(cum, step, axis=0), 0.0)
            step *= 2
        cum_last = cum[q - 1:q, :]
        w_exp = _dot(_split_hi_lo(dt * jnp.exp2(cum_last - cum)), e_ref[...])
        ecum_exp = _dot(_split_hi_lo(jnp.exp2(cum)), e_ref[...])
        cum_t = cum.T
        dt_t = dt.T
        for g in range(SSD_GROUPS):
            gsl = slice(g * gw, (g + 1) * gw)
            bm = xc_ref[rs, ds + g * n:ds + (g + 1) * n]
            cm = xc_ref[rs, ds + SSD_GROUPS * n + g * n:ds + SSD_GROUPS * n + (g + 1) * n].astype(BF16)
            bm_t = bm.T.astype(BF16)
            y = _diag_block(_dot(cm, bm_t), tril, cum, cum_t, dt_t, xs, lane, g, hg, gw, p)
            s_prev = st[g]
            y = y + _dot(cm, s_prev.astype(BF16)) * ecum_exp[:, gsl]
            xw = (xs[:, gsl] * w_exp[:, gsl]).astype(BF16)
            st[g] = s_prev * ecum_exp[q - 1:q, gsl] + _dot(bm_t, xw)
            yb_ref[rs, gsl] = _gated_group_norm(y, xs[:, gsl], z_ref[rs, gsl], dsk_ref[:, gsl], snorm_ref[:, gsl])

    @pl.when(c == pl.num_programs(1) - 1)
    def _():
        for g in range(SSD_GROUPS):
            for slab in range(gw // LANES):
                r0 = g * gw + slab * LANES
                nss_ref[r0:r0 + LANES, :] = st[g, :, slab * LANES:(slab + 1) * LANES].T


def _ssd_prompt(z, xc, dtr, bsz, seq, params, n, p, hg):
    ds = z.shape[1]
    dx = xc.shape[1]
    sub = next(k for k in (4, 2, 1) if seq % (k * SSD_CHUNK) == 0)
    rows = sub * SSD_CHUNK
    nc = seq // rows
    row = lambda w: pl.BlockSpec((rows, w), lambda b, c: (b * nc + c, 0))
    return pl.pallas_call(
        functools.partial(_ssd_prompt_kernel, ds=ds, n=n, p=p, hg=hg, sub=sub),
        grid=(bsz, nc),
        in_specs=[row(ds), row(dx), row(LANES)] + [_resident(a.shape) for a in params],
        out_specs=[row(ds), pl.BlockSpec((None, ds, n), lambda b, c: (b, 0, 0))],
        out_shape=[jax.ShapeDtypeStruct((bsz * seq, ds), BF16), jax.ShapeDtypeStruct((bsz, ds, n), F32)],
        scratch_shapes=[pltpu.VMEM((SSD_GROUPS, n, hg * p), F32)],
        compiler_params=_cparams("arbitrary", "arbitrary"),
        name="ssd_prompt",
    )(z, xc, dtr, *params)


def _ssd_sample_kernel(gb_ref, u_ref, xbc_ref, z_ref, dt_ref, sa_ref, sb_ref, ss_ref, caw_ref, cbw_ref, cbb_ref,
                       dtb_ref, alog_ref, dsk_ref, snorm_ref, e_ref, ya_ref, yb_ref, nca_ref, ncb_ref, nss_ref,
                       *, dc, ds, dx, n, p, hg, steps, bb):
    gw = hg * p
    rows = steps * bb
    ka = caw_ref.shape[0] - 1
    kb = cbw_ref.shape[0] - 1

    def conv(cur_ref, st_ref, w_ref, kprev, width):
        hist = [st_ref[:, j * width:(j + 1) * width] for j in range(kprev)] + [cur_ref[t] for t in range(steps)]
        outs = []
        for t in range(steps):
            acc = w_ref[kprev:kprev + 1, :] * hist[t + kprev]
            for j in range(kprev):
                acc = acc + w_ref[j:j + 1, :] * hist[t + j]
            outs.append(acc)
        return outs, hist[len(hist) - kprev:]

    v, new_a = conv(u_ref, sa_ref, caw_ref, ka, dc)
    for t in range(steps):
        ya_ref[t] = (gb_ref[t] * v[t]).astype(BF16)
    for j in range(ka):
        nca_ref[:, j * dc:(j + 1) * dc] = new_a[j]
    xcs, new_b = conv(xbc_ref, sb_ref, cbw_ref, kb, dx)
    for j in range(kb):
        ncb_ref[:, j * dx:(j + 1) * dx] = new_b[j]
    xc = _silu(jnp.concatenate(xcs, axis=0) + cbb_ref[...])
    xs = xc[:, 0:ds]

    dt = _softplus(jnp.concatenate([dt_ref[t] for t in range(steps)], axis=0) + dtb_ref[...])
    da = dt * (-jnp.exp(alog_ref[...]) * LOG2E)
    cums = [da[0:bb]]
    for t in range(1, steps):
        cums.append(cums[-1] + da[t * bb:(t + 1) * bb])
    cum = jnp.concatenate(cums, axis=0)
    cum_last = jnp.concatenate([cums[-1]] * steps, axis=0)
    w_exp = _dot(_split_hi_lo(dt * jnp.exp2(cum_last - cum)), e_ref[...])
    ecum_exp = _dot(_split_hi_lo(jnp.exp2(cum)), e_ref[...])

    def pad_t(a):
        a = jnp.concatenate([a, jnp.zeros((LANES - rows, LANES), F32)], axis=0) if rows < LANES else a
        return a.T[:, 0:rows]

    cum_t = pad_t(cum)
    dt_t = pad_t(dt)

    ri = lax.broadcasted_iota(jnp.int32, (rows, rows), 0)
    ci = lax.broadcasted_iota(jnp.int32, (rows, rows), 1)
    same = ((ri % bb) == (ci % bb)) & (ri >= ci)
    lane = lax.broadcasted_iota(jnp.int32, (rows, LANES), 1)
    rowid = lax.broadcasted_iota(jnp.int32, (rows, 1), 0) % bb
    seqlane = lax.broadcasted_iota(jnp.int32, (LANES, LANES), 1)
    nslab = gw // LANES
    for g in range(SSD_GROUPS):
        gsl = slice(g * gw, (g + 1) * gw)
        bm = xc[:, ds + g * n:ds + (g + 1) * n].astype(BF16)
        cm = xc[:, ds + SSD_GROUPS * n + g * n:ds + SSD_GROUPS * n + (g + 1) * n].astype(BF16)
        y_diag = _diag_block(_dot_nt(cm, bm), same, cum, cum_t, dt_t, xs, lane, g, hg, gw, p)
        xw = (xs[:, gsl] * w_exp[:, gsl]).astype(BF16)
        dec = ecum_exp[(steps - 1) * bb:steps * bb, gsl]
        dec = jnp.concatenate([dec, jnp.zeros((LANES - bb, gw), F32)], axis=0)
        dec_t = [dec[:, s * LANES:(s + 1) * LANES].T for s in range(nslab)]

        def per_seq(b, y_off, g=g, cm=cm, bm=bm, xw=xw, dec_t=dec_t):
            r0 = g * gw
            s0 = ss_ref[b, r0:r0 + gw, :]
            y_off = jnp.where(rowid == b, _dot_nt(cm, s0.astype(BF16)), y_off)
            upd = _dot_tn(jnp.where(rowid == b, xw, jnp.zeros_like(xw)), bm)
            for s in range(nslab):
                dcol = jnp.sum(jnp.where(seqlane == b, dec_t[s], 0.0), axis=1, keepdims=True)
                nss_ref[b, r0 + s * LANES:r0 + (s + 1) * LANES, :] = (
                    s0[s * LANES:(s + 1) * LANES, :] * dcol + upd[s * LANES:(s + 1) * LANES, :])
            return y_off

        y_off = lax.fori_loop(0, bb, per_seq, jnp.zeros((rows, gw), F32), unroll=4)
        zg = jnp.concatenate([z_ref[t, :, gsl] for t in range(steps)], axis=0)
        yn = _gated_group_norm(y_diag + y_off * ecum_exp[:, gsl], xs[:, gsl], zg, dsk_ref[:, gsl], snorm_ref[:, gsl])
        for t in range(steps):
            yb_ref[t, :, gsl] = yn[t * bb:(t + 1) * bb]


def _ssd_sample_aliased_kernel(*refs, **kw):
    _ssd_sample_kernel(*refs[1:], **kw)


def _ssd_sample(gb, u, xbc, z, dtr, blk0, steps, bs, layer, sa, sb, ss, prev_ss, params, n, p, hg, bb):
    dc, ds, dx = u.shape[2], z.shape[2], xbc.shape[2]
    ka, kb = params[0].shape[0] - 1, params[1].shape[0] - 1
    own = lambda w: pl.BlockSpec((steps, bb, w), lambda i: (0, i, 0))
    tok = lambda w: pl.BlockSpec((steps, bb, w), lambda i: (blk0, i, 0))
    state = pl.BlockSpec((None, bb, ds, n), lambda i: (layer, i, 0, 0))
    return pl.pallas_call(
        functools.partial(_ssd_sample_aliased_kernel, dc=dc, ds=ds, dx=dx, n=n, p=p, hg=hg, steps=steps, bb=bb),
        grid=(bs // bb,),
        in_specs=[pl.BlockSpec(memory_space=pl.ANY), own(dc), own(dc), own(dx), tok(ds), tok(LANES),
                  pl.BlockSpec((None, bb, ka * dc), lambda i: (layer, i, 0)),
                  pl.BlockSpec((None, bb, kb * dx), lambda i: (layer, i, 0)), state]
        + [_resident(a.shape) for a in params],
        out_specs=[own(dc), own(ds), pl.BlockSpec((bb, ka * dc), lambda i: (i, 0)),
                   pl.BlockSpec((bb, kb * dx), lambda i: (i, 0)), state],
        out_shape=[jax.ShapeDtypeStruct((steps, bs, dc), BF16), jax.ShapeDtypeStruct((steps, bs, ds), BF16),
                   jax.ShapeDtypeStruct((bs, ka * dc), F32), jax.ShapeDtypeStruct((bs, kb * dx), F32),
                   jax.ShapeDtypeStruct(prev_ss.shape, F32)],
        input_output_aliases={0: 4},
        compiler_params=_cparams("arbitrary"),
        name="ssd_sample",
    )(prev_ss, gb, u, xbc, z, dtr, sa, sb, ss, *params)


def _route(logits, n_exp, cnt_ref):
    rows = logits.shape[0]
    lane = lax.broadcasted_iota(jnp.int32, logits.shape, 1).astype(F32)
    valid = lane < n_exp
    logits = jnp.where(valid, logits, -jnp.inf)
    e = jnp.exp(logits - jnp.max(logits, axis=-1, keepdims=True))
    prob = jnp.where(valid, e / jnp.sum(e, axis=-1, keepdims=True), -1.0)
    big = float(LANES)
    m1 = jnp.max(prob, axis=-1, keepdims=True)
    i1 = jnp.min(jnp.where(prob == m1, lane, big), axis=-1, keepdims=True)
    rest = jnp.where(lane == i1, -1.0, prob)
    m2 = jnp.max(rest, axis=-1, keepdims=True)
    i2 = jnp.min(jnp.where(rest == m2, lane, big), axis=-1, keepdims=True)
    den = m1 + m2
    chosen = jnp.where((lane == i1) | (lane == i2), 1.0, 0.0)
    before = (lax.broadcasted_iota(jnp.int32, (rows, rows), 0) > lax.broadcasted_iota(jnp.int32, (rows, rows), 1))
    rank = _dot(jnp.where(before, 1.0, 0.0).astype(BF16), chosen.astype(BF16)) + cnt_ref[0:1, :]
    r1 = jnp.sum(jnp.where(lane == i1, rank, 0.0), axis=-1, keepdims=True)
    r2 = jnp.sum(jnp.where(lane == i2, rank, 0.0), axis=-1, keepdims=True)
    cnt_ref[0:1, :] = cnt_ref[0:1, :] + jnp.sum(chosen, axis=0, keepdims=True)
    cols = [i1, i2, m1 / den, m2 / den, r1, r2]
    out = jnp.zeros(logits.shape, F32)
    for k, col in enumerate(cols):
        out = jnp.where(lane == float(k), col, out)
    return out


def _outproj_kernel(yap_ref, yas_ref, ybp_ref, ybs_ref, wch_ref, *rest, geo, n_exp, kc, split_x):
    if split_x:
        xp_ref, xs_ref, *rest = rest
    else:
        xp_ref, *rest = rest
    qg_ref, qsh_ref, qsc_ref, rg_ref, rsh_ref, rsc_ref, npost_ref, npre_ref, *rest = rest
    if n_exp:
        r_ref, x1_ref, h2_ref, sel_ref, cnt_ref, wbf = rest
    else:
        x1_ref, h2_ref, wbf = rest
    i = pl.program_id(0)

    @pl.when(i < geo.pro)
    def _():
        wbf[pl.ds(pl.multiple_of(i * kc, kc), kc), :] = wch_ref[...].astype(BF16)
        if n_exp:
            cnt_ref[...] = jnp.zeros(cnt_ref.shape, F32)

    @pl.when(i >= geo.pro)
    def _():
        is_s = i - geo.pro == geo.nt - 1
        da = yap_ref.shape[1]
        nh = 2 if geo.reps % 2 == 0 else 1
        th = geo.tm // nh
        for hs in range(nh):
            rs = slice(hs * th, (hs + 1) * th)
            mod = lambda q_ref, r_ref: _mod(is_s, q_ref, r_ref, geo.reps // nh)
            mix = (_dot(jnp.where(is_s, yas_ref[rs, :], yap_ref[rs, :]), wbf[0:da, :])
                   + _dot(jnp.where(is_s, ybs_ref[rs, :], ybp_ref[rs, :]), wbf[da:, :]))
            x = jnp.where(is_s, xs_ref[rs, :], xp_ref[rs, :]) if split_x else xp_ref[rs, :]
            x1 = x + mod(qg_ref, rg_ref) * _rms(mix, npost_ref[...])
            x1_ref[rs, :] = x1
            h2 = _rms(x1, npre_ref[...]) * (1.0 + mod(qsc_ref, rsc_ref)) + mod(qsh_ref, rsh_ref)
            h2_ref[rs, :] = h2.astype(h2_ref.dtype)
            if n_exp:
                h_hi = h2.astype(BF16)
                h_lo = (h2 - h_hi.astype(F32)).astype(BF16)
                r = r_ref[...]
                r_hi = r.astype(BF16)
                r_lo = (r - r_hi.astype(F32)).astype(BF16)
                both = _dot(h_hi, jnp.concatenate([r_hi, r_lo], axis=1))
                logits = both[:, 0:LANES] + (_dot(h_lo, r_hi) + both[:, LANES:])
                sel_ref[rs, :] = _route(logits, n_exp, cnt_ref)


def _outproj(ya_p, ya_s, yb_p, yb_s, w_out, layer, x_p, x_s, seq_mod, row_mod, geo_args, npost, npre, router,
             n_exp):
    tm, nt, tps, reps, d = geo_args
    da, db = ya_p.shape[1], yb_p.shape[1]
    dm = da + db
    kc = _chunk(dm) // 2
    npro = dm // kc
    geo = _Rows(tm, nt, tps, reps, d, npro)
    t_all = tm * nt
    nrow = row_mod.shape[0]
    vec = pl.BlockSpec((1, d), lambda i: (0, 0))
    split_x = x_s is not None
    in_specs = [geo.prompt_rows(da), geo.sample_rows(da), geo.prompt_rows(db), geo.sample_rows(db),
                pl.BlockSpec((None, kc, d), lambda i: (layer, jnp.minimum(i, npro - 1), 0))]
    in_specs += [geo.prompt_rows(d), geo.sample_rows(d)] if split_x else [geo.rows(d)]
    in_specs += [geo.seq_mod(2), geo.seq_mod(3), geo.seq_mod(4),
                 geo.row_mod(2, nrow), geo.row_mod(3, nrow), geo.row_mod(4, nrow), vec, vec]
    args = [ya_p, ya_s, yb_p, yb_s, w_out] + ([x_p, x_s] if split_x else [x_p])
    args += [seq_mod, seq_mod, seq_mod, row_mod, row_mod, row_mod, npost.reshape(1, d), npre.reshape(1, d)]
    out_specs = [geo.rows(d), geo.rows(d)]
    out_shape = [jax.ShapeDtypeStruct((t_all, d), F32), jax.ShapeDtypeStruct((t_all, d), F32 if n_exp else BF16)]
    if n_exp:
        in_specs.append(_resident((d, LANES)))
        args.append(router)
        out_specs += [geo.rows(LANES), pl.BlockSpec((SUBLANES, LANES), lambda i: (0, 0))]
        out_shape += [jax.ShapeDtypeStruct((t_all, LANES), F32), jax.ShapeDtypeStruct((SUBLANES, LANES), F32)]
    return pl.pallas_call(
        functools.partial(_outproj_kernel, geo=geo, n_exp=n_exp, kc=kc, split_x=split_x),
        grid=(npro + nt,),
        in_specs=in_specs,
        out_specs=out_specs,
        out_shape=out_shape,
        scratch_shapes=[pltpu.VMEM((dm, d), BF16)],
        compiler_params=_cparams("arbitrary"),
        name="outproj",
    )(*args)


def _write_split(is_s, val, outp_ref, outs_ref):
    @pl.when(jnp.logical_not(is_s))
    def _():
        outp_ref[...] = val

    @pl.when(is_s)
    def _():
        outs_ref[...] = val


def _ffn_kernel(h_ref, wg_ref, wu_ref, wd_ref, x1_ref, qg_ref, rg_ref, npost_ref, *rest, geo, cf, split_out):
    outs, (wgb, wub, wdb, act) = rest[:-4], rest[-4:]
    i = pl.program_id(0)

    @pl.when(i < geo.pro)
    def _():
        wgb[i] = wg_ref[...].astype(BF16)
        wub[i] = wu_ref[...].astype(BF16)
        wdb[pl.ds(pl.multiple_of(i * cf, cf), cf), :] = wd_ref[...].astype(BF16)

    @pl.when(i >= geo.pro)
    def _():
        is_s = i - geo.pro == geo.nt - 1
        h = h_ref[...]
        for c in range(geo.pro):
            act[:, c * cf:(c + 1) * cf] = (_silu(_dot(h, wgb[c])) * _dot(h, wub[c])).astype(BF16)
        f = _dot(act[...], wdb[...])
        val = x1_ref[...] + _mod(is_s, qg_ref, rg_ref, geo.reps) * _rms(f, npost_ref[...])
        if split_out:
            _write_split(is_s, val, *outs)
        else:
            outs[0][...] = val


def _out_rows(geo, d, split_out):
    t_all = geo.tm * geo.nt
    if split_out:
        return ([geo.prompt_rows(d), geo.sample_rows(d)],
                [jax.ShapeDtypeStruct((t_all - geo.tm, d), F32), jax.ShapeDtypeStruct((geo.tm, d), F32)])
    return [geo.rows(d)], [jax.ShapeDtypeStruct((t_all, d), F32)]


def _ffn_dense(h2, wg, wu, wd, j, x1, seq_mod, row_mod, geo_args, npost, split_out):
    tm, nt, tps, reps, d = geo_args
    f = wg.shape[2]
    cf = 256 if f % 256 == 0 else LANES
    npro = f // cf
    geo = _Rows(tm, nt, tps, reps, d, npro)
    out_specs, out_shape = _out_rows(geo, d, split_out)
    chunk = lambda i: jnp.minimum(i, npro - 1)
    return pl.pallas_call(
        functools.partial(_ffn_kernel, geo=geo, cf=cf, split_out=split_out),
        grid=(npro + nt,),
        in_specs=[geo.rows(d),
                  pl.BlockSpec((None, d, cf), lambda i: (j, 0, chunk(i))),
                  pl.BlockSpec((None, d, cf), lambda i: (j, 0, chunk(i))),
                  pl.BlockSpec((None, cf, d), lambda i: (j, chunk(i), 0)),
                  geo.rows(d), geo.seq_mod(5), geo.row_mod(5, row_mod.shape[0]),
                  pl.BlockSpec((1, d), lambda i: (0, 0))],
        out_specs=out_specs,
        out_shape=out_shape,
        scratch_shapes=[pltpu.VMEM((npro, d, cf), BF16), pltpu.VMEM((npro, d, cf), BF16),
                        pltpu.VMEM((f, d), BF16), pltpu.VMEM((tm, f), BF16)],
        compiler_params=_cparams("arbitrary"),
        name="ffn_dense",
    )(h2, wg, wu, wd, x1, seq_mod, row_mod, npost.reshape(1, d))


def _row_copy(src, src_row, dst, dst_row, sem):
    return pltpu.make_async_copy(src.at[pl.ds(src_row, 1)], dst.at[pl.ds(dst_row, 1)], sem)


def _moe_dispatch_kernel(pad_start_ref, pad_len_ref, nvalid_ref, pos_ref, h_ref, wg_ref, wu_ref, wd_ref,
                         xs_hbm, wgb_ref, wub_ref, wdb_ref, zbuf, hbuf, sems, *, tm, n_exp, nt):
    i = pl.program_id(0)
    ts = zbuf.shape[0]
    wgb_ref[...] = wg_ref[...].astype(BF16)
    wub_ref[...] = wu_ref[...].astype(BF16)
    wdb_ref[...] = wd_ref[...].astype(BF16)

    @pl.when(i == 0)
    def _():
        zbuf[...] = jnp.zeros(zbuf.shape, F32)
        for e in range(n_exp):
            def start(k, c, e=e):
                _row_copy(zbuf, 0, xs_hbm, pad_start_ref[e] + k, sems.at[2]).start()
                return c

            def wait(k, c, e=e):
                _row_copy(zbuf, 0, xs_hbm, pad_start_ref[e] + k, sems.at[2]).wait()
                return c

            lax.fori_loop(0, pad_len_ref[e], start, 0)
            lax.fori_loop(0, pad_len_ref[e], wait, 0)

        def tile_copy(j):
            return pltpu.make_async_copy(zbuf, xs_hbm.at[pl.ds(pl.multiple_of(j * ts, ts), ts)], sems.at[2])

        def start_tile(j, c):
            tile_copy(j).start()
            return c

        def wait_tile(j, c):
            tile_copy(j).wait()
            return c

        lax.fori_loop(nvalid_ref[0], xs_hbm.shape[0] // ts, start_tile, 0)
        lax.fori_loop(nvalid_ref[0], xs_hbm.shape[0] // ts, wait_tile, 0)

    def wait_slot(s):
        for _ in range(TOP_K):
            pltpu.make_async_copy(hbuf.at[s], xs_hbm.at[pl.ds(0, tm)], sems.at[s]).wait()

    for s in range(2):
        @pl.when(i % 2 == s)
        def _(s=s):
            @pl.when(i >= 2)
            def _():
                wait_slot(s)

            hbuf[s] = h_ref[...]
            for r in range(tm):
                _row_copy(hbuf.at[s], r, xs_hbm, pos_ref[0, r], sems.at[s]).start(priority=0)
                _row_copy(hbuf.at[s], r, xs_hbm, pos_ref[0, tm + r], sems.at[s]).start(priority=1)

    @pl.when(i == nt - 1)
    def _():
        wait_slot((nt - 1) % 2)
        if nt > 1:
            wait_slot(nt % 2)


def _moe_dispatch(h2, pos_tiles, pad_start, pad_len, n_valid, n_slots, tm, tm_slot, wg, wu, wd):
    t, d = h2.shape
    nt = t // tm
    n_exp = pad_start.shape[0]
    pack = 2 * SUBLANES
    ncast = next(k for k in range(nt, 0, -1)
                 if wg.shape[0] % (k * pack) == 0 and wd.shape[0] % (k * pack) == 0)
    wrows = lambda a: pl.BlockSpec((a.shape[0] // ncast, a.shape[1]), lambda i, *_: (jnp.minimum(i, ncast - 1), 0))
    grid_spec = pltpu.PrefetchScalarGridSpec(
        num_scalar_prefetch=3,
        grid=(nt,),
        in_specs=[pl.BlockSpec((None, 1, TOP_K * tm), lambda i, *_: (i, 0, 0), memory_space=pltpu.SMEM),
                  pl.BlockSpec((tm, d), lambda i, *_: (i, 0)), wrows(wg), wrows(wu), wrows(wd)],
        out_specs=[pl.BlockSpec(memory_space=pl.ANY), wrows(wg), wrows(wu), wrows(wd)],
        scratch_shapes=[pltpu.VMEM((tm_slot, d), F32), pltpu.VMEM((2, tm, d), F32), pltpu.SemaphoreType.DMA((3,))],
    )
    return pl.pallas_call(
        functools.partial(_moe_dispatch_kernel, tm=tm, n_exp=n_exp, nt=nt),
        grid_spec=grid_spec,
        out_shape=[jax.ShapeDtypeStruct((n_slots, d), F32)] + [jax.ShapeDtypeStruct(a.shape, BF16) for a in (wg, wu, wd)],
        compiler_params=_cparams("arbitrary"),
        name="moe_dispatch",
    )(pad_start, pad_len, n_valid, pos_tiles.reshape(nt, 1, TOP_K * tm), h2, wg, wu, wd)


def _moe_expert_kernel(texp_ref, nvalid_ref, x_ref, wg_ref, wu_ref, wd_ref, o_ref):
    i = pl.program_id(0)

    @pl.when(i < nvalid_ref[0])
    def _():
        h = x_ref[...].astype(BF16)
        act = (_silu(_dot(h, wg_ref[...])) * _dot(h, wu_ref[...])).astype(BF16)
        o_ref[...] = _dot(act, wd_ref[...])

    @pl.when(i >= nvalid_ref[0])
    def _():
        o_ref[...] = jnp.zeros(o_ref.shape, F32)


def _moe_experts(x_sorted, tile_expert, n_valid, wg, wu, wd, tm):
    n_tiles = tile_expert.shape[0]
    d = x_sorted.shape[1]
    n_exp, _, fe = wg.shape
    grid_spec = pltpu.PrefetchScalarGridSpec(
        num_scalar_prefetch=2,
        grid=(n_tiles,),
        in_specs=[
            pl.BlockSpec((tm, d), lambda i, te, nv: (jnp.minimum(i, nv[0] - 1), 0)),
            pl.BlockSpec((None, d, fe), lambda i, te, nv: (te[i], 0, 0)),
            pl.BlockSpec((None, d, fe), lambda i, te, nv: (te[i], 0, 0)),
            pl.BlockSpec((None, fe, d), lambda i, te, nv: (te[i], 0, 0)),
        ],
        out_specs=pl.BlockSpec((tm, d), lambda i, te, nv: (i, 0)),
    )
    return pl.pallas_call(
        _moe_expert_kernel,
        grid_spec=grid_spec,
        out_shape=jax.ShapeDtypeStruct((n_tiles * tm, d), F32),
        compiler_params=_cparams("arbitrary"),
        name="moe_experts",
    )(tile_expert, n_valid, x_sorted, wg, wu, wd)


def _start_row_gather(idx_ref, src_hbm, dst, sem, count):
    for r in range(count):
        _row_copy(src_hbm, idx_ref[0, r], dst, r, sem).start(priority=r % 2)


def _wait_row_gather(src_hbm, dst, sem):
    pltpu.make_async_copy(src_hbm.at[pl.ds(0, dst.shape[0])], dst, sem).wait()


def _moe_combine_kernel(pos0_ref, posnext_ref, ye_hbm, sel_ref, x1_ref, qg_ref, rg_ref, npost_ref, *rest,
                        geo, split_out):
    outs, (ybuf, sems) = rest[:-2], rest[-2:]
    i = pl.program_id(0)
    tm = geo.tm
    slot = i % 2

    @pl.when(i == 0)
    def _():
        _start_row_gather(pos0_ref, ye_hbm, ybuf.at[0], sems.at[0], 2 * tm)

    @pl.when(i + 1 < geo.nt)
    def _():
        _start_row_gather(posnext_ref, ye_hbm, ybuf.at[1 - slot], sems.at[1 - slot], 2 * tm)

    _wait_row_gather(ye_hbm, ybuf.at[slot], sems.at[slot])
    is_s = i == geo.nt - 1
    sel = sel_ref[...]
    lane = lax.broadcasted_iota(jnp.int32, sel.shape, 1)
    w1 = jnp.sum(jnp.where(lane == 2, sel, 0.0), axis=-1, keepdims=True)
    w2 = jnp.sum(jnp.where(lane == 3, sel, 0.0), axis=-1, keepdims=True)
    f = w1 * ybuf[slot, 0:tm, :] + w2 * ybuf[slot, tm:2 * tm, :]
    val = x1_ref[...] + _mod(is_s, qg_ref, rg_ref, geo.reps) * _rms(f, npost_ref[...])
    if split_out:
        _write_split(is_s, val, *outs)
    else:
        outs[0][...] = val


def _moe_combine(ye, pos, sel, x1, seq_mod, row_mod, geo_args, npost, split_out):
    tm, nt, tps, reps, d = geo_args
    geo = _Rows(tm, nt, tps, reps, d, 0)
    out_specs, out_shape = _out_rows(geo, d, split_out)
    pos3 = pos.reshape(nt, 1, 2 * tm)
    smem_blk = lambda fn: pl.BlockSpec((None, 1, 2 * tm), fn, memory_space=pltpu.SMEM)
    return pl.pallas_call(
        functools.partial(_moe_combine_kernel, geo=geo, split_out=split_out),
        grid=(nt,),
        in_specs=[smem_blk(lambda i: (0, 0, 0)),
                  smem_blk(lambda i: (jnp.minimum(i + 1, nt - 1), 0, 0)),
                  pl.BlockSpec(memory_space=pl.ANY),
                  geo.rows(LANES), geo.rows(d), geo.seq_mod(5), geo.row_mod(5, row_mod.shape[0]),
                  pl.BlockSpec((1, d), lambda i: (0, 0))],
        out_specs=out_specs,
        out_shape=out_shape,
        scratch_shapes=[pltpu.VMEM((2, 2 * tm, d), F32), pltpu.SemaphoreType.DMA((2,))],
        compiler_params=_cparams("arbitrary"),
        name="moe_combine",
    )(pos3, pos3, ye, sel, x1, seq_mod, row_mod, npost.reshape(1, d))


def _moe_tables(sel, counts, n_exp, tm_tok, tm_slot):
    t = sel.shape[0]
    counts = counts[0, :n_exp].astype(jnp.int32)
    tiles_e = (counts + tm_slot - 1) // tm_slot
    tile_end = jnp.cumsum(tiles_e)
    slot_start = (tile_end - tiles_e) * tm_slot
    choice = sel[:, 0:TOP_K].astype(jnp.int32)
    rank = sel[:, 2 * TOP_K:3 * TOP_K].astype(jnp.int32)
    onehot = choice[:, :, None] == jnp.arange(n_exp, dtype=jnp.int32)[None, None, :]
    pos = jnp.sum(jnp.where(onehot, slot_start[None, None, :], 0), axis=-1) + rank
    n_tiles = (TOP_K * t + n_exp * (tm_slot - 1)) // tm_slot
    tile_expert = jnp.minimum(
        jnp.sum((jnp.arange(n_tiles, dtype=jnp.int32)[:, None] >= tile_end[None, :]).astype(jnp.int32), axis=1),
        n_exp - 1)
    pos_tiles = jnp.concatenate([pos[:, k].reshape(-1, tm_tok) for k in range(TOP_K)], axis=1)
    pad_start = slot_start + counts
    pad_len = tiles_e * tm_slot - counts
    return tile_expert, tile_end[-1:].astype(jnp.int32), pos_tiles, pad_start, pad_len, n_tiles * tm_slot


def _largest_tile(t, want):
    tm = min(want, t)
    while t % tm:
        tm //= 2
    return tm


def kernel(x_prompt, x_sample, c_prompt, c_sample, state_conva, state_convb, state_ssm, ada_w, ada_b, norm_pre_mix,
           norm_post_mix, norm_pre_ffn, norm_post_ffn, w_in, w_out, conva_w, convb_w, convb_b, dt_bias, a_log, d_skip,
           ssd_norm, ffd_w_gate, ffd_w_up, ffd_w_down, moe_router, moe_w_gate, moe_w_up, moe_w_down):
    bp, seq, d = x_prompt.shape
    bs, steps, _ = x_sample.shape
    depth = w_in.shape[0]
    dc = conva_w.shape[-1]
    dx = convb_w.shape[-1]
    ds = ssd_norm.shape[-1]
    heads = dt_bias.shape[-1]
    p = ds // heads
    n = (dx - ds) // (2 * SSD_GROUPS)
    hg = heads // SSD_GROUPS
    n_exp = moe_router.shape[-1]
    ka, kb = conva_w.shape[1] - 1, convb_w.shape[1] - 1
    tm = steps * bs
    t_p = bp * seq
    assert seq % SSD_CHUNK == 0 and LANES % p == 0 and (hg * p) % LANES == 0 and n == LANES
    assert heads <= LANES and n_exp <= LANES and dc % LANES == 0 and dx % LANES == 0 and d % LANES == 0
    assert w_in.shape[-1] == 3 * dc + ds + dx + heads and seq % tm == 0 and tm % SUBLANES == 0
    assert bs % SUBLANES == 0 and (t_p // bs) % steps == 0
    nt = t_p // tm + 1
    geo_args = (tm, nt, seq // tm, steps, d)

    w_dt = jnp.pad(w_in[:, :, 3 * dc + ds + dx:], ((0, 0), (0, 0), (0, LANES - heads))).astype(BF16)
    router_p = jnp.pad(moe_router, ((0, 0), (0, 0), (0, LANES - n_exp)))
    padh = lambda a: jnp.pad(a, ((0, 0), (0, LANES - heads))).reshape(depth, 1, LANES)
    dtb_p, alog_p = padh(dt_bias), padh(a_log)
    dsk_e = jnp.repeat(d_skip, p, axis=-1).reshape(depth, 1, ds)
    hot = (jnp.arange(LANES)[:, None] == (jnp.arange(ds)[None, :] // p)).astype(BF16)
    emat = jnp.concatenate([hot, hot], axis=0)

    mod = _adaln(jnp.concatenate([c_prompt, c_sample], axis=0), ada_w, ada_b)
    seq_mod = jnp.pad(mod[:, :bp], ((0, 0), (0, 1), (0, 0))).reshape(depth, bp + 1, 1, 6 * d)
    row_mod = mod[:, bp:]

    bb = _largest_tile(bs, 16)
    x_all = None
    xs_tm = x_sample.transpose(1, 0, 2).reshape(tm, d)
    xp2d = x_prompt.reshape(t_p, d)
    sa_all = state_conva.reshape(depth, bs, ka * dc)
    sb_all = state_convb.reshape(depth, bs, kb * dx)
    ss_all = state_ssm.reshape(depth, bs, ds, n)
    pa, pb, ps, sa_l, sb_l = [], [], [], [], []
    ss_new = jnp.zeros((depth, bs, ds, n), F32)
    for i in range(depth):
        j = i // 2
        last = i == depth - 1
        x_in = (xp2d, xs_tm) if x_all is None else (x_all, None)
        conv_params = [conva_w[i], convb_w[i], convb_b[i].reshape(1, dx)]
        ya, z, xc, dtr, na, nb, gb_s, u_s, xbc_s = _inproj(
            *x_in, seq_mod[i], row_mod[i], geo_args, norm_pre_mix[i], w_in, i, w_dt[i], *conv_params, bp, ds)
        params = [dtb_p[i], alog_p[i], dsk_e[i], ssd_norm[i].reshape(1, ds), emat]
        yb_p, ns = _ssd_prompt(z, xc, dtr, bp, seq, params, n, p, hg)
        r3 = lambda a: a.reshape(a.shape[0] // bs, bs, a.shape[-1])
        ya_s, yb_s, sna, snb, ss_new = _ssd_sample(
            r3(gb_s), r3(u_s), r3(xbc_s), r3(z), r3(dtr), t_p // (bs * steps), steps, bs, i, sa_all, sb_all, ss_all,
            ss_new, conv_params + params, n, p, hg, bb)
        pa.append(na)
        pb.append(nb)
        ps.append(ns.reshape(bp, heads, p, n))
        sa_l.append(sna.reshape(bs, ka, dc))
        sb_l.append(snb.reshape(bs, kb, dx))

        moe = i % 2 == 1
        res = _outproj(ya, ya_s.reshape(tm, dc), yb_p, yb_s.reshape(tm, ds), w_out, i, *x_in, seq_mod[i], row_mod[i],
                       geo_args,
                       norm_post_mix[i], norm_pre_ffn[i], router_p[j] if moe else None, n_exp if moe else 0)
        if moe:
            x1, h2, sel, counts = res
            tile_expert, n_valid, pos_tiles, pad_start, pad_len, n_slots = _moe_tables(sel, counts, n_exp, tm, MOE_TILE)
            fe = moe_w_gate.shape[-1]
            x_sorted, wg_b, wu_b, wd_b = _moe_dispatch(
                h2, pos_tiles, pad_start, pad_len, n_valid, n_slots, tm, MOE_TILE, moe_w_gate[j].reshape(n_exp * d, fe),
                moe_w_up[j].reshape(n_exp * d, fe), moe_w_down[j].reshape(n_exp * fe, d))
            ye = _moe_experts(x_sorted, tile_expert, n_valid, wg_b.reshape(n_exp, d, fe), wu_b.reshape(n_exp, d, fe),
                              wd_b.reshape(n_exp, fe, d), MOE_TILE)
            out = _moe_combine(ye, pos_tiles, sel, x1, seq_mod[i], row_mod[i], geo_args, norm_post_ffn[i], last)
        else:
            x1, h2 = res
            out = _ffn_dense(h2, ffd_w_gate, ffd_w_up, ffd_w_down, j, x1, seq_mod[i], row_mod[i], geo_args,
                             norm_post_ffn[i], last)
        if last:
            y_p, y_s = out
        else:
            x_all = out[0]

    y_prompt = y_p.reshape(bp, seq, d)
    y_sample = y_s.reshape(steps, bs, d).transpose(1, 0, 2)
    return (y_prompt, y_sample, jnp.stack(pa), jnp.stack(pb), jnp.stack(ps),
            jnp.stack(sa_l), jnp.stack(sb_l), ss_new.reshape(depth, bs, heads, p, n))
```

```python
import functools

import jax
import jax.numpy as jnp
from jax import lax
from jax.experimental import pallas as pl
from jax.experimental.pallas import tpu as pltpu

EPS = 1e-6
SSD_GROUPS = 2
SSD_CHUNK = 128
TOP_K = 2
LOG2E = 1.4426950408889634
LANES = 128
SUBLANES = 8
VMEM_LIMIT_BYTES = 56 * 1024 * 1024
MOE_TILE = 512

F32 = jnp.float32
BF16 = jnp.bfloat16


def _cparams(*sem):
    return pltpu.CompilerParams(dimension_semantics=sem, vmem_limit_bytes=VMEM_LIMIT_BYTES)


def _resident(shape):
    return pl.BlockSpec(shape, lambda *_: (0,) * len(shape), pipeline_mode=pl.Buffered(1))


def _silu(x):
    return x * (1.0 / (1.0 + jnp.exp(-x)))


def _softplus(x):
    return jnp.maximum(x, 0.0) + jnp.log1p(jnp.exp(-jnp.abs(x)))


def _rms(x, g):
    return x * lax.rsqrt(jnp.mean(x * x, axis=-1, keepdims=True) + EPS) * g


def _dot(a, b):
    return jnp.dot(a, b, preferred_element_type=F32)


def _dot_nt(a, b):
    return lax.dot_general(a, b, (((1,), (1,)), ((), ())), preferred_element_type=F32)


def _dot_tn(a, b):
    return lax.dot_general(a, b, (((0,), (0,)), ((), ())), preferred_element_type=F32)


def _chunk(*widths):
    return next(c for c in (512, 256, LANES) if all(w % c == 0 for w in widths))


def _adaln_kernel(c_ref, w_ref, b_ref, o_ref):
    s = _silu(c_ref[...]).astype(BF16)
    o_ref[...] = _dot(s, w_ref[...].astype(BF16)) + b_ref[...]


def _adaln(c_all, ada_w, ada_b):
    depth, d, d6 = ada_w.shape
    rows = c_all.shape[0]
    tn = _chunk(d6 // 6) * 2
    return pl.pallas_call(
        _adaln_kernel,
        grid=(depth, d6 // tn),
        in_specs=[
            pl.BlockSpec((rows, d), lambda l, j: (0, 0)),
            pl.BlockSpec((None, d, tn), lambda l, j: (l, 0, j)),
            pl.BlockSpec((None, 1, tn), lambda l, j: (l, 0, j)),
        ],
        out_specs=pl.BlockSpec((None, rows, tn), lambda l, j: (l, 0, j)),
        out_shape=jax.ShapeDtypeStruct((depth, rows, d6), F32),
        compiler_params=_cparams("arbitrary", "arbitrary"),
        name="adaln",
    )(c_all, ada_w, ada_b.reshape(depth, 1, d6))


class _Rows:
    def __init__(self, tm, nt, tps, reps, d, pro):
        self.tm, self.nt, self.tps, self.reps, self.d, self.pro = tm, nt, tps, reps, d, pro

    def tile(self, i):
        return jnp.maximum(i - self.pro, 0)

    def rows(self, w):
        return pl.BlockSpec((self.tm, w), lambda i, *_: (self.tile(i), 0))

    def prompt_rows(self, w):
        return pl.BlockSpec((self.tm, w), lambda i, *_: (jnp.minimum(self.tile(i), self.nt - 2), 0))

    def sample_rows(self, w):
        return pl.BlockSpec((self.tm, w), lambda i, *_: (0, 0))

    def seq_mod(self, k):
        return pl.BlockSpec((None, 1, self.d), lambda i, *_: (self.tile(i) // self.tps, 0, k))

    def row_mod(self, k, rows):
        return pl.BlockSpec((rows, self.d), lambda i, *_: (0, k))


def _mod(is_sample, seq_ref, row_ref, reps):
    rowm = jnp.concatenate([row_ref[...]] * reps, axis=0)
    return seq_ref[...] + jnp.where(is_sample, rowm, 0.0)


def _causal_conv_cols(cur, c0, w_ref, tail, ext, nst_ref, keep_state, first, emit):
    tm, cw = cur.shape
    kp = w_ref.shape[0] - 1
    wcol = lambda j: w_ref[j:j + 1, c0:c0 + cw]
    ext[0:SUBLANES, :] = jnp.where(first, 0.0, tail[:, c0:c0 + cw])
    ext[SUBLANES:SUBLANES + tm, :] = cur
    acc = wcol(kp) * cur
    for j in range(kp):
        off = SUBLANES - (kp - j)
        acc = acc + wcol(j) * ext[off:off + tm, :]
    tail[:, c0:c0 + cw] = ext[tm:tm + SUBLANES, :]
    nst_ref[:, c0:c0 + cw] = jnp.where(keep_state, nst_ref[:, c0:c0 + cw], ext[SUBLANES + tm - kp:SUBLANES + tm, :])
    emit(acc)


def _inproj_kernel(*refs, geo, dc, ds, dx, cw, split_x, zero_fill):
    if split_x:
        xp_ref, xs_ref, *refs = refs
    else:
        xp_ref, *refs = refs
    (qsh_ref, qsc_ref, rsh_ref, rsc_ref, g_ref, wch_ref, wdt_ref, caw_ref, cbw_ref, cbb_ref,
     ya_ref, z_ref, xc_ref, dt_ref, ncap_ref, ncbp_ref, gbs_ref, us_ref, xbcs_ref, *rest) = refs
    wbf, ext, tail_u, tail_x = rest[-4:]
    i = pl.program_id(0)

    @pl.when(i < geo.pro)
    def _():
        wbf[i] = wch_ref[...].astype(BF16)

    @pl.when(i >= geo.pro)
    def _():
        r = i - geo.pro
        is_s = r == geo.nt - 1
        first = r % geo.tps == 0
        x = jnp.where(is_s, xs_ref[...], xp_ref[...]) if split_x else xp_ref[...]
        sc = _mod(is_s, qsc_ref, rsc_ref, geo.reps)
        sh = _mod(is_s, qsh_ref, rsh_ref, geo.reps)
        h = (_rms(x, g_ref[...]) * (1.0 + sc) + sh).astype(BF16)
        nc = dc // cw
        for a in range(nc):
            c0 = a * cw
            gate_b = _dot(h, wbf[a])
            u = _dot(h, wbf[nc + a]) * _dot(h, wbf[2 * nc + a])
            gbs_ref[:, c0:c0 + cw] = gate_b
            us_ref[:, c0:c0 + cw] = u

            def emit_a(v, c0=c0, gate_b=gate_b):
                ya_ref[:, c0:c0 + cw] = (gate_b * v).astype(BF16)

            _causal_conv_cols(u, c0, caw_ref, tail_u, ext, ncap_ref, is_s, first, emit_a)
        for a in range(ds // cw):
            z_ref[:, a * cw:(a + 1) * cw] = _dot(h, wbf[3 * nc + a])
        for a in range(dx // cw):
            c0 = a * cw
            xbc = _dot(h, wbf[3 * nc + ds // cw + a])
            xbcs_ref[:, c0:c0 + cw] = xbc

            def emit_b(v, c0=c0):
                xc_ref[:, c0:c0 + cw] = _silu(v + cbb_ref[:, c0:c0 + cw])

            _causal_conv_cols(xbc, c0, cbw_ref, tail_x, ext, ncbp_ref, is_s, first, emit_b)
        dt_ref[...] = _dot(h, wdt_ref[...])
        if zero_fill:
            rest[0][...] = jnp.zeros(rest[0].shape, F32)


def _inproj(x_p, x_s, seq_mod, row_mod, geo_args, norm_w, w_in, layer, w_dt, caw, cbw, cbb, bp, ds, zero_rows=0):
    tm, nt, tps, reps, d = geo_args
    dc, dx = caw.shape[1], cbw.shape[1]
    ka, kb = caw.shape[0] - 1, cbw.shape[0] - 1
    cw = _chunk(dc, ds, dx)
    npro = (3 * dc + ds + dx) // cw
    geo = _Rows(tm, nt, tps, reps, d, npro)
    t_all = tm * nt
    split_x = x_s is not None
    xin = [geo.prompt_rows(d), geo.sample_rows(d)] if split_x else [geo.rows(d)]
    xargs = [x_p, x_s] if split_x else [x_p]
    nrow = row_mod.shape[0]
    seq_blk = lambda k, w: pl.BlockSpec((None, k, w), lambda i: (jnp.minimum(geo.tile(i) // tps, bp - 1), 0, 0))
    zero_specs, zero_shapes = [], []
    if zero_rows:
        zero_specs = [pl.BlockSpec((zero_rows // (nt - 1), LANES), lambda i: (jnp.minimum(geo.tile(i), nt - 2), 0))]
        zero_shapes = [jax.ShapeDtypeStruct((zero_rows, LANES), F32)]
    return pl.pallas_call(
        functools.partial(_inproj_kernel, geo=geo, dc=dc, ds=ds, dx=dx, cw=cw, split_x=split_x,
                          zero_fill=bool(zero_rows)),
        grid=(npro + nt,),
        in_specs=xin + [
            geo.seq_mod(0), geo.seq_mod(1), geo.row_mod(0, nrow), geo.row_mod(1, nrow),
            pl.BlockSpec((1, d), lambda i: (0, 0)),
            pl.BlockSpec((None, d, cw), lambda i: (layer, 0, jnp.minimum(i, npro - 1))),
            _resident((d, LANES)), _resident(caw.shape), _resident(cbw.shape), _resident(cbb.shape),
        ],
        out_specs=[geo.rows(dc), geo.rows(ds), geo.rows(dx), geo.rows(LANES), seq_blk(ka, dc), seq_blk(kb, dx),
                   geo.sample_rows(dc), geo.sample_rows(dc), geo.sample_rows(dx)] + zero_specs,
        out_shape=[
            jax.ShapeDtypeStruct((t_all, dc), BF16),
            jax.ShapeDtypeStruct((t_all, ds), F32),
            jax.ShapeDtypeStruct((t_all, dx), F32),
            jax.ShapeDtypeStruct((t_all, LANES), F32),
            jax.ShapeDtypeStruct((bp, ka, dc), F32),
            jax.ShapeDtypeStruct((bp, kb, dx), F32),
            jax.ShapeDtypeStruct((tm, dc), F32),
            jax.ShapeDtypeStruct((tm, dc), F32),
            jax.ShapeDtypeStruct((tm, dx), F32),
        ] + zero_shapes,
        scratch_shapes=[pltpu.VMEM((npro, d, cw), BF16), pltpu.VMEM((SUBLANES + tm, cw), F32),
                        pltpu.VMEM((SUBLANES, dc), F32), pltpu.VMEM((SUBLANES, dx), F32)],
        compiler_params=_cparams("arbitrary"),
        name="inproj",
    )(*xargs, seq_mod, seq_mod, row_mod, row_mod, norm_w.reshape(1, d), w_in, w_dt, caw, cbw, cbb)


def _split_hi_lo(v):
    hi = v.astype(BF16)
    lo = (v - hi.astype(F32)).astype(BF16)
    return jnp.concatenate([hi, lo], axis=1)


def _diag_block(cb, mask, cum, cum_t, dt_t, xs, lane, g, hg, gw, p):
    hpl = LANES // p
    parts = []
    for slab in range(gw // LANES):
        lhs, rhs = [], []
        xslab = xs[:, g * gw + slab * LANES:g * gw + (slab + 1) * LANES]
        for j in range(hpl):
            h = g * hg + slab * hpl + j
            seg = cum[:, h:h + 1] - cum_t[h:h + 1, :]
            m = cb * jnp.where(mask, jnp.exp2(seg), 0.0) * dt_t[h:h + 1, :]
            lhs.append(m.astype(BF16))
            rhs.append(jnp.where((lane >= j * p) & (lane < (j + 1) * p), xslab, 0.0).astype(BF16))
        parts.append(_dot(jnp.concatenate(lhs, axis=1), jnp.concatenate(rhs, axis=0)))
    return jnp.concatenate(parts, axis=1)


def _gated_group_norm(y, xs_g, z_g, dsk_g, snorm_g):
    y = (y + dsk_g * xs_g) * _silu(z_g)
    return (y * lax.rsqrt(jnp.mean(y * y, axis=-1, keepdims=True) + EPS) * snorm_g).astype(BF16)


def _ssd_prompt_kernel(z_ref, xc_ref, dt_ref, dtb_ref, alog_ref, dsk_ref, snorm_ref, e_ref, yb_ref, nss_ref, st,
                       *, ds, n, p, hg, sub):
    c = pl.program_id(1)
    q = SSD_CHUNK
    gw = hg * p

    @pl.when(c == 0)
    def _():
        st[...] = jnp.zeros(st.shape, F32)

    row = lax.broadcasted_iota(jnp.int32, (q, LANES), 0)
    tril = lax.broadcasted_iota(jnp.int32, (q, q), 0) >= lax.broadcasted_iota(jnp.int32, (q, q), 1)
    lane = lax.broadcasted_iota(jnp.int32, (q, LANES), 1)
    neg_a = -jnp.exp(alog_ref[...]) * LOG2E
    for k in range(sub):
        rs = slice(k * q, (k + 1) * q)
        xs = xc_ref[rs, 0:ds]
        dt = _softplus(dt_ref[rs, :] + dtb_ref[...])
        cum = dt * neg_a
        step = 1
        while step < q:
            cum = cum + jnp.where(row >= step, pltpu.roll(cum, step, axis=0), 0.0)
            step *= 2
        cum_last = cum[q - 1:q, :]
        w_exp = _dot(_split_hi_lo(dt * jnp.exp2(cum_last - cum)), e_ref[...])
        ecum_exp = _dot(_split_hi_lo(jnp.exp2(cum)), e_ref[...])
        cum_t = cum.T
        dt_t = dt.T
        for g in range(SSD_GROUPS):
            gsl = slice(g * gw, (g + 1) * gw)
            bm = xc_ref[rs, ds + g * n:ds + (g + 1) * n]
            cm = xc_ref[rs, ds + SSD_GROUPS * n + g * n:ds + SSD_GROUPS * n + (g + 1) * n].astype(BF16)
            bm_t = bm.T.astype(BF16)
            y = _diag_block(_dot(cm, bm_t), tril, cum, cum_t, dt_t, xs, lane, g, hg, gw, p)
            s_prev = st[g]
            y = y + _dot(cm, s_prev.astype(BF16)) * ecum_exp[:, gsl]
            xw = (xs[:, gsl] * w_exp[:, gsl]).astype(BF16)
            st[g] = s_prev * ecum_exp[q - 1:q, gsl] + _dot(bm_t, xw)
            yb_ref[rs, gsl] = _gated_group_norm(y, xs[:, gsl], z_ref[rs, gsl], dsk_ref[:, gsl], snorm_ref[:, gsl])

    @pl.when(c == pl.num_programs(1) - 1)
    def _():
        for g in range(SSD_GROUPS):
            for slab in range(gw // LANES):
                r0 = g * gw + slab * LANES
                nss_ref[r0:r0 + LANES, :] = st[g, :, slab * LANES:(slab + 1) * LANES].T


def _ssd_prompt(z, xc, dtr, bsz, seq, params, n, p, hg):
    ds = z.shape[1]
    dx = xc.shape[1]
    sub = next(k for k in (4, 2, 1) if seq % (k * SSD_CHUNK) == 0)
    rows = sub * SSD_CHUNK
    nc = seq // rows
    row = lambda w: pl.BlockSpec((rows, w), lambda b, c: (b * nc + c, 0))
    return pl.pallas_call(
        functools.partial(_ssd_prompt_kernel, ds=ds, n=n, p=p, hg=hg, sub=sub),
        grid=(bsz, nc),
        in_specs=[row(ds), row(dx), row(LANES)] + [_resident(a.shape) for a in params],
        out_specs=[row(ds), pl.BlockSpec((None, ds, n), lambda b, c: (b, 0, 0))],
        out_shape=[jax.ShapeDtypeStruct((bsz * seq, ds), BF16), jax.ShapeDtypeStruct((bsz, ds, n), F32)],
        scratch_shapes=[pltpu.VMEM((SSD_GROUPS, n, hg * p), F32)],
        compiler_params=_cparams("arbitrary", "arbitrary"),
        name="ssd_prompt",
    )(z, xc, dtr, *params)


def _ssd_sample_kernel(gb_ref, u_ref, xbc_ref, z_ref, dt_ref, sa_ref, sb_ref, ss_ref, caw_ref, cbw_ref, cbb_ref,
                       dtb_ref, alog_ref, dsk_ref, snorm_ref, e_ref, ya_ref, yb_ref, nca_ref, ncb_ref, nss_ref,
                       *, dc, ds, dx, n, p, hg, steps, bb):
    gw = hg * p
    rows = steps * bb
    ka = caw_ref.shape[0] - 1
    kb = cbw_ref.shape[0] - 1

    def conv(cur_ref, st_ref, w_ref, kprev, width):
        hist = [st_ref[:, j * width:(j + 1) * width] for j in range(kprev)] + [cur_ref[t] for t in range(steps)]
        outs = []
        for t in range(steps):
            acc = w_ref[kprev:kprev + 1, :] * hist[t + kprev]
            for j in range(kprev):
                acc = acc + w_ref[j:j + 1, :] * hist[t + j]
            outs.append(acc)
        return outs, hist[len(hist) - kprev:]

    v, new_a = conv(u_ref, sa_ref, caw_ref, ka, dc)
    for t in range(steps):
        ya_ref[t] = (gb_ref[t] * v[t]).astype(BF16)
    for j in range(ka):
        nca_ref[:, j * dc:(j + 1) * dc] = new_a[j]
    xcs, new_b = conv(xbc_ref, sb_ref, cbw_ref, kb, dx)
    for j in range(kb):
        ncb_ref[:, j * dx:(j + 1) * dx] = new_b[j]
    xc = _silu(jnp.concatenate(xcs, axis=0) + cbb_ref[...])
    xs = xc[:, 0:ds]

    dt = _softplus(jnp.concatenate([dt_ref[t] for t in range(steps)], axis=0) + dtb_ref[...])
    da = dt * (-jnp.exp(alog_ref[...]) * LOG2E)
    cums = [da[0:bb]]
    for t in range(1, steps):
        cums.append(cums[-1] + da[t * bb:(t + 1) * bb])
    cum = jnp.concatenate(cums, axis=0)
    cum_last = jnp.concatenate([cums[-1]] * steps, axis=0)
    w_exp = _dot(_split_hi_lo(dt * jnp.exp2(cum_last - cum)), e_ref[...])
    ecum_exp = _dot(_split_hi_lo(jnp.exp2(cum)), e_ref[...])

    def pad_t(a):
        a = jnp.concatenate([a, jnp.zeros((LANES - rows, LANES), F32)], axis=0) if rows < LANES else a
        return a.T[:, 0:rows]

    cum_t = pad_t(cum)
    dt_t = pad_t(dt)

    ri = lax.broadcasted_iota(jnp.int32, (rows, rows), 0)
    ci = lax.broadcasted_iota(jnp.int32, (rows, rows), 1)
    same = ((ri % bb) == (ci % bb)) & (ri >= ci)
    lane = lax.broadcasted_iota(jnp.int32, (rows, LANES), 1)
    rowid = lax.broadcasted_iota(jnp.int32, (rows, 1), 0) % bb
    seqlane = lax.broadcasted_iota(jnp.int32, (LANES, LANES), 1)
    nslab = gw // LANES
    for g in range(SSD_GROUPS):
        gsl = slice(g * gw, (g + 1) * gw)
        bm = xc[:, ds + g * n:ds + (g + 1) * n].astype(BF16)
        cm = xc[:, ds + SSD_GROUPS * n + g * n:ds + SSD_GROUPS * n + (g + 1) * n].astype(BF16)
        y_diag = _diag_block(_dot_nt(cm, bm), same, cum, cum_t, dt_t, xs, lane, g, hg, gw, p)
        xw = (xs[:, gsl] * w_exp[:, gsl]).astype(BF16)
        dec = ecum_exp[(steps - 1) * bb:steps * bb, gsl]
        dec = jnp.concatenate([dec, jnp.zeros((LANES - bb, gw), F32)], axis=0)
        dec_t = [dec[:, s * LANES:(s + 1) * LANES].T for s in range(nslab)]

        def per_seq(b, y_off, g=g, cm=cm, bm=bm, xw=xw, dec_t=dec_t):
            r0 = g * gw
            s0 = ss_ref[b, r0:r0 + gw, :]
            y_off = jnp.where(rowid == b, _dot_nt(cm, s0.astype(BF16)), y_off)
            upd = _dot_tn(jnp.where(rowid == b, xw, jnp.zeros_like(xw)), bm)
            for s in range(nslab):
                dcol = jnp.sum(jnp.where(seqlane == b, dec_t[s], 0.0), axis=1, keepdims=True)
                nss_ref[b, r0 + s * LANES:r0 + (s + 1) * LANES, :] = (
                    s0[s * LANES:(s + 1) * LANES, :] * dcol + upd[s * LANES:(s + 1) * LANES, :])
            return y_off

        y_off = lax.fori_loop(0, bb, per_seq, jnp.zeros((rows, gw), F32), unroll=4)
        zg = jnp.concatenate([z_ref[t, :, gsl] for t in range(steps)], axis=0)
        yn = _gated_group_norm(y_diag + y_off * ecum_exp[:, gsl], xs[:, gsl], zg, dsk_ref[:, gsl], snorm_ref[:, gsl])
        for t in range(steps):
            yb_ref[t, :, gsl] = yn[t * bb:(t + 1) * bb]


def _ssd_sample_aliased_kernel(*refs, **kw):
    _ssd_sample_kernel(*refs[1:], **kw)


def _ssd_sample(gb, u, xbc, z, dtr, blk0, steps, bs, layer, sa, sb, ss, prev_ss, params, n, p, hg, bb):
    dc, ds, dx = u.shape[2], z.shape[2], xbc.shape[2]
    ka, kb = params[0].shape[0] - 1, params[1].shape[0] - 1
    own = lambda w: pl.BlockSpec((steps, bb, w), lambda i: (0, i, 0))
    tok = lambda w: pl.BlockSpec((steps, bb, w), lambda i: (blk0, i, 0))
    state = pl.BlockSpec((None, bb, ds, n), lambda i: (layer, i, 0, 0))
    return pl.pallas_call(
        functools.partial(_ssd_sample_aliased_kernel, dc=dc, ds=ds, dx=dx, n=n, p=p, hg=hg, steps=steps, bb=bb),
        grid=(bs // bb,),
        in_specs=[pl.BlockSpec(memory_space=pl.ANY), own(dc), own(dc), own(dx), tok(ds), tok(LANES),
                  pl.BlockSpec((None, bb, ka * dc), lambda i: (layer, i, 0)),
                  pl.BlockSpec((None, bb, kb * dx), lambda i: (layer, i, 0)), state]
        + [_resident(a.shape) for a in params],
        out_specs=[own(dc), own(ds), pl.BlockSpec((bb, ka * dc), lambda i: (i, 0)),
                   pl.BlockSpec((bb, kb * dx), lambda i: (i, 0)), state],
        out_shape=[jax.ShapeDtypeStruct((steps, bs, dc), BF16), jax.ShapeDtypeStruct((steps, bs, ds), BF16),
                   jax.ShapeDtypeStruct((bs, ka * dc), F32), jax.ShapeDtypeStruct((bs, kb * dx), F32),
                   jax.ShapeDtypeStruct(prev_ss.shape, F32)],
        input_output_aliases={0: 4},
        compiler_params=_cparams("arbitrary"),
        name="ssd_sample",
    )(prev_ss, gb, u, xbc, z, dtr, sa, sb, ss, *params)


def _route(logits, n_exp, cnt_ref):
    rows = logits.shape[0]
    lane = lax.broadcasted_iota(jnp.int32, logits.shape, 1).astype(F32)
    valid = lane < n_exp
    logits = jnp.where(valid, logits, -jnp.inf)
    e = jnp.exp(logits - jnp.max(logits, axis=-1, keepdims=True))
    prob = jnp.where(valid, e / jnp.sum(e, axis=-1, keepdims=True), -1.0)
    big = float(LANES)
    m1 = jnp.max(prob, axis=-1, keepdims=True)
    i1 = jnp.min(jnp.where(prob == m1, lane, big), axis=-1, keepdims=True)
    rest = jnp.where(lane == i1, -1.0, prob)
    m2 = jnp.max(rest, axis=-1, keepdims=True)
    i2 = jnp.min(jnp.where(rest == m2, lane, big), axis=-1, keepdims=True)
    den = m1 + m2
    chosen = jnp.where((lane == i1) | (lane == i2), 1.0, 0.0)
    before = (lax.broadcasted_iota(jnp.int32, (rows, rows), 0) > lax.broadcasted_iota(jnp.int32, (rows, rows), 1))
    rank = _dot(jnp.where(before, 1.0, 0.0).astype(BF16), chosen.astype(BF16)) + cnt_ref[0:1, :]
    r1 = jnp.sum(jnp.where(lane == i1, rank, 0.0), axis=-1, keepdims=True)
    r2 = jnp.sum(jnp.where(lane == i2, rank, 0.0), axis=-1, keepdims=True)
    cnt_ref[0:1, :] = cnt_ref[0:1, :] + jnp.sum(chosen, axis=0, keepdims=True)
    cols = [i1, i2, m1 / den, m2 / den, r1, r2]
    out = jnp.zeros(logits.shape, F32)
    for k, col in enumerate(cols):
        out = jnp.where(lane == float(k), col, out)
    return out


def _outproj_kernel(yap_ref, yas_ref, ybp_ref, ybs_ref, wch_ref, *rest, geo, n_exp, kc, split_x):
    if split_x:
        xp_ref, xs_ref, *rest = rest
    else:
        xp_ref, *rest = rest
    qg_ref, qsh_ref, qsc_ref, rg_ref, rsh_ref, rsc_ref, npost_ref, npre_ref, *rest = rest
    if n_exp:
        r_ref, x1_ref, h2_ref, sel_ref, cnt_ref, wbf = rest
    else:
        x1_ref, h2_ref, wbf = rest
    i = pl.program_id(0)

    @pl.when(i < geo.pro)
    def _():
        wbf[pl.ds(pl.multiple_of(i * kc, kc), kc), :] = wch_ref[...].astype(BF16)
        if n_exp:
            cnt_ref[...] = jnp.zeros(cnt_ref.shape, F32)

    @pl.when(i >= geo.pro)
    def _():
        is_s = i - geo.pro == geo.nt - 1
        da = yap_ref.shape[1]
        nh = 2 if geo.reps % 2 == 0 else 1
        th = geo.tm // nh
        for hs in range(nh):
            rs = slice(hs * th, (hs + 1) * th)
            mod = lambda q_ref, r_ref: _mod(is_s, q_ref, r_ref, geo.reps // nh)
            mix = (_dot(jnp.where(is_s, yas_ref[rs, :], yap_ref[rs, :]), wbf[0:da, :])
                   + _dot(jnp.where(is_s, ybs_ref[rs, :], ybp_ref[rs, :]), wbf[da:, :]))
            x = jnp.where(is_s, xs_ref[rs, :], xp_ref[rs, :]) if split_x else xp_ref[rs, :]
            x1 = x + mod(qg_ref, rg_ref) * _rms(mix, npost_ref[...])
            x1_ref[rs, :] = x1
            h2 = _rms(x1, npre_ref[...]) * (1.0 + mod(qsc_ref, rsc_ref)) + mod(qsh_ref, rsh_ref)
            h2_ref[rs, :] = h2.astype(h2_ref.dtype)
            if n_exp:
                h_hi = h2.astype(BF16)
                h_lo = (h2 - h_hi.astype(F32)).astype(BF16)
                r = r_ref[...]
                r_hi = r.astype(BF16)
                r_lo = (r - r_hi.astype(F32)).astype(BF16)
                both = _dot(h_hi, jnp.concatenate([r_hi, r_lo], axis=1))
                logits = both[:, 0:LANES] + (_dot(h_lo, r_hi) + both[:, LANES:])
                sel_ref[rs, :] = _route(logits, n_exp, cnt_ref)


def _outproj(ya_p, ya_s, yb_p, yb_s, w_out, layer, x_p, x_s, seq_mod, row_mod, geo_args, npost, npre, router,
             n_exp):
    tm, nt, tps, reps, d = geo_args
    da, db = ya_p.shape[1], yb_p.shape[1]
    dm = da + db
    kc = _chunk(dm) // 2
    npro = dm // kc
    geo = _Rows(tm, nt, tps, reps, d, npro)
    t_all = tm * nt
    nrow = row_mod.shape[0]
    vec = pl.BlockSpec((1, d), lambda i: (0, 0))
    split_x = x_s is not None
    in_specs = [geo.prompt_rows(da), geo.sample_rows(da), geo.prompt_rows(db), geo.sample_rows(db),
                pl.BlockSpec((None, kc, d), lambda i: (layer, jnp.minimum(i, npro - 1), 0))]
    in_specs += [geo.prompt_rows(d), geo.sample_rows(d)] if split_x else [geo.rows(d)]
    in_specs += [geo.seq_mod(2), geo.seq_mod(3), geo.seq_mod(4),
                 geo.row_mod(2, nrow), geo.row_mod(3, nrow), geo.row_mod(4, nrow), vec, vec]
    args = [ya_p, ya_s, yb_p, yb_s, w_out] + ([x_p, x_s] if split_x else [x_p])
    args += [seq_mod, seq_mod, seq_mod, row_mod, row_mod, row_mod, npost.reshape(1, d), npre.reshape(1, d)]
    out_specs = [geo.rows(d), geo.rows(d)]
    out_shape = [jax.ShapeDtypeStruct((t_all, d), F32), jax.ShapeDtypeStruct((t_all, d), F32 if n_exp else BF16)]
    if n_exp:
        in_specs.append(_resident((d, LANES)))
        args.append(router)
        out_specs += [geo.rows(LANES), pl.BlockSpec((SUBLANES, LANES), lambda i: (0, 0))]
        out_shape += [jax.ShapeDtypeStruct((t_all, LANES), F32), jax.ShapeDtypeStruct((SUBLANES, LANES), F32)]
    return pl.pallas_call(
        functools.partial(_outproj_kernel, geo=geo, n_exp=n_exp, kc=kc, split_x=split_x),
        grid=(npro + nt,),
        in_specs=in_specs,
        out_specs=out_specs,
        out_shape=out_shape,
        scratch_shapes=[pltpu.VMEM((dm, d), BF16)],
        compiler_params=_cparams("arbitrary"),
        name="outproj",
    )(*args)


def _write_split(is_s, val, outp_ref, outs_ref):
    @pl.when(jnp.logical_not(is_s))
    def _():
        outp_ref[...] = val

    @pl.when(is_s)
    def _():
        outs_ref[...] = val


def _out_rows(geo, d, split_out):
    t_all = geo.tm * geo.nt
    if split_out:
        return ([geo.prompt_rows(d), geo.sample_rows(d)],
                [jax.ShapeDtypeStruct((t_all - geo.tm, d), F32), jax.ShapeDtypeStruct((geo.tm, d), F32)])
    return [geo.rows(d)], [jax.ShapeDtypeStruct((t_all, d), F32)]


def _mix_ffn_kernel(yap_ref, yas_ref, ybp_ref, ybs_ref, wo_ref, wg_ref, wu_ref, wd_ref, *rest, geo, kc, cf, n_out,
                    n_ffn, split_x, split_out):
    if split_x:
        xp_ref, xs_ref, *rest = rest
    else:
        xp_ref, *rest = rest
    (qg1_ref, qsh_ref, qsc_ref, qg2_ref, rg1_ref, rsh_ref, rsc_ref, rg2_ref, npost_ref, npre_ref, nffn_ref,
     *rest) = rest
    outs, (wob, wgb, wub, wdb, act) = rest[:-5], rest[-5:]
    i = pl.program_id(0)

    @pl.when(i < n_out)
    def _():
        wob[pl.ds(pl.multiple_of(i * kc, kc), kc), :] = wo_ref[...].astype(BF16)

    @pl.when(i < n_ffn)
    def _():
        wgb[i] = wg_ref[...].astype(BF16)
        wub[i] = wu_ref[...].astype(BF16)
        wdb[pl.ds(pl.multiple_of(i * cf, cf), cf), :] = wd_ref[...].astype(BF16)

    @pl.when(i >= geo.pro)
    def _():
        is_s = i - geo.pro == geo.nt - 1
        da = yap_ref.shape[1]
        nh = 2 if geo.reps % 2 == 0 else 1
        th = geo.tm // nh
        vals = []
        for hs in range(nh):
            rs = slice(hs * th, (hs + 1) * th)
            mod = lambda q_ref, r_ref: _mod(is_s, q_ref, r_ref, geo.reps // nh)
            mix = (_dot(jnp.where(is_s, yas_ref[rs, :], yap_ref[rs, :]), wob[0:da, :])
                   + _dot(jnp.where(is_s, ybs_ref[rs, :], ybp_ref[rs, :]), wob[da:, :]))
            x = jnp.where(is_s, xs_ref[rs, :], xp_ref[rs, :]) if split_x else xp_ref[rs, :]
            x1 = x + mod(qg1_ref, rg1_ref) * _rms(mix, npost_ref[...])
            h = (_rms(x1, npre_ref[...]) * (1.0 + mod(qsc_ref, rsc_ref)) + mod(qsh_ref, rsh_ref)).astype(BF16)
            for c in range(n_ffn):
                act[rs, c * cf:(c + 1) * cf] = (_silu(_dot(h, wgb[c])) * _dot(h, wub[c])).astype(BF16)
            f = _dot(act[rs, :], wdb[...])
            vals.append(x1 + mod(qg2_ref, rg2_ref) * _rms(f, nffn_ref[...]))
        val = jnp.concatenate(vals, axis=0)
        if split_out:
            _write_split(is_s, val, *outs)
        else:
            outs[0][...] = val


def _mix_ffn(ya_p, ya_s, yb_p, yb_s, w_out, layer, wg, wu, wd, j, x_p, x_s, seq_mod, row_mod, geo_args, npost, npre,
             nffn, split_out):
    tm, nt, tps, reps, d = geo_args
    da, db = ya_p.shape[1], yb_p.shape[1]
    dm = da + db
    f = wg.shape[2]
    kc = _chunk(dm) // 2
    cf = 256 if f % 256 == 0 else LANES
    n_out, n_ffn = dm // kc, f // cf
    geo = _Rows(tm, nt, tps, reps, d, max(n_out, n_ffn))
    nrow = row_mod.shape[0]
    vec = pl.BlockSpec((1, d), lambda i: (0, 0))
    split_x = x_s is not None
    out_specs, out_shape = _out_rows(geo, d, split_out)
    oc = lambda i: jnp.minimum(i, n_out - 1)
    fc = lambda i: jnp.minimum(i, n_ffn - 1)
    in_specs = [geo.prompt_rows(da), geo.sample_rows(da), geo.prompt_rows(db), geo.sample_rows(db),
                pl.BlockSpec((None, kc, d), lambda i: (layer, oc(i), 0)),
                pl.BlockSpec((None, d, cf), lambda i: (j, 0, fc(i))),
                pl.BlockSpec((None, d, cf), lambda i: (j, 0, fc(i))),
                pl.BlockSpec((None, cf, d), lambda i: (j, fc(i), 0))]
    in_specs += [geo.prompt_rows(d), geo.sample_rows(d)] if split_x else [geo.rows(d)]
    in_specs += [geo.seq_mod(k) for k in (2, 3, 4, 5)] + [geo.row_mod(k, nrow) for k in (2, 3, 4, 5)] + [vec] * 3
    args = [ya_p, ya_s, yb_p, yb_s, w_out, wg, wu, wd] + ([x_p, x_s] if split_x else [x_p])
    args += [seq_mod] * 4 + [row_mod] * 4 + [npost.reshape(1, d), npre.reshape(1, d), nffn.reshape(1, d)]
    return pl.pallas_call(
        functools.partial(_mix_ffn_kernel, geo=geo, kc=kc, cf=cf, n_out=n_out, n_ffn=n_ffn, split_x=split_x,
                          split_out=split_out),
        grid=(geo.pro + nt,),
        in_specs=in_specs,
        out_specs=out_specs,
        out_shape=out_shape,
        scratch_shapes=[pltpu.VMEM((dm, d), BF16), pltpu.VMEM((n_ffn, d, cf), BF16), pltpu.VMEM((n_ffn, d, cf), BF16),
                        pltpu.VMEM((f, d), BF16), pltpu.VMEM((tm, f), BF16)],
        compiler_params=_cparams("arbitrary"),
        name="mix_ffn",
    )(*args)


def _row_copy(src, src_row, dst, dst_row, sem):
    return pltpu.make_async_copy(src.at[pl.ds(src_row, 1)], dst.at[pl.ds(dst_row, 1)], sem)


def _moe_dispatch_kernel(pad_start_ref, pad_len_ref, nvalid_ref, pos_ref, h_ref, wg_ref, wu_ref, wd_ref,
                         xs_hbm, wgb_ref, wub_ref, wdb_ref, zbuf, hbuf, sems, *, tm, n_exp, nt):
    i = pl.program_id(0)
    ts = zbuf.shape[0]
    wgb_ref[...] = wg_ref[...].astype(BF16)
    wub_ref[...] = wu_ref[...].astype(BF16)
    wdb_ref[...] = wd_ref[...].astype(BF16)

    @pl.when(i == 0)
    def _():
        zbuf[...] = jnp.zeros(zbuf.shape, F32)
        for e in range(n_exp):
            def start(k, c, e=e):
                _row_copy(zbuf, 0, xs_hbm, pad_start_ref[e] + k, sems.at[2]).start()
                return c

            def wait(k, c, e=e):
                _row_copy(zbuf, 0, xs_hbm, pad_start_ref[e] + k, sems.at[2]).wait()
                return c

            lax.fori_loop(0, pad_len_ref[e], start, 0)
            lax.fori_loop(0, pad_len_ref[e], wait, 0)

        def tile_copy(j):
            return pltpu.make_async_copy(zbuf, xs_hbm.at[pl.ds(pl.multiple_of(j * ts, ts), ts)], sems.at[2])

        def start_tile(j, c):
            tile_copy(j).start()
            return c

        def wait_tile(j, c):
            tile_copy(j).wait()
            return c

        lax.fori_loop(nvalid_ref[0], xs_hbm.shape[0] // ts, start_tile, 0)
        lax.fori_loop(nvalid_ref[0], xs_hbm.shape[0] // ts, wait_tile, 0)

    def wait_slot(s):
        for _ in range(TOP_K):
            pltpu.make_async_copy(hbuf.at[s], xs_hbm.at[pl.ds(0, tm)], sems.at[s]).wait()

    for s in range(2):
        @pl.when(i % 2 == s)
        def _(s=s):
            @pl.when(i >= 2)
            def _():
                wait_slot(s)

            hbuf[s] = h_ref[...]
            for r in range(tm):
                _row_copy(hbuf.at[s], r, xs_hbm, pos_ref[0, r], sems.at[s]).start(priority=0)
                _row_copy(hbuf.at[s], r, xs_hbm, pos_ref[0, tm + r], sems.at[s]).start(priority=1)

    @pl.when(i == nt - 1)
    def _():
        wait_slot((nt - 1) % 2)
        if nt > 1:
            wait_slot(nt % 2)


def _moe_dispatch(h2, pos_tiles, pad_start, pad_len, n_valid, n_slots, tm, tm_slot, wg, wu, wd):
    t, d = h2.shape
    nt = t // tm
    n_exp = pad_start.shape[0]
    pack = 2 * SUBLANES
    ncast = next(k for k in range(nt, 0, -1)
                 if wg.shape[0] % (k * pack) == 0 and wd.shape[0] % (k * pack) == 0)
    wrows = lambda a: pl.BlockSpec((a.shape[0] // ncast, a.shape[1]), lambda i, *_: (jnp.minimum(i, ncast - 1), 0))
    grid_spec = pltpu.PrefetchScalarGridSpec(
        num_scalar_prefetch=3,
        grid=(nt,),
        in_specs=[pl.BlockSpec((None, 1, TOP_K * tm), lambda i, *_: (i, 0, 0), memory_space=pltpu.SMEM),
                  pl.BlockSpec((tm, d), lambda i, *_: (i, 0)), wrows(wg), wrows(wu), wrows(wd)],
        out_specs=[pl.BlockSpec(memory_space=pl.ANY), wrows(wg), wrows(wu), wrows(wd)],
        scratch_shapes=[pltpu.VMEM((tm_slot, d), F32), pltpu.VMEM((2, tm, d), F32), pltpu.SemaphoreType.DMA((3,))],
    )
    return pl.pallas_call(
        functools.partial(_moe_dispatch_kernel, tm=tm, n_exp=n_exp, nt=nt),
        grid_spec=grid_spec,
        out_shape=[jax.ShapeDtypeStruct((n_slots, d), F32)] + [jax.ShapeDtypeStruct(a.shape, BF16) for a in (wg, wu, wd)],
        compiler_params=_cparams("arbitrary"),
        name="moe_dispatch",
    )(pad_start, pad_len, n_valid, pos_tiles.reshape(nt, 1, TOP_K * tm), h2, wg, wu, wd)


def _moe_expert_kernel(texp_ref, nvalid_ref, x_ref, wg_ref, wu_ref, wd_ref, o_ref):
    i = pl.program_id(0)

    @pl.when(i < nvalid_ref[0])
    def _():
        h = x_ref[...].astype(BF16)
        act = (_silu(_dot(h, wg_ref[...])) * _dot(h, wu_ref[...])).astype(BF16)
        o_ref[...] = _dot(act, wd_ref[...])

    @pl.when(i >= nvalid_ref[0])
    def _():
        o_ref[...] = jnp.zeros(o_ref.shape, F32)


def _moe_experts(x_sorted, tile_expert, n_valid, wg, wu, wd, tm):
    n_tiles = tile_expert.shape[0]
    d = x_sorted.shape[1]
    n_exp, _, fe = wg.shape
    grid_spec = pltpu.PrefetchScalarGridSpec(
        num_scalar_prefetch=2,
        grid=(n_tiles,),
        in_specs=[
            pl.BlockSpec((tm, d), lambda i, te, nv: (jnp.minimum(i, nv[0] - 1), 0)),
            pl.BlockSpec((None, d, fe), lambda i, te, nv: (te[i], 0, 0)),
            pl.BlockSpec((None, d, fe), lambda i, te, nv: (te[i], 0, 0)),
            pl.BlockSpec((None, fe, d), lambda i, te, nv: (te[i], 0, 0)),
        ],
        out_specs=pl.BlockSpec((tm, d), lambda i, te, nv: (i, 0)),
    )
    return pl.pallas_call(
        _moe_expert_kernel,
        grid_spec=grid_spec,
        out_shape=jax.ShapeDtypeStruct((n_tiles * tm, d), F32),
        compiler_params=_cparams("arbitrary"),
        name="moe_experts",
    )(tile_expert, n_valid, x_sorted, wg, wu, wd)


def _start_row_gather(idx_ref, src_hbm, dst, sem, count):
    for r in range(count):
        _row_copy(src_hbm, idx_ref[0, r], dst, r, sem).start(priority=r % 2)


def _wait_row_gather(src_hbm, dst, sem):
    pltpu.make_async_copy(src_hbm.at[pl.ds(0, dst.shape[0])], dst, sem).wait()


def _moe_combine_kernel(pos0_ref, posnext_ref, ye_hbm, sel_ref, x1_ref, qg_ref, rg_ref, npost_ref, *rest,
                        geo, split_out):
    outs, (ybuf, sems) = rest[:-2], rest[-2:]
    i = pl.program_id(0)
    tm = geo.tm
    slot = i % 2

    @pl.when(i == 0)
    def _():
        _start_row_gather(pos0_ref, ye_hbm, ybuf.at[0], sems.at[0], 2 * tm)

    @pl.when(i + 1 < geo.nt)
    def _():
        _start_row_gather(posnext_ref, ye_hbm, ybuf.at[1 - slot], sems.at[1 - slot], 2 * tm)

    _wait_row_gather(ye_hbm, ybuf.at[slot], sems.at[slot])
    is_s = i == geo.nt - 1
    sel = sel_ref[...]
    lane = lax.broadcasted_iota(jnp.int32, sel.shape, 1)
    w1 = jnp.sum(jnp.where(lane == 2, sel, 0.0), axis=-1, keepdims=True)
    w2 = jnp.sum(jnp.where(lane == 3, sel, 0.0), axis=-1, keepdims=True)
    f = w1 * ybuf[slot, 0:tm, :] + w2 * ybuf[slot, tm:2 * tm, :]
    val = x1_ref[...] + _mod(is_s, qg_ref, rg_ref, geo.reps) * _rms(f, npost_ref[...])
    if split_out:
        _write_split(is_s, val, *outs)
    else:
        outs[0][...] = val


def _moe_combine(ye, pos, sel, x1, seq_mod, row_mod, geo_args, npost, split_out):
    tm, nt, tps, reps, d = geo_args
    geo = _Rows(tm, nt, tps, reps, d, 0)
    out_specs, out_shape = _out_rows(geo, d, split_out)
    pos3 = pos.reshape(nt, 1, 2 * tm)
    smem_blk = lambda fn: pl.BlockSpec((None, 1, 2 * tm), fn, memory_space=pltpu.SMEM)
    return pl.pallas_call(
        functools.partial(_moe_combine_kernel, geo=geo, split_out=split_out),
        grid=(nt,),
        in_specs=[smem_blk(lambda i: (0, 0, 0)),
                  smem_blk(lambda i: (jnp.minimum(i + 1, nt - 1), 0, 0)),
                  pl.BlockSpec(memory_space=pl.ANY),
                  geo.rows(LANES), geo.rows(d), geo.seq_mod(5), geo.row_mod(5, row_mod.shape[0]),
                  pl.BlockSpec((1, d), lambda i: (0, 0))],
        out_specs=out_specs,
        out_shape=out_shape,
        scratch_shapes=[pltpu.VMEM((2, 2 * tm, d), F32), pltpu.SemaphoreType.DMA((2,))],
        compiler_params=_cparams("arbitrary"),
        name="moe_combine",
    )(pos3, pos3, ye, sel, x1, seq_mod, row_mod, npost.reshape(1, d))


def _moe_tables(sel, counts, n_exp, tm_tok, tm_slot):
    t = sel.shape[0]
    counts = counts[0, :n_exp].astype(jnp.int32)
    tiles_e = (counts + tm_slot - 1) // tm_slot
    tile_end = jnp.cumsum(tiles_e)
    slot_start = (tile_end - tiles_e) * tm_slot
    choice = sel[:, 0:TOP_K].astype(jnp.int32)
    rank = sel[:, 2 * TOP_K:3 * TOP_K].astype(jnp.int32)
    onehot = choice[:, :, None] == jnp.arange(n_exp, dtype=jnp.int32)[None, None, :]
    pos = jnp.sum(jnp.where(onehot, slot_start[None, None, :], 0), axis=-1) + rank
    n_tiles = (TOP_K * t + n_exp * (tm_slot - 1)) // tm_slot
    tile_expert = jnp.minimum(
        jnp.sum((jnp.arange(n_tiles, dtype=jnp.int32)[:, None] >= tile_end[None, :]).astype(jnp.int32), axis=1),
        n_exp - 1)
    pos_tiles = jnp.concatenate([pos[:, k].reshape(-1, tm_tok) for k in range(TOP_K)], axis=1)
    pad_start = slot_start + counts
    pad_len = tiles_e * tm_slot - counts
    return tile_expert, tile_end[-1:].astype(jnp.int32), pos_tiles, pad_start, pad_len, n_tiles * tm_slot


def _largest_tile(t, want):
    tm = min(want, t)
    while t % tm:
        tm //= 2
    return tm


def kernel(x_prompt, x_sample, c_prompt, c_sample, state_conva, state_convb, state_ssm, ada_w, ada_b, norm_pre_mix,
           norm_post_mix, norm_pre_ffn, norm_post_ffn, w_in, w_out, conva_w, convb_w, convb_b, dt_bias, a_log, d_skip,
           ssd_norm, ffd_w_gate, ffd_w_up, ffd_w_down, moe_router, moe_w_gate, moe_w_up, moe_w_down):
    bp, seq, d = x_prompt.shape
    bs, steps, _ = x_sample.shape
    depth = w_in.shape[0]
    dc = conva_w.shape[-1]
    dx = convb_w.shape[-1]
    ds = ssd_norm.shape[-1]
    heads = dt_bias.shape[-1]
    p = ds // heads
    n = (dx - ds) // (2 * SSD_GROUPS)
    hg = heads // SSD_GROUPS
    n_exp = moe_router.shape[-1]
    ka, kb = conva_w.shape[1] - 1, convb_w.shape[1] - 1
    tm = steps * bs
    t_p = bp * seq
    assert seq % SSD_CHUNK == 0 and LANES % p == 0 and (hg * p) % LANES == 0 and n == LANES
    assert heads <= LANES and n_exp <= LANES and dc % LANES == 0 and dx % LANES == 0 and d % LANES == 0
    assert w_in.shape[-1] == 3 * dc + ds + dx + heads and seq % tm == 0 and tm % SUBLANES == 0
    assert bs % SUBLANES == 0 and (t_p // bs) % steps == 0
    nt = t_p // tm + 1
    geo_args = (tm, nt, seq // tm, steps, d)

    w_dt = jnp.pad(w_in[:, :, 3 * dc + ds + dx:], ((0, 0), (0, 0), (0, LANES - heads))).astype(BF16)
    router_p = jnp.pad(moe_router, ((0, 0), (0, 0), (0, LANES - n_exp)))
    padh = lambda a: jnp.pad(a, ((0, 0), (0, LANES - heads))).reshape(depth, 1, LANES)
    dtb_p, alog_p = padh(dt_bias), padh(a_log)
    dsk_e = jnp.repeat(d_skip, p, axis=-1).reshape(depth, 1, ds)
    hot = (jnp.arange(LANES)[:, None] == (jnp.arange(ds)[None, :] // p)).astype(BF16)
    emat = jnp.concatenate([hot, hot], axis=0)

    mod = _adaln(jnp.concatenate([c_prompt, c_sample], axis=0), ada_w, ada_b)
    seq_mod = jnp.pad(mod[:, :bp], ((0, 0), (0, 1), (0, 0))).reshape(depth, bp + 1, 1, 6 * d)
    row_mod = mod[:, bp:]

    bb = _largest_tile(bs, 16)
    x_all = None
    xs_tm = x_sample.transpose(1, 0, 2).reshape(tm, d)
    xp2d = x_prompt.reshape(t_p, d)
    sa_all = state_conva.reshape(depth, bs, ka * dc)
    sb_all = state_convb.reshape(depth, bs, kb * dx)
    ss_all = state_ssm.reshape(depth, bs, ds, n)
    pa, pb, ps, sa_l, sb_l = [], [], [], [], []
    ss_new = None
    for i in range(depth):
        j = i // 2
        last = i == depth - 1
        x_in = (xp2d, xs_tm) if x_all is None else (x_all, None)
        conv_params = [conva_w[i], convb_w[i], convb_b[i].reshape(1, dx)]
        ya, z, xc, dtr, na, nb, gb_s, u_s, xbc_s, *zeros = _inproj(
            *x_in, seq_mod[i], row_mod[i], geo_args, norm_pre_mix[i], w_in, i, w_dt[i], *conv_params, bp, ds,
            zero_rows=depth * bs * ds * n // LANES if ss_new is None else 0)
        if zeros:
            ss_new = zeros[0].reshape(depth, bs, ds, n)
        params = [dtb_p[i], alog_p[i], dsk_e[i], ssd_norm[i].reshape(1, ds), emat]
        yb_p, ns = _ssd_prompt(z, xc, dtr, bp, seq, params, n, p, hg)
        r3 = lambda a: a.reshape(a.shape[0] // bs, bs, a.shape[-1])
        ya_s, yb_s, sna, snb, ss_new = _ssd_sample(
            r3(gb_s), r3(u_s), r3(xbc_s), r3(z), r3(dtr), t_p // (bs * steps), steps, bs, i, sa_all, sb_all, ss_all,
            ss_new, conv_params + params, n, p, hg, bb)
        pa.append(na)
        pb.append(nb)
        ps.append(ns.reshape(bp, heads, p, n))
        sa_l.append(sna.reshape(bs, ka, dc))
        sb_l.append(snb.reshape(bs, kb, dx))

        if i % 2 == 1:
            x1, h2, sel, counts = _outproj(ya, ya_s.reshape(tm, dc), yb_p, yb_s.reshape(tm, ds), w_out, i, *x_in,
                                           seq_mod[i], row_mod[i], geo_args, norm_post_mix[i], norm_pre_ffn[i],
                                           router_p[j], n_exp)
            tile_expert, n_valid, pos_tiles, pad_start, pad_len, n_slots = _moe_tables(sel, counts, n_exp, tm, MOE_TILE)
            fe = moe_w_gate.shape[-1]
            x_sorted, wg_b, wu_b, wd_b = _moe_dispatch(
                h2, pos_tiles, pad_start, pad_len, n_valid, n_slots, tm, MOE_TILE, moe_w_gate[j].reshape(n_exp * d, fe),
                moe_w_up[j].reshape(n_exp * d, fe), moe_w_down[j].reshape(n_exp * fe, d))
            ye = _moe_experts(x_sorted, tile_expert, n_valid, wg_b.reshape(n_exp, d, fe), wu_b.reshape(n_exp, d, fe),
                              wd_b.reshape(n_exp, fe, d), MOE_TILE)
            out = _moe_combine(ye, pos_tiles, sel, x1, seq_mod[i], row_mod[i], geo_args, norm_post_ffn[i], last)
        else:
            out = _mix_ffn(ya, ya_s.reshape(tm, dc), yb_p, yb_s.reshape(tm, ds), w_out, i, ffd_w_gate, ffd_w_up,
                           ffd_w_down, j, *x_in, seq_mod[i], row_mod[i], geo_args, norm_post_mix[i], norm_pre_ffn[i],
                           norm_post_ffn[i], last)
        if last:
            y_p, y_s = out
        else:
            x_all = out[0]

    y_prompt = y_p.reshape(bp, seq, d)
    y_sample = y_s.reshape(steps, bs, d).transpose(1, 0, 2)
    return (y_prompt, y_sample, jnp.stack(pa), jnp.stack(pb), jnp.stack(ps),
            jnp.stack(sa_l), jnp.stack(sb_l), ss_new.reshape(depth, bs, heads, p, n))
```

```python
import functools

import jax
import jax.numpy as jnp
from jax import lax
from jax.experimental import pallas as pl
from jax.experimental.pallas import tpu as pltpu

EPS = 1e-6
SSD_GROUPS = 2
SSD_CHUNK = 128
TOP_K = 2
LOG2E = 1.4426950408889634
LANES = 128
SUBLANES = 8
VMEM_LIMIT_BYTES = 56 * 1024 * 1024
MOE_TILE = 512

F32 = jnp.float32
BF16 = jnp.bfloat16


def _cparams(*sem):
    return pltpu.CompilerParams(dimension_semantics=sem, vmem_limit_bytes=VMEM_LIMIT_BYTES)


def _resident(shape):
    return pl.BlockSpec(shape, lambda *_: (0,) * len(shape), pipeline_mode=pl.Buffered(1))


def _silu(x):
    return x * (1.0 / (1.0 + jnp.exp(-x)))


def _softplus(x):
    return jnp.maximum(x, 0.0) + jnp.log1p(jnp.exp(-jnp.abs(x)))


def _rms(x, g):
    return x * lax.rsqrt(jnp.mean(x * x, axis=-1, keepdims=True) + EPS) * g


def _dot(a, b):
    return jnp.dot(a, b, preferred_element_type=F32)


def _dot_nt(a, b):
    return lax.dot_general(a, b, (((1,), (1,)), ((), ())), preferred_element_type=F32)


def _dot_tn(a, b):
    return lax.dot_general(a, b, (((0,), (0,)), ((), ())), preferred_element_type=F32)


def _chunk(*widths):
    return next(c for c in (512, 256, LANES) if all(w % c == 0 for w in widths))


def _adaln_kernel(c_ref, w_ref, b_ref, o_ref):
    s = _silu(c_ref[...]).astype(BF16)
    o_ref[...] = _dot(s, w_ref[...].astype(BF16)) + b_ref[...]


def _adaln(c_all, ada_w, ada_b):
    depth, d, d6 = ada_w.shape
    rows = c_all.shape[0]
    tn = _chunk(d6 // 6) * 2
    return pl.pallas_call(
        _adaln_kernel,
        grid=(depth, d6 // tn),
        in_specs=[
            pl.BlockSpec((rows, d), lambda l, j: (0, 0)),
            pl.BlockSpec((None, d, tn), lambda l, j: (l, 0, j)),
            pl.BlockSpec((None, 1, tn), lambda l, j: (l, 0, j)),
        ],
        out_specs=pl.BlockSpec((None, rows, tn), lambda l, j: (l, 0, j)),
        out_shape=jax.ShapeDtypeStruct((depth, rows, d6), F32),
        compiler_params=_cparams("arbitrary", "arbitrary"),
        name="adaln",
    )(c_all, ada_w, ada_b.reshape(depth, 1, d6))


class _Rows:
    def __init__(self, tm, nt, tps, reps, d, pro):
        self.tm, self.nt, self.tps, self.reps, self.d, self.pro = tm, nt, tps, reps, d, pro

    def tile(self, i):
        return jnp.maximum(i - self.pro, 0)

    def rows(self, w):
        return pl.BlockSpec((self.tm, w), lambda i, *_: (self.tile(i), 0))

    def prompt_rows(self, w):
        return pl.BlockSpec((self.tm, w), lambda i, *_: (jnp.minimum(self.tile(i), self.nt - 2), 0))

    def sample_rows(self, w):
        return pl.BlockSpec((self.tm, w), lambda i, *_: (0, 0))

    def seq_mod(self, k):
        return pl.BlockSpec((None, 1, self.d), lambda i, *_: (self.tile(i) // self.tps, 0, k))

    def row_mod(self, k, rows):
        return pl.BlockSpec((rows, self.d), lambda i, *_: (0, k))


def _mod(is_sample, seq_ref, row_ref, reps):
    rowm = jnp.concatenate([row_ref[...]] * reps, axis=0)
    return seq_ref[...] + jnp.where(is_sample, rowm, 0.0)


def _causal_conv_cols(cur, c0, w_ref, tail, ext, nst_ref, keep_state, first, emit):
    tm, cw = cur.shape
    kp = w_ref.shape[0] - 1
    wcol = lambda j: w_ref[j:j + 1, c0:c0 + cw]
    ext[0:SUBLANES, :] = jnp.where(first, 0.0, tail[:, c0:c0 + cw])
    ext[SUBLANES:SUBLANES + tm, :] = cur
    acc = wcol(kp) * cur
    for j in range(kp):
        off = SUBLANES - (kp - j)
        acc = acc + wcol(j) * ext[off:off + tm, :]
    tail[:, c0:c0 + cw] = ext[tm:tm + SUBLANES, :]
    nst_ref[:, c0:c0 + cw] = jnp.where(keep_state, nst_ref[:, c0:c0 + cw], ext[SUBLANES + tm - kp:SUBLANES + tm, :])
    emit(acc)


def _inproj_kernel(*refs, geo, dc, ds, dx, cw, split_x, zero_fill):
    if split_x:
        xp_ref, xs_ref, *refs = refs
    else:
        xp_ref, *refs = refs
    (qsh_ref, qsc_ref, rsh_ref, rsc_ref, g_ref, wch_ref, wdt_ref, caw_ref, cbw_ref, cbb_ref,
     ya_ref, z_ref, xc_ref, dt_ref, ncap_ref, ncbp_ref, gbs_ref, us_ref, xbcs_ref, *rest) = refs
    wbf, ext, tail_u, tail_x = rest[-4:]
    i = pl.program_id(0)

    @pl.when(i < geo.pro)
    def _():
        wbf[i] = wch_ref[...].astype(BF16)

    @pl.when(i >= geo.pro)
    def _():
        r = i - geo.pro
        is_s = r == geo.nt - 1
        first = r % geo.tps == 0
        x = jnp.where(is_s, xs_ref[...], xp_ref[...]) if split_x else xp_ref[...]
        sc = _mod(is_s, qsc_ref, rsc_ref, geo.reps)
        sh = _mod(is_s, qsh_ref, rsh_ref, geo.reps)
        h = (_rms(x, g_ref[...]) * (1.0 + sc) + sh).astype(BF16)
        nc = dc // cw
        for a in range(nc):
            c0 = a * cw
            gate_b = _dot_nt(h, wbf[a])
            u = _dot_nt(h, wbf[nc + a]) * _dot_nt(h, wbf[2 * nc + a])
            gbs_ref[:, c0:c0 + cw] = gate_b
            us_ref[:, c0:c0 + cw] = u

            def emit_a(v, c0=c0, gate_b=gate_b):
                ya_ref[:, c0:c0 + cw] = (gate_b * v).astype(BF16)

            _causal_conv_cols(u, c0, caw_ref, tail_u, ext, ncap_ref, is_s, first, emit_a)
        for a in range(ds // cw):
            z_ref[:, a * cw:(a + 1) * cw] = _dot_nt(h, wbf[3 * nc + a])
        for a in range(dx // cw):
            c0 = a * cw
            xbc = _dot_nt(h, wbf[3 * nc + ds // cw + a])
            xbcs_ref[:, c0:c0 + cw] = xbc

            def emit_b(v, c0=c0):
                xc_ref[:, c0:c0 + cw] = _silu(v + cbb_ref[:, c0:c0 + cw])

            _causal_conv_cols(xbc, c0, cbw_ref, tail_x, ext, ncbp_ref, is_s, first, emit_b)
        dt_ref[...] = _dot_nt(h, wdt_ref[...])
        if zero_fill:
            rest[0][...] = jnp.zeros(rest[0].shape, F32)


def _inproj(x_p, x_s, seq_mod, row_mod, geo_args, norm_w, w_in, layer, w_dt, caw, cbw, cbb, bp, ds, zero_rows=0):
    tm, nt, tps, reps, d = geo_args
    dc, dx = caw.shape[1], cbw.shape[1]
    ka, kb = caw.shape[0] - 1, cbw.shape[0] - 1
    cw = _chunk(dc, ds, dx)
    npro = (3 * dc + ds + dx) // cw
    geo = _Rows(tm, nt, tps, reps, d, npro)
    t_all = tm * nt
    split_x = x_s is not None
    xin = [geo.prompt_rows(d), geo.sample_rows(d)] if split_x else [geo.rows(d)]
    xargs = [x_p, x_s] if split_x else [x_p]
    nrow = row_mod.shape[0]
    seq_blk = lambda k, w: pl.BlockSpec((None, k, w), lambda i: (jnp.minimum(geo.tile(i) // tps, bp - 1), 0, 0))
    zero_specs, zero_shapes = [], []
    if zero_rows:
        zero_specs = [pl.BlockSpec((zero_rows // (nt - 1), LANES), lambda i: (jnp.minimum(geo.tile(i), nt - 2), 0))]
        zero_shapes = [jax.ShapeDtypeStruct((zero_rows, LANES), F32)]
    return pl.pallas_call(
        functools.partial(_inproj_kernel, geo=geo, dc=dc, ds=ds, dx=dx, cw=cw, split_x=split_x,
                          zero_fill=bool(zero_rows)),
        grid=(npro + nt,),
        in_specs=xin + [
            geo.seq_mod(0), geo.seq_mod(1), geo.row_mod(0, nrow), geo.row_mod(1, nrow),
            pl.BlockSpec((1, d), lambda i: (0, 0)),
            pl.BlockSpec((None, cw, d), lambda i: (layer, jnp.minimum(i, npro - 1), 0)),
            _resident((LANES, d)), _resident(caw.shape), _resident(cbw.shape), _resident(cbb.shape),
        ],
        out_specs=[geo.rows(dc), geo.rows(ds), geo.rows(dx), geo.rows(LANES), seq_blk(ka, dc), seq_blk(kb, dx),
                   geo.sample_rows(dc), geo.sample_rows(dc), geo.sample_rows(dx)] + zero_specs,
        out_shape=[
            jax.ShapeDtypeStruct((t_all, dc), BF16),
            jax.ShapeDtypeStruct((t_all, ds), F32),
            jax.ShapeDtypeStruct((t_all, dx), F32),
            jax.ShapeDtypeStruct((t_all, LANES), F32),
            jax.ShapeDtypeStruct((bp, ka, dc), F32),
            jax.ShapeDtypeStruct((bp, kb, dx), F32),
            jax.ShapeDtypeStruct((tm, dc), F32),
            jax.ShapeDtypeStruct((tm, dc), F32),
            jax.ShapeDtypeStruct((tm, dx), F32),
        ] + zero_shapes,
        scratch_shapes=[pltpu.VMEM((npro, cw, d), BF16), pltpu.VMEM((SUBLANES + tm, cw), F32),
                        pltpu.VMEM((SUBLANES, dc), F32), pltpu.VMEM((SUBLANES, dx), F32)],
        compiler_params=_cparams("arbitrary"),
        name="inproj",
    )(*xargs, seq_mod, seq_mod, row_mod, row_mod, norm_w.reshape(1, d), w_in, w_dt, caw, cbw, cbb)


def _split_hi_lo(v):
    hi = v.astype(BF16)
    lo = (v - hi.astype(F32)).astype(BF16)
    return jnp.concatenate([hi, lo], axis=1)


def _diag_block(cb, mask, cum, cum_t, dt_t, xs, lane, g, hg, gw, p):
    hpl = LANES // p
    parts = []
    for slab in range(gw // LANES):
        lhs, rhs = [], []
        xslab = xs[:, g * gw + slab * LANES:g * gw + (slab + 1) * LANES]
        for j in range(hpl):
            h = g * hg + slab * hpl + j
            seg = cum[:, h:h + 1] - cum_t[h:h + 1, :]
            m = cb * jnp.where(mask, jnp.exp2(seg), 0.0) * dt_t[h:h + 1, :]
            lhs.append(m.astype(BF16))
            rhs.append(jnp.where((lane >= j * p) & (lane < (j + 1) * p), xslab, 0.0).astype(BF16))
        parts.append(_dot(jnp.concatenate(lhs, axis=1), jnp.concatenate(rhs, axis=0)))
    return jnp.concatenate(parts, axis=1)


def _gated_group_norm(y, xs_g, z_g, dsk_g, snorm_g):
    y = (y + dsk_g * xs_g) * _silu(z_g)
    return (y * lax.rsqrt(jnp.mean(y * y, axis=-1, keepdims=True) + EPS) * snorm_g).astype(BF16)


def _ssd_prompt_kernel(z_ref, xc_ref, dt_ref, dtb_ref, alog_ref, dsk_ref, snorm_ref, e_ref, yb_ref, nss_ref, st,
                       *, ds, n, p, hg, sub):
    c = pl.program_id(1)
    q = SSD_CHUNK
    gw = hg * p

    @pl.when(c == 0)
    def _():
        st[...] = jnp.zeros(st.shape, F32)

    row = lax.broadcasted_iota(jnp.int32, (q, LANES), 0)
    tril = lax.broadcasted_iota(jnp.int32, (q, q), 0) >= lax.broadcasted_iota(jnp.int32, (q, q), 1)
    lane = lax.broadcasted_iota(jnp.int32, (q, LANES), 1)
    neg_a = -jnp.exp(alog_ref[...]) * LOG2E
    for k in range(sub):
        rs = slice(k * q, (k + 1) * q)
        xs = xc_ref[rs, 0:ds]
        dt = _softplus(dt_ref[rs, :] + dtb_ref[...])
        cum = dt * neg_a
        step = 1
        while step < q:
            cum = cum + jnp.where(row >= step, pltpu.roll(cum, step, axis=0), 0.0)
            step *= 2
        cum_last = cum[q - 1:q, :]
        w_exp = _dot(_split_hi_lo(dt * jnp.exp2(cum_last - cum)), e_ref[...])
        ecum_exp = _dot(_split_hi_lo(jnp.exp2(cum)), e_ref[...])
        cum_t = cum.T
        dt_t = dt.T
        for g in range(SSD_GROUPS):
            gsl = slice(g * gw, (g + 1) * gw)
            bm = xc_ref[rs, ds + g * n:ds + (g + 1) * n]
            cm = xc_ref[rs, ds + SSD_GROUPS * n + g * n:ds + SSD_GROUPS * n + (g + 1) * n].astype(BF16)
            bm_t = bm.T.astype(BF16)
            y = _diag_block(_dot(cm, bm_t), tril, cum, cum_t, dt_t, xs, lane, g, hg, gw, p)
            s_prev = st[g]
            y = y + _dot(cm, s_prev.astype(BF16)) * ecum_exp[:, gsl]
            xw = (xs[:, gsl] * w_exp[:, gsl]).astype(BF16)
            st[g] = s_prev * ecum_exp[q - 1:q, gsl] + _dot(bm_t, xw)
            yb_ref[rs, gsl] = _gated_group_norm(y, xs[:, gsl], z_ref[rs, gsl], dsk_ref[:, gsl], snorm_ref[:, gsl])

    @pl.when(c == pl.num_programs(1) - 1)
    def _():
        for g in range(SSD_GROUPS):
            for slab in range(gw // LANES):
                r0 = g * gw + slab * LANES
                nss_ref[r0:r0 + LANES, :] = st[g, :, slab * LANES:(slab + 1) * LANES].T


def _ssd_prompt(z, xc, dtr, bsz, seq, params, n, p, hg):
    ds = z.shape[1]
    dx = xc.shape[1]
    sub = next(k for k in (4, 2, 1) if seq % (k * SSD_CHUNK) == 0)
    rows = sub * SSD_CHUNK
    nc = seq // rows
    row = lambda w: pl.BlockSpec((rows, w), lambda b, c: (b * nc + c, 0))
    return pl.pallas_call(
        functools.partial(_ssd_prompt_kernel, ds=ds, n=n, p=p, hg=hg, sub=sub),
        grid=(bsz, nc),
        in_specs=[row(ds), row(dx), row(LANES)] + [_resident(a.shape) for a in params],
        out_specs=[row(ds), pl.BlockSpec((None, ds, n), lambda b, c: (b, 0, 0))],
        out_shape=[jax.ShapeDtypeStruct((bsz * seq, ds), BF16), jax.ShapeDtypeStruct((bsz, ds, n), F32)],
        scratch_shapes=[pltpu.VMEM((SSD_GROUPS, n, hg * p), F32)],
        compiler_params=_cparams("arbitrary", "arbitrary"),
        name="ssd_prompt",
    )(z, xc, dtr, *params)


def _ssd_sample_kernel(gb_ref, u_ref, xbc_ref, z_ref, dt_ref, sa_ref, sb_ref, ss_ref, caw_ref, cbw_ref, cbb_ref,
                       dtb_ref, alog_ref, dsk_ref, snorm_ref, e_ref, ya_ref, yb_ref, nca_ref, ncb_ref, nss_ref,
                       *, dc, ds, dx, n, p, hg, steps, bb):
    gw = hg * p
    rows = steps * bb
    ka = caw_ref.shape[0] - 1
    kb = cbw_ref.shape[0] - 1

    def conv(cur_ref, st_ref, w_ref, kprev, width):
        hist = [st_ref[:, j * width:(j + 1) * width] for j in range(kprev)] + [cur_ref[t] for t in range(steps)]
        outs = []
        for t in range(steps):
            acc = w_ref[kprev:kprev + 1, :] * hist[t + kprev]
            for j in range(kprev):
                acc = acc + w_ref[j:j + 1, :] * hist[t + j]
            outs.append(acc)
        return outs, hist[len(hist) - kprev:]

    v, new_a = conv(u_ref, sa_ref, caw_ref, ka, dc)
    for t in range(steps):
        ya_ref[t] = (gb_ref[t] * v[t]).astype(BF16)
    for j in range(ka):
        nca_ref[:, j * dc:(j + 1) * dc] = new_a[j]
    xcs, new_b = conv(xbc_ref, sb_ref, cbw_ref, kb, dx)
    for j in range(kb):
        ncb_ref[:, j * dx:(j + 1) * dx] = new_b[j]
    xc = _silu(jnp.concatenate(xcs, axis=0) + cbb_ref[...])
    xs = xc[:, 0:ds]

    dt = _softplus(jnp.concatenate([dt_ref[t] for t in range(steps)], axis=0) + dtb_ref[...])
    da = dt * (-jnp.exp(alog_ref[...]) * LOG2E)
    cums = [da[0:bb]]
    for t in range(1, steps):
        cums.append(cums[-1] + da[t * bb:(t + 1) * bb])
    cum = jnp.concatenate(cums, axis=0)
    cum_last = jnp.concatenate([cums[-1]] * steps, axis=0)
    w_exp = _dot(_split_hi_lo(dt * jnp.exp2(cum_last - cum)), e_ref[...])
    ecum_exp = _dot(_split_hi_lo(jnp.exp2(cum)), e_ref[...])

    def pad_t(a):
        a = jnp.concatenate([a, jnp.zeros((LANES - rows, LANES), F32)], axis=0) if rows < LANES else a
        return a.T[:, 0:rows]

    cum_t = pad_t(cum)
    dt_t = pad_t(dt)

    ri = lax.broadcasted_iota(jnp.int32, (rows, rows), 0)
    ci = lax.broadcasted_iota(jnp.int32, (rows, rows), 1)
    same = ((ri % bb) == (ci % bb)) & (ri >= ci)
    lane = lax.broadcasted_iota(jnp.int32, (rows, LANES), 1)
    rowid = lax.broadcasted_iota(jnp.int32, (rows, 1), 0) % bb
    seqlane = lax.broadcasted_iota(jnp.int32, (LANES, LANES), 1)
    nslab = gw // LANES
    for g in range(SSD_GROUPS):
        gsl = slice(g * gw, (g + 1) * gw)
        bm = xc[:, ds + g * n:ds + (g + 1) * n].astype(BF16)
        cm = xc[:, ds + SSD_GROUPS * n + g * n:ds + SSD_GROUPS * n + (g + 1) * n].astype(BF16)
        y_diag = _diag_block(_dot_nt(cm, bm), same, cum, cum_t, dt_t, xs, lane, g, hg, gw, p)
        xw = (xs[:, gsl] * w_exp[:, gsl]).astype(BF16)
        dec = ecum_exp[(steps - 1) * bb:steps * bb, gsl]
        dec = jnp.concatenate([dec, jnp.zeros((LANES - bb, gw), F32)], axis=0)
        dec_t = [dec[:, s * LANES:(s + 1) * LANES].T for s in range(nslab)]

        def per_seq(b, y_off, g=g, cm=cm, bm=bm, xw=xw, dec_t=dec_t):
            r0 = g * gw
            s0 = ss_ref[b, r0:r0 + gw, :]
            y_off = jnp.where(rowid == b, _dot_nt(cm, s0.astype(BF16)), y_off)
            upd = _dot_tn(jnp.where(rowid == b, xw, jnp.zeros_like(xw)), bm)
            for s in range(nslab):
                dcol = jnp.sum(jnp.where(seqlane == b, dec_t[s], 0.0), axis=1, keepdims=True)
                nss_ref[b, r0 + s * LANES:r0 + (s + 1) * LANES, :] = (
                    s0[s * LANES:(s + 1) * LANES, :] * dcol + upd[s * LANES:(s + 1) * LANES, :])
            return y_off

        y_off = lax.fori_loop(0, bb, per_seq, jnp.zeros((rows, gw), F32), unroll=4)
        zg = jnp.concatenate([z_ref[t, :, gsl] for t in range(steps)], axis=0)
        yn = _gated_group_norm(y_diag + y_off * ecum_exp[:, gsl], xs[:, gsl], zg, dsk_ref[:, gsl], snorm_ref[:, gsl])
        for t in range(steps):
            yb_ref[t, :, gsl] = yn[t * bb:(t + 1) * bb]


def _ssd_sample_aliased_kernel(*refs, **kw):
    _ssd_sample_kernel(*refs[1:], **kw)


def _ssd_sample(gb, u, xbc, z, dtr, blk0, steps, bs, layer, sa, sb, ss, prev_ss, params, n, p, hg, bb):
    dc, ds, dx = u.shape[2], z.shape[2], xbc.shape[2]
    ka, kb = params[0].shape[0] - 1, params[1].shape[0] - 1
    own = lambda w: pl.BlockSpec((steps, bb, w), lambda i: (0, i, 0))
    tok = lambda w: pl.BlockSpec((steps, bb, w), lambda i: (blk0, i, 0))
    state = pl.BlockSpec((None, bb, ds, n), lambda i: (layer, i, 0, 0))
    return pl.pallas_call(
        functools.partial(_ssd_sample_aliased_kernel, dc=dc, ds=ds, dx=dx, n=n, p=p, hg=hg, steps=steps, bb=bb),
        grid=(bs // bb,),
        in_specs=[pl.BlockSpec(memory_space=pl.ANY), own(dc), own(dc), own(dx), tok(ds), tok(LANES),
                  pl.BlockSpec((None, bb, ka * dc), lambda i: (layer, i, 0)),
                  pl.BlockSpec((None, bb, kb * dx), lambda i: (layer, i, 0)), state]
        + [_resident(a.shape) for a in params],
        out_specs=[own(dc), own(ds), pl.BlockSpec((bb, ka * dc), lambda i: (i, 0)),
                   pl.BlockSpec((bb, kb * dx), lambda i: (i, 0)), state],
        out_shape=[jax.ShapeDtypeStruct((steps, bs, dc), BF16), jax.ShapeDtypeStruct((steps, bs, ds), BF16),
                   jax.ShapeDtypeStruct((bs, ka * dc), F32), jax.ShapeDtypeStruct((bs, kb * dx), F32),
                   jax.ShapeDtypeStruct(prev_ss.shape, F32)],
        input_output_aliases={0: 4},
        compiler_params=_cparams("arbitrary"),
        name="ssd_sample",
    )(prev_ss, gb, u, xbc, z, dtr, sa, sb, ss, *params)


def _route(logits, n_exp, cnt_ref):
    rows = logits.shape[0]
    lane = lax.broadcasted_iota(jnp.int32, logits.shape, 1).astype(F32)
    valid = lane < n_exp
    logits = jnp.where(valid, logits, -jnp.inf)
    e = jnp.exp(logits - jnp.max(logits, axis=-1, keepdims=True))
    prob = jnp.where(valid, e / jnp.sum(e, axis=-1, keepdims=True), -1.0)
    big = float(LANES)
    m1 = jnp.max(prob, axis=-1, keepdims=True)
    i1 = jnp.min(jnp.where(prob == m1, lane, big), axis=-1, keepdims=True)
    rest = jnp.where(lane == i1, -1.0, prob)
    m2 = jnp.max(rest, axis=-1, keepdims=True)
    i2 = jnp.min(jnp.where(rest == m2, lane, big), axis=-1, keepdims=True)
    den = m1 + m2
    chosen = jnp.where((lane == i1) | (lane == i2), 1.0, 0.0)
    before = (lax.broadcasted_iota(jnp.int32, (rows, rows), 0) > lax.broadcasted_iota(jnp.int32, (rows, rows), 1))
    rank = _dot(jnp.where(before, 1.0, 0.0).astype(BF16), chosen.astype(BF16)) + cnt_ref[0:1, :]
    r1 = jnp.sum(jnp.where(lane == i1, rank, 0.0), axis=-1, keepdims=True)
    r2 = jnp.sum(jnp.where(lane == i2, rank, 0.0), axis=-1, keepdims=True)
    cnt_ref[0:1, :] = cnt_ref[0:1, :] + jnp.sum(chosen, axis=0, keepdims=True)
    cols = [i1, i2, m1 / den, m2 / den, r1, r2]
    out = jnp.zeros(logits.shape, F32)
    for k, col in enumerate(cols):
        out = jnp.where(lane == float(k), col, out)
    return out


def _outproj_kernel(yap_ref, yas_ref, ybp_ref, ybs_ref, wch_ref, *rest, geo, n_exp, kc, split_x):
    if split_x:
        xp_ref, xs_ref, *rest = rest
    else:
        xp_ref, *rest = rest
    (qg_ref, qsh_ref, qsc_ref, rg_ref, rsh_ref, rsc_ref, npost_ref, npre_ref, r_ref,
     x1_ref, h2_ref, sel_ref, cnt_ref, wbf) = rest
    i = pl.program_id(0)

    @pl.when(i < geo.pro)
    def _():
        wbf[pl.ds(pl.multiple_of(i * kc, kc), kc), :] = wch_ref[...].astype(BF16)
        cnt_ref[...] = jnp.zeros(cnt_ref.shape, F32)

    @pl.when(i >= geo.pro)
    def _():
        is_s = i - geo.pro == geo.nt - 1
        da = yap_ref.shape[1]
        nh = 2 if geo.reps % 2 == 0 else 1
        th = geo.tm // nh
        for hs in range(nh):
            rs = slice(hs * th, (hs + 1) * th)
            mod = lambda q_ref, r_ref: _mod(is_s, q_ref, r_ref, geo.reps // nh)
            mix = (_dot(jnp.where(is_s, yas_ref[rs, :], yap_ref[rs, :]), wbf[0:da, :])
                   + _dot(jnp.where(is_s, ybs_ref[rs, :], ybp_ref[rs, :]), wbf[da:, :]))
            x = jnp.where(is_s, xs_ref[rs, :], xp_ref[rs, :]) if split_x else xp_ref[rs, :]
            x1 = x + mod(qg_ref, rg_ref) * _rms(mix, npost_ref[...])
            x1_ref[rs, :] = x1
            h2 = _rms(x1, npre_ref[...]) * (1.0 + mod(qsc_ref, rsc_ref)) + mod(qsh_ref, rsh_ref)
            h2_ref[rs, :] = h2
            h_hi = h2.astype(BF16)
            h_lo = (h2 - h_hi.astype(F32)).astype(BF16)
            r = r_ref[...]
            r_hi = r.astype(BF16)
            r_lo = (r - r_hi.astype(F32)).astype(BF16)
            both = _dot(h_hi, jnp.concatenate([r_hi, r_lo], axis=1))
            logits = both[:, 0:LANES] + (_dot(h_lo, r_hi) + both[:, LANES:])
            sel_ref[rs, :] = _route(logits, n_exp, cnt_ref)


def _outproj(ya_p, ya_s, yb_p, yb_s, w_out, layer, x_p, x_s, seq_mod, row_mod, geo_args, npost, npre, router,
             n_exp):
    tm, nt, tps, reps, d = geo_args
    da, db = ya_p.shape[1], yb_p.shape[1]
    dm = da + db
    kc = _chunk(dm) // 2
    npro = dm // kc
    geo = _Rows(tm, nt, tps, reps, d, npro)
    t_all = tm * nt
    nrow = row_mod.shape[0]
    vec = pl.BlockSpec((1, d), lambda i: (0, 0))
    split_x = x_s is not None
    in_specs = [geo.prompt_rows(da), geo.sample_rows(da), geo.prompt_rows(db), geo.sample_rows(db),
                pl.BlockSpec((None, kc, d), lambda i: (layer, jnp.minimum(i, npro - 1), 0))]
    in_specs += [geo.prompt_rows(d), geo.sample_rows(d)] if split_x else [geo.rows(d)]
    in_specs += [geo.seq_mod(2), geo.seq_mod(3), geo.seq_mod(4),
                 geo.row_mod(2, nrow), geo.row_mod(3, nrow), geo.row_mod(4, nrow), vec, vec]
    args = [ya_p, ya_s, yb_p, yb_s, w_out] + ([x_p, x_s] if split_x else [x_p])
    in_specs.append(_resident((d, LANES)))
    args += [seq_mod, seq_mod, seq_mod, row_mod, row_mod, row_mod, npost.reshape(1, d), npre.reshape(1, d), router]
    out_specs = [geo.rows(d), geo.rows(d), geo.rows(LANES), pl.BlockSpec((SUBLANES, LANES), lambda i: (0, 0))]
    out_shape = [jax.ShapeDtypeStruct((t_all, d), F32), jax.ShapeDtypeStruct((t_all, d), F32),
                 jax.ShapeDtypeStruct((t_all, LANES), F32), jax.ShapeDtypeStruct((SUBLANES, LANES), F32)]
    return pl.pallas_call(
        functools.partial(_outproj_kernel, geo=geo, n_exp=n_exp, kc=kc, split_x=split_x),
        grid=(npro + nt,),
        in_specs=in_specs,
        out_specs=out_specs,
        out_shape=out_shape,
        scratch_shapes=[pltpu.VMEM((dm, d), BF16)],
        compiler_params=_cparams("arbitrary"),
        name="outproj",
    )(*args)


def _write_split(is_s, val, outp_ref, outs_ref):
    @pl.when(jnp.logical_not(is_s))
    def _():
        outp_ref[...] = val

    @pl.when(is_s)
    def _():
        outs_ref[...] = val


def _out_rows(geo, d, split_out):
    t_all = geo.tm * geo.nt
    if split_out:
        return ([geo.prompt_rows(d), geo.sample_rows(d)],
                [jax.ShapeDtypeStruct((t_all - geo.tm, d), F32), jax.ShapeDtypeStruct((geo.tm, d), F32)])
    return [geo.rows(d)], [jax.ShapeDtypeStruct((t_all, d), F32)]


def _mix_ffn_kernel(yap_ref, yas_ref, ybp_ref, ybs_ref, wo_ref, wg_ref, wu_ref, wd_ref, *rest, geo, kc, cf, n_out,
                    n_ffn, split_x, split_out):
    if split_x:
        xp_ref, xs_ref, *rest = rest
    else:
        xp_ref, *rest = rest
    (qg1_ref, qsh_ref, qsc_ref, qg2_ref, rg1_ref, rsh_ref, rsc_ref, rg2_ref, npost_ref, npre_ref, nffn_ref,
     *rest) = rest
    outs, (wob, wgb, wub, wdb, act) = rest[:-5], rest[-5:]
    i = pl.program_id(0)

    @pl.when(i < n_out)
    def _():
        wob[pl.ds(pl.multiple_of(i * kc, kc), kc), :] = wo_ref[...].astype(BF16)

    @pl.when(i < n_ffn)
    def _():
        wgb[i] = wg_ref[...].astype(BF16)
        wub[i] = wu_ref[...].astype(BF16)
        wdb[pl.ds(pl.multiple_of(i * cf, cf), cf), :] = wd_ref[...].astype(BF16)

    @pl.when(i >= geo.pro)
    def _():
        is_s = i - geo.pro == geo.nt - 1
        da = yap_ref.shape[1]
        nh = 2 if geo.reps % 2 == 0 else 1
        th = geo.tm // nh
        vals = []
        for hs in range(nh):
            rs = slice(hs * th, (hs + 1) * th)
            mod = lambda q_ref, r_ref: _mod(is_s, q_ref, r_ref, geo.reps // nh)
            mix = (_dot(jnp.where(is_s, yas_ref[rs, :], yap_ref[rs, :]), wob[0:da, :])
                   + _dot(jnp.where(is_s, ybs_ref[rs, :], ybp_ref[rs, :]), wob[da:, :]))
            x = jnp.where(is_s, xs_ref[rs, :], xp_ref[rs, :]) if split_x else xp_ref[rs, :]
            x1 = x + mod(qg1_ref, rg1_ref) * _rms(mix, npost_ref[...])
            h = (_rms(x1, npre_ref[...]) * (1.0 + mod(qsc_ref, rsc_ref)) + mod(qsh_ref, rsh_ref)).astype(BF16)
            for c in range(n_ffn):
                act[rs, c * cf:(c + 1) * cf] = (_silu(_dot(h, wgb[c])) * _dot(h, wub[c])).astype(BF16)
            f = _dot(act[rs, :], wdb[...])
            vals.append(x1 + mod(qg2_ref, rg2_ref) * _rms(f, nffn_ref[...]))
        val = jnp.concatenate(vals, axis=0)
        if split_out:
            _write_split(is_s, val, *outs)
        else:
            outs[0][...] = val


def _mix_ffn(ya_p, ya_s, yb_p, yb_s, w_out, layer, wg, wu, wd, j, x_p, x_s, seq_mod, row_mod, geo_args, npost, npre,
             nffn, split_out):
    tm, nt, tps, reps, d = geo_args
    da, db = ya_p.shape[1], yb_p.shape[1]
    dm = da + db
    f = wg.shape[2]
    kc = _chunk(dm) // 2
    cf = 256 if f % 256 == 0 else LANES
    n_out, n_ffn = dm // kc, f // cf
    geo = _Rows(tm, nt, tps, reps, d, max(n_out, n_ffn))
    nrow = row_mod.shape[0]
    vec = pl.BlockSpec((1, d), lambda i: (0, 0))
    split_x = x_s is not None
    out_specs, out_shape = _out_rows(geo, d, split_out)
    oc = lambda i: jnp.minimum(i, n_out - 1)
    fc = lambda i: jnp.minimum(i, n_ffn - 1)
    in_specs = [geo.prompt_rows(da), geo.sample_rows(da), geo.prompt_rows(db), geo.sample_rows(db),
                pl.BlockSpec((None, kc, d), lambda i: (layer, oc(i), 0)),
                pl.BlockSpec((None, d, cf), lambda i: (j, 0, fc(i))),
                pl.BlockSpec((None, d, cf), lambda i: (j, 0, fc(i))),
                pl.BlockSpec((None, cf, d), lambda i: (j, fc(i), 0))]
    in_specs += [geo.prompt_rows(d), geo.sample_rows(d)] if split_x else [geo.rows(d)]
    in_specs += [geo.seq_mod(k) for k in (2, 3, 4, 5)] + [geo.row_mod(k, nrow) for k in (2, 3, 4, 5)] + [vec] * 3
    args = [ya_p, ya_s, yb_p, yb_s, w_out, wg, wu, wd] + ([x_p, x_s] if split_x else [x_p])
    args += [seq_mod] * 4 + [row_mod] * 4 + [npost.reshape(1, d), npre.reshape(1, d), nffn.reshape(1, d)]
    return pl.pallas_call(
        functools.partial(_mix_ffn_kernel, geo=geo, kc=kc, cf=cf, n_out=n_out, n_ffn=n_ffn, split_x=split_x,
                          split_out=split_out),
        grid=(geo.pro + nt,),
        in_specs=in_specs,
        out_specs=out_specs,
        out_shape=out_shape,
        scratch_shapes=[pltpu.VMEM((dm, d), BF16), pltpu.VMEM((n_ffn, d, cf), BF16), pltpu.VMEM((n_ffn, d, cf), BF16),
                        pltpu.VMEM((f, d), BF16), pltpu.VMEM((tm, f), BF16)],
        compiler_params=_cparams("arbitrary"),
        name="mix_ffn",
    )(*args)


def _row_copy(src, src_row, dst, dst_row, sem):
    return pltpu.make_async_copy(src.at[pl.ds(src_row, 1)], dst.at[pl.ds(dst_row, 1)], sem)


def _moe_dispatch_kernel(pad_start_ref, pad_len_ref, nvalid_ref, pos_ref, h_ref, wg_ref, wu_ref, wd_ref,
                         xs_hbm, wgb_ref, wub_ref, wdb_ref, zbuf, hbuf, sems, *, tm, n_exp, nt):
    i = pl.program_id(0)
    ts = zbuf.shape[0]
    wgb_ref[...] = wg_ref[...].astype(BF16)
    wub_ref[...] = wu_ref[...].astype(BF16)
    wdb_ref[...] = wd_ref[...].astype(BF16)

    @pl.when(i == 0)
    def _():
        zbuf[...] = jnp.zeros(zbuf.shape, F32)
        for e in range(n_exp):
            def start(k, c, e=e):
                _row_copy(zbuf, 0, xs_hbm, pad_start_ref[e] + k, sems.at[2]).start()
                return c

            def wait(k, c, e=e):
                _row_copy(zbuf, 0, xs_hbm, pad_start_ref[e] + k, sems.at[2]).wait()
                return c

            lax.fori_loop(0, pad_len_ref[e], start, 0)
            lax.fori_loop(0, pad_len_ref[e], wait, 0)

        def tile_copy(j):
            return pltpu.make_async_copy(zbuf, xs_hbm.at[pl.ds(pl.multiple_of(j * ts, ts), ts)], sems.at[2])

        def start_tile(j, c):
            tile_copy(j).start()
            return c

        def wait_tile(j, c):
            tile_copy(j).wait()
            return c

        lax.fori_loop(nvalid_ref[0], xs_hbm.shape[0] // ts, start_tile, 0)
        lax.fori_loop(nvalid_ref[0], xs_hbm.shape[0] // ts, wait_tile, 0)

    def wait_slot(s):
        for _ in range(TOP_K):
            pltpu.make_async_copy(hbuf.at[s], xs_hbm.at[pl.ds(0, tm)], sems.at[s]).wait()

    for s in range(2):
        @pl.when(i % 2 == s)
        def _(s=s):
            @pl.when(i >= 2)
            def _():
                wait_slot(s)

            hbuf[s] = h_ref[...]
            for r in range(tm):
                _row_copy(hbuf.at[s], r, xs_hbm, pos_ref[0, r], sems.at[s]).start(priority=0)
                _row_copy(hbuf.at[s], r, xs_hbm, pos_ref[0, tm + r], sems.at[s]).start(priority=1)

    @pl.when(i == nt - 1)
    def _():
        wait_slot((nt - 1) % 2)
        if nt > 1:
            wait_slot(nt % 2)


def _moe_dispatch(h2, pos_tiles, pad_start, pad_len, n_valid, n_slots, tm, tm_slot, wg, wu, wd):
    t, d = h2.shape
    nt = t // tm
    n_exp = pad_start.shape[0]
    pack = 2 * SUBLANES
    ncast = next(k for k in range(nt, 0, -1)
                 if wg.shape[0] % (k * pack) == 0 and wd.shape[0] % (k * pack) == 0)
    wrows = lambda a: pl.BlockSpec((a.shape[0] // ncast, a.shape[1]), lambda i, *_: (jnp.minimum(i, ncast - 1), 0))
    grid_spec = pltpu.PrefetchScalarGridSpec(
        num_scalar_prefetch=3,
        grid=(nt,),
        in_specs=[pl.BlockSpec((None, 1, TOP_K * tm), lambda i, *_: (i, 0, 0), memory_space=pltpu.SMEM),
                  pl.BlockSpec((tm, d), lambda i, *_: (i, 0)), wrows(wg), wrows(wu), wrows(wd)],
        out_specs=[pl.BlockSpec(memory_space=pl.ANY), wrows(wg), wrows(wu), wrows(wd)],
        scratch_shapes=[pltpu.VMEM((tm_slot, d), F32), pltpu.VMEM((2, tm, d), F32), pltpu.SemaphoreType.DMA((3,))],
    )
    return pl.pallas_call(
        functools.partial(_moe_dispatch_kernel, tm=tm, n_exp=n_exp, nt=nt),
        grid_spec=grid_spec,
        out_shape=[jax.ShapeDtypeStruct((n_slots, d), F32)] + [jax.ShapeDtypeStruct(a.shape, BF16) for a in (wg, wu, wd)],
        compiler_params=_cparams("arbitrary"),
        name="moe_dispatch",
    )(pad_start, pad_len, n_valid, pos_tiles.reshape(nt, 1, TOP_K * tm), h2, wg, wu, wd)


def _moe_expert_kernel(texp_ref, nvalid_ref, x_ref, wg_ref, wu_ref, wd_ref, o_ref):
    i = pl.program_id(0)

    @pl.when(i < nvalid_ref[0])
    def _():
        h = x_ref[...].astype(BF16)
        act = (_silu(_dot(h, wg_ref[...])) * _dot(h, wu_ref[...])).astype(BF16)
        o_ref[...] = _dot(act, wd_ref[...])

    @pl.when(i >= nvalid_ref[0])
    def _():
        o_ref[...] = jnp.zeros(o_ref.shape, F32)


def _moe_experts(x_sorted, tile_expert, n_valid, wg, wu, wd, tm):
    n_tiles = tile_expert.shape[0]
    d = x_sorted.shape[1]
    n_exp, _, fe = wg.shape
    grid_spec = pltpu.PrefetchScalarGridSpec(
        num_scalar_prefetch=2,
        grid=(n_tiles,),
        in_specs=[
            pl.BlockSpec((tm, d), lambda i, te, nv: (jnp.minimum(i, nv[0] - 1), 0)),
            pl.BlockSpec((None, d, fe), lambda i, te, nv: (te[i], 0, 0)),
            pl.BlockSpec((None, d, fe), lambda i, te, nv: (te[i], 0, 0)),
            pl.BlockSpec((None, fe, d), lambda i, te, nv: (te[i], 0, 0)),
        ],
        out_specs=pl.BlockSpec((tm, d), lambda i, te, nv: (i, 0)),
    )
    return pl.pallas_call(
        _moe_expert_kernel,
        grid_spec=grid_spec,
        out_shape=jax.ShapeDtypeStruct((n_tiles * tm, d), F32),
        compiler_params=_cparams("arbitrary"),
        name="moe_experts",
    )(tile_expert, n_valid, x_sorted, wg, wu, wd)


def _start_row_gather(idx_ref, src_hbm, dst, sem, count):
    for r in range(count):
        _row_copy(src_hbm, idx_ref[0, r], dst, r, sem).start(priority=r % 2)


def _wait_row_gather(src_hbm, dst, sem):
    pltpu.make_async_copy(src_hbm.at[pl.ds(0, dst.shape[0])], dst, sem).wait()


def _moe_combine_kernel(pos0_ref, posnext_ref, ye_hbm, sel_ref, x1_ref, qg_ref, rg_ref, npost_ref, *rest,
                        geo, split_out):
    outs, (ybuf, sems) = rest[:-2], rest[-2:]
    i = pl.program_id(0)
    tm = geo.tm
    slot = i % 2

    @pl.when(i == 0)
    def _():
        _start_row_gather(pos0_ref, ye_hbm, ybuf.at[0], sems.at[0], 2 * tm)

    @pl.when(i + 1 < geo.nt)
    def _():
        _start_row_gather(posnext_ref, ye_hbm, ybuf.at[1 - slot], sems.at[1 - slot], 2 * tm)

    _wait_row_gather(ye_hbm, ybuf.at[slot], sems.at[slot])
    is_s = i == geo.nt - 1
    sel = sel_ref[...]
    lane = lax.broadcasted_iota(jnp.int32, sel.shape, 1)
    w1 = jnp.sum(jnp.where(lane == 2, sel, 0.0), axis=-1, keepdims=True)
    w2 = jnp.sum(jnp.where(lane == 3, sel, 0.0), axis=-1, keepdims=True)
    f = w1 * ybuf[slot, 0:tm, :] + w2 * ybuf[slot, tm:2 * tm, :]
    val = x1_ref[...] + _mod(is_s, qg_ref, rg_ref, geo.reps) * _rms(f, npost_ref[...])
    if split_out:
        _write_split(is_s, val, *outs)
    else:
        outs[0][...] = val


def _moe_combine(ye, pos, sel, x1, seq_mod, row_mod, geo_args, npost, split_out):
    tm, nt, tps, reps, d = geo_args
    geo = _Rows(tm, nt, tps, reps, d, 0)
    out_specs, out_shape = _out_rows(geo, d, split_out)
    pos3 = pos.reshape(nt, 1, 2 * tm)
    smem_blk = lambda fn: pl.BlockSpec((None, 1, 2 * tm), fn, memory_space=pltpu.SMEM)
    return pl.pallas_call(
        functools.partial(_moe_combine_kernel, geo=geo, split_out=split_out),
        grid=(nt,),
        in_specs=[smem_blk(lambda i: (0, 0, 0)),
                  smem_blk(lambda i: (jnp.minimum(i + 1, nt - 1), 0, 0)),
                  pl.BlockSpec(memory_space=pl.ANY),
                  geo.rows(LANES), geo.rows(d), geo.seq_mod(5), geo.row_mod(5, row_mod.shape[0]),
                  pl.BlockSpec((1, d), lambda i: (0, 0))],
        out_specs=out_specs,
        out_shape=out_shape,
        scratch_shapes=[pltpu.VMEM((2, 2 * tm, d), F32), pltpu.SemaphoreType.DMA((2,))],
        compiler_params=_cparams("arbitrary"),
        name="moe_combine",
    )(pos3, pos3, ye, sel, x1, seq_mod, row_mod, npost.reshape(1, d))


def _moe_tables(sel, counts, n_exp, tm_tok, tm_slot):
    t = sel.shape[0]
    counts = counts[0, :n_exp].astype(jnp.int32)
    tiles_e = (counts + tm_slot - 1) // tm_slot
    tile_end = jnp.cumsum(tiles_e)
    slot_start = (tile_end - tiles_e) * tm_slot
    choice = sel[:, 0:TOP_K].astype(jnp.int32)
    rank = sel[:, 2 * TOP_K:3 * TOP_K].astype(jnp.int32)
    onehot = choice[:, :, None] == jnp.arange(n_exp, dtype=jnp.int32)[None, None, :]
    pos = jnp.sum(jnp.where(onehot, slot_start[None, None, :], 0), axis=-1) + rank
    n_tiles = (TOP_K * t + n_exp * (tm_slot - 1)) // tm_slot
    tile_expert = jnp.minimum(
        jnp.sum((jnp.arange(n_tiles, dtype=jnp.int32)[:, None] >= tile_end[None, :]).astype(jnp.int32), axis=1),
        n_exp - 1)
    pos_tiles = jnp.concatenate([pos[:, k].reshape(-1, tm_tok) for k in range(TOP_K)], axis=1)
    pad_start = slot_start + counts
    pad_len = tiles_e * tm_slot - counts
    return tile_expert, tile_end[-1:].astype(jnp.int32), pos_tiles, pad_start, pad_len, n_tiles * tm_slot


def _largest_tile(t, want):
    tm = min(want, t)
    while t % tm:
        tm //= 2
    return tm


def kernel(x_prompt, x_sample, c_prompt, c_sample, state_conva, state_convb, state_ssm, ada_w, ada_b, norm_pre_mix,
           norm_post_mix, norm_pre_ffn, norm_post_ffn, w_in, w_out, conva_w, convb_w, convb_b, dt_bias, a_log, d_skip,
           ssd_norm, ffd_w_gate, ffd_w_up, ffd_w_down, moe_router, moe_w_gate, moe_w_up, moe_w_down):
    bp, seq, d = x_prompt.shape
    bs, steps, _ = x_sample.shape
    depth = w_in.shape[0]
    dc = conva_w.shape[-1]
    dx = convb_w.shape[-1]
    ds = ssd_norm.shape[-1]
    heads = dt_bias.shape[-1]
    p = ds // heads
    n = (dx - ds) // (2 * SSD_GROUPS)
    hg = heads // SSD_GROUPS
    n_exp = moe_router.shape[-1]
    ka, kb = conva_w.shape[1] - 1, convb_w.shape[1] - 1
    tm = steps * bs
    t_p = bp * seq
    assert seq % SSD_CHUNK == 0 and LANES % p == 0 and (hg * p) % LANES == 0 and n == LANES
    assert heads <= LANES and n_exp <= LANES and dc % LANES == 0 and dx % LANES == 0 and d % LANES == 0
    assert w_in.shape[-1] == 3 * dc + ds + dx + heads and seq % tm == 0 and tm % SUBLANES == 0
    assert bs % SUBLANES == 0 and (t_p // bs) % steps == 0
    nt = t_p // tm + 1
    geo_args = (tm, nt, seq // tm, steps, d)

    w_in_t = jnp.swapaxes(w_in, 1, 2)
    w_dt = jnp.pad(w_in_t[:, 3 * dc + ds + dx:, :], ((0, 0), (0, LANES - heads), (0, 0))).astype(BF16)
    router_p = jnp.pad(moe_router, ((0, 0), (0, 0), (0, LANES - n_exp)))
    padh = lambda a: jnp.pad(a, ((0, 0), (0, LANES - heads))).reshape(depth, 1, LANES)
    dtb_p, alog_p = padh(dt_bias), padh(a_log)
    dsk_e = jnp.repeat(d_skip, p, axis=-1).reshape(depth, 1, ds)
    hot = (jnp.arange(LANES)[:, None] == (jnp.arange(ds)[None, :] // p)).astype(BF16)
    emat = jnp.concatenate([hot, hot], axis=0)

    mod = _adaln(jnp.concatenate([c_prompt, c_sample], axis=0), ada_w, ada_b)
    seq_mod = jnp.pad(mod[:, :bp], ((0, 0), (0, 1), (0, 0))).reshape(depth, bp + 1, 1, 6 * d)
    row_mod = mod[:, bp:]

    bb = _largest_tile(bs, 16)
    x_all = None
    xs_tm = x_sample.transpose(1, 0, 2).reshape(tm, d)
    xp2d = x_prompt.reshape(t_p, d)
    sa_all = state_conva.reshape(depth, bs, ka * dc)
    sb_all = state_convb.reshape(depth, bs, kb * dx)
    ss_all = state_ssm.reshape(depth, bs, ds, n)
    pa, pb, ps, sa_l, sb_l = [], [], [], [], []
    ss_new = None
    for i in range(depth):
        j = i // 2
        last = i == depth - 1
        x_in = (xp2d, xs_tm) if x_all is None else (x_all, None)
        conv_params = [conva_w[i], convb_w[i], convb_b[i].reshape(1, dx)]
        ya, z, xc, dtr, na, nb, gb_s, u_s, xbc_s, *zeros = _inproj(
            *x_in, seq_mod[i], row_mod[i], geo_args, norm_pre_mix[i], w_in_t, i, w_dt[i], *conv_params, bp, ds,
            zero_rows=depth * bs * ds * n // LANES if ss_new is None else 0)
        if zeros:
            ss_new = zeros[0].reshape(depth, bs, ds, n)
        params = [dtb_p[i], alog_p[i], dsk_e[i], ssd_norm[i].reshape(1, ds), emat]
        yb_p, ns = _ssd_prompt(z, xc, dtr, bp, seq, params, n, p, hg)
        r3 = lambda a: a.reshape(a.shape[0] // bs, bs, a.shape[-1])
        ya_s, yb_s, sna, snb, ss_new = _ssd_sample(
            r3(gb_s), r3(u_s), r3(xbc_s), r3(z), r3(dtr), t_p // (bs * steps), steps, bs, i, sa_all, sb_all, ss_all,
            ss_new, conv_params + params, n, p, hg, bb)
        pa.append(na)
        pb.append(nb)
        ps.append(ns.reshape(bp, heads, p, n))
        sa_l.append(sna.reshape(bs, ka, dc))
        sb_l.append(snb.reshape(bs, kb, dx))

        if i % 2 == 1:
            x1, h2, sel, counts = _outproj(ya, ya_s.reshape(tm, dc), yb_p, yb_s.reshape(tm, ds), w_out, i, *x_in,
                                           seq_mod[i], row_mod[i], geo_args, norm_post_mix[i], norm_pre_ffn[i],
                                           router_p[j], n_exp)
            tile_expert, n_valid, pos_tiles, pad_start, pad_len, n_slots = _moe_tables(sel, counts, n_exp, tm, MOE_TILE)
            fe = moe_w_gate.shape[-1]
            x_sorted, wg_b, wu_b, wd_b = _moe_dispatch(
                h2, pos_tiles, pad_start, pad_len, n_valid, n_slots, tm, MOE_TILE, moe_w_gate[j].reshape(n_exp * d, fe),
                moe_w_up[j].reshape(n_exp * d, fe), moe_w_down[j].reshape(n_exp * fe, d))
            ye = _moe_experts(x_sorted, tile_expert, n_valid, wg_b.reshape(n_exp, d, fe), wu_b.reshape(n_exp, d, fe),
                              wd_b.reshape(n_exp, fe, d), MOE_TILE)
            out = _moe_combine(ye, pos_tiles, sel, x1, seq_mod[i], row_mod[i], geo_args, norm_post_ffn[i], last)
        else:
            out = _mix_ffn(ya, ya_s.reshape(tm, dc), yb_p, yb_s.reshape(tm, ds), w_out, i, ffd_w_gate, ffd_w_up,
                           ffd_w_down, j, *x_in, seq_mod[i], row_mod[i], geo_args, norm_post_mix[i], norm_pre_ffn[i],
                           norm_post_ffn[i], last)
        if last:
            y_p, y_s = out
        else:
            x_all = out[0]

    y_prompt = y_p.reshape(bp, seq, d)
    y_sample = y_s.reshape(steps, bs, d).transpose(1, 0, 2)
    return (y_prompt, y_sample, jnp.stack(pa), jnp.stack(pb), jnp.stack(ps),
            jnp.stack(sa_l), jnp.stack(sb_l), ss_new.reshape(depth, bs, heads, p, n))
```

```python
import functools

import jax
import jax.numpy as jnp
from jax import lax
from jax.experimental import pallas as pl
from jax.experimental.pallas import tpu as pltpu

EPS = 1e-6
SSD_GROUPS = 2
SSD_CHUNK = 128
TOP_K = 2
LOG2E = 1.4426950408889634
LANES = 128
SUBLANES = 8
VMEM_LIMIT_BYTES = 56 * 1024 * 1024
MOE_TILE = 512

F32 = jnp.float32
BF16 = jnp.bfloat16


def _cparams(*sem):
    return pltpu.CompilerParams(dimension_semantics=sem, vmem_limit_bytes=VMEM_LIMIT_BYTES)


def _resident(shape):
    return pl.BlockSpec(shape, lambda *_: (0,) * len(shape), pipeline_mode=pl.Buffered(1))


def _silu(x):
    return x * (1.0 / (1.0 + jnp.exp(-x)))


def _softplus(x):
    return jnp.maximum(x, 0.0) + jnp.log1p(jnp.exp(-jnp.abs(x)))


def _rms(x, g):
    return x * lax.rsqrt(jnp.mean(x * x, axis=-1, keepdims=True) + EPS) * g


def _dot(a, b):
    return jnp.dot(a, b, preferred_element_type=F32)


def _dot_nt(a, b):
    return lax.dot_general(a, b, (((1,), (1,)), ((), ())), preferred_element_type=F32)


def _dot_tn(a, b):
    return lax.dot_general(a, b, (((0,), (0,)), ((), ())), preferred_element_type=F32)


def _chunk(*widths):
    return next(c for c in (512, 256, LANES) if all(w % c == 0 for w in widths))


def _adaln_kernel(c_ref, w_ref, b_ref, o_ref):
    s = _silu(c_ref[...]).astype(BF16)
    o_ref[...] = _dot(s, w_ref[...].astype(BF16)) + b_ref[...]


def _adaln(c_all, ada_w, ada_b):
    depth, d, d6 = ada_w.shape
    rows = c_all.shape[0]
    tn = _chunk(d6 // 6) * 2
    return pl.pallas_call(
        _adaln_kernel,
        grid=(depth, d6 // tn),
        in_specs=[
            pl.BlockSpec((rows, d), lambda l, j: (0, 0)),
            pl.BlockSpec((None, d, tn), lambda l, j: (l, 0, j)),
            pl.BlockSpec((None, 1, tn), lambda l, j: (l, 0, j)),
        ],
        out_specs=pl.BlockSpec((None, rows, tn), lambda l, j: (l, 0, j)),
        out_shape=jax.ShapeDtypeStruct((depth, rows, d6), F32),
        compiler_params=_cparams("arbitrary", "arbitrary"),
        name="adaln",
    )(c_all, ada_w, ada_b.reshape(depth, 1, d6))


class _Rows:
    def __init__(self, tm, nt, tps, reps, d, pro):
        self.tm, self.nt, self.tps, self.reps, self.d, self.pro = tm, nt, tps, reps, d, pro

    def tile(self, i):
        return jnp.maximum(i - self.pro, 0)

    def rows(self, w):
        return pl.BlockSpec((self.tm, w), lambda i, *_: (self.tile(i), 0))

    def prompt_rows(self, w):
        return pl.BlockSpec((self.tm, w), lambda i, *_: (jnp.minimum(self.tile(i), self.nt - 2), 0))

    def sample_rows(self, w):
        return pl.BlockSpec((self.tm, w), lambda i, *_: (0, 0))

    def seq_mod(self, k):
        return pl.BlockSpec((None, 1, self.d), lambda i, *_: (self.tile(i) // self.tps, 0, k))

    def row_mod(self, k, rows):
        return pl.BlockSpec((rows, self.d), lambda i, *_: (0, k))


def _mod(is_sample, seq_ref, row_ref, reps):
    rowm = jnp.concatenate([row_ref[...]] * reps, axis=0)
    return seq_ref[...] + jnp.where(is_sample, rowm, 0.0)


def _causal_conv_cols(cur, c0, w_ref, tail, ext_all, slot, nst_ref, keep_state, first, emit):
    tm, cw = cur.shape
    kp = w_ref.shape[0] - 1
    wcol = lambda j: w_ref[j:j + 1, c0:c0 + cw]
    ext = ext_all.at[slot]
    ext[0:SUBLANES, :] = jnp.where(first, 0.0, tail[:, c0:c0 + cw])
    ext[SUBLANES:SUBLANES + tm, :] = cur
    acc = wcol(kp) * cur
    for j in range(kp):
        off = SUBLANES - (kp - j)
        acc = acc + wcol(j) * ext[off:off + tm, :]
    tail[:, c0:c0 + cw] = ext[tm:tm + SUBLANES, :]
    nst_ref[:, c0:c0 + cw] = jnp.where(keep_state, nst_ref[:, c0:c0 + cw], ext[SUBLANES + tm - kp:SUBLANES + tm, :])
    emit(acc)


def _inproj_kernel(*refs, geo, dc, ds, dx, cw, split_x, zero_fill):
    if split_x:
        xp_ref, xs_ref, *refs = refs
    else:
        xp_ref, *refs = refs
    (qsh_ref, qsc_ref, rsh_ref, rsc_ref, g_ref, wch_ref, wdt_ref, caw_ref, cbw_ref, cbb_ref,
     ya_ref, z_ref, xc_ref, dt_ref, ncap_ref, ncbp_ref, gbs_ref, us_ref, xbcs_ref, *rest) = refs
    wbf, ext, tail_u, tail_x = rest[-4:]
    i = pl.program_id(0)

    @pl.when(i < geo.pro)
    def _():
        wbf[i] = wch_ref[...].astype(BF16)

    @pl.when(i >= geo.pro)
    def _():
        r = i - geo.pro
        is_s = r == geo.nt - 1
        first = r % geo.tps == 0
        x = jnp.where(is_s, xs_ref[...], xp_ref[...]) if split_x else xp_ref[...]
        sc = _mod(is_s, qsc_ref, rsc_ref, geo.reps)
        sh = _mod(is_s, qsh_ref, rsh_ref, geo.reps)
        h = (_rms(x, g_ref[...]) * (1.0 + sc) + sh).astype(BF16)
        nc = dc // cw
        for a in range(nc):
            c0 = a * cw
            gate_b = _dot_nt(h, wbf[a])
            u = _dot_nt(h, wbf[nc + a]) * _dot_nt(h, wbf[2 * nc + a])
            gbs_ref[:, c0:c0 + cw] = gate_b
            us_ref[:, c0:c0 + cw] = u

            def emit_a(v, c0=c0, gate_b=gate_b):
                ya_ref[:, c0:c0 + cw] = (gate_b * v).astype(BF16)

            _causal_conv_cols(u, c0, caw_ref, tail_u, ext, a % 2, ncap_ref, is_s, first, emit_a)
        for a in range(ds // cw):
            z_ref[:, a * cw:(a + 1) * cw] = _dot_nt(h, wbf[3 * nc + a])
        for a in range(dx // cw):
            c0 = a * cw
            xbc = _dot_nt(h, wbf[3 * nc + ds // cw + a])
            xbcs_ref[:, c0:c0 + cw] = xbc

            def emit_b(v, c0=c0):
                xc_ref[:, c0:c0 + cw] = _silu(v + cbb_ref[:, c0:c0 + cw])

            _causal_conv_cols(xbc, c0, cbw_ref, tail_x, ext, (nc + a) % 2, ncbp_ref, is_s, first, emit_b)
        dt_ref[...] = _dot_nt(h, wdt_ref[...])
        if zero_fill:
            rest[0][...] = jnp.zeros(rest[0].shape, F32)


def _inproj(x_p, x_s, seq_mod, row_mod, geo_args, norm_w, w_in, layer, w_dt, caw, cbw, cbb, bp, ds, zero_rows=0):
    tm, nt, tps, reps, d = geo_args
    dc, dx = caw.shape[1], cbw.shape[1]
    ka, kb = caw.shape[0] - 1, cbw.shape[0] - 1
    cw = _chunk(dc, ds, dx)
    npro = (3 * dc + ds + dx) // cw
    geo = _Rows(tm, nt, tps, reps, d, npro)
    t_all = tm * nt
    split_x = x_s is not None
    xin = [geo.prompt_rows(d), geo.sample_rows(d)] if split_x else [geo.rows(d)]
    xargs = [x_p, x_s] if split_x else [x_p]
    nrow = row_mod.shape[0]
    seq_blk = lambda k, w: pl.BlockSpec((None, k, w), lambda i: (jnp.minimum(geo.tile(i) // tps, bp - 1), 0, 0))
    zero_specs, zero_shapes = [], []
    if zero_rows:
        zero_specs = [pl.BlockSpec((zero_rows // (nt - 1), LANES), lambda i: (jnp.minimum(geo.tile(i), nt - 2), 0))]
        zero_shapes = [jax.ShapeDtypeStruct((zero_rows, LANES), F32)]
    return pl.pallas_call(
        functools.partial(_inproj_kernel, geo=geo, dc=dc, ds=ds, dx=dx, cw=cw, split_x=split_x,
                          zero_fill=bool(zero_rows)),
        grid=(npro + nt,),
        in_specs=xin + [
            geo.seq_mod(0), geo.seq_mod(1), geo.row_mod(0, nrow), geo.row_mod(1, nrow),
            pl.BlockSpec((1, d), lambda i: (0, 0)),
            pl.BlockSpec((None, cw, d), lambda i: (layer, jnp.minimum(i, npro - 1), 0)),
            _resident((LANES, d)), _resident(caw.shape), _resident(cbw.shape), _resident(cbb.shape),
        ],
        out_specs=[geo.rows(dc), geo.rows(ds), geo.rows(dx), geo.rows(LANES), seq_blk(ka, dc), seq_blk(kb, dx),
                   geo.sample_rows(dc), geo.sample_rows(dc), geo.sample_rows(dx)] + zero_specs,
        out_shape=[
            jax.ShapeDtypeStruct((t_all, dc), BF16),
            jax.ShapeDtypeStruct((t_all, ds), F32),
            jax.ShapeDtypeStruct((t_all, dx), F32),
            jax.ShapeDtypeStruct((t_all, LANES), F32),
            jax.ShapeDtypeStruct((bp, ka, dc), F32),
            jax.ShapeDtypeStruct((bp, kb, dx), F32),
            jax.ShapeDtypeStruct((tm, dc), F32),
            jax.ShapeDtypeStruct((tm, dc), F32),
            jax.ShapeDtypeStruct((tm, dx), F32),
        ] + zero_shapes,
        scratch_shapes=[pltpu.VMEM((npro, cw, d), BF16), pltpu.VMEM((2, SUBLANES + tm, cw), F32),
                        pltpu.VMEM((SUBLANES, dc), F32), pltpu.VMEM((SUBLANES, dx), F32)],
        compiler_params=_cparams("arbitrary"),
        name="inproj",
    )(*xargs, seq_mod, seq_mod, row_mod, row_mod, norm_w.reshape(1, d), w_in, w_dt, caw, cbw, cbb)


def _split_hi_lo(v):
    hi = v.astype(BF16)
    lo = (v - hi.astype(F32)).astype(BF16)
    return jnp.concatenate([hi, lo], axis=1)


def _diag_block(cb, mask, cum, cum_t, dt_t, xs, lane, g, hg, gw, p):
    hpl = LANES // p
    parts = []
    for slab in range(gw // LANES):
        lhs, rhs = [], []
        xslab = xs[:, g * gw + slab * LANES:g * gw + (slab + 1) * LANES]
        for j in range(hpl):
            h = g * hg + slab * hpl + j
            seg = cum[:, h:h + 1] - cum_t[h:h + 1, :]
            m = cb * jnp.where(mask, jnp.exp2(seg), 0.0) * dt_t[h:h + 1, :]
            lhs.append(m.astype(BF16))
            rhs.append(jnp.where((lane >= j * p) & (lane < (j + 1) * p), xslab, 0.0).astype(BF16))
        parts.append(_dot(jnp.concatenate(lhs, axis=1), jnp.concatenate(rhs, axis=0)))
    return jnp.concatenate(parts, axis=1)


def _gated_group_norm(y, xs_g, z_g, dsk_g, snorm_g):
    y = (y + dsk_g * xs_g) * _silu(z_g)
    return (y * lax.rsqrt(jnp.mean(y * y, axis=-1, keepdims=True) + EPS) * snorm_g).astype(BF16)


def _ssd_prompt_kernel(z_ref, xc_ref, dt_ref, dtb_ref, alog_ref, dsk_ref, snorm_ref, e_ref, yb_ref, nss_ref, st,
                       *, ds, n, p, hg, sub):
    c = pl.program_id(1)
    q = SSD_CHUNK
    gw = hg * p

    @pl.when(c == 0)
    def _():
        st[...] = jnp.zeros(st.shape, F32)

    row = lax.broadcasted_iota(jnp.int32, (q, LANES), 0)
    tril = lax.broadcasted_iota(jnp.int32, (q, q), 0) >= lax.broadcasted_iota(jnp.int32, (q, q), 1)
    lane = lax.broadcasted_iota(jnp.int32, (q, LANES), 1)
    neg_a = -jnp.exp(alog_ref[...]) * LOG2E
    for k in range(sub):
        rs = slice(k * q, (k + 1) * q)
        xs = xc_ref[rs, 0:ds]
        dt = _softplus(dt_ref[rs, :] + dtb_ref[...])
        cum = dt * neg_a
        step = 1
        while step < q:
            cum = cum + jnp.where(row >= step, pltpu.roll(cum, step, axis=0), 0.0)
            step *= 2
        cum_last = cum[q - 1:q, :]
        w_exp = _dot(_split_hi_lo(dt * jnp.exp2(cum_last - cum)), e_ref[...])
        ecum_exp = _dot(_split_hi_lo(jnp.exp2(cum)), e_ref[...])
        cum_t = cum.T
        dt_t = dt.T
        for g in range(SSD_GROUPS):
            gsl = slice(g * gw, (g + 1) * gw)
            bm = xc_ref[rs, ds + g * n:ds + (g + 1) * n]
            cm = xc_ref[rs, ds + SSD_GROUPS * n + g * n:ds + SSD_GROUPS * n + (g + 1) * n].astype(BF16)
            bm_t = bm.T.astype(BF16)
            y = _diag_block(_dot(cm, bm_t), tril, cum, cum_t, dt_t, xs, lane, g, hg, gw, p)
            s_prev = st[g]
            y = y + _dot(cm, s_prev.astype(BF16)) * ecum_exp[:, gsl]
            xw = (xs[:, gsl] * w_exp[:, gsl]).astype(BF16)
            st[g] = s_prev * ecum_exp[q - 1:q, gsl] + _dot(bm_t, xw)
            yb_ref[rs, gsl] = _gated_group_norm(y, xs[:, gsl], z_ref[rs, gsl], dsk_ref[:, gsl], snorm_ref[:, gsl])

    @pl.when(c == pl.num_programs(1) - 1)
    def _():
        for g in range(SSD_GROUPS):
            for slab in range(gw // LANES):
                r0 = g * gw + slab * LANES
                nss_ref[r0:r0 + LANES, :] = st[g, :, slab * LANES:(slab + 1) * LANES].T


def _ssd_prompt(z, xc, dtr, bsz, seq, params, n, p, hg):
    ds = z.shape[1]
    dx = xc.shape[1]
    sub = next(k for k in (8, 4, 2, 1) if seq % (k * SSD_CHUNK) == 0)
    rows = sub * SSD_CHUNK
    nc = seq // rows
    row = lambda w: pl.BlockSpec((rows, w), lambda b, c: (b * nc + c, 0))
    return pl.pallas_call(
        functools.partial(_ssd_prompt_kernel, ds=ds, n=n, p=p, hg=hg, sub=sub),
        grid=(bsz, nc),
        in_specs=[row(ds), row(dx), row(LANES)] + [_resident(a.shape) for a in params],
        out_specs=[row(ds), pl.BlockSpec((None, ds, n), lambda b, c: (b, 0, 0))],
        out_shape=[jax.ShapeDtypeStruct((bsz * seq, ds), BF16), jax.ShapeDtypeStruct((bsz, ds, n), F32)],
        scratch_shapes=[pltpu.VMEM((SSD_GROUPS, n, hg * p), F32)],
        compiler_params=_cparams("arbitrary", "arbitrary"),
        name="ssd_prompt",
    )(z, xc, dtr, *params)


def _ssd_sample_kernel(gb_ref, u_ref, xbc_ref, z_ref, dt_ref, sa_ref, sb_ref, ss_ref, caw_ref, cbw_ref, cbb_ref,
                       dtb_ref, alog_ref, dsk_ref, snorm_ref, e_ref, ya_ref, yb_ref, nca_ref, ncb_ref, nss_ref,
                       *, dc, ds, dx, n, p, hg, steps, bb):
    gw = hg * p
    rows = steps * bb
    ka = caw_ref.shape[0] - 1
    kb = cbw_ref.shape[0] - 1

    def conv(cur_ref, st_ref, w_ref, kprev, width):
        hist = [st_ref[:, j * width:(j + 1) * width] for j in range(kprev)] + [cur_ref[t] for t in range(steps)]
        outs = []
        for t in range(steps):
            acc = w_ref[kprev:kprev + 1, :] * hist[t + kprev]
            for j in range(kprev):
                acc = acc + w_ref[j:j + 1, :] * hist[t + j]
            outs.append(acc)
        return outs, hist[len(hist) - kprev:]

    v, new_a = conv(u_ref, sa_ref, caw_ref, ka, dc)
    for t in range(steps):
        ya_ref[t] = (gb_ref[t] * v[t]).astype(BF16)
    for j in range(ka):
        nca_ref[:, j * dc:(j + 1) * dc] = new_a[j]
    xcs, new_b = conv(xbc_ref, sb_ref, cbw_ref, kb, dx)
    for j in range(kb):
        ncb_ref[:, j * dx:(j + 1) * dx] = new_b[j]
    xc = _silu(jnp.concatenate(xcs, axis=0) + cbb_ref[...])
    xs = xc[:, 0:ds]

    dt = _softplus(jnp.concatenate([dt_ref[t] for t in range(steps)], axis=0) + dtb_ref[...])
    da = dt * (-jnp.exp(alog_ref[...]) * LOG2E)
    cums = [da[0:bb]]
    for t in range(1, steps):
        cums.append(cums[-1] + da[t * bb:(t + 1) * bb])
    cum = jnp.concatenate(cums, axis=0)
    cum_last = jnp.concatenate([cums[-1]] * steps, axis=0)
    w_exp = _dot(_split_hi_lo(dt * jnp.exp2(cum_last - cum)), e_ref[...])
    ecum_exp = _dot(_split_hi_lo(jnp.exp2(cum)), e_ref[...])

    def pad_t(a):
        a = jnp.concatenate([a, jnp.zeros((LANES - rows, LANES), F32)], axis=0) if rows < LANES else a
        return a.T[:, 0:rows]

    cum_t = pad_t(cum)
    dt_t = pad_t(dt)

    ri = lax.broadcasted_iota(jnp.int32, (rows, rows), 0)
    ci = lax.broadcasted_iota(jnp.int32, (rows, rows), 1)
    same = ((ri % bb) == (ci % bb)) & (ri >= ci)
    lane = lax.broadcasted_iota(jnp.int32, (rows, LANES), 1)
    rowid = lax.broadcasted_iota(jnp.int32, (rows, 1), 0) % bb
    seqlane = lax.broadcasted_iota(jnp.int32, (LANES, LANES), 1)
    nslab = gw // LANES
    for g in range(SSD_GROUPS):
        gsl = slice(g * gw, (g + 1) * gw)
        bm = xc[:, ds + g * n:ds + (g + 1) * n].astype(BF16)
        cm = xc[:, ds + SSD_GROUPS * n + g * n:ds + SSD_GROUPS * n + (g + 1) * n].astype(BF16)
        y_diag = _diag_block(_dot_nt(cm, bm), same, cum, cum_t, dt_t, xs, lane, g, hg, gw, p)
        xw = (xs[:, gsl] * w_exp[:, gsl]).astype(BF16)
        dec = ecum_exp[(steps - 1) * bb:steps * bb, gsl]
        dec = jnp.concatenate([dec, jnp.zeros((LANES - bb, gw), F32)], axis=0)
        dec_t = [dec[:, s * LANES:(s + 1) * LANES].T for s in range(nslab)]

        def per_seq(b, y_off, g=g, cm=cm, bm=bm, xw=xw, dec_t=dec_t):
            r0 = g * gw
            s0 = ss_ref[b, r0:r0 + gw, :]
            y_off = jnp.where(rowid == b, _dot_nt(cm, s0.astype(BF16)), y_off)
            upd = _dot_tn(jnp.where(rowid == b, xw, jnp.zeros_like(xw)), bm)
            for s in range(nslab):
                dcol = jnp.sum(jnp.where(seqlane == b, dec_t[s], 0.0), axis=1, keepdims=True)
                nss_ref[b, r0 + s * LANES:r0 + (s + 1) * LANES, :] = (
                    s0[s * LANES:(s + 1) * LANES, :] * dcol + upd[s * LANES:(s + 1) * LANES, :])
            return y_off

        y_off = lax.fori_loop(0, bb, per_seq, jnp.zeros((rows, gw), F32), unroll=4)
        zg = jnp.concatenate([z_ref[t, :, gsl] for t in range(steps)], axis=0)
        yn = _gated_group_norm(y_diag + y_off * ecum_exp[:, gsl], xs[:, gsl], zg, dsk_ref[:, gsl], snorm_ref[:, gsl])
        for t in range(steps):
            yb_ref[t, :, gsl] = yn[t * bb:(t + 1) * bb]


def _ssd_sample_aliased_kernel(*refs, **kw):
    _ssd_sample_kernel(*refs[1:], **kw)


def _ssd_sample(gb, u, xbc, z, dtr, blk0, steps, bs, layer, sa, sb, ss, prev_ss, params, n, p, hg, bb):
    dc, ds, dx = u.shape[2], z.shape[2], xbc.shape[2]
    ka, kb = params[0].shape[0] - 1, params[1].shape[0] - 1
    own = lambda w: pl.BlockSpec((steps, bb, w), lambda i: (0, i, 0))
    tok = lambda w: pl.BlockSpec((steps, bb, w), lambda i: (blk0, i, 0))
    state = pl.BlockSpec((None, bb, ds, n), lambda i: (layer, i, 0, 0))
    return pl.pallas_call(
        functools.partial(_ssd_sample_aliased_kernel, dc=dc, ds=ds, dx=dx, n=n, p=p, hg=hg, steps=steps, bb=bb),
        grid=(bs // bb,),
        in_specs=[pl.BlockSpec(memory_space=pl.ANY), own(dc), own(dc), own(dx), tok(ds), tok(LANES),
                  pl.BlockSpec((None, bb, ka * dc), lambda i: (layer, i, 0)),
                  pl.BlockSpec((None, bb, kb * dx), lambda i: (layer, i, 0)), state]
        + [_resident(a.shape) for a in params],
        out_specs=[own(dc), own(ds), pl.BlockSpec((bb, ka * dc), lambda i: (i, 0)),
                   pl.BlockSpec((bb, kb * dx), lambda i: (i, 0)), state],
        out_shape=[jax.ShapeDtypeStruct((steps, bs, dc), BF16), jax.ShapeDtypeStruct((steps, bs, ds), BF16),
                   jax.ShapeDtypeStruct((bs, ka * dc), F32), jax.ShapeDtypeStruct((bs, kb * dx), F32),
                   jax.ShapeDtypeStruct(prev_ss.shape, F32)],
        input_output_aliases={0: 4},
        compiler_params=_cparams("arbitrary"),
        name="ssd_sample",
    )(prev_ss, gb, u, xbc, z, dtr, sa, sb, ss, *params)


def _route(logits, n_exp, cnt_ref):
    rows = logits.shape[0]
    lane = lax.broadcasted_iota(jnp.int32, logits.shape, 1).astype(F32)
    valid = lane < n_exp
    logits = jnp.where(valid, logits, -jnp.inf)
    e = jnp.exp(logits - jnp.max(logits, axis=-1, keepdims=True))
    prob = jnp.where(valid, e / jnp.sum(e, axis=-1, keepdims=True), -1.0)
    big = float(LANES)
    m1 = jnp.max(prob, axis=-1, keepdims=True)
    i1 = jnp.min(jnp.where(prob == m1, lane, big), axis=-1, keepdims=True)
    rest = jnp.where(lane == i1, -1.0, prob)
    m2 = jnp.max(rest, axis=-1, keepdims=True)
    i2 = jnp.min(jnp.where(rest == m2, lane, big), axis=-1, keepdims=True)
    den = m1 + m2
    chosen = jnp.where((lane == i1) | (lane == i2), 1.0, 0.0)
    before = (lax.broadcasted_iota(jnp.int32, (rows, rows), 0) > lax.broadcasted_iota(jnp.int32, (rows, rows), 1))
    rank = _dot(jnp.where(before, 1.0, 0.0).astype(BF16), chosen.astype(BF16)) + cnt_ref[0:1, :]
    r1 = jnp.sum(jnp.where(lane == i1, rank, 0.0), axis=-1, keepdims=True)
    r2 = jnp.sum(jnp.where(lane == i2, rank, 0.0), axis=-1, keepdims=True)
    cnt_ref[0:1, :] = cnt_ref[0:1, :] + jnp.sum(chosen, axis=0, keepdims=True)
    cols = [i1, i2, m1 / den, m2 / den, r1, r2]
    out = jnp.zeros(logits.shape, F32)
    for k, col in enumerate(cols):
        out = jnp.where(lane == float(k), col, out)
    return out


def _outproj_kernel(yap_ref, yas_ref, ybp_ref, ybs_ref, wch_ref, *rest, geo, n_exp, kc, split_x):
    if split_x:
        xp_ref, xs_ref, *rest = rest
    else:
        xp_ref, *rest = rest
    (qg_ref, qsh_ref, qsc_ref, rg_ref, rsh_ref, rsc_ref, npost_ref, npre_ref, r_ref,
     x1_ref, h2_ref, sel_ref, cnt_ref, wbf) = rest
    i = pl.program_id(0)

    @pl.when(i < geo.pro)
    def _():
        wbf[pl.ds(pl.multiple_of(i * kc, kc), kc), :] = wch_ref[...].astype(BF16)
        cnt_ref[...] = jnp.zeros(cnt_ref.shape, F32)

    @pl.when(i >= geo.pro)
    def _():
        is_s = i - geo.pro == geo.nt - 1
        da = yap_ref.shape[1]
        nh = 2 if geo.reps % 2 == 0 else 1
        th = geo.tm // nh
        for hs in range(nh):
            rs = slice(hs * th, (hs + 1) * th)
            mod = lambda q_ref, r_ref: _mod(is_s, q_ref, r_ref, geo.reps // nh)
            mix = (_dot(jnp.where(is_s, yas_ref[rs, :], yap_ref[rs, :]), wbf[0:da, :])
                   + _dot(jnp.where(is_s, ybs_ref[rs, :], ybp_ref[rs, :]), wbf[da:, :]))
            x = jnp.where(is_s, xs_ref[rs, :], xp_ref[rs, :]) if split_x else xp_ref[rs, :]
            x1 = x + mod(qg_ref, rg_ref) * _rms(mix, npost_ref[...])
            x1_ref[rs, :] = x1
            h2 = _rms(x1, npre_ref[...]) * (1.0 + mod(qsc_ref, rsc_ref)) + mod(qsh_ref, rsh_ref)
            h2_ref[rs, :] = h2
            h_hi = h2.astype(BF16)
            h_lo = (h2 - h_hi.astype(F32)).astype(BF16)
            r = r_ref[...]
            r_hi = r.astype(BF16)
            r_lo = (r - r_hi.astype(F32)).astype(BF16)
            both = _dot(h_hi, jnp.concatenate([r_hi, r_lo], axis=1))
            logits = both[:, 0:LANES] + (_dot(h_lo, r_hi) + both[:, LANES:])
            sel_ref[rs, :] = _route(logits, n_exp, cnt_ref)


def _outproj(ya_p, ya_s, yb_p, yb_s, w_out, layer, x_p, x_s, seq_mod, row_mod, geo_args, npost, npre, router,
             n_exp):
    tm, nt, tps, reps, d = geo_args
    da, db = ya_p.shape[1], yb_p.shape[1]
    dm = da + db
    kc = _chunk(dm) // 2
    npro = dm // kc
    geo = _Rows(tm, nt, tps, reps, d, npro)
    t_all = tm * nt
    nrow = row_mod.shape[0]
    vec = pl.BlockSpec((1, d), lambda i: (0, 0))
    split_x = x_s is not None
    in_specs = [geo.prompt_rows(da), geo.sample_rows(da), geo.prompt_rows(db), geo.sample_rows(db),
                pl.BlockSpec((None, kc, d), lambda i: (layer, jnp.minimum(i, npro - 1), 0))]
    in_specs += [geo.prompt_rows(d), geo.sample_rows(d)] if split_x else [geo.rows(d)]
    in_specs += [geo.seq_mod(2), geo.seq_mod(3), geo.seq_mod(4),
                 geo.row_mod(2, nrow), geo.row_mod(3, nrow), geo.row_mod(4, nrow), vec, vec]
    args = [ya_p, ya_s, yb_p, yb_s, w_out] + ([x_p, x_s] if split_x else [x_p])
    in_specs.append(_resident((d, LANES)))
    args += [seq_mod, seq_mod, seq_mod, row_mod, row_mod, row_mod, npost.reshape(1, d), npre.reshape(1, d), router]
    out_specs = [geo.rows(d), geo.rows(d), geo.rows(LANES), pl.BlockSpec((SUBLANES, LANES), lambda i: (0, 0))]
    out_shape = [jax.ShapeDtypeStruct((t_all, d), F32), jax.ShapeDtypeStruct((t_all, d), F32),
                 jax.ShapeDtypeStruct((t_all, LANES), F32), jax.ShapeDtypeStruct((SUBLANES, LANES), F32)]
    return pl.pallas_call(
        functools.partial(_outproj_kernel, geo=geo, n_exp=n_exp, kc=kc, split_x=split_x),
        grid=(npro + nt,),
        in_specs=in_specs,
        out_specs=out_specs,
        out_shape=out_shape,
        scratch_shapes=[pltpu.VMEM((dm, d), BF16)],
        compiler_params=_cparams("arbitrary"),
        name="outproj",
    )(*args)


def _write_split(is_s, val, outp_ref, outs_ref):
    @pl.when(jnp.logical_not(is_s))
    def _():
        outp_ref[...] = val

    @pl.when(is_s)
    def _():
        outs_ref[...] = val


def _out_rows(geo, d, split_out):
    t_all = geo.tm * geo.nt
    if split_out:
        return ([geo.prompt_rows(d), geo.sample_rows(d)],
                [jax.ShapeDtypeStruct((t_all - geo.tm, d), F32), jax.ShapeDtypeStruct((geo.tm, d), F32)])
    return [geo.rows(d)], [jax.ShapeDtypeStruct((t_all, d), F32)]


def _mix_ffn_kernel(yap_ref, yas_ref, ybp_ref, ybs_ref, wo_ref, wg_ref, wu_ref, wd_ref, *rest, geo, kc, cf, n_out,
                    n_ffn, split_x, split_out):
    if split_x:
        xp_ref, xs_ref, *rest = rest
    else:
        xp_ref, *rest = rest
    (qg1_ref, qsh_ref, qsc_ref, qg2_ref, rg1_ref, rsh_ref, rsc_ref, rg2_ref, npost_ref, npre_ref, nffn_ref,
     *rest) = rest
    outs, (wob, wgb, wub, wdb, act) = rest[:-5], rest[-5:]
    i = pl.program_id(0)

    @pl.when(i < n_out)
    def _():
        wob[pl.ds(pl.multiple_of(i * kc, kc), kc), :] = wo_ref[...].astype(BF16)

    @pl.when(i < n_ffn)
    def _():
        wgb[i] = wg_ref[...].astype(BF16)
        wub[i] = wu_ref[...].astype(BF16)
        wdb[pl.ds(pl.multiple_of(i * cf, cf), cf), :] = wd_ref[...].astype(BF16)

    @pl.when(i >= geo.pro)
    def _():
        is_s = i - geo.pro == geo.nt - 1
        da = yap_ref.shape[1]
        mod = lambda q_ref, r_ref: _mod(is_s, q_ref, r_ref, geo.reps)
        mix = (_dot(jnp.where(is_s, yas_ref[...], yap_ref[...]), wob[0:da, :])
               + _dot(jnp.where(is_s, ybs_ref[...], ybp_ref[...]), wob[da:, :]))
        x = jnp.where(is_s, xs_ref[...], xp_ref[...]) if split_x else xp_ref[...]
        x1 = x + mod(qg1_ref, rg1_ref) * _rms(mix, npost_ref[...])
        h = (_rms(x1, npre_ref[...]) * (1.0 + mod(qsc_ref, rsc_ref)) + mod(qsh_ref, rsh_ref)).astype(BF16)
        for c in range(n_ffn):
            act[:, c * cf:(c + 1) * cf] = (_silu(_dot(h, wgb[c])) * _dot(h, wub[c])).astype(BF16)
        f = _dot(act[...], wdb[...])
        val = x1 + mod(qg2_ref, rg2_ref) * _rms(f, nffn_ref[...])
        if split_out:
            _write_split(is_s, val, *outs)
        else:
            outs[0][...] = val


def _mix_ffn(ya_p, ya_s, yb_p, yb_s, w_out, layer, wg, wu, wd, j, x_p, x_s, seq_mod, row_mod, geo_args, npost, npre,
             nffn, split_out):
    tm, nt, tps, reps, d = geo_args
    da, db = ya_p.shape[1], yb_p.shape[1]
    dm = da + db
    f = wg.shape[2]
    kc = _chunk(dm) // 2
    cf = 256 if f % 256 == 0 else LANES
    n_out, n_ffn = dm // kc, f // cf
    geo = _Rows(tm, nt, tps, reps, d, max(n_out, n_ffn))
    nrow = row_mod.shape[0]
    vec = pl.BlockSpec((1, d), lambda i: (0, 0))
    split_x = x_s is not None
    out_specs, out_shape = _out_rows(geo, d, split_out)
    oc = lambda i: jnp.minimum(i, n_out - 1)
    fc = lambda i: jnp.minimum(i, n_ffn - 1)
    in_specs = [geo.prompt_rows(da), geo.sample_rows(da), geo.prompt_rows(db), geo.sample_rows(db),
                pl.BlockSpec((None, kc, d), lambda i: (layer, oc(i), 0)),
                pl.BlockSpec((None, d, cf), lambda i: (j, 0, fc(i))),
                pl.BlockSpec((None, d, cf), lambda i: (j, 0, fc(i))),
                pl.BlockSpec((None, cf, d), lambda i: (j, fc(i), 0))]
    in_specs += [geo.prompt_rows(d), geo.sample_rows(d)] if split_x else [geo.rows(d)]
    in_specs += [geo.seq_mod(k) for k in (2, 3, 4, 5)] + [geo.row_mod(k, nrow) for k in (2, 3, 4, 5)] + [vec] * 3
    args = [ya_p, ya_s, yb_p, yb_s, w_out, wg, wu, wd] + ([x_p, x_s] if split_x else [x_p])
    args += [seq_mod] * 4 + [row_mod] * 4 + [npost.reshape(1, d), npre.reshape(1, d), nffn.reshape(1, d)]
    return pl.pallas_call(
        functools.partial(_mix_ffn_kernel, geo=geo, kc=kc, cf=cf, n_out=n_out, n_ffn=n_ffn, split_x=split_x,
                          split_out=split_out),
        grid=(geo.pro + nt,),
        in_specs=in_specs,
        out_specs=out_specs,
        out_shape=out_shape,
        scratch_shapes=[pltpu.VMEM((dm, d), BF16), pltpu.VMEM((n_ffn, d, cf), BF16), pltpu.VMEM((n_ffn, d, cf), BF16),
                        pltpu.VMEM((f, d), BF16), pltpu.VMEM((tm, f), BF16)],
        compiler_params=_cparams("arbitrary"),
        name="mix_ffn",
    )(*args)


def _row_copy(src, src_row, dst, dst_row, sem):
    return pltpu.make_async_copy(src.at[pl.ds(src_row, 1)], dst.at[pl.ds(dst_row, 1)], sem)


def _moe_dispatch_kernel(pad_start_ref, pad_len_ref, nvalid_ref, pos_ref, h_ref, wg_ref, wu_ref, wd_ref,
                         xs_hbm, wgb_ref, wub_ref, wdb_ref, zbuf, hbuf, sems, *, tm, n_exp, nt):
    i = pl.program_id(0)
    ts = zbuf.shape[0]
    wgb_ref[...] = wg_ref[...].astype(BF16)
    wub_ref[...] = wu_ref[...].astype(BF16)
    wdb_ref[...] = wd_ref[...].astype(BF16)

    @pl.when(i == 0)
    def _():
        zbuf[...] = jnp.zeros(zbuf.shape, F32)
        for e in range(n_exp):
            def start(k, c, e=e):
                _row_copy(zbuf, 0, xs_hbm, pad_start_ref[e] + k, sems.at[2]).start()
                return c

            def wait(k, c, e=e):
                _row_copy(zbuf, 0, xs_hbm, pad_start_ref[e] + k, sems.at[2]).wait()
                return c

            lax.fori_loop(0, pad_len_ref[e], start, 0)
            lax.fori_loop(0, pad_len_ref[e], wait, 0)

        def tile_copy(j):
            return pltpu.make_async_copy(zbuf, xs_hbm.at[pl.ds(pl.multiple_of(j * ts, ts), ts)], sems.at[2])

        def start_tile(j, c):
            tile_copy(j).start()
            return c

        def wait_tile(j, c):
            tile_copy(j).wait()
            return c

        lax.fori_loop(nvalid_ref[0], xs_hbm.shape[0] // ts, start_tile, 0)
        lax.fori_loop(nvalid_ref[0], xs_hbm.shape[0] // ts, wait_tile, 0)

    def wait_slot(s):
        for _ in range(TOP_K):
            pltpu.make_async_copy(hbuf.at[s], xs_hbm.at[pl.ds(0, tm)], sems.at[s]).wait()

    for s in range(2):
        @pl.when(i % 2 == s)
        def _(s=s):
            @pl.when(i >= 2)
            def _():
                wait_slot(s)

            hbuf[s] = h_ref[...]
            for r in range(tm):
                _row_copy(hbuf.at[s], r, xs_hbm, pos_ref[0, r], sems.at[s]).start(priority=0)
                _row_copy(hbuf.at[s], r, xs_hbm, pos_ref[0, tm + r], sems.at[s]).start(priority=1)

    @pl.when(i == nt - 1)
    def _():
        wait_slot((nt - 1) % 2)
        if nt > 1:
            wait_slot(nt % 2)


def _moe_dispatch(h2, pos_tiles, pad_start, pad_len, n_valid, n_slots, tm, tm_slot, wg, wu, wd):
    t, d = h2.shape
    nt = t // tm
    n_exp = pad_start.shape[0]
    pack = 2 * SUBLANES
    ncast = next(k for k in range(nt, 0, -1)
                 if wg.shape[0] % (k * pack) == 0 and wd.shape[0] % (k * pack) == 0)
    wrows = lambda a: pl.BlockSpec((a.shape[0] // ncast, a.shape[1]), lambda i, *_: (jnp.minimum(i, ncast - 1), 0))
    grid_spec = pltpu.PrefetchScalarGridSpec(
        num_scalar_prefetch=3,
        grid=(nt,),
        in_specs=[pl.BlockSpec((None, 1, TOP_K * tm), lambda i, *_: (i, 0, 0), memory_space=pltpu.SMEM),
                  pl.BlockSpec((tm, d), lambda i, *_: (i, 0)), wrows(wg), wrows(wu), wrows(wd)],
        out_specs=[pl.BlockSpec(memory_space=pl.ANY), wrows(wg), wrows(wu), wrows(wd)],
        scratch_shapes=[pltpu.VMEM((tm_slot, d), F32), pltpu.VMEM((2, tm, d), F32), pltpu.SemaphoreType.DMA((3,))],
    )
    return pl.pallas_call(
        functools.partial(_moe_dispatch_kernel, tm=tm, n_exp=n_exp, nt=nt),
        grid_spec=grid_spec,
        out_shape=[jax.ShapeDtypeStruct((n_slots, d), F32)] + [jax.ShapeDtypeStruct(a.shape, BF16) for a in (wg, wu, wd)],
        compiler_params=_cparams("arbitrary"),
        name="moe_dispatch",
    )(pad_start, pad_len, n_valid, pos_tiles.reshape(nt, 1, TOP_K * tm), h2, wg, wu, wd)


def _moe_expert_kernel(texp_ref, nvalid_ref, x_ref, wg_ref, wu_ref, wd_ref, o_ref):
    i = pl.program_id(0)

    @pl.when(i < nvalid_ref[0])
    def _():
        h = x_ref[...].astype(BF16)
        act = (_silu(_dot(h, wg_ref[...])) * _dot(h, wu_ref[...])).astype(BF16)
        o_ref[...] = _dot(act, wd_ref[...])

    @pl.when(i >= nvalid_ref[0])
    def _():
        o_ref[...] = jnp.zeros(o_ref.shape, F32)


def _moe_experts(x_sorted, tile_expert, n_valid, wg, wu, wd, tm):
    n_tiles = tile_expert.shape[0]
    d = x_sorted.shape[1]
    n_exp, _, fe = wg.shape
    grid_spec = pltpu.PrefetchScalarGridSpec(
        num_scalar_prefetch=2,
        grid=(n_tiles,),
        in_specs=[
            pl.BlockSpec((tm, d), lambda i, te, nv: (jnp.minimum(i, nv[0] - 1), 0)),
            pl.BlockSpec((None, d, fe), lambda i, te, nv: (te[i], 0, 0)),
            pl.BlockSpec((None, d, fe), lambda i, te, nv: (te[i], 0, 0)),
            pl.BlockSpec((None, fe, d), lambda i, te, nv: (te[i], 0, 0)),
        ],
        out_specs=pl.BlockSpec((tm, d), lambda i, te, nv: (i, 0)),
    )
    return pl.pallas_call(
        _moe_expert_kernel,
        grid_spec=grid_spec,
        out_shape=jax.ShapeDtypeStruct((n_tiles * tm, d), F32),
        compiler_params=_cparams("arbitrary"),
        name="moe_experts",
    )(tile_expert, n_valid, x_sorted, wg, wu, wd)


def _start_row_gather(idx_ref, src_hbm, dst, sem, count):
    for r in range(count):
        _row_copy(src_hbm, idx_ref[0, r], dst, r, sem).start(priority=r % 2)


def _wait_row_gather(src_hbm, dst, sem):
    pltpu.make_async_copy(src_hbm.at[pl.ds(0, dst.shape[0])], dst, sem).wait()


def _moe_combine_kernel(pos0_ref, posnext_ref, ye_hbm, sel_ref, x1_ref, qg_ref, rg_ref, npost_ref, *rest,
                        geo, split_out):
    outs, (ybuf, sems) = rest[:-2], rest[-2:]
    i = pl.program_id(0)
    tm = geo.tm
    slot = i % 2

    @pl.when(i == 0)
    def _():
        _start_row_gather(pos0_ref, ye_hbm, ybuf.at[0], sems.at[0], 2 * tm)

    @pl.when(i + 1 < geo.nt)
    def _():
        _start_row_gather(posnext_ref, ye_hbm, ybuf.at[1 - slot], sems.at[1 - slot], 2 * tm)

    _wait_row_gather(ye_hbm, ybuf.at[slot], sems.at[slot])
    is_s = i == geo.nt - 1
    sel = sel_ref[...]
    lane = lax.broadcasted_iota(jnp.int32, sel.shape, 1)
    w1 = jnp.sum(jnp.where(lane == 2, sel, 0.0), axis=-1, keepdims=True)
    w2 = jnp.sum(jnp.where(lane == 3, sel, 0.0), axis=-1, keepdims=True)
    f = w1 * ybuf[slot, 0:tm, :] + w2 * ybuf[slot, tm:2 * tm, :]
    val = x1_ref[...] + _mod(is_s, qg_ref, rg_ref, geo.reps) * _rms(f, npost_ref[...])
    if split_out:
        _write_split(is_s, val, *outs)
    else:
        outs[0][...] = val


def _moe_combine(ye, pos, sel, x1, seq_mod, row_mod, geo_args, npost, split_out):
    tm, nt, tps, reps, d = geo_args
    geo = _Rows(tm, nt, tps, reps, d, 0)
    out_specs, out_shape = _out_rows(geo, d, split_out)
    pos3 = pos.reshape(nt, 1, 2 * tm)
    smem_blk = lambda fn: pl.BlockSpec((None, 1, 2 * tm), fn, memory_space=pltpu.SMEM)
    return pl.pallas_call(
        functools.partial(_moe_combine_kernel, geo=geo, split_out=split_out),
        grid=(nt,),
        in_specs=[smem_blk(lambda i: (0, 0, 0)),
                  smem_blk(lambda i: (jnp.minimum(i + 1, nt - 1), 0, 0)),
                  pl.BlockSpec(memory_space=pl.ANY),
                  geo.rows(LANES), geo.rows(d), geo.seq_mod(5), geo.row_mod(5, row_mod.shape[0]),
                  pl.BlockSpec((1, d), lambda i: (0, 0))],
        out_specs=out_specs,
        out_shape=out_shape,
        scratch_shapes=[pltpu.VMEM((2, 2 * tm, d), F32), pltpu.SemaphoreType.DMA((2,))],
        compiler_params=_cparams("arbitrary"),
        name="moe_combine",
    )(pos3, pos3, ye, sel, x1, seq_mod, row_mod, npost.reshape(1, d))


def _moe_tables(sel, counts, n_exp, tm_tok, tm_slot):
    t = sel.shape[0]
    counts = counts[0, :n_exp].astype(jnp.int32)
    tiles_e = (counts + tm_slot - 1) // tm_slot
    tile_end = jnp.cumsum(tiles_e)
    slot_start = (tile_end - tiles_e) * tm_slot
    choice = sel[:, 0:TOP_K].astype(jnp.int32)
    rank = sel[:, 2 * TOP_K:3 * TOP_K].astype(jnp.int32)
    onehot = choice[:, :, None] == jnp.arange(n_exp, dtype=jnp.int32)[None, None, :]
    pos = jnp.sum(jnp.where(onehot, slot_start[None, None, :], 0), axis=-1) + rank
    n_tiles = (TOP_K * t + n_exp * (tm_slot - 1)) // tm_slot
    tile_expert = jnp.minimum(
        jnp.sum((jnp.arange(n_tiles, dtype=jnp.int32)[:, None] >= tile_end[None, :]).astype(jnp.int32), axis=1),
        n_exp - 1)
    pos_tiles = jnp.concatenate([pos[:, k].reshape(-1, tm_tok) for k in range(TOP_K)], axis=1)
    pad_start = slot_start + counts
    pad_len = tiles_e * tm_slot - counts
    return tile_expert, tile_end[-1:].astype(jnp.int32), pos_tiles, pad_start, pad_len, n_tiles * tm_slot


def _largest_tile(t, want):
    tm = min(want, t)
    while t % tm:
        tm //= 2
    return tm


def kernel(x_prompt, x_sample, c_prompt, c_sample, state_conva, state_convb, state_ssm, ada_w, ada_b, norm_pre_mix,
           norm_post_mix, norm_pre_ffn, norm_post_ffn, w_in, w_out, conva_w, convb_w, convb_b, dt_bias, a_log, d_skip,
           ssd_norm, ffd_w_gate, ffd_w_up, ffd_w_down, moe_router, moe_w_gate, moe_w_up, moe_w_down):
    bp, seq, d = x_prompt.shape
    bs, steps, _ = x_sample.shape
    depth = w_in.shape[0]
    dc = conva_w.shape[-1]
    dx = convb_w.shape[-1]
    ds = ssd_norm.shape[-1]
    heads = dt_bias.shape[-1]
    p = ds // heads
    n = (dx - ds) // (2 * SSD_GROUPS)
    hg = heads // SSD_GROUPS
    n_exp = moe_router.shape[-1]
    ka, kb = conva_w.shape[1] - 1, convb_w.shape[1] - 1
    tm = steps * bs
    t_p = bp * seq
    assert seq % SSD_CHUNK == 0 and LANES % p == 0 and (hg * p) % LANES == 0 and n == LANES
    assert heads <= LANES and n_exp <= LANES and dc % LANES == 0 and dx % LANES == 0 and d % LANES == 0
    assert w_in.shape[-1] == 3 * dc + ds + dx + heads and seq % tm == 0 and tm % SUBLANES == 0
    assert bs % SUBLANES == 0 and (t_p // bs) % steps == 0
    nt = t_p // tm + 1
    geo_args = (tm, nt, seq // tm, steps, d)

    w_in_t = jnp.swapaxes(w_in, 1, 2)
    w_dt = jnp.pad(w_in_t[:, 3 * dc + ds + dx:, :], ((0, 0), (0, LANES - heads), (0, 0))).astype(BF16)
    router_p = jnp.pad(moe_router, ((0, 0), (0, 0), (0, LANES - n_exp)))
    padh = lambda a: jnp.pad(a, ((0, 0), (0, LANES - heads))).reshape(depth, 1, LANES)
    dtb_p, alog_p = padh(dt_bias), padh(a_log)
    dsk_e = jnp.repeat(d_skip, p, axis=-1).reshape(depth, 1, ds)
    hot = (jnp.arange(LANES)[:, None] == (jnp.arange(ds)[None, :] // p)).astype(BF16)
    emat = jnp.concatenate([hot, hot], axis=0)

    mod = _adaln(jnp.concatenate([c_prompt, c_sample], axis=0), ada_w, ada_b)
    seq_mod = jnp.pad(mod[:, :bp], ((0, 0), (0, 1), (0, 0))).reshape(depth, bp + 1, 1, 6 * d)
    row_mod = mod[:, bp:]

    bb = _largest_tile(bs, 16)
    x_all = None
    xs_tm = x_sample.transpose(1, 0, 2).reshape(tm, d)
    xp2d = x_prompt.reshape(t_p, d)
    sa_all = state_conva.reshape(depth, bs, ka * dc)
    sb_all = state_convb.reshape(depth, bs, kb * dx)
    ss_all = state_ssm.reshape(depth, bs, ds, n)
    pa, pb, ps, sa_l, sb_l = [], [], [], [], []
    ss_new = None
    for i in range(depth):
        j = i // 2
        last = i == depth - 1
        x_in = (xp2d, xs_tm) if x_all is None else (x_all, None)
        conv_params = [conva_w[i], convb_w[i], convb_b[i].reshape(1, dx)]
        ya, z, xc, dtr, na, nb, gb_s, u_s, xbc_s, *zeros = _inproj(
            *x_in, seq_mod[i], row_mod[i], geo_args, norm_pre_mix[i], w_in_t, i, w_dt[i], *conv_params, bp, ds,
            zero_rows=depth * bs * ds * n // LANES if ss_new is None else 0)
        if zeros:
            ss_new = zeros[0].reshape(depth, bs, ds, n)
        params = [dtb_p[i], alog_p[i], dsk_e[i], ssd_norm[i].reshape(1, ds), emat]
        yb_p, ns = _ssd_prompt(z, xc, dtr, bp, seq, params, n, p, hg)
        r3 = lambda a: a.reshape(a.shape[0] // bs, bs, a.shape[-1])
        ya_s, yb_s, sna, snb, ss_new = _ssd_sample(
            r3(gb_s), r3(u_s), r3(xbc_s), r3(z), r3(dtr), t_p // (bs * steps), steps, bs, i, sa_all, sb_all, ss_all,
            ss_new, conv_params + params, n, p, hg, bb)
        pa.append(na)
        pb.append(nb)
        ps.append(ns.reshape(bp, heads, p, n))
        sa_l.append(sna.reshape(bs, ka, dc))
        sb_l.append(snb.reshape(bs, kb, dx))

        if i % 2 == 1:
            x1, h2, sel, counts = _outproj(ya, ya_s.reshape(tm, dc), yb_p, yb_s.reshape(tm, ds), w_out, i, *x_in,
                                           seq_mod[i], row_mod[i], geo_args, norm_post_mix[i], norm_pre_ffn[i],
                                           router_p[j], n_exp)
            tile_expert, n_valid, pos_tiles, pad_start, pad_len, n_slots = _moe_tables(sel, counts, n_exp, tm, MOE_TILE)
            fe = moe_w_gate.shape[-1]
            x_sorted, wg_b, wu_b, wd_b = _moe_dispatch(
                h2, pos_tiles, pad_start, pad_len, n_valid, n_slots, tm, MOE_TILE, moe_w_gate[j].reshape(n_exp * d, fe),
                moe_w_up[j].reshape(n_exp * d, fe), moe_w_down[j].reshape(n_exp * fe, d))
            ye = _moe_experts(x_sorted, tile_expert, n_valid, wg_b.reshape(n_exp, d, fe), wu_b.reshape(n_exp, d, fe),
                              wd_b.reshape(n_exp, fe, d), MOE_TILE)
            out = _moe_combine(ye, pos_tiles, sel, x1, seq_mod[i], row_mod[i], geo_args, norm_post_ffn[i], last)
        else:
            out = _mix_ffn(ya, ya_s.reshape(tm, dc), yb_p, yb_s.reshape(tm, ds), w_out, i, ffd_w_gate, ffd_w_up,
                           ffd_w_down, j, *x_in, seq_mod[i], row_mod[i], geo_args, norm_post_mix[i], norm_pre_ffn[i],
                           norm_post_ffn[i], last)
        if last:
            y_p, y_s = out
        else:
            x_all = out[0]

    y_prompt = y_p.reshape(bp, seq, d)
    y_sample = y_s.reshape(steps, bs, d).transpose(1, 0, 2)
    return (y_prompt, y_sample, jnp.stack(pa), jnp.stack(pb), jnp.stack(ps),
            jnp.stack(sa_l), jnp.stack(sb_l), ss_new.reshape(depth, bs, heads, p, n))
```

```python
import functools

import jax
import jax.numpy as jnp
from jax import lax
from jax.experimental import pallas as pl
from jax.experimental.pallas import tpu as pltpu

EPS = 1e-6
SSD_GROUPS = 2
SSD_CHUNK = 128
TOP_K = 2
LOG2E = 1.4426950408889634
LANES = 128
SUBLANES = 8
VMEM_LIMIT_BYTES = 56 * 1024 * 1024
MOE_TILE = 512

F32 = jnp.float32
BF16 = jnp.bfloat16


def _cparams(*sem):
    return pltpu.CompilerParams(dimension_semantics=sem, vmem_limit_bytes=VMEM_LIMIT_BYTES)


def _resident(shape):
    return pl.BlockSpec(shape, lambda *_: (0,) * len(shape), pipeline_mode=pl.Buffered(1))


def _silu(x):
    return x * (1.0 / (1.0 + jnp.exp(-x)))


def _softplus(x):
    return jnp.maximum(x, 0.0) + jnp.log1p(jnp.exp(-jnp.abs(x)))


def _rms(x, g):
    return x * lax.rsqrt(jnp.mean(x * x, axis=-1, keepdims=True) + EPS) * g


def _dot(a, b):
    return jnp.dot(a, b, preferred_element_type=F32)


def _dot_nt(a, b):
    return lax.dot_general(a, b, (((1,), (1,)), ((), ())), preferred_element_type=F32)


def _dot_tn(a, b):
    return lax.dot_general(a, b, (((0,), (0,)), ((), ())), preferred_element_type=F32)


def _chunk(*widths):
    return next(c for c in (512, 256, LANES) if all(w % c == 0 for w in widths))


def _adaln_kernel(c_ref, w_ref, b_ref, o_ref):
    s = _silu(c_ref[...]).astype(BF16)
    o_ref[...] = _dot(s, w_ref[...].astype(BF16)) + b_ref[...]


def _adaln(c_all, ada_w, ada_b):
    depth, d, d6 = ada_w.shape
    rows = c_all.shape[0]
    tn = _chunk(d6 // 6) * 2
    return pl.pallas_call(
        _adaln_kernel,
        grid=(depth, d6 // tn),
        in_specs=[
            pl.BlockSpec((rows, d), lambda l, j: (0, 0)),
            pl.BlockSpec((None, d, tn), lambda l, j: (l, 0, j)),
            pl.BlockSpec((None, 1, tn), lambda l, j: (l, 0, j)),
        ],
        out_specs=pl.BlockSpec((None, rows, tn), lambda l, j: (l, 0, j)),
        out_shape=jax.ShapeDtypeStruct((depth, rows, d6), F32),
        compiler_params=_cparams("arbitrary", "arbitrary"),
        name="adaln",
    )(c_all, ada_w, ada_b.reshape(depth, 1, d6))


class _Rows:
    def __init__(self, tm, nt, tps, reps, d, pro):
        self.tm, self.nt, self.tps, self.reps, self.d, self.pro = tm, nt, tps, reps, d, pro

    def tile(self, i):
        return jnp.maximum(i - self.pro, 0)

    def rows(self, w):
        return pl.BlockSpec((self.tm, w), lambda i, *_: (self.tile(i), 0))

    def prompt_rows(self, w):
        return pl.BlockSpec((self.tm, w), lambda i, *_: (jnp.minimum(self.tile(i), self.nt - 2), 0))

    def sample_rows(self, w):
        return pl.BlockSpec((self.tm, w), lambda i, *_: (0, 0))

    def seq_mod(self, k):
        return pl.BlockSpec((None, 1, self.d), lambda i, *_: (self.tile(i) // self.tps, 0, k))

    def row_mod(self, k, rows):
        return pl.BlockSpec((rows, self.d), lambda i, *_: (0, k))


def _mod(is_sample, seq_ref, row_ref, reps):
    rowm = jnp.concatenate([row_ref[...]] * reps, axis=0)
    return seq_ref[...] + jnp.where(is_sample, rowm, 0.0)


def _causal_conv_cols(cur, c0, w_ref, tail, ext_all, slot, nst_ref, keep_state, first, emit):
    tm, cw = cur.shape
    kp = w_ref.shape[0] - 1
    wcol = lambda j: w_ref[j:j + 1, c0:c0 + cw]
    ext = ext_all.at[slot]
    ext[0:SUBLANES, :] = jnp.where(first, 0.0, tail[:, c0:c0 + cw])
    ext[SUBLANES:SUBLANES + tm, :] = cur
    acc = wcol(kp) * cur
    for j in range(kp):
        off = SUBLANES - (kp - j)
        acc = acc + wcol(j) * ext[off:off + tm, :]
    tail[:, c0:c0 + cw] = ext[tm:tm + SUBLANES, :]
    nst_ref[:, c0:c0 + cw] = jnp.where(keep_state, nst_ref[:, c0:c0 + cw], ext[SUBLANES + tm - kp:SUBLANES + tm, :])
    emit(acc)


def _inproj_kernel(*refs, geo, dc, ds, dx, cw, split_x, zero_fill):
    if split_x:
        xp_ref, xs_ref, *refs = refs
    else:
        xp_ref, *refs = refs
    (qsh_ref, qsc_ref, rsh_ref, rsc_ref, g_ref, wch_ref, wdt_ref, caw_ref, cbw_ref, cbb_ref,
     ya_ref, z_ref, xc_ref, dt_ref, ncap_ref, ncbp_ref, gbs_ref, us_ref, xbcs_ref, *rest) = refs
    wbf, ext, tail_u, tail_x = rest[-4:]
    i = pl.program_id(0)

    @pl.when(i < geo.pro)
    def _():
        wbf[i] = wch_ref[...].astype(BF16)

    @pl.when(i >= geo.pro)
    def _():
        r = i - geo.pro
        is_s = r == geo.nt - 1
        first = r % geo.tps == 0
        x = jnp.where(is_s, xs_ref[...], xp_ref[...]) if split_x else xp_ref[...]
        sc = _mod(is_s, qsc_ref, rsc_ref, geo.reps)
        sh = _mod(is_s, qsh_ref, rsh_ref, geo.reps)
        h = (_rms(x, g_ref[...]) * (1.0 + sc) + sh).astype(BF16)
        nc = dc // cw
        for a in range(nc):
            c0 = a * cw
            gate_b = _dot_nt(h, wbf[a])
            u = _dot_nt(h, wbf[nc + a]) * _dot_nt(h, wbf[2 * nc + a])
            gbs_ref[:, c0:c0 + cw] = gate_b
            us_ref[:, c0:c0 + cw] = u

            def emit_a(v, c0=c0, gate_b=gate_b):
                ya_ref[:, c0:c0 + cw] = (gate_b * v).astype(BF16)

            _causal_conv_cols(u, c0, caw_ref, tail_u, ext, a % 2, ncap_ref, is_s, first, emit_a)
        for a in range(ds // cw):
            z_ref[:, a * cw:(a + 1) * cw] = _dot_nt(h, wbf[3 * nc + a])
        for a in range(dx // cw):
            c0 = a * cw
            xbc = _dot_nt(h, wbf[3 * nc + ds // cw + a])
            xbcs_ref[:, c0:c0 + cw] = xbc

            def emit_b(v, c0=c0):
                xc_ref[:, c0:c0 + cw] = _silu(v + cbb_ref[:, c0:c0 + cw])

            _causal_conv_cols(xbc, c0, cbw_ref, tail_x, ext, (nc + a) % 2, ncbp_ref, is_s, first, emit_b)
        dt_ref[...] = _dot_nt(h, wdt_ref[...])
        if zero_fill:
            rest[0][...] = jnp.zeros(rest[0].shape, F32)


def _inproj(x_p, x_s, seq_mod, row_mod, geo_args, norm_w, w_in, layer, w_dt, caw, cbw, cbb, bp, ds, zero_rows=0):
    tm, nt, tps, reps, d = geo_args
    dc, dx = caw.shape[1], cbw.shape[1]
    ka, kb = caw.shape[0] - 1, cbw.shape[0] - 1
    cw = _chunk(dc, ds, dx)
    npro = (3 * dc + ds + dx) // cw
    geo = _Rows(tm, nt, tps, reps, d, npro)
    t_all = tm * nt
    split_x = x_s is not None
    xin = [geo.prompt_rows(d), geo.sample_rows(d)] if split_x else [geo.rows(d)]
    xargs = [x_p, x_s] if split_x else [x_p]
    nrow = row_mod.shape[0]
    seq_blk = lambda k, w: pl.BlockSpec((None, k, w), lambda i: (jnp.minimum(geo.tile(i) // tps, bp - 1), 0, 0))
    zero_specs, zero_shapes = [], []
    if zero_rows:
        zero_specs = [pl.BlockSpec((zero_rows // (nt - 1), LANES), lambda i: (jnp.minimum(geo.tile(i), nt - 2), 0))]
        zero_shapes = [jax.ShapeDtypeStruct((zero_rows, LANES), F32)]
    return pl.pallas_call(
        functools.partial(_inproj_kernel, geo=geo, dc=dc, ds=ds, dx=dx, cw=cw, split_x=split_x,
                          zero_fill=bool(zero_rows)),
        grid=(npro + nt,),
        in_specs=xin + [
            geo.seq_mod(0), geo.seq_mod(1), geo.row_mod(0, nrow), geo.row_mod(1, nrow),
            pl.BlockSpec((1, d), lambda i: (0, 0)),
            pl.BlockSpec((None, cw, d), lambda i: (layer, jnp.minimum(i, npro - 1), 0)),
            _resident((LANES, d)), _resident(caw.shape), _resident(cbw.shape), _resident(cbb.shape),
        ],
        out_specs=[geo.rows(dc), geo.rows(ds), geo.rows(dx), geo.rows(LANES), seq_blk(ka, dc), seq_blk(kb, dx),
                   geo.sample_rows(dc), geo.sample_rows(dc), geo.sample_rows(dx)] + zero_specs,
        out_shape=[
            jax.ShapeDtypeStruct((t_all, dc), BF16),
            jax.ShapeDtypeStruct((t_all, ds), F32),
            jax.ShapeDtypeStruct((t_all, dx), F32),
            jax.ShapeDtypeStruct((t_all, LANES), F32),
            jax.ShapeDtypeStruct((bp, ka, dc), F32),
            jax.ShapeDtypeStruct((bp, kb, dx), F32),
            jax.ShapeDtypeStruct((tm, dc), F32),
            jax.ShapeDtypeStruct((tm, dc), F32),
            jax.ShapeDtypeStruct((tm, dx), F32),
        ] + zero_shapes,
        scratch_shapes=[pltpu.VMEM((npro, cw, d), BF16), pltpu.VMEM((2, SUBLANES + tm, cw), F32),
                        pltpu.VMEM((SUBLANES, dc), F32), pltpu.VMEM((SUBLANES, dx), F32)],
        compiler_params=_cparams("arbitrary"),
        name="inproj",
    )(*xargs, seq_mod, seq_mod, row_mod, row_mod, norm_w.reshape(1, d), w_in, w_dt, caw, cbw, cbb)


def _split_hi_lo(v):
    hi = v.astype(BF16)
    lo = (v - hi.astype(F32)).astype(BF16)
    return jnp.concatenate([hi, lo], axis=1)


def _diag_block(cb, mask, cum, cum_t, dt_t, xs, lane, g, hg, gw, p):
    hpl = LANES // p
    parts = []
    for slab in range(gw // LANES):
        lhs, rhs = [], []
        xslab = xs[:, g * gw + slab * LANES:g * gw + (slab + 1) * LANES]
        for j in range(hpl):
            h = g * hg + slab * hpl + j
            seg = cum[:, h:h + 1] - cum_t[h:h + 1, :]
            m = cb * jnp.where(mask, jnp.exp2(seg), 0.0) * dt_t[h:h + 1, :]
            lhs.append(m.astype(BF16))
            rhs.append(jnp.where((lane >= j * p) & (lane < (j + 1) * p), xslab, 0.0).astype(BF16))
        parts.append(_dot(jnp.concatenate(lhs, axis=1), jnp.concatenate(rhs, axis=0)))
    return jnp.concatenate(parts, axis=1)


def _gated_group_norm(y, xs_g, z_g, dsk_g, snorm_g):
    y = (y + dsk_g * xs_g) * _silu(z_g)
    return (y * lax.rsqrt(jnp.mean(y * y, axis=-1, keepdims=True) + EPS) * snorm_g).astype(BF16)


def _ssd_prompt_kernel(z_ref, xc_ref, dt_ref, dtb_ref, alog_ref, dsk_ref, snorm_ref, e_ref, yb_ref, nss_ref, st,
                       *, ds, n, p, hg, sub):
    c = pl.program_id(1)
    q = SSD_CHUNK
    gw = hg * p

    @pl.when(c == 0)
    def _():
        st[...] = jnp.zeros(st.shape, F32)

    row = lax.broadcasted_iota(jnp.int32, (q, LANES), 0)
    tril = lax.broadcasted_iota(jnp.int32, (q, q), 0) >= lax.broadcasted_iota(jnp.int32, (q, q), 1)
    lane = lax.broadcasted_iota(jnp.int32, (q, LANES), 1)
    neg_a = -jnp.exp(alog_ref[...]) * LOG2E
    for k in range(sub):
        rs = slice(k * q, (k + 1) * q)
        xs = xc_ref[rs, 0:ds]
        dt = _softplus(dt_ref[rs, :] + dtb_ref[...])
        cum = dt * neg_a
        step = 1
        while step < q:
            cum = cum + jnp.where(row >= step, pltpu.roll(cum, step, axis=0), 0.0)
            step *= 2
        cum_last = cum[q - 1:q, :]
        w_exp = _dot(_split_hi_lo(dt * jnp.exp2(cum_last - cum)), e_ref[...])
        ecum_exp = _dot(_split_hi_lo(jnp.exp2(cum)), e_ref[...])
        cum_t = cum.T
        dt_t = dt.T
        for g in range(SSD_GROUPS):
            gsl = slice(g * gw, (g + 1) * gw)
            bm = xc_ref[rs, ds + g * n:ds + (g + 1) * n]
            cm = xc_ref[rs, ds + SSD_GROUPS * n + g * n:ds + SSD_GROUPS * n + (g + 1) * n].astype(BF16)
            bm_t = bm.T.astype(BF16)
            y = _diag_block(_dot(cm, bm_t), tril, cum, cum_t, dt_t, xs, lane, g, hg, gw, p)
            s_prev = st[g]
            y = y + _dot(cm, s_prev.astype(BF16)) * ecum_exp[:, gsl]
            xw = (xs[:, gsl] * w_exp[:, gsl]).astype(BF16)
            st[g] = s_prev * ecum_exp[q - 1:q, gsl] + _dot(bm_t, xw)
            yb_ref[rs, gsl] = _gated_group_norm(y, xs[:, gsl], z_ref[rs, gsl], dsk_ref[:, gsl], snorm_ref[:, gsl])

    @pl.when(c == pl.num_programs(1) - 1)
    def _():
        for g in range(SSD_GROUPS):
            for slab in range(gw // LANES):
                r0 = g * gw + slab * LANES
                nss_ref[r0:r0 + LANES, :] = st[g, :, slab * LANES:(slab + 1) * LANES].T


def _ssd_prompt(z, xc, dtr, bsz, seq, params, n, p, hg):
    ds = z.shape[1]
    dx = xc.shape[1]
    sub = next(k for k in (8, 4, 2, 1) if seq % (k * SSD_CHUNK) == 0)
    rows = sub * SSD_CHUNK
    nc = seq // rows
    row = lambda w: pl.BlockSpec((rows, w), lambda b, c: (b * nc + c, 0))
    return pl.pallas_call(
        functools.partial(_ssd_prompt_kernel, ds=ds, n=n, p=p, hg=hg, sub=sub),
        grid=(bsz, nc),
        in_specs=[row(ds), row(dx), row(LANES)] + [_resident(a.shape) for a in params],
        out_specs=[row(ds), pl.BlockSpec((None, ds, n), lambda b, c: (b, 0, 0))],
        out_shape=[jax.ShapeDtypeStruct((bsz * seq, ds), BF16), jax.ShapeDtypeStruct((bsz, ds, n), F32)],
        scratch_shapes=[pltpu.VMEM((SSD_GROUPS, n, hg * p), F32)],
        compiler_params=_cparams("arbitrary", "arbitrary"),
        name="ssd_prompt",
    )(z, xc, dtr, *params)


def _ssd_sample_kernel(gb_ref, u_ref, xbc_ref, z_ref, dt_ref, sa_ref, sb_ref, ss_ref, caw_ref, cbw_ref, cbb_ref,
                       dtb_ref, alog_ref, dsk_ref, snorm_ref, e_ref, ya_ref, yb_ref, nca_ref, ncb_ref, nss_ref,
                       *, dc, ds, dx, n, p, hg, steps, bb):
    gw = hg * p
    rows = steps * bb
    ka = caw_ref.shape[0] - 1
    kb = cbw_ref.shape[0] - 1

    def conv(cur_ref, st_ref, w_ref, kprev, width):
        hist = [st_ref[:, j * width:(j + 1) * width] for j in range(kprev)] + [cur_ref[t] for t in range(steps)]
        outs = []
        for t in range(steps):
            acc = w_ref[kprev:kprev + 1, :] * hist[t + kprev]
            for j in range(kprev):
                acc = acc + w_ref[j:j + 1, :] * hist[t + j]
            outs.append(acc)
        return outs, hist[len(hist) - kprev:]

    v, new_a = conv(u_ref, sa_ref, caw_ref, ka, dc)
    for t in range(steps):
        ya_ref[t] = (gb_ref[t] * v[t]).astype(BF16)
    for j in range(ka):
        nca_ref[:, j * dc:(j + 1) * dc] = new_a[j]
    xcs, new_b = conv(xbc_ref, sb_ref, cbw_ref, kb, dx)
    for j in range(kb):
        ncb_ref[:, j * dx:(j + 1) * dx] = new_b[j]
    xc = _silu(jnp.concatenate(xcs, axis=0) + cbb_ref[...])
    xs = xc[:, 0:ds]

    dt = _softplus(jnp.concatenate([dt_ref[t] for t in range(steps)], axis=0) + dtb_ref[...])
    da = dt * (-jnp.exp(alog_ref[...]) * LOG2E)
    cums = [da[0:bb]]
    for t in range(1, steps):
        cums.append(cums[-1] + da[t * bb:(t + 1) * bb])
    cum = jnp.concatenate(cums, axis=0)
    cum_last = jnp.concatenate([cums[-1]] * steps, axis=0)
    w_exp = _dot(_split_hi_lo(dt * jnp.exp2(cum_last - cum)), e_ref[...])
    ecum_exp = _dot(_split_hi_lo(jnp.exp2(cum)), e_ref[...])

    def pad_t(a):
        a = jnp.concatenate([a, jnp.zeros((LANES - rows, LANES), F32)], axis=0) if rows < LANES else a
        return a.T[:, 0:rows]

    cum_t = pad_t(cum)
    dt_t = pad_t(dt)

    ri = lax.broadcasted_iota(jnp.int32, (rows, rows), 0)
    ci = lax.broadcasted_iota(jnp.int32, (rows, rows), 1)
    same = ((ri % bb) == (ci % bb)) & (ri >= ci)
    lane = lax.broadcasted_iota(jnp.int32, (rows, LANES), 1)
    rowid = lax.broadcasted_iota(jnp.int32, (rows, 1), 0) % bb
    seqlane = lax.broadcasted_iota(jnp.int32, (LANES, LANES), 1)
    nslab = gw // LANES
    for g in range(SSD_GROUPS):
        gsl = slice(g * gw, (g + 1) * gw)
        bm = xc[:, ds + g * n:ds + (g + 1) * n].astype(BF16)
        cm = xc[:, ds + SSD_GROUPS * n + g * n:ds + SSD_GROUPS * n + (g + 1) * n].astype(BF16)
        y_diag = _diag_block(_dot_nt(cm, bm), same, cum, cum_t, dt_t, xs, lane, g, hg, gw, p)
        xw = (xs[:, gsl] * w_exp[:, gsl]).astype(BF16)
        dec = ecum_exp[(steps - 1) * bb:steps * bb, gsl]
        dec = jnp.concatenate([dec, jnp.zeros((LANES - bb, gw), F32)], axis=0)
        dec_t = [dec[:, s * LANES:(s + 1) * LANES].T for s in range(nslab)]

        def per_seq(b, y_off, g=g, cm=cm, bm=bm, xw=xw, dec_t=dec_t):
            r0 = g * gw
            s0 = ss_ref[b, r0:r0 + gw, :]
            y_off = jnp.where(rowid == b, _dot_nt(cm, s0.astype(BF16)), y_off)
            upd = _dot_tn(jnp.where(rowid == b, xw, jnp.zeros_like(xw)), bm)
            for s in range(nslab):
                dcol = jnp.sum(jnp.where(seqlane == b, dec_t[s], 0.0), axis=1, keepdims=True)
                nss_ref[b, r0 + s * LANES:r0 + (s + 1) * LANES, :] = (
                    s0[s * LANES:(s + 1) * LANES, :] * dcol + upd[s * LANES:(s + 1) * LANES, :])
            return y_off

        y_off = lax.fori_loop(0, bb, per_seq, jnp.zeros((rows, gw), F32), unroll=8)
        zg = jnp.concatenate([z_ref[t, :, gsl] for t in range(steps)], axis=0)
        yn = _gated_group_norm(y_diag + y_off * ecum_exp[:, gsl], xs[:, gsl], zg, dsk_ref[:, gsl], snorm_ref[:, gsl])
        for t in range(steps):
            yb_ref[t, :, gsl] = yn[t * bb:(t + 1) * bb]


def _ssd_sample_aliased_kernel(*refs, **kw):
    _ssd_sample_kernel(*refs[1:], **kw)


def _ssd_sample(gb, u, xbc, z, dtr, blk0, steps, bs, layer, sa, sb, ss, prev_ss, params, n, p, hg, bb):
    dc, ds, dx = u.shape[2], z.shape[2], xbc.shape[2]
    ka, kb = params[0].shape[0] - 1, params[1].shape[0] - 1
    own = lambda w: pl.BlockSpec((steps, bb, w), lambda i: (0, i, 0))
    tok = lambda w: pl.BlockSpec((steps, bb, w), lambda i: (blk0, i, 0))
    state = pl.BlockSpec((None, bb, ds, n), lambda i: (layer, i, 0, 0))
    return pl.pallas_call(
        functools.partial(_ssd_sample_aliased_kernel, dc=dc, ds=ds, dx=dx, n=n, p=p, hg=hg, steps=steps, bb=bb),
        grid=(bs // bb,),
        in_specs=[pl.BlockSpec(memory_space=pl.ANY), own(dc), own(dc), own(dx), tok(ds), tok(LANES),
                  pl.BlockSpec((None, bb, ka * dc), lambda i: (layer, i, 0)),
                  pl.BlockSpec((None, bb, kb * dx), lambda i: (layer, i, 0)), state]
        + [_resident(a.shape) for a in params],
        out_specs=[own(dc), own(ds), pl.BlockSpec((bb, ka * dc), lambda i: (i, 0)),
                   pl.BlockSpec((bb, kb * dx), lambda i: (i, 0)), state],
        out_shape=[jax.ShapeDtypeStruct((steps, bs, dc), BF16), jax.ShapeDtypeStruct((steps, bs, ds), BF16),
                   jax.ShapeDtypeStruct((bs, ka * dc), F32), jax.ShapeDtypeStruct((bs, kb * dx), F32),
                   jax.ShapeDtypeStruct(prev_ss.shape, F32)],
        input_output_aliases={0: 4},
        compiler_params=_cparams("arbitrary"),
        name="ssd_sample",
    )(prev_ss, gb, u, xbc, z, dtr, sa, sb, ss, *params)


def _route(logits_t, n_exp, cnt_ref):
    ep, toks = logits_t.shape
    sub = lax.broadcasted_iota(jnp.int32, logits_t.shape, 0).astype(F32)
    valid = sub < n_exp
    logits_t = jnp.where(valid, logits_t, -jnp.inf)
    e = jnp.exp(logits_t - jnp.max(logits_t, axis=0, keepdims=True))
    prob = jnp.where(valid, e / jnp.sum(e, axis=0, keepdims=True), -1.0)
    big = float(LANES)
    m1 = jnp.max(prob, axis=0, keepdims=True)
    i1 = jnp.min(jnp.where(prob == m1, sub, big), axis=0, keepdims=True)
    rest = jnp.where(sub == i1, -1.0, prob)
    m2 = jnp.max(rest, axis=0, keepdims=True)
    i2 = jnp.min(jnp.where(rest == m2, sub, big), axis=0, keepdims=True)
    den = m1 + m2
    chosen = jnp.where((sub == i1) | (sub == i2), 1.0, 0.0)
    earlier = lax.broadcasted_iota(jnp.int32, (toks, toks), 0) < lax.broadcasted_iota(jnp.int32, (toks, toks), 1)
    rank = _dot(chosen.astype(BF16), jnp.where(earlier, 1.0, 0.0).astype(BF16)) + cnt_ref[:, 0:1]
    r1 = jnp.sum(jnp.where(sub == i1, rank, 0.0), axis=0, keepdims=True)
    r2 = jnp.sum(jnp.where(sub == i2, rank, 0.0), axis=0, keepdims=True)
    cnt_ref[...] = cnt_ref[...] + jnp.sum(chosen, axis=1, keepdims=True)
    rows = [i1, i2, m1 / den, m2 / den, r1, r2]
    rows += [jnp.zeros((LANES - len(rows), toks), F32)]
    return jnp.concatenate(rows, axis=0).T


def _outproj_kernel(yap_ref, yas_ref, ybp_ref, ybs_ref, wch_ref, *rest, geo, n_exp, kc, split_x):
    if split_x:
        xp_ref, xs_ref, *rest = rest
    else:
        xp_ref, *rest = rest
    (qg_ref, qsh_ref, qsc_ref, rg_ref, rsh_ref, rsc_ref, npost_ref, npre_ref, r_ref,
     x1_ref, h2_ref, sel_ref, cnt_ref, wbf) = rest
    i = pl.program_id(0)

    @pl.when(i < geo.pro)
    def _():
        wbf[pl.ds(pl.multiple_of(i * kc, kc), kc), :] = wch_ref[...].astype(BF16)
        cnt_ref[...] = jnp.zeros(cnt_ref.shape, F32)

    @pl.when(i >= geo.pro)
    def _():
        is_s = i - geo.pro == geo.nt - 1
        da = yap_ref.shape[1]
        nh = 2 if geo.reps % 2 == 0 else 1
        th = geo.tm // nh
        for hs in range(nh):
            rs = slice(hs * th, (hs + 1) * th)
            mod = lambda q_ref, r_ref: _mod(is_s, q_ref, r_ref, geo.reps // nh)
            mix = (_dot(jnp.where(is_s, yas_ref[rs, :], yap_ref[rs, :]), wbf[0:da, :])
                   + _dot(jnp.where(is_s, ybs_ref[rs, :], ybp_ref[rs, :]), wbf[da:, :]))
            x = jnp.where(is_s, xs_ref[rs, :], xp_ref[rs, :]) if split_x else xp_ref[rs, :]
            x1 = x + mod(qg_ref, rg_ref) * _rms(mix, npost_ref[...])
            x1_ref[rs, :] = x1
            h2 = _rms(x1, npre_ref[...]) * (1.0 + mod(qsc_ref, rsc_ref)) + mod(qsh_ref, rsh_ref)
            h2_ref[rs, :] = h2
            h_hi = h2.astype(BF16)
            h_lo = (h2 - h_hi.astype(F32)).astype(BF16)
            r = r_ref[...]
            ep = r.shape[0]
            r_hi = r.astype(BF16)
            r_lo = (r - r_hi.astype(F32)).astype(BF16)
            both = _dot_nt(jnp.concatenate([r_hi, r_lo], axis=0), h_hi)
            logits_t = both[0:ep, :] + (_dot_nt(r_hi, h_lo) + both[ep:, :])
            sel_ref[rs, :] = _route(logits_t, n_exp, cnt_ref)


def _outproj(ya_p, ya_s, yb_p, yb_s, w_out, layer, x_p, x_s, seq_mod, row_mod, geo_args, npost, npre, router,
             n_exp):
    tm, nt, tps, reps, d = geo_args
    da, db = ya_p.shape[1], yb_p.shape[1]
    dm = da + db
    kc = _chunk(dm) // 2
    npro = dm // kc
    geo = _Rows(tm, nt, tps, reps, d, npro)
    t_all = tm * nt
    nrow = row_mod.shape[0]
    vec = pl.BlockSpec((1, d), lambda i: (0, 0))
    split_x = x_s is not None
    in_specs = [geo.prompt_rows(da), geo.sample_rows(da), geo.prompt_rows(db), geo.sample_rows(db),
                pl.BlockSpec((None, kc, d), lambda i: (layer, jnp.minimum(i, npro - 1), 0))]
    in_specs += [geo.prompt_rows(d), geo.sample_rows(d)] if split_x else [geo.rows(d)]
    in_specs += [geo.seq_mod(2), geo.seq_mod(3), geo.seq_mod(4),
                 geo.row_mod(2, nrow), geo.row_mod(3, nrow), geo.row_mod(4, nrow), vec, vec]
    args = [ya_p, ya_s, yb_p, yb_s, w_out] + ([x_p, x_s] if split_x else [x_p])
    in_specs.append(_resident(router.shape))
    args += [seq_mod, seq_mod, seq_mod, row_mod, row_mod, row_mod, npost.reshape(1, d), npre.reshape(1, d), router]
    cnt_shape = (router.shape[0], LANES)
    out_specs = [geo.rows(d), geo.rows(d), geo.rows(LANES), pl.BlockSpec(cnt_shape, lambda i: (0, 0))]
    out_shape = [jax.ShapeDtypeStruct((t_all, d), F32), jax.ShapeDtypeStruct((t_all, d), F32),
                 jax.ShapeDtypeStruct((t_all, LANES), F32), jax.ShapeDtypeStruct(cnt_shape, F32)]
    return pl.pallas_call(
        functools.partial(_outproj_kernel, geo=geo, n_exp=n_exp, kc=kc, split_x=split_x),
        grid=(npro + nt,),
        in_specs=in_specs,
        out_specs=out_specs,
        out_shape=out_shape,
        scratch_shapes=[pltpu.VMEM((dm, d), BF16)],
        compiler_params=_cparams("arbitrary"),
        name="outproj",
    )(*args)


def _write_split(is_s, val, outp_ref, outs_ref):
    @pl.when(jnp.logical_not(is_s))
    def _():
        outp_ref[...] = val

    @pl.when(is_s)
    def _():
        outs_ref[...] = val


def _out_rows(geo, d, split_out):
    t_all = geo.tm * geo.nt
    if split_out:
        return ([geo.prompt_rows(d), geo.sample_rows(d)],
                [jax.ShapeDtypeStruct((t_all - geo.tm, d), F32), jax.ShapeDtypeStruct((geo.tm, d), F32)])
    return [geo.rows(d)], [jax.ShapeDtypeStruct((t_all, d), F32)]


def _mix_ffn_kernel(yap_ref, yas_ref, ybp_ref, ybs_ref, wo_ref, wg_ref, wu_ref, wd_ref, *rest, geo, kc, cf, n_out,
                    n_ffn, split_x, split_out):
    if split_x:
        xp_ref, xs_ref, *rest = rest
    else:
        xp_ref, *rest = rest
    (qg1_ref, qsh_ref, qsc_ref, qg2_ref, rg1_ref, rsh_ref, rsc_ref, rg2_ref, npost_ref, npre_ref, nffn_ref,
     *rest) = rest
    outs, (wob, wgb, wub, wdb, act) = rest[:-5], rest[-5:]
    i = pl.program_id(0)

    @pl.when(i < n_out)
    def _():
        wob[pl.ds(pl.multiple_of(i * kc, kc), kc), :] = wo_ref[...].astype(BF16)

    @pl.when(i < n_ffn)
    def _():
        wgb[i] = wg_ref[...].astype(BF16)
        wub[i] = wu_ref[...].astype(BF16)
        wdb[pl.ds(pl.multiple_of(i * cf, cf), cf), :] = wd_ref[...].astype(BF16)

    @pl.when(i >= geo.pro)
    def _():
        is_s = i - geo.pro == geo.nt - 1
        da = yap_ref.shape[1]
        mod = lambda q_ref, r_ref: _mod(is_s, q_ref, r_ref, geo.reps)
        mix = (_dot(jnp.where(is_s, yas_ref[...], yap_ref[...]), wob[0:da, :])
               + _dot(jnp.where(is_s, ybs_ref[...], ybp_ref[...]), wob[da:, :]))
        x = jnp.where(is_s, xs_ref[...], xp_ref[...]) if split_x else xp_ref[...]
        x1 = x + mod(qg1_ref, rg1_ref) * _rms(mix, npost_ref[...])
        h = (_rms(x1, npre_ref[...]) * (1.0 + mod(qsc_ref, rsc_ref)) + mod(qsh_ref, rsh_ref)).astype(BF16)
        for c in range(n_ffn):
            act[:, c * cf:(c + 1) * cf] = (_silu(_dot(h, wgb[c])) * _dot(h, wub[c])).astype(BF16)
        f = _dot(act[...], wdb[...])
        val = x1 + mod(qg2_ref, rg2_ref) * _rms(f, nffn_ref[...])
        if split_out:
            _write_split(is_s, val, *outs)
        else:
            outs[0][...] = val


def _mix_ffn(ya_p, ya_s, yb_p, yb_s, w_out, layer, wg, wu, wd, j, x_p, x_s, seq_mod, row_mod, geo_args, npost, npre,
             nffn, split_out):
    tm, nt, tps, reps, d = geo_args
    da, db = ya_p.shape[1], yb_p.shape[1]
    dm = da + db
    f = wg.shape[2]
    kc = _chunk(dm) // 2
    cf = 256 if f % 256 == 0 else LANES
    n_out, n_ffn = dm // kc, f // cf
    geo = _Rows(tm, nt, tps, reps, d, max(n_out, n_ffn))
    nrow = row_mod.shape[0]
    vec = pl.BlockSpec((1, d), lambda i: (0, 0))
    split_x = x_s is not None
    out_specs, out_shape = _out_rows(geo, d, split_out)
    oc = lambda i: jnp.minimum(i, n_out - 1)
    fc = lambda i: jnp.minimum(i, n_ffn - 1)
    in_specs = [geo.prompt_rows(da), geo.sample_rows(da), geo.prompt_rows(db), geo.sample_rows(db),
                pl.BlockSpec((None, kc, d), lambda i: (layer, oc(i), 0)),
                pl.BlockSpec((None, d, cf), lambda i: (j, 0, fc(i))),
                pl.BlockSpec((None, d, cf), lambda i: (j, 0, fc(i))),
                pl.BlockSpec((None, cf, d), lambda i: (j, fc(i), 0))]
    in_specs += [geo.prompt_rows(d), geo.sample_rows(d)] if split_x else [geo.rows(d)]
    in_specs += [geo.seq_mod(k) for k in (2, 3, 4, 5)] + [geo.row_mod(k, nrow) for k in (2, 3, 4, 5)] + [vec] * 3
    args = [ya_p, ya_s, yb_p, yb_s, w_out, wg, wu, wd] + ([x_p, x_s] if split_x else [x_p])
    args += [seq_mod] * 4 + [row_mod] * 4 + [npost.reshape(1, d), npre.reshape(1, d), nffn.reshape(1, d)]
    return pl.pallas_call(
        functools.partial(_mix_ffn_kernel, geo=geo, kc=kc, cf=cf, n_out=n_out, n_ffn=n_ffn, split_x=split_x,
                          split_out=split_out),
        grid=(geo.pro + nt,),
        in_specs=in_specs,
        out_specs=out_specs,
        out_shape=out_shape,
        scratch_shapes=[pltpu.VMEM((dm, d), BF16), pltpu.VMEM((n_ffn, d, cf), BF16), pltpu.VMEM((n_ffn, d, cf), BF16),
                        pltpu.VMEM((f, d), BF16), pltpu.VMEM((tm, f), BF16)],
        compiler_params=_cparams("arbitrary"),
        name="mix_ffn",
    )(*args)


def _row_copy(src, src_row, dst, dst_row, sem):
    return pltpu.make_async_copy(src.at[pl.ds(src_row, 1)], dst.at[pl.ds(dst_row, 1)], sem)


def _moe_dispatch_kernel(pad_start_ref, pad_len_ref, nvalid_ref, pos_ref, h_ref, wg_ref, wu_ref, wd_ref,
                         xs_hbm, wgb_ref, wub_ref, wdb_ref, zbuf, hbuf, sems, *, tm, n_exp, nt):
    i = pl.program_id(0)
    ts = zbuf.shape[0]
    wgb_ref[...] = wg_ref[...].astype(BF16)
    wub_ref[...] = wu_ref[...].astype(BF16)
    wdb_ref[...] = wd_ref[...].astype(BF16)

    @pl.when(i == 0)
    def _():
        zbuf[...] = jnp.zeros(zbuf.shape, F32)
        for e in range(n_exp):
            def start(k, c, e=e):
                _row_copy(zbuf, 0, xs_hbm, pad_start_ref[e] + k, sems.at[2]).start()
                return c

            def wait(k, c, e=e):
                _row_copy(zbuf, 0, xs_hbm, pad_start_ref[e] + k, sems.at[2]).wait()
                return c

            lax.fori_loop(0, pad_len_ref[e], start, 0)
            lax.fori_loop(0, pad_len_ref[e], wait, 0)

        def tile_copy(j):
            return pltpu.make_async_copy(zbuf, xs_hbm.at[pl.ds(pl.multiple_of(j * ts, ts), ts)], sems.at[2])

        def start_tile(j, c):
            tile_copy(j).start()
            return c

        def wait_tile(j, c):
            tile_copy(j).wait()
            return c

        lax.fori_loop(nvalid_ref[0], xs_hbm.shape[0] // ts, start_tile, 0)
        lax.fori_loop(nvalid_ref[0], xs_hbm.shape[0] // ts, wait_tile, 0)

    def wait_slot(s):
        for _ in range(TOP_K):
            pltpu.make_async_copy(hbuf.at[s], xs_hbm.at[pl.ds(0, tm)], sems.at[s]).wait()

    for s in range(2):
        @pl.when(i % 2 == s)
        def _(s=s):
            @pl.when(i >= 2)
            def _():
                wait_slot(s)

            hbuf[s] = h_ref[...]
            for r in range(tm):
                _row_copy(hbuf.at[s], r, xs_hbm, pos_ref[0, r], sems.at[s]).start(priority=0)
                _row_copy(hbuf.at[s], r, xs_hbm, pos_ref[0, tm + r], sems.at[s]).start(priority=1)

    @pl.when(i == nt - 1)
    def _():
        wait_slot((nt - 1) % 2)
        if nt > 1:
            wait_slot(nt % 2)


def _moe_dispatch(h2, pos_tiles, pad_start, pad_len, n_valid, n_slots, tm, tm_slot, wg, wu, wd):
    t, d = h2.shape
    nt = t // tm
    n_exp = pad_start.shape[0]
    pack = 2 * SUBLANES
    ncast = next(k for k in range(nt, 0, -1)
                 if wg.shape[0] % (k * pack) == 0 and wd.shape[0] % (k * pack) == 0)
    wrows = lambda a: pl.BlockSpec((a.shape[0] // ncast, a.shape[1]), lambda i, *_: (jnp.minimum(i, ncast - 1), 0))
    grid_spec = pltpu.PrefetchScalarGridSpec(
        num_scalar_prefetch=3,
        grid=(nt,),
        in_specs=[pl.BlockSpec((None, 1, TOP_K * tm), lambda i, *_: (i, 0, 0), memory_space=pltpu.SMEM),
                  pl.BlockSpec((tm, d), lambda i, *_: (i, 0)), wrows(wg), wrows(wu), wrows(wd)],
        out_specs=[pl.BlockSpec(memory_space=pl.ANY), wrows(wg), wrows(wu), wrows(wd)],
        scratch_shapes=[pltpu.VMEM((tm_slot, d), F32), pltpu.VMEM((2, tm, d), F32), pltpu.SemaphoreType.DMA((3,))],
    )
    return pl.pallas_call(
        functools.partial(_moe_dispatch_kernel, tm=tm, n_exp=n_exp, nt=nt),
        grid_spec=grid_spec,
        out_shape=[jax.ShapeDtypeStruct((n_slots, d), F32)] + [jax.ShapeDtypeStruct(a.shape, BF16) for a in (wg, wu, wd)],
        compiler_params=_cparams("arbitrary"),
        name="moe_dispatch",
    )(pad_start, pad_len, n_valid, pos_tiles.reshape(nt, 1, TOP_K * tm), h2, wg, wu, wd)


def _moe_expert_kernel(texp_ref, nvalid_ref, x_ref, wg_ref, wu_ref, wd_ref, o_ref):
    i = pl.program_id(0)

    @pl.when(i < nvalid_ref[0])
    def _():
        h = x_ref[...].astype(BF16)
        act = (_silu(_dot(h, wg_ref[...])) * _dot(h, wu_ref[...])).astype(BF16)
        o_ref[...] = _dot(act, wd_ref[...])

    @pl.when(i >= nvalid_ref[0])
    def _():
        o_ref[...] = jnp.zeros(o_ref.shape, F32)


def _moe_experts(x_sorted, tile_expert, n_valid, wg, wu, wd, tm):
    n_tiles = tile_expert.shape[0]
    d = x_sorted.shape[1]
    n_exp, _, fe = wg.shape
    grid_spec = pltpu.PrefetchScalarGridSpec(
        num_scalar_prefetch=2,
        grid=(n_tiles,),
        in_specs=[
            pl.BlockSpec((tm, d), lambda i, te, nv: (jnp.minimum(i, nv[0] - 1), 0)),
            pl.BlockSpec((None, d, fe), lambda i, te, nv: (te[i], 0, 0)),
            pl.BlockSpec((None, d, fe), lambda i, te, nv: (te[i], 0, 0)),
            pl.BlockSpec((None, fe, d), lambda i, te, nv: (te[i], 0, 0)),
        ],
        out_specs=pl.BlockSpec((tm, d), lambda i, te, nv: (i, 0)),
    )
    return pl.pallas_call(
        _moe_expert_kernel,
        grid_spec=grid_spec,
        out_shape=jax.ShapeDtypeStruct((n_tiles * tm, d), F32),
        compiler_params=_cparams("arbitrary"),
        name="moe_experts",
    )(tile_expert, n_valid, x_sorted, wg, wu, wd)


def _start_row_gather(idx_ref, src_hbm, dst, sem, count):
    for r in range(count):
        _row_copy(src_hbm, idx_ref[0, r], dst, r, sem).start(priority=r % 2)


def _wait_row_gather(src_hbm, dst, sem):
    pltpu.make_async_copy(src_hbm.at[pl.ds(0, dst.shape[0])], dst, sem).wait()


def _moe_combine_kernel(pos0_ref, posnext_ref, ye_hbm, sel_ref, x1_ref, qg_ref, rg_ref, npost_ref, *rest,
                        geo, split_out):
    outs, (ybuf, sems) = rest[:-2], rest[-2:]
    i = pl.program_id(0)
    tm = geo.tm
    slot = i % 2

    @pl.when(i == 0)
    def _():
        _start_row_gather(pos0_ref, ye_hbm, ybuf.at[0], sems.at[0], 2 * tm)

    @pl.when(i + 1 < geo.nt)
    def _():
        _start_row_gather(posnext_ref, ye_hbm, ybuf.at[1 - slot], sems.at[1 - slot], 2 * tm)

    _wait_row_gather(ye_hbm, ybuf.at[slot], sems.at[slot])
    is_s = i == geo.nt - 1
    sel = sel_ref[...]
    lane = lax.broadcasted_iota(jnp.int32, sel.shape, 1)
    w1 = jnp.sum(jnp.where(lane == 2, sel, 0.0), axis=-1, keepdims=True)
    w2 = jnp.sum(jnp.where(lane == 3, sel, 0.0), axis=-1, keepdims=True)
    f = w1 * ybuf[slot, 0:tm, :] + w2 * ybuf[slot, tm:2 * tm, :]
    val = x1_ref[...] + _mod(is_s, qg_ref, rg_ref, geo.reps) * _rms(f, npost_ref[...])
    if split_out:
        _write_split(is_s, val, *outs)
    else:
        outs[0][...] = val


def _moe_combine(ye, pos, sel, x1, seq_mod, row_mod, geo_args, npost, split_out):
    tm, nt, tps, reps, d = geo_args
    geo = _Rows(tm, nt, tps, reps, d, 0)
    out_specs, out_shape = _out_rows(geo, d, split_out)
    pos3 = pos.reshape(nt, 1, 2 * tm)
    smem_blk = lambda fn: pl.BlockSpec((None, 1, 2 * tm), fn, memory_space=pltpu.SMEM)
    return pl.pallas_call(
        functools.partial(_moe_combine_kernel, geo=geo, split_out=split_out),
        grid=(nt,),
        in_specs=[smem_blk(lambda i: (0, 0, 0)),
                  smem_blk(lambda i: (jnp.minimum(i + 1, nt - 1), 0, 0)),
                  pl.BlockSpec(memory_space=pl.ANY),
                  geo.rows(LANES), geo.rows(d), geo.seq_mod(5), geo.row_mod(5, row_mod.shape[0]),
                  pl.BlockSpec((1, d), lambda i: (0, 0))],
        out_specs=out_specs,
        out_shape=out_shape,
        scratch_shapes=[pltpu.VMEM((2, 2 * tm, d), F32), pltpu.SemaphoreType.DMA((2,))],
        compiler_params=_cparams("arbitrary"),
        name="moe_combine",
    )(pos3, pos3, ye, sel, x1, seq_mod, row_mod, npost.reshape(1, d))


def _moe_tables(sel, counts, n_exp, tm_tok, tm_slot):
    t = sel.shape[0]
    counts = counts[:n_exp, 0].astype(jnp.int32)
    tiles_e = (counts + tm_slot - 1) // tm_slot
    tile_end = jnp.cumsum(tiles_e)
    slot_start = (tile_end - tiles_e) * tm_slot
    choice = sel[:, 0:TOP_K].astype(jnp.int32)
    rank = sel[:, 2 * TOP_K:3 * TOP_K].astype(jnp.int32)
    onehot = choice[:, :, None] == jnp.arange(n_exp, dtype=jnp.int32)[None, None, :]
    pos = jnp.sum(jnp.where(onehot, slot_start[None, None, :], 0), axis=-1) + rank
    n_tiles = (TOP_K * t + n_exp * (tm_slot - 1)) // tm_slot
    tile_expert = jnp.minimum(
        jnp.sum((jnp.arange(n_tiles, dtype=jnp.int32)[:, None] >= tile_end[None, :]).astype(jnp.int32), axis=1),
        n_exp - 1)
    pos_tiles = jnp.concatenate([pos[:, k].reshape(-1, tm_tok) for k in range(TOP_K)], axis=1)
    pad_start = slot_start + counts
    pad_len = tiles_e * tm_slot - counts
    return tile_expert, tile_end[-1:].astype(jnp.int32), pos_tiles, pad_start, pad_len, n_tiles * tm_slot


def _largest_tile(t, want):
    tm = min(want, t)
    while t % tm:
        tm //= 2
    return tm


def kernel(x_prompt, x_sample, c_prompt, c_sample, state_conva, state_convb, state_ssm, ada_w, ada_b, norm_pre_mix,
           norm_post_mix, norm_pre_ffn, norm_post_ffn, w_in, w_out, conva_w, convb_w, convb_b, dt_bias, a_log, d_skip,
           ssd_norm, ffd_w_gate, ffd_w_up, ffd_w_down, moe_router, moe_w_gate, moe_w_up, moe_w_down):
    bp, seq, d = x_prompt.shape
    bs, steps, _ = x_sample.shape
    depth = w_in.shape[0]
    dc = conva_w.shape[-1]
    dx = convb_w.shape[-1]
    ds = ssd_norm.shape[-1]
    heads = dt_bias.shape[-1]
    p = ds // heads
    n = (dx - ds) // (2 * SSD_GROUPS)
    hg = heads // SSD_GROUPS
    n_exp = moe_router.shape[-1]
    ka, kb = conva_w.shape[1] - 1, convb_w.shape[1] - 1
    tm = steps * bs
    t_p = bp * seq
    assert seq % SSD_CHUNK == 0 and LANES % p == 0 and (hg * p) % LANES == 0 and n == LANES
    assert heads <= LANES and n_exp <= LANES and dc % LANES == 0 and dx % LANES == 0 and d % LANES == 0
    assert w_in.shape[-1] == 3 * dc + ds + dx + heads and seq % tm == 0 and tm % SUBLANES == 0
    assert bs % SUBLANES == 0 and (t_p // bs) % steps == 0
    nt = t_p // tm + 1
    geo_args = (tm, nt, seq // tm, steps, d)

    w_in_t = jnp.swapaxes(w_in, 1, 2)
    w_dt = jnp.pad(w_in_t[:, 3 * dc + ds + dx:, :], ((0, 0), (0, LANES - heads), (0, 0))).astype(BF16)
    router_t = jnp.pad(jnp.swapaxes(moe_router, 1, 2), ((0, 0), (0, -n_exp % SUBLANES), (0, 0)))
    padh = lambda a: jnp.pad(a, ((0, 0), (0, LANES - heads))).reshape(depth, 1, LANES)
    dtb_p, alog_p = padh(dt_bias), padh(a_log)
    dsk_e = jnp.repeat(d_skip, p, axis=-1).reshape(depth, 1, ds)
    hot = (jnp.arange(LANES)[:, None] == (jnp.arange(ds)[None, :] // p)).astype(BF16)
    emat = jnp.concatenate([hot, hot], axis=0)

    mod = _adaln(jnp.concatenate([c_prompt, c_sample], axis=0), ada_w, ada_b)
    seq_mod = jnp.pad(mod[:, :bp], ((0, 0), (0, 1), (0, 0))).reshape(depth, bp + 1, 1, 6 * d)
    row_mod = mod[:, bp:]

    bb = _largest_tile(bs, 16)
    x_all = None
    xs_tm = x_sample.transpose(1, 0, 2).reshape(tm, d)
    xp2d = x_prompt.reshape(t_p, d)
    sa_all = state_conva.reshape(depth, bs, ka * dc)
    sb_all = state_convb.reshape(depth, bs, kb * dx)
    ss_all = state_ssm.reshape(depth, bs, ds, n)
    pa, pb, ps, sa_l, sb_l = [], [], [], [], []
    ss_new = None
    for i in range(depth):
        j = i // 2
        last = i == depth - 1
        x_in = (xp2d, xs_tm) if x_all is None else (x_all, None)
        conv_params = [conva_w[i], convb_w[i], convb_b[i].reshape(1, dx)]
        ya, z, xc, dtr, na, nb, gb_s, u_s, xbc_s, *zeros = _inproj(
            *x_in, seq_mod[i], row_mod[i], geo_args, norm_pre_mix[i], w_in_t, i, w_dt[i], *conv_params, bp, ds,
            zero_rows=depth * bs * ds * n // LANES if ss_new is None else 0)
        if zeros:
            ss_new = zeros[0].reshape(depth, bs, ds, n)
        params = [dtb_p[i], alog_p[i], dsk_e[i], ssd_norm[i].reshape(1, ds), emat]
        yb_p, ns = _ssd_prompt(z, xc, dtr, bp, seq, params, n, p, hg)
        r3 = lambda a: a.reshape(a.shape[0] // bs, bs, a.shape[-1])
        ya_s, yb_s, sna, snb, ss_new = _ssd_sample(
            r3(gb_s), r3(u_s), r3(xbc_s), r3(z), r3(dtr), t_p // (bs * steps), steps, bs, i, sa_all, sb_all, ss_all,
            ss_new, conv_params + params, n, p, hg, bb)
        pa.append(na)
        pb.append(nb)
        ps.append(ns.reshape(bp, heads, p, n))
        sa_l.append(sna.reshape(bs, ka, dc))
        sb_l.append(snb.reshape(bs, kb, dx))

        if i % 2 == 1:
            x1, h2, sel, counts = _outproj(ya, ya_s.reshape(tm, dc), yb_p, yb_s.reshape(tm, ds), w_out, i, *x_in,
                                           seq_mod[i], row_mod[i], geo_args, norm_post_mix[i], norm_pre_ffn[i],
                                           router_t[j], n_exp)
            tile_expert, n_valid, pos_tiles, pad_start, pad_len, n_slots = _moe_tables(sel, counts, n_exp, tm, MOE_TILE)
            fe = moe_w_gate.shape[-1]
            x_sorted, wg_b, wu_b, wd_b = _moe_dispatch(
                h2, pos_tiles, pad_start, pad_len, n_valid, n_slots, tm, MOE_TILE, moe_w_gate[j].reshape(n_exp * d, fe),
                moe_w_up[j].reshape(n_exp * d, fe), moe_w_down[j].reshape(n_exp * fe, d))
            ye = _moe_experts(x_sorted, tile_expert, n_valid, wg_b.reshape(n_exp, d, fe), wu_b.reshape(n_exp, d, fe),
                              wd_b.reshape(n_exp, fe, d), MOE_TILE)
            out = _moe_combine(ye, pos_tiles, sel, x1, seq_mod[i], row_mod[i], geo_args, norm_post_ffn[i], last)
        else:
            out = _mix_ffn(ya, ya_s.reshape(tm, dc), yb_p, yb_s.reshape(tm, ds), w_out, i, ffd_w_gate, ffd_w_up,
                           ffd_w_down, j, *x_in, seq_mod[i], row_mod[i], geo_args, norm_post_mix[i], norm_pre_ffn[i],
                           norm_post_ffn[i], last)
        if last:
            y_p, y_s = out
        else:
            x_all = out[0]

    y_prompt = y_p.reshape(bp, seq, d)
    y_sample = y_s.reshape(steps, bs, d).transpose(1, 0, 2)
    return (y_prompt, y_sample, jnp.stack(pa), jnp.stack(pb), jnp.stack(ps),
            jnp.stack(sa_l), jnp.stack(sb_l), ss_new.reshape(depth, bs, heads, p, n))
```

```python
import functools

import jax
import jax.numpy as jnp
from jax import lax
from jax.experimental import pallas as pl
from jax.experimental.pallas import tpu as pltpu

EPS = 1e-6
SSD_GROUPS = 2
SSD_CHUNK = 128
TOP_K = 2
LOG2E = 1.4426950408889634
LANES = 128
SUBLANES = 8
VMEM_LIMIT_BYTES = 56 * 1024 * 1024
MOE_TILE = 512

F32 = jnp.float32
BF16 = jnp.bfloat16


def _cparams(*sem):
    return pltpu.CompilerParams(dimension_semantics=sem, vmem_limit_bytes=VMEM_LIMIT_BYTES)


def _resident(shape):
    return pl.BlockSpec(shape, lambda *_: (0,) * len(shape), pipeline_mode=pl.Buffered(1))


def _silu(x):
    return x * (1.0 / (1.0 + jnp.exp(-x)))


def _softplus(x):
    return jnp.maximum(x, 0.0) + jnp.log1p(jnp.exp(-jnp.abs(x)))


def _rms(x, g):
    return x * lax.rsqrt(jnp.mean(x * x, axis=-1, keepdims=True) + EPS) * g


def _dot(a, b):
    return jnp.dot(a, b, preferred_element_type=F32)


def _dot_nt(a, b):
    return lax.dot_general(a, b, (((1,), (1,)), ((), ())), preferred_element_type=F32)


def _dot_tn(a, b):
    return lax.dot_general(a, b, (((0,), (0,)), ((), ())), preferred_element_type=F32)


def _chunk(*widths):
    return next(c for c in (512, 256, LANES) if all(w % c == 0 for w in widths))


def _adaln_kernel(c_ref, w_ref, b_ref, o_ref):
    s = _silu(c_ref[...]).astype(BF16)
    o_ref[...] = _dot(s, w_ref[...].astype(BF16)) + b_ref[...]


def _adaln(c_all, ada_w, ada_b):
    depth, d, d6 = ada_w.shape
    rows = c_all.shape[0]
    tn = _chunk(d6 // 6) * 2
    return pl.pallas_call(
        _adaln_kernel,
        grid=(depth, d6 // tn),
        in_specs=[
            pl.BlockSpec((rows, d), lambda l, j: (0, 0)),
            pl.BlockSpec((None, d, tn), lambda l, j: (l, 0, j)),
            pl.BlockSpec((None, 1, tn), lambda l, j: (l, 0, j)),
        ],
        out_specs=pl.BlockSpec((None, rows, tn), lambda l, j: (l, 0, j)),
        out_shape=jax.ShapeDtypeStruct((depth, rows, d6), F32),
        compiler_params=_cparams("arbitrary", "arbitrary"),
        name="adaln",
    )(c_all, ada_w, ada_b.reshape(depth, 1, d6))


class _Rows:
    def __init__(self, tm, nt, tps, reps, d, pro):
        self.tm, self.nt, self.tps, self.reps, self.d, self.pro = tm, nt, tps, reps, d, pro

    def tile(self, i):
        return jnp.maximum(i - self.pro, 0)

    def rows(self, w):
        return pl.BlockSpec((self.tm, w), lambda i, *_: (self.tile(i), 0))

    def prompt_rows(self, w):
        return pl.BlockSpec((self.tm, w), lambda i, *_: (jnp.minimum(self.tile(i), self.nt - 2), 0))

    def sample_rows(self, w):
        return pl.BlockSpec((self.tm, w), lambda i, *_: (0, 0))

    def seq_mod(self, k):
        return pl.BlockSpec((None, 1, self.d), lambda i, *_: (self.tile(i) // self.tps, 0, k))

    def row_mod(self, k, rows):
        return pl.BlockSpec((rows, self.d), lambda i, *_: (0, k))


def _mod(is_sample, seq_ref, row_ref, reps):
    rowm = jnp.concatenate([row_ref[...]] * reps, axis=0)
    return seq_ref[...] + jnp.where(is_sample, rowm, 0.0)


def _causal_conv_cols(cur, c0, w_ref, tail, ext_all, slot, nst_ref, keep_state, first, emit):
    tm, cw = cur.shape
    kp = w_ref.shape[0] - 1
    wcol = lambda j: w_ref[j:j + 1, c0:c0 + cw]
    ext = ext_all.at[slot]
    ext[0:SUBLANES, :] = jnp.where(first, 0.0, tail[:, c0:c0 + cw])
    ext[SUBLANES:SUBLANES + tm, :] = cur
    acc = wcol(kp) * cur
    for j in range(kp):
        off = SUBLANES - (kp - j)
        acc = acc + wcol(j) * ext[off:off + tm, :]
    tail[:, c0:c0 + cw] = ext[tm:tm + SUBLANES, :]
    nst_ref[:, c0:c0 + cw] = jnp.where(keep_state, nst_ref[:, c0:c0 + cw], ext[SUBLANES + tm - kp:SUBLANES + tm, :])
    emit(acc)


def _inproj_kernel(*refs, geo, dc, ds, dx, cw, split_x, zero_fill):
    if split_x:
        xp_ref, xs_ref, *refs = refs
    else:
        xp_ref, *refs = refs
    (qsh_ref, qsc_ref, rsh_ref, rsc_ref, g_ref, wch_ref, wdt_ref, caw_ref, cbw_ref, cbb_ref,
     ya_ref, z_ref, xc_ref, dt_ref, ncap_ref, ncbp_ref, gbs_ref, us_ref, xbcs_ref, *rest) = refs
    wbf, ext, tail_u, tail_x = rest[-4:]
    i = pl.program_id(0)

    @pl.when(i < geo.pro)
    def _():
        wbf[i] = wch_ref[...].astype(BF16)

    @pl.when(i >= geo.pro)
    def _():
        r = i - geo.pro
        is_s = r == geo.nt - 1
        first = r % geo.tps == 0
        x = jnp.where(is_s, xs_ref[...], xp_ref[...]) if split_x else xp_ref[...]
        sc = _mod(is_s, qsc_ref, rsc_ref, geo.reps)
        sh = _mod(is_s, qsh_ref, rsh_ref, geo.reps)
        h = (_rms(x, g_ref[...]) * (1.0 + sc) + sh).astype(BF16)
        nc = dc // cw
        for a in range(nc):
            c0 = a * cw
            gate_b = _dot_nt(h, wbf[a])
            u = _dot_nt(h, wbf[nc + a]) * _dot_nt(h, wbf[2 * nc + a])
            gbs_ref[:, c0:c0 + cw] = gate_b
            us_ref[:, c0:c0 + cw] = u

            def emit_a(v, c0=c0, gate_b=gate_b):
                ya_ref[:, c0:c0 + cw] = (gate_b * v).astype(BF16)

            _causal_conv_cols(u, c0, caw_ref, tail_u, ext, a % 2, ncap_ref, is_s, first, emit_a)
        for a in range(ds // cw):
            z_ref[:, a * cw:(a + 1) * cw] = _dot_nt(h, wbf[3 * nc + a])
        for a in range(dx // cw):
            c0 = a * cw
            xbc = _dot_nt(h, wbf[3 * nc + ds // cw + a])
            xbcs_ref[:, c0:c0 + cw] = xbc

            def emit_b(v, c0=c0):
                xc_ref[:, c0:c0 + cw] = _silu(v + cbb_ref[:, c0:c0 + cw])

            _causal_conv_cols(xbc, c0, cbw_ref, tail_x, ext, (nc + a) % 2, ncbp_ref, is_s, first, emit_b)
        dt_ref[...] = _dot_nt(h, wdt_ref[...])
        if zero_fill:
            rest[0][...] = jnp.zeros(rest[0].shape, F32)


def _inproj(x_p, x_s, seq_mod, row_mod, geo_args, norm_w, w_in, layer, w_dt, caw, cbw, cbb, bp, ds, zero_rows=0):
    tm, nt, tps, reps, d = geo_args
    dc, dx = caw.shape[1], cbw.shape[1]
    ka, kb = caw.shape[0] - 1, cbw.shape[0] - 1
    cw = _chunk(dc, ds, dx)
    npro = (3 * dc + ds + dx) // cw
    geo = _Rows(tm, nt, tps, reps, d, npro)
    t_all = tm * nt
    split_x = x_s is not None
    xin = [geo.prompt_rows(d), geo.sample_rows(d)] if split_x else [geo.rows(d)]
    xargs = [x_p, x_s] if split_x else [x_p]
    nrow = row_mod.shape[0]
    seq_blk = lambda k, w: pl.BlockSpec((None, k, w), lambda i: (jnp.minimum(geo.tile(i) // tps, bp - 1), 0, 0))
    zero_specs, zero_shapes = [], []
    if zero_rows:
        zero_specs = [pl.BlockSpec((zero_rows // (nt - 1), LANES), lambda i: (jnp.minimum(geo.tile(i), nt - 2), 0))]
        zero_shapes = [jax.ShapeDtypeStruct((zero_rows, LANES), F32)]
    return pl.pallas_call(
        functools.partial(_inproj_kernel, geo=geo, dc=dc, ds=ds, dx=dx, cw=cw, split_x=split_x,
                          zero_fill=bool(zero_rows)),
        grid=(npro + nt,),
        in_specs=xin + [
            geo.seq_mod(0), geo.seq_mod(1), geo.row_mod(0, nrow), geo.row_mod(1, nrow),
            pl.BlockSpec((1, d), lambda i: (0, 0)),
            pl.BlockSpec((None, cw, d), lambda i: (layer, jnp.minimum(i, npro - 1), 0)),
            _resident((LANES, d)), _resident(caw.shape), _resident(cbw.shape), _resident(cbb.shape),
        ],
        out_specs=[geo.rows(dc), geo.rows(ds), geo.rows(dx), geo.rows(LANES), seq_blk(ka, dc), seq_blk(kb, dx),
                   geo.sample_rows(dc), geo.sample_rows(dc), geo.sample_rows(dx)] + zero_specs,
        out_shape=[
            jax.ShapeDtypeStruct((t_all, dc), BF16),
            jax.ShapeDtypeStruct((t_all, ds), F32),
            jax.ShapeDtypeStruct((t_all, dx), F32),
            jax.ShapeDtypeStruct((t_all, LANES), F32),
            jax.ShapeDtypeStruct((bp, ka, dc), F32),
            jax.ShapeDtypeStruct((bp, kb, dx), F32),
            jax.ShapeDtypeStruct((tm, dc), F32),
            jax.ShapeDtypeStruct((tm, dc), F32),
            jax.ShapeDtypeStruct((tm, dx), F32),
        ] + zero_shapes,
        scratch_shapes=[pltpu.VMEM((npro, cw, d), BF16), pltpu.VMEM((2, SUBLANES + tm, cw), F32),
                        pltpu.VMEM((SUBLANES, dc), F32), pltpu.VMEM((SUBLANES, dx), F32)],
        compiler_params=_cparams("arbitrary"),
        name="inproj",
    )(*xargs, seq_mod, seq_mod, row_mod, row_mod, norm_w.reshape(1, d), w_in, w_dt, caw, cbw, cbb)


def _split_hi_lo(v):
    hi = v.astype(BF16)
    lo = (v - hi.astype(F32)).astype(BF16)
    return jnp.concatenate([hi, lo], axis=1)


def _diag_block(cb, mask, cum, cum_t, dt_t, xs, lane, g, hg, gw, p):
    hpl = LANES // p
    parts = []
    for slab in range(gw // LANES):
        lhs, rhs = [], []
        xslab = xs[:, g * gw + slab * LANES:g * gw + (slab + 1) * LANES]
        for j in range(hpl):
            h = g * hg + slab * hpl + j
            seg = cum[:, h:h + 1] - cum_t[h:h + 1, :]
            m = cb * jnp.where(mask, jnp.exp2(seg), 0.0) * dt_t[h:h + 1, :]
            lhs.append(m.astype(BF16))
            rhs.append(jnp.where((lane >= j * p) & (lane < (j + 1) * p), xslab, 0.0).astype(BF16))
        parts.append(_dot(jnp.concatenate(lhs, axis=1), jnp.concatenate(rhs, axis=0)))
    return jnp.concatenate(parts, axis=1)


def _gated_group_norm(y, xs_g, z_g, dsk_g, snorm_g):
    y = (y + dsk_g * xs_g) * _silu(z_g)
    return (y * lax.rsqrt(jnp.mean(y * y, axis=-1, keepdims=True) + EPS) * snorm_g).astype(BF16)


def _ssd_prompt_kernel(z_ref, xc_ref, dt_ref, dtb_ref, alog_ref, dsk_ref, snorm_ref, e_ref, yb_ref, nss_ref, st,
                       *, ds, n, p, hg, sub):
    c = pl.program_id(1)
    q = SSD_CHUNK
    gw = hg * p

    @pl.when(c == 0)
    def _():
        st[...] = jnp.zeros(st.shape, F32)

    row = lax.broadcasted_iota(jnp.int32, (q, LANES), 0)
    tril = lax.broadcasted_iota(jnp.int32, (q, q), 0) >= lax.broadcasted_iota(jnp.int32, (q, q), 1)
    lane = lax.broadcasted_iota(jnp.int32, (q, LANES), 1)
    neg_a = -jnp.exp(alog_ref[...]) * LOG2E
    for k in range(sub):
        rs = slice(k * q, (k + 1) * q)
        xs = xc_ref[rs, 0:ds]
        dt = _softplus(dt_ref[rs, :] + dtb_ref[...])
        cum = dt * neg_a
        step = 1
        while step < q:
            cum = cum + jnp.where(row >= step, pltpu.roll(cum, step, axis=0), 0.0)
            step *= 2
        cum_last = cum[q - 1:q, :]
        w_exp = _dot(_split_hi_lo(dt * jnp.exp2(cum_last - cum)), e_ref[...])
        ecum_exp = _dot(_split_hi_lo(jnp.exp2(cum)), e_ref[...])
        cum_t = cum.T
        dt_t = dt.T
        for g in range(SSD_GROUPS):
            gsl = slice(g * gw, (g + 1) * gw)
            bm = xc_ref[rs, ds + g * n:ds + (g + 1) * n]
            cm = xc_ref[rs, ds + SSD_GROUPS * n + g * n:ds + SSD_GROUPS * n + (g + 1) * n].astype(BF16)
            bm_t = bm.T.astype(BF16)
            y = _diag_block(_dot(cm, bm_t), tril, cum, cum_t, dt_t, xs, lane, g, hg, gw, p)
            s_prev = st[g]
            y = y + _dot(cm, s_prev.astype(BF16)) * ecum_exp[:, gsl]
            xw = (xs[:, gsl] * w_exp[:, gsl]).astype(BF16)
            st[g] = s_prev * ecum_exp[q - 1:q, gsl] + _dot(bm_t, xw)
            yb_ref[rs, gsl] = _gated_group_norm(y, xs[:, gsl], z_ref[rs, gsl], dsk_ref[:, gsl], snorm_ref[:, gsl])

    @pl.when(c == pl.num_programs(1) - 1)
    def _():
        for g in range(SSD_GROUPS):
            for slab in range(gw // LANES):
                r0 = g * gw + slab * LANES
                nss_ref[r0:r0 + LANES, :] = st[g, :, slab * LANES:(slab + 1) * LANES].T


def _ssd_prompt(z, xc, dtr, bsz, seq, params, n, p, hg):
    ds = z.shape[1]
    dx = xc.shape[1]
    sub = next(k for k in (8, 4, 2, 1) if seq % (k * SSD_CHUNK) == 0)
    rows = sub * SSD_CHUNK
    nc = seq // rows
    row = lambda w: pl.BlockSpec((rows, w), lambda b, c: (b * nc + c, 0))
    return pl.pallas_call(
        functools.partial(_ssd_prompt_kernel, ds=ds, n=n, p=p, hg=hg, sub=sub),
        grid=(bsz, nc),
        in_specs=[row(ds), row(dx), row(LANES)] + [_resident(a.shape) for a in params],
        out_specs=[row(ds), pl.BlockSpec((None, ds, n), lambda b, c: (b, 0, 0))],
        out_shape=[jax.ShapeDtypeStruct((bsz * seq, ds), BF16), jax.ShapeDtypeStruct((bsz, ds, n), F32)],
        scratch_shapes=[pltpu.VMEM((SSD_GROUPS, n, hg * p), F32)],
        compiler_params=_cparams("arbitrary", "arbitrary"),
        name="ssd_prompt",
    )(z, xc, dtr, *params)


def _ssd_sample_kernel(gb_ref, u_ref, xbc_ref, z_ref, dt_ref, sa_ref, sb_ref, ss_ref, caw_ref, cbw_ref, cbb_ref,
                       dtb_ref, alog_ref, dsk_ref, snorm_ref, e_ref, ya_ref, yb_ref, nca_ref, ncb_ref, nss_ref,
                       *, dc, ds, dx, n, p, hg, steps, bb):
    gw = hg * p
    rows = steps * bb
    ka = caw_ref.shape[0] - 1
    kb = cbw_ref.shape[0] - 1

    def conv(cur_ref, st_ref, w_ref, kprev, width):
        hist = [st_ref[:, j * width:(j + 1) * width] for j in range(kprev)] + [cur_ref[t] for t in range(steps)]
        outs = []
        for t in range(steps):
            acc = w_ref[kprev:kprev + 1, :] * hist[t + kprev]
            for j in range(kprev):
                acc = acc + w_ref[j:j + 1, :] * hist[t + j]
            outs.append(acc)
        return outs, hist[len(hist) - kprev:]

    v, new_a = conv(u_ref, sa_ref, caw_ref, ka, dc)
    for t in range(steps):
        ya_ref[t] = (gb_ref[t] * v[t]).astype(BF16)
    for j in range(ka):
        nca_ref[:, j * dc:(j + 1) * dc] = new_a[j]
    xcs, new_b = conv(xbc_ref, sb_ref, cbw_ref, kb, dx)
    for j in range(kb):
        ncb_ref[:, j * dx:(j + 1) * dx] = new_b[j]
    xc = _silu(jnp.concatenate(xcs, axis=0) + cbb_ref[...])
    xs = xc[:, 0:ds]

    dt = _softplus(jnp.concatenate([dt_ref[t] for t in range(steps)], axis=0) + dtb_ref[...])
    da = dt * (-jnp.exp(alog_ref[...]) * LOG2E)
    cums = [da[0:bb]]
    for t in range(1, steps):
        cums.append(cums[-1] + da[t * bb:(t + 1) * bb])
    cum = jnp.concatenate(cums, axis=0)
    cum_last = jnp.concatenate([cums[-1]] * steps, axis=0)
    w_exp = _dot(_split_hi_lo(dt * jnp.exp2(cum_last - cum)), e_ref[...])
    ecum_exp = _dot(_split_hi_lo(jnp.exp2(cum)), e_ref[...])

    def pad_t(a):
        a = jnp.concatenate([a, jnp.zeros((LANES - rows, LANES), F32)], axis=0) if rows < LANES else a
        return a.T[:, 0:rows]

    cum_t = pad_t(cum)
    dt_t = pad_t(dt)

    ri = lax.broadcasted_iota(jnp.int32, (rows, rows), 0)
    ci = lax.broadcasted_iota(jnp.int32, (rows, rows), 1)
    same = ((ri % bb) == (ci % bb)) & (ri >= ci)
    lane = lax.broadcasted_iota(jnp.int32, (rows, LANES), 1)
    rowid = lax.broadcasted_iota(jnp.int32, (rows, 1), 0) % bb
    seqlane = lax.broadcasted_iota(jnp.int32, (LANES, LANES), 1)
    nslab = gw // LANES
    for g in range(SSD_GROUPS):
        gsl = slice(g * gw, (g + 1) * gw)
        bm = xc[:, ds + g * n:ds + (g + 1) * n].astype(BF16)
        cm = xc[:, ds + SSD_GROUPS * n + g * n:ds + SSD_GROUPS * n + (g + 1) * n].astype(BF16)
        y_diag = _diag_block(_dot_nt(cm, bm), same, cum, cum_t, dt_t, xs, lane, g, hg, gw, p)
        xw = (xs[:, gsl] * w_exp[:, gsl]).astype(BF16)
        dec = ecum_exp[(steps - 1) * bb:steps * bb, gsl]
        dec = jnp.concatenate([dec, jnp.zeros((LANES - bb, gw), F32)], axis=0)
        dec_t = [dec[:, s * LANES:(s + 1) * LANES].T for s in range(nslab)]

        def per_seq(b, y_off, g=g, cm=cm, bm=bm, xw=xw, dec_t=dec_t):
            r0 = g * gw
            s0 = ss_ref[b, r0:r0 + gw, :]
            y_off = jnp.where(rowid == b, _dot_nt(cm, s0.astype(BF16)), y_off)
            upd = _dot_tn(jnp.where(rowid == b, xw, jnp.zeros_like(xw)), bm)
            for s in range(nslab):
                dcol = jnp.sum(jnp.where(seqlane == b, dec_t[s], 0.0), axis=1, keepdims=True)
                nss_ref[b, r0 + s * LANES:r0 + (s + 1) * LANES, :] = (
                    s0[s * LANES:(s + 1) * LANES, :] * dcol + upd[s * LANES:(s + 1) * LANES, :])
            return y_off

        y_off = lax.fori_loop(0, bb, per_seq, jnp.zeros((rows, gw), F32), unroll=8)
        zg = jnp.concatenate([z_ref[t, :, gsl] for t in range(steps)], axis=0)
        yn = _gated_group_norm(y_diag + y_off * ecum_exp[:, gsl], xs[:, gsl], zg, dsk_ref[:, gsl], snorm_ref[:, gsl])
        for t in range(steps):
            yb_ref[t, :, gsl] = yn[t * bb:(t + 1) * bb]


def _ssd_sample_aliased_kernel(*refs, **kw):
    _ssd_sample_kernel(*refs[1:], **kw)


def _ssd_sample(gb, u, xbc, z, dtr, blk0, steps, bs, layer, sa, sb, ss, prev_ss, params, n, p, hg, bb):
    dc, ds, dx = u.shape[2], z.shape[2], xbc.shape[2]
    ka, kb = params[0].shape[0] - 1, params[1].shape[0] - 1
    own = lambda w: pl.BlockSpec((steps, bb, w), lambda i: (0, i, 0))
    tok = lambda w: pl.BlockSpec((steps, bb, w), lambda i: (blk0, i, 0))
    state = pl.BlockSpec((None, bb, ds, n), lambda i: (layer, i, 0, 0))
    return pl.pallas_call(
        functools.partial(_ssd_sample_aliased_kernel, dc=dc, ds=ds, dx=dx, n=n, p=p, hg=hg, steps=steps, bb=bb),
        grid=(bs // bb,),
        in_specs=[pl.BlockSpec(memory_space=pl.ANY), own(dc), own(dc), own(dx), tok(ds), tok(LANES),
                  pl.BlockSpec((None, bb, ka * dc), lambda i: (layer, i, 0)),
                  pl.BlockSpec((None, bb, kb * dx), lambda i: (layer, i, 0)), state]
        + [_resident(a.shape) for a in params],
        out_specs=[own(dc), own(ds), pl.BlockSpec((bb, ka * dc), lambda i: (i, 0)),
                   pl.BlockSpec((bb, kb * dx), lambda i: (i, 0)), state],
        out_shape=[jax.ShapeDtypeStruct((steps, bs, dc), BF16), jax.ShapeDtypeStruct((steps, bs, ds), BF16),
                   jax.ShapeDtypeStruct((bs, ka * dc), F32), jax.ShapeDtypeStruct((bs, kb * dx), F32),
                   jax.ShapeDtypeStruct(prev_ss.shape, F32)],
        input_output_aliases={0: 4},
        compiler_params=_cparams("arbitrary"),
        name="ssd_sample",
    )(prev_ss, gb, u, xbc, z, dtr, sa, sb, ss, *params)


def _route(logits_t, n_exp, cnt_ref):
    ep, toks = logits_t.shape
    sub = lax.broadcasted_iota(jnp.int32, logits_t.shape, 0).astype(F32)
    valid = sub < n_exp
    logits_t = jnp.where(valid, logits_t, -jnp.inf)
    e = jnp.exp(logits_t - jnp.max(logits_t, axis=0, keepdims=True))
    prob = jnp.where(valid, e / jnp.sum(e, axis=0, keepdims=True), -1.0)
    big = float(LANES)
    m1 = jnp.max(prob, axis=0, keepdims=True)
    i1 = jnp.min(jnp.where(prob == m1, sub, big), axis=0, keepdims=True)
    rest = jnp.where(sub == i1, -1.0, prob)
    m2 = jnp.max(rest, axis=0, keepdims=True)
    i2 = jnp.min(jnp.where(rest == m2, sub, big), axis=0, keepdims=True)
    den = m1 + m2
    chosen = jnp.where((sub == i1) | (sub == i2), 1.0, 0.0)
    earlier = lax.broadcasted_iota(jnp.int32, (toks, toks), 0) < lax.broadcasted_iota(jnp.int32, (toks, toks), 1)
    rank = _dot(chosen.astype(BF16), jnp.where(earlier, 1.0, 0.0).astype(BF16)) + cnt_ref[:, 0:1]
    r1 = jnp.sum(jnp.where(sub == i1, rank, 0.0), axis=0, keepdims=True)
    r2 = jnp.sum(jnp.where(sub == i2, rank, 0.0), axis=0, keepdims=True)
    cnt_ref[...] = cnt_ref[...] + jnp.sum(chosen, axis=1, keepdims=True)
    rows = [i1, i2, m1 / den, m2 / den, r1, r2]
    compact = jnp.concatenate(rows + [jnp.zeros((SUBLANES - len(rows), toks), F32)], axis=0)
    wide = jnp.concatenate([compact, jnp.zeros((LANES - SUBLANES, toks), F32)], axis=0).T
    return wide, compact


def _outproj_kernel(yap_ref, yas_ref, ybp_ref, ybs_ref, wch_ref, *rest, geo, n_exp, kc, split_x):
    if split_x:
        xp_ref, xs_ref, *rest = rest
    else:
        xp_ref, *rest = rest
    (qg_ref, qsh_ref, qsc_ref, rg_ref, rsh_ref, rsc_ref, npost_ref, npre_ref, r_ref,
     x1_ref, h2_ref, sel_ref, selt_ref, cnt_ref, wbf) = rest
    i = pl.program_id(0)

    @pl.when(i < geo.pro)
    def _():
        wbf[pl.ds(pl.multiple_of(i * kc, kc), kc), :] = wch_ref[...].astype(BF16)
        cnt_ref[...] = jnp.zeros(cnt_ref.shape, F32)

    @pl.when(i >= geo.pro)
    def _():
        is_s = i - geo.pro == geo.nt - 1
        da = yap_ref.shape[1]
        nh = 2 if geo.reps % 2 == 0 else 1
        th = geo.tm // nh
        for hs in range(nh):
            rs = slice(hs * th, (hs + 1) * th)
            mod = lambda q_ref, r_ref: _mod(is_s, q_ref, r_ref, geo.reps // nh)
            mix = (_dot(jnp.where(is_s, yas_ref[rs, :], yap_ref[rs, :]), wbf[0:da, :])
                   + _dot(jnp.where(is_s, ybs_ref[rs, :], ybp_ref[rs, :]), wbf[da:, :]))
            x = jnp.where(is_s, xs_ref[rs, :], xp_ref[rs, :]) if split_x else xp_ref[rs, :]
            x1 = x + mod(qg_ref, rg_ref) * _rms(mix, npost_ref[...])
            x1_ref[rs, :] = x1
            h2 = _rms(x1, npre_ref[...]) * (1.0 + mod(qsc_ref, rsc_ref)) + mod(qsh_ref, rsh_ref)
            h2_ref[rs, :] = h2
            h_hi = h2.astype(BF16)
            h_lo = (h2 - h_hi.astype(F32)).astype(BF16)
            r = r_ref[...]
            ep = r.shape[0]
            r_hi = r.astype(BF16)
            r_lo = (r - r_hi.astype(F32)).astype(BF16)
            both = _dot_nt(jnp.concatenate([r_hi, r_lo], axis=0), h_hi)
            logits_t = both[0:ep, :] + (_dot_nt(r_hi, h_lo) + both[ep:, :])
            sel_ref[rs, :], selt_ref[:, rs] = _route(logits_t, n_exp, cnt_ref)


def _outproj(ya_p, ya_s, yb_p, yb_s, w_out, layer, x_p, x_s, seq_mod, row_mod, geo_args, npost, npre, router,
             n_exp):
    tm, nt, tps, reps, d = geo_args
    da, db = ya_p.shape[1], yb_p.shape[1]
    dm = da + db
    kc = _chunk(dm) // 2
    npro = dm // kc
    geo = _Rows(tm, nt, tps, reps, d, npro)
    t_all = tm * nt
    nrow = row_mod.shape[0]
    vec = pl.BlockSpec((1, d), lambda i: (0, 0))
    split_x = x_s is not None
    in_specs = [geo.prompt_rows(da), geo.sample_rows(da), geo.prompt_rows(db), geo.sample_rows(db),
                pl.BlockSpec((None, kc, d), lambda i: (layer, jnp.minimum(i, npro - 1), 0))]
    in_specs += [geo.prompt_rows(d), geo.sample_rows(d)] if split_x else [geo.rows(d)]
    in_specs += [geo.seq_mod(2), geo.seq_mod(3), geo.seq_mod(4),
                 geo.row_mod(2, nrow), geo.row_mod(3, nrow), geo.row_mod(4, nrow), vec, vec]
    args = [ya_p, ya_s, yb_p, yb_s, w_out] + ([x_p, x_s] if split_x else [x_p])
    in_specs.append(_resident(router.shape))
    args += [seq_mod, seq_mod, seq_mod, row_mod, row_mod, row_mod, npost.reshape(1, d), npre.reshape(1, d), router]
    cnt_shape = (router.shape[0], LANES)
    out_specs = [geo.rows(d), geo.rows(d), geo.rows(LANES), pl.BlockSpec((SUBLANES, tm), lambda i: (0, geo.tile(i))),
                 pl.BlockSpec(cnt_shape, lambda i: (0, 0))]
    out_shape = [jax.ShapeDtypeStruct((t_all, d), F32), jax.ShapeDtypeStruct((t_all, d), F32),
                 jax.ShapeDtypeStruct((t_all, LANES), F32), jax.ShapeDtypeStruct((SUBLANES, t_all), F32),
                 jax.ShapeDtypeStruct(cnt_shape, F32)]
    return pl.pallas_call(
        functools.partial(_outproj_kernel, geo=geo, n_exp=n_exp, kc=kc, split_x=split_x),
        grid=(npro + nt,),
        in_specs=in_specs,
        out_specs=out_specs,
        out_shape=out_shape,
        scratch_shapes=[pltpu.VMEM((dm, d), BF16)],
        compiler_params=_cparams("arbitrary"),
        name="outproj",
    )(*args)


def _write_split(is_s, val, outp_ref, outs_ref):
    @pl.when(jnp.logical_not(is_s))
    def _():
        outp_ref[...] = val

    @pl.when(is_s)
    def _():
        outs_ref[...] = val


def _out_rows(geo, d, split_out):
    t_all = geo.tm * geo.nt
    if split_out:
        return ([geo.prompt_rows(d), geo.sample_rows(d)],
                [jax.ShapeDtypeStruct((t_all - geo.tm, d), F32), jax.ShapeDtypeStruct((geo.tm, d), F32)])
    return [geo.rows(d)], [jax.ShapeDtypeStruct((t_all, d), F32)]


def _mix_ffn_kernel(yap_ref, yas_ref, ybp_ref, ybs_ref, wo_ref, wg_ref, wu_ref, wd_ref, *rest, geo, kc, cf, n_out,
                    n_ffn, split_x, split_out):
    if split_x:
        xp_ref, xs_ref, *rest = rest
    else:
        xp_ref, *rest = rest
    (qg1_ref, qsh_ref, qsc_ref, qg2_ref, rg1_ref, rsh_ref, rsc_ref, rg2_ref, npost_ref, npre_ref, nffn_ref,
     *rest) = rest
    outs, (wob, wgb, wub, wdb, act) = rest[:-5], rest[-5:]
    i = pl.program_id(0)

    @pl.when(i < n_out)
    def _():
        wob[pl.ds(pl.multiple_of(i * kc, kc), kc), :] = wo_ref[...].astype(BF16)

    @pl.when(i < n_ffn)
    def _():
        wgb[i] = wg_ref[...].astype(BF16)
        wub[i] = wu_ref[...].astype(BF16)
        wdb[pl.ds(pl.multiple_of(i * cf, cf), cf), :] = wd_ref[...].astype(BF16)

    @pl.when(i >= geo.pro)
    def _():
        is_s = i - geo.pro == geo.nt - 1
        da = yap_ref.shape[1]
        mod = lambda q_ref, r_ref: _mod(is_s, q_ref, r_ref, geo.reps)
        mix = (_dot(jnp.where(is_s, yas_ref[...], yap_ref[...]), wob[0:da, :])
               + _dot(jnp.where(is_s, ybs_ref[...], ybp_ref[...]), wob[da:, :]))
        x = jnp.where(is_s, xs_ref[...], xp_ref[...]) if split_x else xp_ref[...]
        x1 = x + mod(qg1_ref, rg1_ref) * _rms(mix, npost_ref[...])
        h = (_rms(x1, npre_ref[...]) * (1.0 + mod(qsc_ref, rsc_ref)) + mod(qsh_ref, rsh_ref)).astype(BF16)
        for c in range(n_ffn):
            act[:, c * cf:(c + 1) * cf] = (_silu(_dot(h, wgb[c])) * _dot(h, wub[c])).astype(BF16)
        f = _dot(act[...], wdb[...])
        val = x1 + mod(qg2_ref, rg2_ref) * _rms(f, nffn_ref[...])
        if split_out:
            _write_split(is_s, val, *outs)
        else:
            outs[0][...] = val


def _mix_ffn(ya_p, ya_s, yb_p, yb_s, w_out, layer, wg, wu, wd, j, x_p, x_s, seq_mod, row_mod, geo_args, npost, npre,
             nffn, split_out):
    tm, nt, tps, reps, d = geo_args
    da, db = ya_p.shape[1], yb_p.shape[1]
    dm = da + db
    f = wg.shape[2]
    kc = _chunk(dm) // 2
    cf = 256 if f % 256 == 0 else LANES
    n_out, n_ffn = dm // kc, f // cf
    geo = _Rows(tm, nt, tps, reps, d, max(n_out, n_ffn))
    nrow = row_mod.shape[0]
    vec = pl.BlockSpec((1, d), lambda i: (0, 0))
    split_x = x_s is not None
    out_specs, out_shape = _out_rows(geo, d, split_out)
    oc = lambda i: jnp.minimum(i, n_out - 1)
    fc = lambda i: jnp.minimum(i, n_ffn - 1)
    in_specs = [geo.prompt_rows(da), geo.sample_rows(da), geo.prompt_rows(db), geo.sample_rows(db),
                pl.BlockSpec((None, kc, d), lambda i: (layer, oc(i), 0)),
                pl.BlockSpec((None, d, cf), lambda i: (j, 0, fc(i))),
                pl.BlockSpec((None, d, cf), lambda i: (j, 0, fc(i))),
                pl.BlockSpec((None, cf, d), lambda i: (j, fc(i), 0))]
    in_specs += [geo.prompt_rows(d), geo.sample_rows(d)] if split_x else [geo.rows(d)]
    in_specs += [geo.seq_mod(k) for k in (2, 3, 4, 5)] + [geo.row_mod(k, nrow) for k in (2, 3, 4, 5)] + [vec] * 3
    args = [ya_p, ya_s, yb_p, yb_s, w_out, wg, wu, wd] + ([x_p, x_s] if split_x else [x_p])
    args += [seq_mod] * 4 + [row_mod] * 4 + [npost.reshape(1, d), npre.reshape(1, d), nffn.reshape(1, d)]
    return pl.pallas_call(
        functools.partial(_mix_ffn_kernel, geo=geo, kc=kc, cf=cf, n_out=n_out, n_ffn=n_ffn, split_x=split_x,
                          split_out=split_out),
        grid=(geo.pro + nt,),
        in_specs=in_specs,
        out_specs=out_specs,
        out_shape=out_shape,
        scratch_shapes=[pltpu.VMEM((dm, d), BF16), pltpu.VMEM((n_ffn, d, cf), BF16), pltpu.VMEM((n_ffn, d, cf), BF16),
                        pltpu.VMEM((f, d), BF16), pltpu.VMEM((tm, f), BF16)],
        compiler_params=_cparams("arbitrary"),
        name="mix_ffn",
    )(*args)


def _row_copy(src, src_row, dst, dst_row, sem):
    return pltpu.make_async_copy(src.at[pl.ds(src_row, 1)], dst.at[pl.ds(dst_row, 1)], sem)


def _moe_dispatch_kernel(pad_start_ref, pad_len_ref, nvalid_ref, pos_ref, h_ref, wg_ref, wu_ref, wd_ref,
                         xs_hbm, wgb_ref, wub_ref, wdb_ref, zbuf, hbuf, sems, *, tm, n_exp, nt):
    i = pl.program_id(0)
    ts = zbuf.shape[0]
    wgb_ref[...] = wg_ref[...].astype(BF16)
    wub_ref[...] = wu_ref[...].astype(BF16)
    wdb_ref[...] = wd_ref[...].astype(BF16)

    @pl.when(i == 0)
    def _():
        zbuf[...] = jnp.zeros(zbuf.shape, F32)
        for e in range(n_exp):
            def start(k, c, e=e):
                _row_copy(zbuf, 0, xs_hbm, pad_start_ref[e] + k, sems.at[2]).start()
                return c

            def wait(k, c, e=e):
                _row_copy(zbuf, 0, xs_hbm, pad_start_ref[e] + k, sems.at[2]).wait()
                return c

            lax.fori_loop(0, pad_len_ref[e], start, 0)
            lax.fori_loop(0, pad_len_ref[e], wait, 0)

        def tile_copy(j):
            return pltpu.make_async_copy(zbuf, xs_hbm.at[pl.ds(pl.multiple_of(j * ts, ts), ts)], sems.at[2])

        def start_tile(j, c):
            tile_copy(j).start()
            return c

        def wait_tile(j, c):
            tile_copy(j).wait()
            return c

        lax.fori_loop(nvalid_ref[0], xs_hbm.shape[0] // ts, start_tile, 0)
        lax.fori_loop(nvalid_ref[0], xs_hbm.shape[0] // ts, wait_tile, 0)

    def wait_slot(s):
        for _ in range(TOP_K):
            pltpu.make_async_copy(hbuf.at[s], xs_hbm.at[pl.ds(0, tm)], sems.at[s]).wait()

    for s in range(2):
        @pl.when(i % 2 == s)
        def _(s=s):
            @pl.when(i >= 2)
            def _():
                wait_slot(s)

            hbuf[s] = h_ref[...]
            for r in range(tm):
                _row_copy(hbuf.at[s], r, xs_hbm, pos_ref[0, r], sems.at[s]).start(priority=0)
                _row_copy(hbuf.at[s], r, xs_hbm, pos_ref[0, tm + r], sems.at[s]).start(priority=1)

    @pl.when(i == nt - 1)
    def _():
        wait_slot((nt - 1) % 2)
        if nt > 1:
            wait_slot(nt % 2)


def _moe_dispatch(h2, pos_tiles, pad_start, pad_len, n_valid, n_slots, tm, tm_slot, wg, wu, wd):
    t, d = h2.shape
    nt = t // tm
    n_exp = pad_start.shape[0]
    pack = 2 * SUBLANES
    ncast = next(k for k in range(nt, 0, -1)
                 if wg.shape[0] % (k * pack) == 0 and wd.shape[0] % (k * pack) == 0)
    wrows = lambda a: pl.BlockSpec((a.shape[0] // ncast, a.shape[1]), lambda i, *_: (jnp.minimum(i, ncast - 1), 0))
    grid_spec = pltpu.PrefetchScalarGridSpec(
        num_scalar_prefetch=3,
        grid=(nt,),
        in_specs=[pl.BlockSpec((None, 1, TOP_K * tm), lambda i, *_: (i, 0, 0), memory_space=pltpu.SMEM),
                  pl.BlockSpec((tm, d), lambda i, *_: (i, 0)), wrows(wg), wrows(wu), wrows(wd)],
        out_specs=[pl.BlockSpec(memory_space=pl.ANY), wrows(wg), wrows(wu), wrows(wd)],
        scratch_shapes=[pltpu.VMEM((tm_slot, d), F32), pltpu.VMEM((2, tm, d), F32), pltpu.SemaphoreType.DMA((3,))],
    )
    return pl.pallas_call(
        functools.partial(_moe_dispatch_kernel, tm=tm, n_exp=n_exp, nt=nt),
        grid_spec=grid_spec,
        out_shape=[jax.ShapeDtypeStruct((n_slots, d), F32)] + [jax.ShapeDtypeStruct(a.shape, BF16) for a in (wg, wu, wd)],
        compiler_params=_cparams("arbitrary"),
        name="moe_dispatch",
    )(pad_start, pad_len, n_valid, pos_tiles.reshape(nt, 1, TOP_K * tm), h2, wg, wu, wd)


def _moe_expert_kernel(texp_ref, nvalid_ref, x_ref, wg_ref, wu_ref, wd_ref, o_ref):
    i = pl.program_id(0)

    @pl.when(i < nvalid_ref[0])
    def _():
        h = x_ref[...].astype(BF16)
        act = (_silu(_dot(h, wg_ref[...])) * _dot(h, wu_ref[...])).astype(BF16)
        o_ref[...] = _dot(act, wd_ref[...])

    @pl.when(i >= nvalid_ref[0])
    def _():
        o_ref[...] = jnp.zeros(o_ref.shape, F32)


def _moe_experts(x_sorted, tile_expert, n_valid, wg, wu, wd, tm):
    n_tiles = tile_expert.shape[0]
    d = x_sorted.shape[1]
    n_exp, _, fe = wg.shape
    grid_spec = pltpu.PrefetchScalarGridSpec(
        num_scalar_prefetch=2,
        grid=(n_tiles,),
        in_specs=[
            pl.BlockSpec((tm, d), lambda i, te, nv: (jnp.minimum(i, nv[0] - 1), 0)),
            pl.BlockSpec((None, d, fe), lambda i, te, nv: (te[i], 0, 0)),
            pl.BlockSpec((None, d, fe), lambda i, te, nv: (te[i], 0, 0)),
            pl.BlockSpec((None, fe, d), lambda i, te, nv: (te[i], 0, 0)),
        ],
        out_specs=pl.BlockSpec((tm, d), lambda i, te, nv: (i, 0)),
    )
    return pl.pallas_call(
        _moe_expert_kernel,
        grid_spec=grid_spec,
        out_shape=jax.ShapeDtypeStruct((n_tiles * tm, d), F32),
        compiler_params=_cparams("arbitrary"),
        name="moe_experts",
    )(tile_expert, n_valid, x_sorted, wg, wu, wd)


def _start_row_gather(idx_ref, src_hbm, dst, sem, count):
    for r in range(count):
        _row_copy(src_hbm, idx_ref[0, r], dst, r, sem).start(priority=r % 2)


def _wait_row_gather(src_hbm, dst, sem):
    pltpu.make_async_copy(src_hbm.at[pl.ds(0, dst.shape[0])], dst, sem).wait()


def _moe_combine_kernel(pos0_ref, posnext_ref, ye_hbm, sel_ref, x1_ref, qg_ref, rg_ref, npost_ref, *rest,
                        geo, split_out):
    outs, (ybuf, sems) = rest[:-2], rest[-2:]
    i = pl.program_id(0)
    tm = geo.tm
    slot = i % 2

    @pl.when(i == 0)
    def _():
        _start_row_gather(pos0_ref, ye_hbm, ybuf.at[0], sems.at[0], 2 * tm)

    @pl.when(i + 1 < geo.nt)
    def _():
        _start_row_gather(posnext_ref, ye_hbm, ybuf.at[1 - slot], sems.at[1 - slot], 2 * tm)

    _wait_row_gather(ye_hbm, ybuf.at[slot], sems.at[slot])
    is_s = i == geo.nt - 1
    sel = sel_ref[...]
    lane = lax.broadcasted_iota(jnp.int32, sel.shape, 1)
    w1 = jnp.sum(jnp.where(lane == 2, sel, 0.0), axis=-1, keepdims=True)
    w2 = jnp.sum(jnp.where(lane == 3, sel, 0.0), axis=-1, keepdims=True)
    f = w1 * ybuf[slot, 0:tm, :] + w2 * ybuf[slot, tm:2 * tm, :]
    val = x1_ref[...] + _mod(is_s, qg_ref, rg_ref, geo.reps) * _rms(f, npost_ref[...])
    if split_out:
        _write_split(is_s, val, *outs)
    else:
        outs[0][...] = val


def _moe_combine(ye, pos, sel, x1, seq_mod, row_mod, geo_args, npost, split_out):
    tm, nt, tps, reps, d = geo_args
    geo = _Rows(tm, nt, tps, reps, d, 0)
    out_specs, out_shape = _out_rows(geo, d, split_out)
    pos3 = pos.reshape(nt, 1, 2 * tm)
    smem_blk = lambda fn: pl.BlockSpec((None, 1, 2 * tm), fn, memory_space=pltpu.SMEM)
    return pl.pallas_call(
        functools.partial(_moe_combine_kernel, geo=geo, split_out=split_out),
        grid=(nt,),
        in_specs=[smem_blk(lambda i: (0, 0, 0)),
                  smem_blk(lambda i: (jnp.minimum(i + 1, nt - 1), 0, 0)),
                  pl.BlockSpec(memory_space=pl.ANY),
                  geo.rows(LANES), geo.rows(d), geo.seq_mod(5), geo.row_mod(5, row_mod.shape[0]),
                  pl.BlockSpec((1, d), lambda i: (0, 0))],
        out_specs=out_specs,
        out_shape=out_shape,
        scratch_shapes=[pltpu.VMEM((2, 2 * tm, d), F32), pltpu.SemaphoreType.DMA((2,))],
        compiler_params=_cparams("arbitrary"),
        name="moe_combine",
    )(pos3, pos3, ye, sel, x1, seq_mod, row_mod, npost.reshape(1, d))


def _moe_tables(sel_t, counts, n_exp, tm_tok, tm_slot):
    t = sel_t.shape[1]
    counts = counts[:n_exp, 0].astype(jnp.int32)
    tiles_e = (counts + tm_slot - 1) // tm_slot
    tile_end = jnp.cumsum(tiles_e)
    slot_start = (tile_end - tiles_e) * tm_slot
    choice = sel_t[0:TOP_K].astype(jnp.int32)
    rank = sel_t[2 * TOP_K:3 * TOP_K].astype(jnp.int32)
    onehot = choice[:, :, None] == jnp.arange(n_exp, dtype=jnp.int32)[None, None, :]
    pos = jnp.sum(jnp.where(onehot, slot_start[None, None, :], 0), axis=-1) + rank
    n_tiles = (TOP_K * t + n_exp * (tm_slot - 1)) // tm_slot
    tile_expert = jnp.minimum(
        jnp.sum((jnp.arange(n_tiles, dtype=jnp.int32)[:, None] >= tile_end[None, :]).astype(jnp.int32), axis=1),
        n_exp - 1)
    pos_tiles = jnp.concatenate([pos[k].reshape(-1, tm_tok) for k in range(TOP_K)], axis=1)
    pad_start = slot_start + counts
    pad_len = tiles_e * tm_slot - counts
    return tile_expert, tile_end[-1:].astype(jnp.int32), pos_tiles, pad_start, pad_len, n_tiles * tm_slot


def _largest_tile(t, want):
    tm = min(want, t)
    while t % tm:
        tm //= 2
    return tm


def kernel(x_prompt, x_sample, c_prompt, c_sample, state_conva, state_convb, state_ssm, ada_w, ada_b, norm_pre_mix,
           norm_post_mix, norm_pre_ffn, norm_post_ffn, w_in, w_out, conva_w, convb_w, convb_b, dt_bias, a_log, d_skip,
           ssd_norm, ffd_w_gate, ffd_w_up, ffd_w_down, moe_router, moe_w_gate, moe_w_up, moe_w_down):
    bp, seq, d = x_prompt.shape
    bs, steps, _ = x_sample.shape
    depth = w_in.shape[0]
    dc = conva_w.shape[-1]
    dx = convb_w.shape[-1]
    ds = ssd_norm.shape[-1]
    heads = dt_bias.shape[-1]
    p = ds // heads
    n = (dx - ds) // (2 * SSD_GROUPS)
    hg = heads // SSD_GROUPS
    n_exp = moe_router.shape[-1]
    ka, kb = conva_w.shape[1] - 1, convb_w.shape[1] - 1
    tm = steps * bs
    t_p = bp * seq
    assert seq % SSD_CHUNK == 0 and LANES % p == 0 and (hg * p) % LANES == 0 and n == LANES
    assert heads <= LANES and n_exp <= LANES and dc % LANES == 0 and dx % LANES == 0 and d % LANES == 0
    assert w_in.shape[-1] == 3 * dc + ds + dx + heads and seq % tm == 0 and tm % SUBLANES == 0
    assert bs % SUBLANES == 0 and (t_p // bs) % steps == 0
    nt = t_p // tm + 1
    geo_args = (tm, nt, seq // tm, steps, d)

    w_in_t = jnp.swapaxes(w_in, 1, 2)
    w_dt = jnp.pad(w_in_t[:, 3 * dc + ds + dx:, :], ((0, 0), (0, LANES - heads), (0, 0))).astype(BF16)
    router_t = jnp.pad(jnp.swapaxes(moe_router, 1, 2), ((0, 0), (0, -n_exp % SUBLANES), (0, 0)))
    padh = lambda a: jnp.pad(a, ((0, 0), (0, LANES - heads))).reshape(depth, 1, LANES)
    dtb_p, alog_p = padh(dt_bias), padh(a_log)
    dsk_e = jnp.repeat(d_skip, p, axis=-1).reshape(depth, 1, ds)
    hot = (jnp.arange(LANES)[:, None] == (jnp.arange(ds)[None, :] // p)).astype(BF16)
    emat = jnp.concatenate([hot, hot], axis=0)

    mod = _adaln(jnp.concatenate([c_prompt, c_sample], axis=0), ada_w, ada_b)
    seq_mod = jnp.pad(mod[:, :bp], ((0, 0), (0, 1), (0, 0))).reshape(depth, bp + 1, 1, 6 * d)
    row_mod = mod[:, bp:]

    bb = _largest_tile(bs, 16)
    x_all = None
    xs_tm = x_sample.transpose(1, 0, 2).reshape(tm, d)
    xp2d = x_prompt.reshape(t_p, d)
    sa_all = state_conva.reshape(depth, bs, ka * dc)
    sb_all = state_convb.reshape(depth, bs, kb * dx)
    ss_all = state_ssm.reshape(depth, bs, ds, n)
    pa, pb, ps, sa_l, sb_l = [], [], [], [], []
    ss_new = None
    for i in range(depth):
        j = i // 2
        last = i == depth - 1
        x_in = (xp2d, xs_tm) if x_all is None else (x_all, None)
        conv_params = [conva_w[i], convb_w[i], convb_b[i].reshape(1, dx)]
        ya, z, xc, dtr, na, nb, gb_s, u_s, xbc_s, *zeros = _inproj(
            *x_in, seq_mod[i], row_mod[i], geo_args, norm_pre_mix[i], w_in_t, i, w_dt[i], *conv_params, bp, ds,
            zero_rows=depth * bs * ds * n // LANES if ss_new is None else 0)
        if zeros:
            ss_new = zeros[0].reshape(depth, bs, ds, n)
        params = [dtb_p[i], alog_p[i], dsk_e[i], ssd_norm[i].reshape(1, ds), emat]
        yb_p, ns = _ssd_prompt(z, xc, dtr, bp, seq, params, n, p, hg)
        r3 = lambda a: a.reshape(a.shape[0] // bs, bs, a.shape[-1])
        ya_s, yb_s, sna, snb, ss_new = _ssd_sample(
            r3(gb_s), r3(u_s), r3(xbc_s), r3(z), r3(dtr), t_p // (bs * steps), steps, bs, i, sa_all, sb_all, ss_all,
            ss_new, conv_params + params, n, p, hg, bb)
        pa.append(na)
        pb.append(nb)
        ps.append(ns.reshape(bp, heads, p, n))
        sa_l.append(sna.reshape(bs, ka, dc))
        sb_l.append(snb.reshape(bs, kb, dx))

        if i % 2 == 1:
            x1, h2, sel, sel_t, counts = _outproj(ya, ya_s.reshape(tm, dc), yb_p, yb_s.reshape(tm, ds), w_out, i, *x_in,
                                           seq_mod[i], row_mod[i], geo_args, norm_post_mix[i], norm_pre_ffn[i],
                                           router_t[j], n_exp)
            tile_expert, n_valid, pos_tiles, pad_start, pad_len, n_slots = _moe_tables(sel_t, counts, n_exp, tm, MOE_TILE)
            fe = moe_w_gate.shape[-1]
            x_sorted, wg_b, wu_b, wd_b = _moe_dispatch(
                h2, pos_tiles, pad_start, pad_len, n_valid, n_slots, tm, MOE_TILE, moe_w_gate[j].reshape(n_exp * d, fe),
                moe_w_up[j].reshape(n_exp * d, fe), moe_w_down[j].reshape(n_exp * fe, d))
            ye = _moe_experts(x_sorted, tile_expert, n_valid, wg_b.reshape(n_exp, d, fe), wu_b.reshape(n_exp, d, fe),
                              wd_b.reshape(n_exp, fe, d), MOE_TILE)
            out = _moe_combine(ye, pos_tiles, sel, x1, seq_mod[i], row_mod[i], geo_args, norm_post_ffn[i], last)
        else:
            out = _mix_ffn(ya, ya_s.reshape(tm, dc), yb_p, yb_s.reshape(tm, ds), w_out, i, ffd_w_gate, ffd_w_up,
                           ffd_w_down, j, *x_in, seq_mod[i], row_mod[i], geo_args, norm_post_mix[i], norm_pre_ffn[i],
                           norm_post_ffn[i], last)
        if last:
            y_p, y_s = out
        else:
            x_all = out[0]

    y_prompt = y_p.reshape(bp, seq, d)
    y_sample = y_s.reshape(steps, bs, d).transpose(1, 0, 2)
    return (y_prompt, y_sample, jnp.stack(pa), jnp.stack(pb), jnp.stack(ps),
            jnp.stack(sa_l), jnp.stack(sb_l), ss_new.reshape(depth, bs, heads, p, n))
```

```python
import functools

import jax
import jax.numpy as jnp
from jax import lax
from jax.experimental import pallas as pl
from jax.experimental.pallas import tpu as pltpu

EPS = 1e-6
SSD_GROUPS = 2
SSD_CHUNK = 128
TOP_K = 2
LOG2E = 1.4426950408889634
LANES = 128
SUBLANES = 8
VMEM_LIMIT_BYTES = 56 * 1024 * 1024
MXU_WIDTH = 256
MOE_TILE = 2 * MXU_WIDTH
SSD_CHUNKS_PER_STEP = (8, 4, 2, 1)
SAMPLE_SEQS_PER_STEP = 16
SAMPLE_LOOP_UNROLL = 8

F32 = jnp.float32
BF16 = jnp.bfloat16


def _cparams(*sem):
    return pltpu.CompilerParams(dimension_semantics=sem, vmem_limit_bytes=VMEM_LIMIT_BYTES)


def _resident(shape):
    return pl.BlockSpec(shape, lambda *_: (0,) * len(shape), pipeline_mode=pl.Buffered(1))


def _silu(x):
    return x * (1.0 / (1.0 + jnp.exp(-x)))


def _softplus(x):
    return jnp.maximum(x, 0.0) + jnp.log1p(jnp.exp(-jnp.abs(x)))


def _rms(x, g):
    return x * lax.rsqrt(jnp.mean(x * x, axis=-1, keepdims=True) + EPS) * g


def _dot(a, b):
    return jnp.dot(a, b, preferred_element_type=F32)


def _dot_nt(a, b):
    return lax.dot_general(a, b, (((1,), (1,)), ((), ())), preferred_element_type=F32)


def _dot_tn(a, b):
    return lax.dot_general(a, b, (((0,), (0,)), ((), ())), preferred_element_type=F32)


def _chunk(*widths):
    return next(c for c in (2 * MXU_WIDTH, MXU_WIDTH, LANES) if all(w % c == 0 for w in widths))


def _adaln_kernel(c_ref, w_ref, b_ref, o_ref):
    s = _silu(c_ref[...]).astype(BF16)
    o_ref[...] = _dot(s, w_ref[...].astype(BF16)) + b_ref[...]


def _adaln(c_all, ada_w, ada_b):
    depth, d, d6 = ada_w.shape
    rows = c_all.shape[0]
    tn = _chunk(d6 // 6) * 2
    return pl.pallas_call(
        _adaln_kernel,
        grid=(depth, d6 // tn),
        in_specs=[
            pl.BlockSpec((rows, d), lambda l, j: (0, 0)),
            pl.BlockSpec((None, d, tn), lambda l, j: (l, 0, j)),
            pl.BlockSpec((None, 1, tn), lambda l, j: (l, 0, j)),
        ],
        out_specs=pl.BlockSpec((None, rows, tn), lambda l, j: (l, 0, j)),
        out_shape=jax.ShapeDtypeStruct((depth, rows, d6), F32),
        compiler_params=_cparams("arbitrary", "arbitrary"),
        name="adaln",
    )(c_all, ada_w, ada_b.reshape(depth, 1, d6))


class _Rows:
    def __init__(self, tm, nt, tps, reps, d, pro):
        self.tm, self.nt, self.tps, self.reps, self.d, self.pro = tm, nt, tps, reps, d, pro

    def tile(self, i):
        return jnp.maximum(i - self.pro, 0)

    def rows(self, w):
        return pl.BlockSpec((self.tm, w), lambda i, *_: (self.tile(i), 0))

    def prompt_rows(self, w):
        return pl.BlockSpec((self.tm, w), lambda i, *_: (jnp.minimum(self.tile(i), self.nt - 2), 0))

    def sample_rows(self, w):
        return pl.BlockSpec((self.tm, w), lambda i, *_: (0, 0))

    def seq_mod(self, k):
        return pl.BlockSpec((None, 1, self.d), lambda i, *_: (self.tile(i) // self.tps, 0, k))

    def row_mod(self, k, rows):
        return pl.BlockSpec((rows, self.d), lambda i, *_: (0, k))


def _mod(is_sample, seq_ref, row_ref, reps):
    rowm = jnp.concatenate([row_ref[...]] * reps, axis=0)
    return seq_ref[...] + jnp.where(is_sample, rowm, 0.0)


def _causal_conv_cols(cur, c0, w_ref, tail, ext_all, slot, nst_ref, keep_state, first, emit):
    tm, cw = cur.shape
    kp = w_ref.shape[0] - 1
    wcol = lambda j: w_ref[j:j + 1, c0:c0 + cw]
    ext = ext_all.at[slot]
    ext[0:SUBLANES, :] = jnp.where(first, 0.0, tail[:, c0:c0 + cw])
    ext[SUBLANES:SUBLANES + tm, :] = cur
    acc = wcol(kp) * cur
    for j in range(kp):
        off = SUBLANES - (kp - j)
        acc = acc + wcol(j) * ext[off:off + tm, :]
    tail[:, c0:c0 + cw] = ext[tm:tm + SUBLANES, :]
    nst_ref[:, c0:c0 + cw] = jnp.where(keep_state, nst_ref[:, c0:c0 + cw], ext[SUBLANES + tm - kp:SUBLANES + tm, :])
    emit(acc)


def _inproj_kernel(*refs, geo, dc, ds, dx, cw, split_x, zero_fill):
    if split_x:
        xp_ref, xs_ref, *refs = refs
    else:
        xp_ref, *refs = refs
    (qsh_ref, qsc_ref, rsh_ref, rsc_ref, g_ref, wch_ref, wdt_ref, caw_ref, cbw_ref, cbb_ref,
     ya_ref, z_ref, xc_ref, dt_ref, ncap_ref, ncbp_ref, gbs_ref, us_ref, xbcs_ref, *rest) = refs
    wbf, ext, tail_u, tail_x = rest[-4:]
    i = pl.program_id(0)

    @pl.when(i < geo.pro)
    def _():
        wbf[i] = wch_ref[...].astype(BF16)

    @pl.when(i >= geo.pro)
    def _():
        r = i - geo.pro
        is_s = r == geo.nt - 1
        first = r % geo.tps == 0
        x = jnp.where(is_s, xs_ref[...], xp_ref[...]) if split_x else xp_ref[...]
        sc = _mod(is_s, qsc_ref, rsc_ref, geo.reps)
        sh = _mod(is_s, qsh_ref, rsh_ref, geo.reps)
        h = (_rms(x, g_ref[...]) * (1.0 + sc) + sh).astype(BF16)
        nc = dc // cw
        for a in range(nc):
            c0 = a * cw
            gate_b = _dot_nt(h, wbf[a])
            u = _dot_nt(h, wbf[nc + a]) * _dot_nt(h, wbf[2 * nc + a])
            gbs_ref[:, c0:c0 + cw] = gate_b
            us_ref[:, c0:c0 + cw] = u

            def emit_a(v, c0=c0, gate_b=gate_b):
                ya_ref[:, c0:c0 + cw] = (gate_b * v).astype(BF16)

            _causal_conv_cols(u, c0, caw_ref, tail_u, ext, a % 2, ncap_ref, is_s, first, emit_a)
        for a in range(ds // cw):
            z_ref[:, a * cw:(a + 1) * cw] = _dot_nt(h, wbf[3 * nc + a])
        for a in range(dx // cw):
            c0 = a * cw
            xbc = _dot_nt(h, wbf[3 * nc + ds // cw + a])
            xbcs_ref[:, c0:c0 + cw] = xbc

            def emit_b(v, c0=c0):
                xc_ref[:, c0:c0 + cw] = _silu(v + cbb_ref[:, c0:c0 + cw])

            _causal_conv_cols(xbc, c0, cbw_ref, tail_x, ext, (nc + a) % 2, ncbp_ref, is_s, first, emit_b)
        dt_ref[...] = _dot_nt(h, wdt_ref[...])
        if zero_fill:
            rest[0][...] = jnp.zeros(rest[0].shape, F32)


def _inproj(x_p, x_s, seq_mod, row_mod, geo_args, norm_w, w_in, layer, w_dt, caw, cbw, cbb, bp, ds, zero_rows=0):
    tm, nt, tps, reps, d = geo_args
    dc, dx = caw.shape[1], cbw.shape[1]
    ka, kb = caw.shape[0] - 1, cbw.shape[0] - 1
    cw = _chunk(dc, ds, dx)
    npro = (3 * dc + ds + dx) // cw
    geo = _Rows(tm, nt, tps, reps, d, npro)
    t_all = tm * nt
    split_x = x_s is not None
    xin = [geo.prompt_rows(d), geo.sample_rows(d)] if split_x else [geo.rows(d)]
    xargs = [x_p, x_s] if split_x else [x_p]
    nrow = row_mod.shape[0]
    seq_blk = lambda k, w: pl.BlockSpec((None, k, w), lambda i: (jnp.minimum(geo.tile(i) // tps, bp - 1), 0, 0))
    zero_specs, zero_shapes = [], []
    if zero_rows:
        zero_specs = [pl.BlockSpec((zero_rows // (nt - 1), LANES), lambda i: (jnp.minimum(geo.tile(i), nt - 2), 0))]
        zero_shapes = [jax.ShapeDtypeStruct((zero_rows, LANES), F32)]
    return pl.pallas_call(
        functools.partial(_inproj_kernel, geo=geo, dc=dc, ds=ds, dx=dx, cw=cw, split_x=split_x,
                          zero_fill=bool(zero_rows)),
        grid=(npro + nt,),
        in_specs=xin + [
            geo.seq_mod(0), geo.seq_mod(1), geo.row_mod(0, nrow), geo.row_mod(1, nrow),
            pl.BlockSpec((1, d), lambda i: (0, 0)),
            pl.BlockSpec((None, cw, d), lambda i: (layer, jnp.minimum(i, npro - 1), 0)),
            _resident((LANES, d)), _resident(caw.shape), _resident(cbw.shape), _resident(cbb.shape),
        ],
        out_specs=[geo.rows(dc), geo.rows(ds), geo.rows(dx), geo.rows(LANES), seq_blk(ka, dc), seq_blk(kb, dx),
                   geo.sample_rows(dc), geo.sample_rows(dc), geo.sample_rows(dx)] + zero_specs,
        out_shape=[
            jax.ShapeDtypeStruct((t_all, dc), BF16),
            jax.ShapeDtypeStruct((t_all, ds), F32),
            jax.ShapeDtypeStruct((t_all, dx), F32),
            jax.ShapeDtypeStruct((t_all, LANES), F32),
            jax.ShapeDtypeStruct((bp, ka, dc), F32),
            jax.ShapeDtypeStruct((bp, kb, dx), F32),
            jax.ShapeDtypeStruct((tm, dc), F32),
            jax.ShapeDtypeStruct((tm, dc), F32),
            jax.ShapeDtypeStruct((tm, dx), F32),
        ] + zero_shapes,
        scratch_shapes=[pltpu.VMEM((npro, cw, d), BF16), pltpu.VMEM((2, SUBLANES + tm, cw), F32),
                        pltpu.VMEM((SUBLANES, dc), F32), pltpu.VMEM((SUBLANES, dx), F32)],
        compiler_params=_cparams("arbitrary"),
        name="inproj",
    )(*xargs, seq_mod, seq_mod, row_mod, row_mod, norm_w.reshape(1, d), w_in, w_dt, caw, cbw, cbb)


def _split_hi_lo(v):
    hi = v.astype(BF16)
    lo = (v - hi.astype(F32)).astype(BF16)
    return jnp.concatenate([hi, lo], axis=1)


def _diag_block(cb, mask, cum, cum_t, dt_t, xs, lane, g, hg, gw, p):
    hpl = LANES // p
    parts = []
    for slab in range(gw // LANES):
        lhs, rhs = [], []
        xslab = xs[:, g * gw + slab * LANES:g * gw + (slab + 1) * LANES]
        for j in range(hpl):
            h = g * hg + slab * hpl + j
            seg = cum[:, h:h + 1] - cum_t[h:h + 1, :]
            m = cb * jnp.where(mask, jnp.exp2(seg), 0.0) * dt_t[h:h + 1, :]
            lhs.append(m.astype(BF16))
            rhs.append(jnp.where((lane >= j * p) & (lane < (j + 1) * p), xslab, 0.0).astype(BF16))
        parts.append(_dot(jnp.concatenate(lhs, axis=1), jnp.concatenate(rhs, axis=0)))
    return jnp.concatenate(parts, axis=1)


def _gated_group_norm(y, xs_g, z_g, dsk_g, snorm_g):
    y = (y + dsk_g * xs_g) * _silu(z_g)
    return (y * lax.rsqrt(jnp.mean(y * y, axis=-1, keepdims=True) + EPS) * snorm_g).astype(BF16)


def _ssd_prompt_kernel(z_ref, xc_ref, dt_ref, dtb_ref, alog_ref, dsk_ref, snorm_ref, e_ref, yb_ref, nss_ref, st,
                       *, ds, n, p, hg, sub):
    c = pl.program_id(1)
    q = SSD_CHUNK
    gw = hg * p

    @pl.when(c == 0)
    def _():
        st[...] = jnp.zeros(st.shape, F32)

    row = lax.broadcasted_iota(jnp.int32, (q, LANES), 0)
    tril = lax.broadcasted_iota(jnp.int32, (q, q), 0) >= lax.broadcasted_iota(jnp.int32, (q, q), 1)
    lane = lax.broadcasted_iota(jnp.int32, (q, LANES), 1)
    neg_a = -jnp.exp(alog_ref[...]) * LOG2E
    for k in range(sub):
        rs = slice(k * q, (k + 1) * q)
        xs = xc_ref[rs, 0:ds]
        dt = _softplus(dt_ref[rs, :] + dtb_ref[...])
        cum = dt * neg_a
        step = 1
        while step < q:
            cum = cum + jnp.where(row >= step, pltpu.roll(cum, step, axis=0), 0.0)
            step *= 2
        cum_last = cum[q - 1:q, :]
        w_exp = _dot(_split_hi_lo(dt * jnp.exp2(cum_last - cum)), e_ref[...])
        ecum_exp = _dot(_split_hi_lo(jnp.exp2(cum)), e_ref[...])
        cum_t = cum.T
        dt_t = dt.T
        for g in range(SSD_GROUPS):
            gsl = slice(g * gw, (g + 1) * gw)
            bm = xc_ref[rs, ds + g * n:ds + (g + 1) * n]
            cm = xc_ref[rs, ds + SSD_GROUPS * n + g * n:ds + SSD_GROUPS * n + (g + 1) * n].astype(BF16)
            bm_t = bm.T.astype(BF16)
            y = _diag_block(_dot(cm, bm_t), tril, cum, cum_t, dt_t, xs, lane, g, hg, gw, p)
            s_prev = st[g]
            y = y + _dot(cm, s_prev.astype(BF16)) * ecum_exp[:, gsl]
            xw = (xs[:, gsl] * w_exp[:, gsl]).astype(BF16)
            st[g] = s_prev * ecum_exp[q - 1:q, gsl] + _dot(bm_t, xw)
            yb_ref[rs, gsl] = _gated_group_norm(y, xs[:, gsl], z_ref[rs, gsl], dsk_ref[:, gsl], snorm_ref[:, gsl])

    @pl.when(c == pl.num_programs(1) - 1)
    def _():
        for g in range(SSD_GROUPS):
            for slab in range(gw // LANES):
                r0 = g * gw + slab * LANES
                nss_ref[r0:r0 + LANES, :] = st[g, :, slab * LANES:(slab + 1) * LANES].T


def _ssd_prompt(z, xc, dtr, bsz, seq, params, n, p, hg):
    ds = z.shape[1]
    dx = xc.shape[1]
    sub = next(k for k in SSD_CHUNKS_PER_STEP if seq % (k * SSD_CHUNK) == 0)
    rows = sub * SSD_CHUNK
    nc = seq // rows
    row = lambda w: pl.BlockSpec((rows, w), lambda b, c: (b * nc + c, 0))
    return pl.pallas_call(
        functools.partial(_ssd_prompt_kernel, ds=ds, n=n, p=p, hg=hg, sub=sub),
        grid=(bsz, nc),
        in_specs=[row(ds), row(dx), row(LANES)] + [_resident(a.shape) for a in params],
        out_specs=[row(ds), pl.BlockSpec((None, ds, n), lambda b, c: (b, 0, 0))],
        out_shape=[jax.ShapeDtypeStruct((bsz * seq, ds), BF16), jax.ShapeDtypeStruct((bsz, ds, n), F32)],
        scratch_shapes=[pltpu.VMEM((SSD_GROUPS, n, hg * p), F32)],
        compiler_params=_cparams("arbitrary", "arbitrary"),
        name="ssd_prompt",
    )(z, xc, dtr, *params)


def _ssd_sample_kernel(gb_ref, u_ref, xbc_ref, z_ref, dt_ref, sa_ref, sb_ref, ss_ref, caw_ref, cbw_ref, cbb_ref,
                       dtb_ref, alog_ref, dsk_ref, snorm_ref, e_ref, ya_ref, yb_ref, nca_ref, ncb_ref, nss_ref,
                       *, dc, ds, dx, n, p, hg, steps, bb):
    gw = hg * p
    rows = steps * bb
    ka = caw_ref.shape[0] - 1
    kb = cbw_ref.shape[0] - 1

    def conv(cur_ref, st_ref, w_ref, kprev, width):
        hist = [st_ref[:, j * width:(j + 1) * width] for j in range(kprev)] + [cur_ref[t] for t in range(steps)]
        outs = []
        for t in range(steps):
            acc = w_ref[kprev:kprev + 1, :] * hist[t + kprev]
            for j in range(kprev):
                acc = acc + w_ref[j:j + 1, :] * hist[t + j]
            outs.append(acc)
        return outs, hist[len(hist) - kprev:]

    v, new_a = conv(u_ref, sa_ref, caw_ref, ka, dc)
    for t in range(steps):
        ya_ref[t] = (gb_ref[t] * v[t]).astype(BF16)
    for j in range(ka):
        nca_ref[:, j * dc:(j + 1) * dc] = new_a[j]
    xcs, new_b = conv(xbc_ref, sb_ref, cbw_ref, kb, dx)
    for j in range(kb):
        ncb_ref[:, j * dx:(j + 1) * dx] = new_b[j]
    xc = _silu(jnp.concatenate(xcs, axis=0) + cbb_ref[...])
    xs = xc[:, 0:ds]

    dt = _softplus(jnp.concatenate([dt_ref[t] for t in range(steps)], axis=0) + dtb_ref[...])
    da = dt * (-jnp.exp(alog_ref[...]) * LOG2E)
    cums = [da[0:bb]]
    for t in range(1, steps):
        cums.append(cums[-1] + da[t * bb:(t + 1) * bb])
    cum = jnp.concatenate(cums, axis=0)
    cum_last = jnp.concatenate([cums[-1]] * steps, axis=0)
    w_exp = _dot(_split_hi_lo(dt * jnp.exp2(cum_last - cum)), e_ref[...])
    ecum_exp = _dot(_split_hi_lo(jnp.exp2(cum)), e_ref[...])

    def pad_t(a):
        a = jnp.concatenate([a, jnp.zeros((LANES - rows, LANES), F32)], axis=0) if rows < LANES else a
        return a.T[:, 0:rows]

    cum_t = pad_t(cum)
    dt_t = pad_t(dt)

    ri = lax.broadcasted_iota(jnp.int32, (rows, rows), 0)
    ci = lax.broadcasted_iota(jnp.int32, (rows, rows), 1)
    same = ((ri % bb) == (ci % bb)) & (ri >= ci)
    lane = lax.broadcasted_iota(jnp.int32, (rows, LANES), 1)
    rowid = lax.broadcasted_iota(jnp.int32, (rows, 1), 0) % bb
    seqlane = lax.broadcasted_iota(jnp.int32, (LANES, LANES), 1)
    nslab = gw // LANES
    for g in range(SSD_GROUPS):
        gsl = slice(g * gw, (g + 1) * gw)
        bm = xc[:, ds + g * n:ds + (g + 1) * n].astype(BF16)
        cm = xc[:, ds + SSD_GROUPS * n + g * n:ds + SSD_GROUPS * n + (g + 1) * n].astype(BF16)
        y_diag = _diag_block(_dot_nt(cm, bm), same, cum, cum_t, dt_t, xs, lane, g, hg, gw, p)
        xw = (xs[:, gsl] * w_exp[:, gsl]).astype(BF16)
        dec = ecum_exp[(steps - 1) * bb:steps * bb, gsl]
        dec = jnp.concatenate([dec, jnp.zeros((LANES - bb, gw), F32)], axis=0)
        dec_t = [dec[:, s * LANES:(s + 1) * LANES].T for s in range(nslab)]

        def per_seq(b, y_off, g=g, cm=cm, bm=bm, xw=xw, dec_t=dec_t):
            r0 = g * gw
            s0 = ss_ref[b, r0:r0 + gw, :]
            y_off = jnp.where(rowid == b, _dot_nt(cm, s0.astype(BF16)), y_off)
            upd = _dot_tn(jnp.where(rowid == b, xw, jnp.zeros_like(xw)), bm)
            for s in range(nslab):
                dcol = jnp.sum(jnp.where(seqlane == b, dec_t[s], 0.0), axis=1, keepdims=True)
                nss_ref[b, r0 + s * LANES:r0 + (s + 1) * LANES, :] = (
                    s0[s * LANES:(s + 1) * LANES, :] * dcol + upd[s * LANES:(s + 1) * LANES, :])
            return y_off

        y_off = lax.fori_loop(0, bb, per_seq, jnp.zeros((rows, gw), F32), unroll=min(SAMPLE_LOOP_UNROLL, bb))
        zg = jnp.concatenate([z_ref[t, :, gsl] for t in range(steps)], axis=0)
        yn = _gated_group_norm(y_diag + y_off * ecum_exp[:, gsl], xs[:, gsl], zg, dsk_ref[:, gsl], snorm_ref[:, gsl])
        for t in range(steps):
            yb_ref[t, :, gsl] = yn[t * bb:(t + 1) * bb]


def _ssd_sample_aliased_kernel(*refs, **kw):
    _ssd_sample_kernel(*refs[1:], **kw)


def _ssd_sample(gb, u, xbc, z, dtr, blk0, steps, bs, layer, sa, sb, ss, prev_ss, params, n, p, hg, bb):
    dc, ds, dx = u.shape[2], z.shape[2], xbc.shape[2]
    ka, kb = params[0].shape[0] - 1, params[1].shape[0] - 1
    own = lambda w: pl.BlockSpec((steps, bb, w), lambda i: (0, i, 0))
    tok = lambda w: pl.BlockSpec((steps, bb, w), lambda i: (blk0, i, 0))
    state = pl.BlockSpec((None, bb, ds, n), lambda i: (layer, i, 0, 0))
    return pl.pallas_call(
        functools.partial(_ssd_sample_aliased_kernel, dc=dc, ds=ds, dx=dx, n=n, p=p, hg=hg, steps=steps, bb=bb),
        grid=(bs // bb,),
        in_specs=[pl.BlockSpec(memory_space=pl.ANY), own(dc), own(dc), own(dx), tok(ds), tok(LANES),
                  pl.BlockSpec((None, bb, ka * dc), lambda i: (layer, i, 0)),
                  pl.BlockSpec((None, bb, kb * dx), lambda i: (layer, i, 0)), state]
        + [_resident(a.shape) for a in params],
        out_specs=[own(dc), own(ds), pl.BlockSpec((bb, ka * dc), lambda i: (i, 0)),
                   pl.BlockSpec((bb, kb * dx), lambda i: (i, 0)), state],
        out_shape=[jax.ShapeDtypeStruct((steps, bs, dc), BF16), jax.ShapeDtypeStruct((steps, bs, ds), BF16),
                   jax.ShapeDtypeStruct((bs, ka * dc), F32), jax.ShapeDtypeStruct((bs, kb * dx), F32),
                   jax.ShapeDtypeStruct(prev_ss.shape, F32)],
        input_output_aliases={0: 4},
        compiler_params=_cparams("arbitrary"),
        name="ssd_sample",
    )(prev_ss, gb, u, xbc, z, dtr, sa, sb, ss, *params)


def _route(logits_t, n_exp, cnt_ref):
    ep, toks = logits_t.shape
    sub = lax.broadcasted_iota(jnp.int32, logits_t.shape, 0).astype(F32)
    valid = sub < n_exp
    logits_t = jnp.where(valid, logits_t, -jnp.inf)
    e = jnp.exp(logits_t - jnp.max(logits_t, axis=0, keepdims=True))
    prob = jnp.where(valid, e / jnp.sum(e, axis=0, keepdims=True), -1.0)
    big = float(LANES)
    m1 = jnp.max(prob, axis=0, keepdims=True)
    i1 = jnp.min(jnp.where(prob == m1, sub, big), axis=0, keepdims=True)
    rest = jnp.where(sub == i1, -1.0, prob)
    m2 = jnp.max(rest, axis=0, keepdims=True)
    i2 = jnp.min(jnp.where(rest == m2, sub, big), axis=0, keepdims=True)
    den = m1 + m2
    chosen = jnp.where((sub == i1) | (sub == i2), 1.0, 0.0)
    earlier = lax.broadcasted_iota(jnp.int32, (toks, toks), 0) < lax.broadcasted_iota(jnp.int32, (toks, toks), 1)
    rank = _dot(chosen.astype(BF16), jnp.where(earlier, 1.0, 0.0).astype(BF16)) + cnt_ref[:, 0:1]
    r1 = jnp.sum(jnp.where(sub == i1, rank, 0.0), axis=0, keepdims=True)
    r2 = jnp.sum(jnp.where(sub == i2, rank, 0.0), axis=0, keepdims=True)
    cnt_ref[...] = cnt_ref[...] + jnp.sum(chosen, axis=1, keepdims=True)
    rows = [i1, i2, m1 / den, m2 / den, r1, r2]
    compact = jnp.concatenate(rows + [jnp.zeros((SUBLANES - len(rows), toks), F32)], axis=0)
    wide = jnp.concatenate([compact, jnp.zeros((LANES - SUBLANES, toks), F32)], axis=0).T
    return wide, compact


def _outproj_kernel(yap_ref, yas_ref, ybp_ref, ybs_ref, wch_ref, *rest, geo, n_exp, kc, split_x):
    if split_x:
        xp_ref, xs_ref, *rest = rest
    else:
        xp_ref, *rest = rest
    (qg_ref, qsh_ref, qsc_ref, rg_ref, rsh_ref, rsc_ref, npost_ref, npre_ref, r_ref,
     x1_ref, h2_ref, sel_ref, selt_ref, cnt_ref, wbf) = rest
    i = pl.program_id(0)

    @pl.when(i < geo.pro)
    def _():
        wbf[pl.ds(pl.multiple_of(i * kc, kc), kc), :] = wch_ref[...].astype(BF16)
        cnt_ref[...] = jnp.zeros(cnt_ref.shape, F32)

    @pl.when(i >= geo.pro)
    def _():
        is_s = i - geo.pro == geo.nt - 1
        da = yap_ref.shape[1]
        nh = 2 if geo.reps % 2 == 0 else 1
        th = geo.tm // nh
        for hs in range(nh):
            rs = slice(hs * th, (hs + 1) * th)
            mod = lambda q_ref, r_ref: _mod(is_s, q_ref, r_ref, geo.reps // nh)
            mix = (_dot(jnp.where(is_s, yas_ref[rs, :], yap_ref[rs, :]), wbf[0:da, :])
                   + _dot(jnp.where(is_s, ybs_ref[rs, :], ybp_ref[rs, :]), wbf[da:, :]))
            x = jnp.where(is_s, xs_ref[rs, :], xp_ref[rs, :]) if split_x else xp_ref[rs, :]
            x1 = x + mod(qg_ref, rg_ref) * _rms(mix, npost_ref[...])
            x1_ref[rs, :] = x1
            h2 = _rms(x1, npre_ref[...]) * (1.0 + mod(qsc_ref, rsc_ref)) + mod(qsh_ref, rsh_ref)
            h2_ref[rs, :] = h2
            h_hi = h2.astype(BF16)
            h_lo = (h2 - h_hi.astype(F32)).astype(BF16)
            r = r_ref[...]
            ep = r.shape[0]
            r_hi = r.astype(BF16)
            r_lo = (r - r_hi.astype(F32)).astype(BF16)
            both = _dot_nt(jnp.concatenate([r_hi, r_lo], axis=0), h_hi)
            logits_t = both[0:ep, :] + (_dot_nt(r_hi, h_lo) + both[ep:, :])
            sel_ref[rs, :], selt_ref[:, rs] = _route(logits_t, n_exp, cnt_ref)


def _outproj(ya_p, ya_s, yb_p, yb_s, w_out, layer, x_p, x_s, seq_mod, row_mod, geo_args, npost, npre, router,
             n_exp):
    tm, nt, tps, reps, d = geo_args
    da, db = ya_p.shape[1], yb_p.shape[1]
    dm = da + db
    kc = _chunk(dm) // 2
    npro = dm // kc
    geo = _Rows(tm, nt, tps, reps, d, npro)
    t_all = tm * nt
    nrow = row_mod.shape[0]
    vec = pl.BlockSpec((1, d), lambda i: (0, 0))
    split_x = x_s is not None
    in_specs = [geo.prompt_rows(da), geo.sample_rows(da), geo.prompt_rows(db), geo.sample_rows(db),
                pl.BlockSpec((None, kc, d), lambda i: (layer, jnp.minimum(i, npro - 1), 0))]
    in_specs += [geo.prompt_rows(d), geo.sample_rows(d)] if split_x else [geo.rows(d)]
    in_specs += [geo.seq_mod(2), geo.seq_mod(3), geo.seq_mod(4),
                 geo.row_mod(2, nrow), geo.row_mod(3, nrow), geo.row_mod(4, nrow), vec, vec]
    args = [ya_p, ya_s, yb_p, yb_s, w_out] + ([x_p, x_s] if split_x else [x_p])
    in_specs.append(_resident(router.shape))
    args += [seq_mod, seq_mod, seq_mod, row_mod, row_mod, row_mod, npost.reshape(1, d), npre.reshape(1, d), router]
    cnt_shape = (router.shape[0], LANES)
    out_specs = [geo.rows(d), geo.rows(d), geo.rows(LANES), pl.BlockSpec((SUBLANES, tm), lambda i: (0, geo.tile(i))),
                 pl.BlockSpec(cnt_shape, lambda i: (0, 0))]
    out_shape = [jax.ShapeDtypeStruct((t_all, d), F32), jax.ShapeDtypeStruct((t_all, d), F32),
                 jax.ShapeDtypeStruct((t_all, LANES), F32), jax.ShapeDtypeStruct((SUBLANES, t_all), F32),
                 jax.ShapeDtypeStruct(cnt_shape, F32)]
    return pl.pallas_call(
        functools.partial(_outproj_kernel, geo=geo, n_exp=n_exp, kc=kc, split_x=split_x),
        grid=(npro + nt,),
        in_specs=in_specs,
        out_specs=out_specs,
        out_shape=out_shape,
        scratch_shapes=[pltpu.VMEM((dm, d), BF16)],
        compiler_params=_cparams("arbitrary"),
        name="outproj",
    )(*args)


def _write_split(is_s, val, outp_ref, outs_ref):
    @pl.when(jnp.logical_not(is_s))
    def _():
        outp_ref[...] = val

    @pl.when(is_s)
    def _():
        outs_ref[...] = val


def _out_rows(geo, d, split_out):
    t_all = geo.tm * geo.nt
    if split_out:
        return ([geo.prompt_rows(d), geo.sample_rows(d)],
                [jax.ShapeDtypeStruct((t_all - geo.tm, d), F32), jax.ShapeDtypeStruct((geo.tm, d), F32)])
    return [geo.rows(d)], [jax.ShapeDtypeStruct((t_all, d), F32)]


def _mix_ffn_kernel(yap_ref, yas_ref, ybp_ref, ybs_ref, wo_ref, wg_ref, wu_ref, wd_ref, *rest, geo, kc, cf, n_out,
                    n_ffn, split_x, split_out):
    if split_x:
        xp_ref, xs_ref, *rest = rest
    else:
        xp_ref, *rest = rest
    (qg1_ref, qsh_ref, qsc_ref, qg2_ref, rg1_ref, rsh_ref, rsc_ref, rg2_ref, npost_ref, npre_ref, nffn_ref,
     *rest) = rest
    outs, (wob, wgb, wub, wdb, act) = rest[:-5], rest[-5:]
    i = pl.program_id(0)

    @pl.when(i < n_out)
    def _():
        wob[pl.ds(pl.multiple_of(i * kc, kc), kc), :] = wo_ref[...].astype(BF16)

    @pl.when(i < n_ffn)
    def _():
        wgb[i] = wg_ref[...].astype(BF16)
        wub[i] = wu_ref[...].astype(BF16)
        wdb[pl.ds(pl.multiple_of(i * cf, cf), cf), :] = wd_ref[...].astype(BF16)

    @pl.when(i >= geo.pro)
    def _():
        is_s = i - geo.pro == geo.nt - 1
        da = yap_ref.shape[1]
        mod = lambda q_ref, r_ref: _mod(is_s, q_ref, r_ref, geo.reps)
        mix = (_dot(jnp.where(is_s, yas_ref[...], yap_ref[...]), wob[0:da, :])
               + _dot(jnp.where(is_s, ybs_ref[...], ybp_ref[...]), wob[da:, :]))
        x = jnp.where(is_s, xs_ref[...], xp_ref[...]) if split_x else xp_ref[...]
        x1 = x + mod(qg1_ref, rg1_ref) * _rms(mix, npost_ref[...])
        h = (_rms(x1, npre_ref[...]) * (1.0 + mod(qsc_ref, rsc_ref)) + mod(qsh_ref, rsh_ref)).astype(BF16)
        for c in range(n_ffn):
            act[:, c * cf:(c + 1) * cf] = (_silu(_dot(h, wgb[c])) * _dot(h, wub[c])).astype(BF16)
        f = _dot(act[...], wdb[...])
        val = x1 + mod(qg2_ref, rg2_ref) * _rms(f, nffn_ref[...])
        if split_out:
            _write_split(is_s, val, *outs)
        else:
            outs[0][...] = val


def _mix_ffn(ya_p, ya_s, yb_p, yb_s, w_out, layer, wg, wu, wd, j, x_p, x_s, seq_mod, row_mod, geo_args, npost, npre,
             nffn, split_out):
    tm, nt, tps, reps, d = geo_args
    da, db = ya_p.shape[1], yb_p.shape[1]
    dm = da + db
    f = wg.shape[2]
    kc = _chunk(dm) // 2
    cf = MXU_WIDTH if f % MXU_WIDTH == 0 else LANES
    n_out, n_ffn = dm // kc, f // cf
    geo = _Rows(tm, nt, tps, reps, d, max(n_out, n_ffn))
    nrow = row_mod.shape[0]
    vec = pl.BlockSpec((1, d), lambda i: (0, 0))
    split_x = x_s is not None
    out_specs, out_shape = _out_rows(geo, d, split_out)
    oc = lambda i: jnp.minimum(i, n_out - 1)
    fc = lambda i: jnp.minimum(i, n_ffn - 1)
    in_specs = [geo.prompt_rows(da), geo.sample_rows(da), geo.prompt_rows(db), geo.sample_rows(db),
                pl.BlockSpec((None, kc, d), lambda i: (layer, oc(i), 0)),
                pl.BlockSpec((None, d, cf), lambda i: (j, 0, fc(i))),
                pl.BlockSpec((None, d, cf), lambda i: (j, 0, fc(i))),
                pl.BlockSpec((None, cf, d), lambda i: (j, fc(i), 0))]
    in_specs += [geo.prompt_rows(d), geo.sample_rows(d)] if split_x else [geo.rows(d)]
    in_specs += [geo.seq_mod(k) for k in (2, 3, 4, 5)] + [geo.row_mod(k, nrow) for k in (2, 3, 4, 5)] + [vec] * 3
    args = [ya_p, ya_s, yb_p, yb_s, w_out, wg, wu, wd] + ([x_p, x_s] if split_x else [x_p])
    args += [seq_mod] * 4 + [row_mod] * 4 + [npost.reshape(1, d), npre.reshape(1, d), nffn.reshape(1, d)]
    return pl.pallas_call(
        functools.partial(_mix_ffn_kernel, geo=geo, kc=kc, cf=cf, n_out=n_out, n_ffn=n_ffn, split_x=split_x,
                          split_out=split_out),
        grid=(geo.pro + nt,),
        in_specs=in_specs,
        out_specs=out_specs,
        out_shape=out_shape,
        scratch_shapes=[pltpu.VMEM((dm, d), BF16), pltpu.VMEM((n_ffn, d, cf), BF16), pltpu.VMEM((n_ffn, d, cf), BF16),
                        pltpu.VMEM((f, d), BF16), pltpu.VMEM((tm, f), BF16)],
        compiler_params=_cparams("arbitrary"),
        name="mix_ffn",
    )(*args)


def _row_copy(src, src_row, dst, dst_row, sem):
    return pltpu.make_async_copy(src.at[pl.ds(src_row, 1)], dst.at[pl.ds(dst_row, 1)], sem)


def _moe_dispatch_kernel(pad_start_ref, pad_len_ref, nvalid_ref, pos_ref, h_ref, wg_ref, wu_ref, wd_ref,
                         xs_hbm, wgb_ref, wub_ref, wdb_ref, zbuf, hbuf, sems, *, tm, n_exp, nt):
    i = pl.program_id(0)
    ts = zbuf.shape[0]
    wgb_ref[...] = wg_ref[...].astype(BF16)
    wub_ref[...] = wu_ref[...].astype(BF16)
    wdb_ref[...] = wd_ref[...].astype(BF16)

    @pl.when(i == 0)
    def _():
        zbuf[...] = jnp.zeros(zbuf.shape, F32)
        for e in range(n_exp):
            def start(k, c, e=e):
                _row_copy(zbuf, 0, xs_hbm, pad_start_ref[e] + k, sems.at[2]).start()
                return c

            def wait(k, c, e=e):
                _row_copy(zbuf, 0, xs_hbm, pad_start_ref[e] + k, sems.at[2]).wait()
                return c

            lax.fori_loop(0, pad_len_ref[e], start, 0)
            lax.fori_loop(0, pad_len_ref[e], wait, 0)

        def tile_copy(j):
            return pltpu.make_async_copy(zbuf, xs_hbm.at[pl.ds(pl.multiple_of(j * ts, ts), ts)], sems.at[2])

        def start_tile(j, c):
            tile_copy(j).start()
            return c

        def wait_tile(j, c):
            tile_copy(j).wait()
            return c

        lax.fori_loop(nvalid_ref[0], xs_hbm.shape[0] // ts, start_tile, 0)
        lax.fori_loop(nvalid_ref[0], xs_hbm.shape[0] // ts, wait_tile, 0)

    def wait_slot(s):
        for _ in range(TOP_K):
            pltpu.make_async_copy(hbuf.at[s], xs_hbm.at[pl.ds(0, tm)], sems.at[s]).wait()

    for s in range(2):
        @pl.when(i % 2 == s)
        def _(s=s):
            @pl.when(i >= 2)
            def _():
                wait_slot(s)

            hbuf[s] = h_ref[...]
            for r in range(tm):
                _row_copy(hbuf.at[s], r, xs_hbm, pos_ref[0, r], sems.at[s]).start(priority=0)
                _row_copy(hbuf.at[s], r, xs_hbm, pos_ref[0, tm + r], sems.at[s]).start(priority=1)

    @pl.when(i == nt - 1)
    def _():
        wait_slot((nt - 1) % 2)
        if nt > 1:
            wait_slot(nt % 2)


def _moe_dispatch(h2, pos_tiles, pad_start, pad_len, n_valid, n_slots, tm, tm_slot, wg, wu, wd):
    t, d = h2.shape
    nt = t // tm
    n_exp = pad_start.shape[0]
    pack = 2 * SUBLANES
    ncast = next(k for k in range(nt, 0, -1)
                 if wg.shape[0] % (k * pack) == 0 and wd.shape[0] % (k * pack) == 0)
    wrows = lambda a: pl.BlockSpec((a.shape[0] // ncast, a.shape[1]), lambda i, *_: (jnp.minimum(i, ncast - 1), 0))
    grid_spec = pltpu.PrefetchScalarGridSpec(
        num_scalar_prefetch=3,
        grid=(nt,),
        in_specs=[pl.BlockSpec((None, 1, TOP_K * tm), lambda i, *_: (i, 0, 0), memory_space=pltpu.SMEM),
                  pl.BlockSpec((tm, d), lambda i, *_: (i, 0)), wrows(wg), wrows(wu), wrows(wd)],
        out_specs=[pl.BlockSpec(memory_space=pl.ANY), wrows(wg), wrows(wu), wrows(wd)],
        scratch_shapes=[pltpu.VMEM((tm_slot, d), F32), pltpu.VMEM((2, tm, d), F32), pltpu.SemaphoreType.DMA((3,))],
    )
    return pl.pallas_call(
        functools.partial(_moe_dispatch_kernel, tm=tm, n_exp=n_exp, nt=nt),
        grid_spec=grid_spec,
        out_shape=[jax.ShapeDtypeStruct((n_slots, d), F32)] + [jax.ShapeDtypeStruct(a.shape, BF16) for a in (wg, wu, wd)],
        compiler_params=_cparams("arbitrary"),
        name="moe_dispatch",
    )(pad_start, pad_len, n_valid, pos_tiles.reshape(nt, 1, TOP_K * tm), h2, wg, wu, wd)


def _moe_expert_kernel(texp_ref, nvalid_ref, x_ref, wg_ref, wu_ref, wd_ref, o_ref):
    i = pl.program_id(0)

    @pl.when(i < nvalid_ref[0])
    def _():
        h = x_ref[...].astype(BF16)
        act = (_silu(_dot(h, wg_ref[...])) * _dot(h, wu_ref[...])).astype(BF16)
        o_ref[...] = _dot(act, wd_ref[...])

    @pl.when(i >= nvalid_ref[0])
    def _():
        o_ref[...] = jnp.zeros(o_ref.shape, F32)


def _moe_experts(x_sorted, tile_expert, n_valid, wg, wu, wd, tm):
    n_tiles = tile_expert.shape[0]
    d = x_sorted.shape[1]
    n_exp, _, fe = wg.shape
    grid_spec = pltpu.PrefetchScalarGridSpec(
        num_scalar_prefetch=2,
        grid=(n_tiles,),
        in_specs=[
            pl.BlockSpec((tm, d), lambda i, te, nv: (jnp.minimum(i, nv[0] - 1), 0)),
            pl.BlockSpec((None, d, fe), lambda i, te, nv: (te[i], 0, 0)),
            pl.BlockSpec((None, d, fe), lambda i, te, nv: (te[i], 0, 0)),
            pl.BlockSpec((None, fe, d), lambda i, te, nv: (te[i], 0, 0)),
        ],
        out_specs=pl.BlockSpec((tm, d), lambda i, te, nv: (i, 0)),
    )
    return pl.pallas_call(
        _moe_expert_kernel,
        grid_spec=grid_spec,
        out_shape=jax.ShapeDtypeStruct((n_tiles * tm, d), F32),
        compiler_params=_cparams("arbitrary"),
        name="moe_experts",
    )(tile_expert, n_valid, x_sorted, wg, wu, wd)


def _start_row_gather(idx_ref, src_hbm, dst, sem, count):
    for r in range(count):
        _row_copy(src_hbm, idx_ref[0, r], dst, r, sem).start(priority=r % 2)


def _wait_row_gather(src_hbm, dst, sem):
    pltpu.make_async_copy(src_hbm.at[pl.ds(0, dst.shape[0])], dst, sem).wait()


def _moe_combine_kernel(pos0_ref, posnext_ref, ye_hbm, sel_ref, x1_ref, qg_ref, rg_ref, npost_ref, *rest,
                        geo, split_out):
    outs, (ybuf, sems) = rest[:-2], rest[-2:]
    i = pl.program_id(0)
    tm = geo.tm
    slot = i % 2

    @pl.when(i == 0)
    def _():
        _start_row_gather(pos0_ref, ye_hbm, ybuf.at[0], sems.at[0], 2 * tm)

    @pl.when(i + 1 < geo.nt)
    def _():
        _start_row_gather(posnext_ref, ye_hbm, ybuf.at[1 - slot], sems.at[1 - slot], 2 * tm)

    _wait_row_gather(ye_hbm, ybuf.at[slot], sems.at[slot])
    is_s = i == geo.nt - 1
    sel = sel_ref[...]
    lane = lax.broadcasted_iota(jnp.int32, sel.shape, 1)
    w1 = jnp.sum(jnp.where(lane == 2, sel, 0.0), axis=-1, keepdims=True)
    w2 = jnp.sum(jnp.where(lane == 3, sel, 0.0), axis=-1, keepdims=True)
    f = w1 * ybuf[slot, 0:tm, :] + w2 * ybuf[slot, tm:2 * tm, :]
    val = x1_ref[...] + _mod(is_s, qg_ref, rg_ref, geo.reps) * _rms(f, npost_ref[...])
    if split_out:
        _write_split(is_s, val, *outs)
    else:
        outs[0][...] = val


def _moe_combine(ye, pos, sel, x1, seq_mod, row_mod, geo_args, npost, split_out):
    tm, nt, tps, reps, d = geo_args
    geo = _Rows(tm, nt, tps, reps, d, 0)
    out_specs, out_shape = _out_rows(geo, d, split_out)
    pos3 = pos.reshape(nt, 1, 2 * tm)
    smem_blk = lambda fn: pl.BlockSpec((None, 1, 2 * tm), fn, memory_space=pltpu.SMEM)
    return pl.pallas_call(
        functools.partial(_moe_combine_kernel, geo=geo, split_out=split_out),
        grid=(nt,),
        in_specs=[smem_blk(lambda i: (0, 0, 0)),
                  smem_blk(lambda i: (jnp.minimum(i + 1, nt - 1), 0, 0)),
                  pl.BlockSpec(memory_space=pl.ANY),
                  geo.rows(LANES), geo.rows(d), geo.seq_mod(5), geo.row_mod(5, row_mod.shape[0]),
                  pl.BlockSpec((1, d), lambda i: (0, 0))],
        out_specs=out_specs,
        out_shape=out_shape,
        scratch_shapes=[pltpu.VMEM((2, 2 * tm, d), F32), pltpu.SemaphoreType.DMA((2,))],
        compiler_params=_cparams("arbitrary"),
        name="moe_combine",
    )(pos3, pos3, ye, sel, x1, seq_mod, row_mod, npost.reshape(1, d))


def _moe_tables(sel_t, counts, n_exp, tm_tok, tm_slot):
    t = sel_t.shape[1]
    counts = counts[:n_exp, 0].astype(jnp.int32)
    tiles_e = (counts + tm_slot - 1) // tm_slot
    tile_end = jnp.cumsum(tiles_e)
    slot_start = (tile_end - tiles_e) * tm_slot
    choice = sel_t[0:TOP_K].astype(jnp.int32)
    rank = sel_t[2 * TOP_K:3 * TOP_K].astype(jnp.int32)
    onehot = choice[:, :, None] == jnp.arange(n_exp, dtype=jnp.int32)[None, None, :]
    pos = jnp.sum(jnp.where(onehot, slot_start[None, None, :], 0), axis=-1) + rank
    n_tiles = (TOP_K * t + n_exp * (tm_slot - 1)) // tm_slot
    tile_expert = jnp.minimum(
        jnp.sum((jnp.arange(n_tiles, dtype=jnp.int32)[:, None] >= tile_end[None, :]).astype(jnp.int32), axis=1),
        n_exp - 1)
    pos_tiles = jnp.concatenate([pos[k].reshape(-1, tm_tok) for k in range(TOP_K)], axis=1)
    pad_start = slot_start + counts
    pad_len = tiles_e * tm_slot - counts
    return tile_expert, tile_end[-1:].astype(jnp.int32), pos_tiles, pad_start, pad_len, n_tiles * tm_slot


def _largest_tile(t, want):
    tm = min(want, t)
    while t % tm:
        tm //= 2
    return tm


def kernel(x_prompt, x_sample, c_prompt, c_sample, state_conva, state_convb, state_ssm, ada_w, ada_b, norm_pre_mix,
           norm_post_mix, norm_pre_ffn, norm_post_ffn, w_in, w_out, conva_w, convb_w, convb_b, dt_bias, a_log, d_skip,
           ssd_norm, ffd_w_gate, ffd_w_up, ffd_w_down, moe_router, moe_w_gate, moe_w_up, moe_w_down):
    bp, seq, d = x_prompt.shape
    bs, steps, _ = x_sample.shape
    depth = w_in.shape[0]
    dc = conva_w.shape[-1]
    dx = convb_w.shape[-1]
    ds = ssd_norm.shape[-1]
    heads = dt_bias.shape[-1]
    p = ds // heads
    n = (dx - ds) // (2 * SSD_GROUPS)
    hg = heads // SSD_GROUPS
    n_exp = moe_router.shape[-1]
    ka, kb = conva_w.shape[1] - 1, convb_w.shape[1] - 1
    tm = steps * bs
    t_p = bp * seq
    assert seq % SSD_CHUNK == 0 and LANES % p == 0 and (hg * p) % LANES == 0 and n == LANES
    assert heads <= LANES and n_exp <= LANES and dc % LANES == 0 and dx % LANES == 0 and d % LANES == 0
    assert w_in.shape[-1] == 3 * dc + ds + dx + heads and seq % tm == 0 and tm % SUBLANES == 0
    assert bs % SUBLANES == 0 and (t_p // bs) % steps == 0
    nt = t_p // tm + 1
    geo_args = (tm, nt, seq // tm, steps, d)

    w_in_t = jnp.swapaxes(w_in, 1, 2)
    w_dt = jnp.pad(w_in_t[:, 3 * dc + ds + dx:, :], ((0, 0), (0, LANES - heads), (0, 0))).astype(BF16)
    router_t = jnp.pad(jnp.swapaxes(moe_router, 1, 2), ((0, 0), (0, -n_exp % SUBLANES), (0, 0)))
    padh = lambda a: jnp.pad(a, ((0, 0), (0, LANES - heads))).reshape(depth, 1, LANES)
    dtb_p, alog_p = padh(dt_bias), padh(a_log)
    dsk_e = jnp.repeat(d_skip, p, axis=-1).reshape(depth, 1, ds)
    hot = (jnp.arange(LANES)[:, None] == (jnp.arange(ds)[None, :] // p)).astype(BF16)
    emat = jnp.concatenate([hot, hot], axis=0)

    mod = _adaln(jnp.concatenate([c_prompt, c_sample], axis=0), ada_w, ada_b)
    seq_mod = jnp.pad(mod[:, :bp], ((0, 0), (0, 1), (0, 0))).reshape(depth, bp + 1, 1, 6 * d)
    row_mod = mod[:, bp:]

    bb = _largest_tile(bs, SAMPLE_SEQS_PER_STEP)
    x_all = None
    xs_tm = x_sample.transpose(1, 0, 2).reshape(tm, d)
    xp2d = x_prompt.reshape(t_p, d)
    sa_all = state_conva.reshape(depth, bs, ka * dc)
    sb_all = state_convb.reshape(depth, bs, kb * dx)
    ss_all = state_ssm.reshape(depth, bs, ds, n)
    pa, pb, ps, sa_l, sb_l = [], [], [], [], []
    ss_new = None
    for i in range(depth):
        j = i // 2
        last = i == depth - 1
        x_in = (xp2d, xs_tm) if x_all is None else (x_all, None)
        conv_params = [conva_w[i], convb_w[i], convb_b[i].reshape(1, dx)]
        ya, z, xc, dtr, na, nb, gb_s, u_s, xbc_s, *zeros = _inproj(
            *x_in, seq_mod[i], row_mod[i], geo_args, norm_pre_mix[i], w_in_t, i, w_dt[i], *conv_params, bp, ds,
            zero_rows=depth * bs * ds * n // LANES if ss_new is None else 0)
        if zeros:
            ss_new = zeros[0].reshape(depth, bs, ds, n)
        params = [dtb_p[i], alog_p[i], dsk_e[i], ssd_norm[i].reshape(1, ds), emat]
        yb_p, ns = _ssd_prompt(z, xc, dtr, bp, seq, params, n, p, hg)
        r3 = lambda a: a.reshape(a.shape[0] // bs, bs, a.shape[-1])
        ya_s, yb_s, sna, snb, ss_new = _ssd_sample(
            r3(gb_s), r3(u_s), r3(xbc_s), r3(z), r3(dtr), t_p // (bs * steps), steps, bs, i, sa_all, sb_all, ss_all,
            ss_new, conv_params + params, n, p, hg, bb)
        pa.append(na)
        pb.append(nb)
        ps.append(ns.reshape(bp, heads, p, n))
        sa_l.append(sna.reshape(bs, ka, dc))
        sb_l.append(snb.reshape(bs, kb, dx))

        if i % 2 == 1:
            x1, h2, sel, sel_t, counts = _outproj(ya, ya_s.reshape(tm, dc), yb_p, yb_s.reshape(tm, ds), w_out, i, *x_in,
                                           seq_mod[i], row_mod[i], geo_args, norm_post_mix[i], norm_pre_ffn[i],
                                           router_t[j], n_exp)
            tile_expert, n_valid, pos_tiles, pad_start, pad_len, n_slots = _moe_tables(sel_t, counts, n_exp, tm, MOE_TILE)
            fe = moe_w_gate.shape[-1]
            x_sorted, wg_b, wu_b, wd_b = _moe_dispatch(
                h2, pos_tiles, pad_start, pad_len, n_valid, n_slots, tm, MOE_TILE, moe_w_gate[j].reshape(n_exp * d, fe),
                moe_w_up[j].reshape(n_exp * d, fe), moe_w_down[j].reshape(n_exp * fe, d))
            ye = _moe_experts(x_sorted, tile_expert, n_valid, wg_b.reshape(n_exp, d, fe), wu_b.reshape(n_exp, d, fe),
                              wd_b.reshape(n_exp, fe, d), MOE_TILE)
            out = _moe_combine(ye, pos_tiles, sel, x1, seq_mod[i], row_mod[i], geo_args, norm_post_ffn[i], last)
        else:
            out = _mix_ffn(ya, ya_s.reshape(tm, dc), yb_p, yb_s.reshape(tm, ds), w_out, i, ffd_w_gate, ffd_w_up,
                           ffd_w_down, j, *x_in, seq_mod[i], row_mod[i], geo_args, norm_post_mix[i], norm_pre_ffn[i],
                           norm_post_ffn[i], last)
        if last:
            y_p, y_s = out
        else:
            x_all = out[0]

    y_prompt = y_p.reshape(bp, seq, d)
    y_sample = y_s.reshape(steps, bs, d).transpose(1, 0, 2)
    return (y_prompt, y_sample, jnp.stack(pa), jnp.stack(pb), jnp.stack(ps),
            jnp.stack(sa_l), jnp.stack(sb_l), ss_new.reshape(depth, bs, heads, p, n))
```

```python
import functools

import jax
import jax.numpy as jnp
from jax import lax
from jax.experimental import pallas as pl
from jax.experimental.pallas import tpu as pltpu

EPS = 1e-6
SSD_GROUPS = 2
SSD_CHUNK = 128
TOP_K = 2
LOG2E = 1.4426950408889634
LANES = 128
SUBLANES = 8
VMEM_LIMIT_BYTES = 56 * 1024 * 1024
MXU_WIDTH = 256
MOE_TILE = 2 * MXU_WIDTH
SSD_CHUNKS_PER_STEP = (8, 4, 2, 1)
SAMPLE_SEQS_PER_STEP = 16
SAMPLE_STATE_SLOTS = 3
SAMPLE_LOOP_UNROLL = 8

F32 = jnp.float32
BF16 = jnp.bfloat16


def _cparams(*sem):
    return pltpu.CompilerParams(dimension_semantics=sem, vmem_limit_bytes=VMEM_LIMIT_BYTES)


def _resident(shape):
    return pl.BlockSpec(shape, lambda *_: (0,) * len(shape), pipeline_mode=pl.Buffered(1))


def _silu(x):
    return x * (1.0 / (1.0 + jnp.exp(-x)))


def _softplus(x):
    return jnp.maximum(x, 0.0) + jnp.log1p(jnp.exp(-jnp.abs(x)))


def _rms(x, g):
    return x * lax.rsqrt(jnp.mean(x * x, axis=-1, keepdims=True) + EPS) * g


def _dot(a, b):
    return jnp.dot(a, b, preferred_element_type=F32)


def _dot_nt(a, b):
    return lax.dot_general(a, b, (((1,), (1,)), ((), ())), preferred_element_type=F32)


def _dot_tn(a, b):
    return lax.dot_general(a, b, (((0,), (0,)), ((), ())), preferred_element_type=F32)


def _chunk(*widths):
    return next(c for c in (2 * MXU_WIDTH, MXU_WIDTH, LANES) if all(w % c == 0 for w in widths))


def _adaln_kernel(c_ref, w_ref, b_ref, o_ref):
    s = _silu(c_ref[...]).astype(BF16)
    o_ref[...] = _dot(s, w_ref[...].astype(BF16)) + b_ref[...]


def _adaln(c_all, ada_w, ada_b):
    depth, d, d6 = ada_w.shape
    rows = c_all.shape[0]
    tn = _chunk(d6 // 6) * 2
    return pl.pallas_call(
        _adaln_kernel,
        grid=(depth, d6 // tn),
        in_specs=[
            pl.BlockSpec((rows, d), lambda l, j: (0, 0)),
            pl.BlockSpec((None, d, tn), lambda l, j: (l, 0, j)),
            pl.BlockSpec((None, 1, tn), lambda l, j: (l, 0, j)),
        ],
        out_specs=pl.BlockSpec((None, rows, tn), lambda l, j: (l, 0, j)),
        out_shape=jax.ShapeDtypeStruct((depth, rows, d6), F32),
        compiler_params=_cparams("arbitrary", "arbitrary"),
        name="adaln",
    )(c_all, ada_w, ada_b.reshape(depth, 1, d6))


class _Rows:
    def __init__(self, tm, nt, tps, reps, d, pro):
        self.tm, self.nt, self.tps, self.reps, self.d, self.pro = tm, nt, tps, reps, d, pro

    def tile(self, i):
        return jnp.maximum(i - self.pro, 0)

    def rows(self, w):
        return pl.BlockSpec((self.tm, w), lambda i, *_: (self.tile(i), 0))

    def prompt_rows(self, w):
        return pl.BlockSpec((self.tm, w), lambda i, *_: (jnp.minimum(self.tile(i), self.nt - 2), 0))

    def sample_rows(self, w):
        return pl.BlockSpec((self.tm, w), lambda i, *_: (0, 0))

    def seq_mod(self, k):
        return pl.BlockSpec((None, 1, self.d), lambda i, *_: (self.tile(i) // self.tps, 0, k))

    def row_mod(self, k, rows):
        return pl.BlockSpec((rows, self.d), lambda i, *_: (0, k))


def _mod(is_sample, seq_ref, row_ref, reps):
    rowm = jnp.concatenate([row_ref[...]] * reps, axis=0)
    return seq_ref[...] + jnp.where(is_sample, rowm, 0.0)


def _causal_conv_cols(cur, c0, w_ref, tail, ext_all, slot, nst_ref, keep_state, first, emit):
    tm, cw = cur.shape
    kp = w_ref.shape[0] - 1
    wcol = lambda j: w_ref[j:j + 1, c0:c0 + cw]
    ext = ext_all.at[slot]
    ext[0:SUBLANES, :] = jnp.where(first, 0.0, tail[:, c0:c0 + cw])
    ext[SUBLANES:SUBLANES + tm, :] = cur
    acc = wcol(kp) * cur
    for j in range(kp):
        off = SUBLANES - (kp - j)
        acc = acc + wcol(j) * ext[off:off + tm, :]
    tail[:, c0:c0 + cw] = ext[tm:tm + SUBLANES, :]
    nst_ref[:, c0:c0 + cw] = jnp.where(keep_state, nst_ref[:, c0:c0 + cw], ext[SUBLANES + tm - kp:SUBLANES + tm, :])
    emit(acc)


def _inproj_kernel(*refs, geo, dc, ds, dx, cw, split_x, zero_fill):
    if split_x:
        xp_ref, xs_ref, *refs = refs
    else:
        xp_ref, *refs = refs
    (qsh_ref, qsc_ref, rsh_ref, rsc_ref, g_ref, wch_ref, wdt_ref, caw_ref, cbw_ref, cbb_ref,
     ya_ref, z_ref, xc_ref, dt_ref, ncap_ref, ncbp_ref, gbs_ref, us_ref, xbcs_ref, *rest) = refs
    wbf, ext, tail_u, tail_x = rest[-4:]
    i = pl.program_id(0)

    @pl.when(i < geo.pro)
    def _():
        wbf[i] = wch_ref[...].astype(BF16)

    @pl.when(i >= geo.pro)
    def _():
        r = i - geo.pro
        is_s = r == geo.nt - 1
        first = r % geo.tps == 0
        x = jnp.where(is_s, xs_ref[...], xp_ref[...]) if split_x else xp_ref[...]
        sc = _mod(is_s, qsc_ref, rsc_ref, geo.reps)
        sh = _mod(is_s, qsh_ref, rsh_ref, geo.reps)
        h = (_rms(x, g_ref[...]) * (1.0 + sc) + sh).astype(BF16)
        nc = dc // cw
        for a in range(nc):
            c0 = a * cw
            gate_b = _dot_nt(h, wbf[a])
            u = _dot_nt(h, wbf[nc + a]) * _dot_nt(h, wbf[2 * nc + a])
            gbs_ref[:, c0:c0 + cw] = gate_b
            us_ref[:, c0:c0 + cw] = u

            def emit_a(v, c0=c0, gate_b=gate_b):
                ya_ref[:, c0:c0 + cw] = (gate_b * v).astype(BF16)

            _causal_conv_cols(u, c0, caw_ref, tail_u, ext, a % 2, ncap_ref, is_s, first, emit_a)
        for a in range(ds // cw):
            z_ref[:, a * cw:(a + 1) * cw] = _dot_nt(h, wbf[3 * nc + a])
        for a in range(dx // cw):
            c0 = a * cw
            xbc = _dot_nt(h, wbf[3 * nc + ds // cw + a])
            xbcs_ref[:, c0:c0 + cw] = xbc

            def emit_b(v, c0=c0):
                xc_ref[:, c0:c0 + cw] = _silu(v + cbb_ref[:, c0:c0 + cw])

            _causal_conv_cols(xbc, c0, cbw_ref, tail_x, ext, (nc + a) % 2, ncbp_ref, is_s, first, emit_b)
        dt_ref[...] = _dot_nt(h, wdt_ref[...])
        if zero_fill:
            rest[0][...] = jnp.zeros(rest[0].shape, F32)


def _inproj(x_p, x_s, seq_mod, row_mod, geo_args, norm_w, w_in, layer, w_dt, caw, cbw, cbb, bp, ds, zero_rows=0):
    tm, nt, tps, reps, d = geo_args
    dc, dx = caw.shape[1], cbw.shape[1]
    ka, kb = caw.shape[0] - 1, cbw.shape[0] - 1
    cw = _chunk(dc, ds, dx)
    npro = (3 * dc + ds + dx) // cw
    geo = _Rows(tm, nt, tps, reps, d, npro)
    t_all = tm * nt
    split_x = x_s is not None
    xin = [geo.prompt_rows(d), geo.sample_rows(d)] if split_x else [geo.rows(d)]
    xargs = [x_p, x_s] if split_x else [x_p]
    nrow = row_mod.shape[0]
    seq_blk = lambda k, w: pl.BlockSpec((None, k, w), lambda i: (jnp.minimum(geo.tile(i) // tps, bp - 1), 0, 0))
    zero_specs, zero_shapes = [], []
    if zero_rows:
        zero_specs = [pl.BlockSpec((zero_rows // (nt - 1), LANES), lambda i: (jnp.minimum(geo.tile(i), nt - 2), 0))]
        zero_shapes = [jax.ShapeDtypeStruct((zero_rows, LANES), F32)]
    return pl.pallas_call(
        functools.partial(_inproj_kernel, geo=geo, dc=dc, ds=ds, dx=dx, cw=cw, split_x=split_x,
                          zero_fill=bool(zero_rows)),
        grid=(npro + nt,),
        in_specs=xin + [
            geo.seq_mod(0), geo.seq_mod(1), geo.row_mod(0, nrow), geo.row_mod(1, nrow),
            pl.BlockSpec((1, d), lambda i: (0, 0)),
            pl.BlockSpec((None, cw, d), lambda i: (layer, jnp.minimum(i, npro - 1), 0)),
            _resident((LANES, d)), _resident(caw.shape), _resident(cbw.shape), _resident(cbb.shape),
        ],
        out_specs=[geo.rows(dc), geo.rows(ds), geo.rows(dx), geo.rows(LANES), seq_blk(ka, dc), seq_blk(kb, dx),
                   geo.sample_rows(dc), geo.sample_rows(dc), geo.sample_rows(dx)] + zero_specs,
        out_shape=[
            jax.ShapeDtypeStruct((t_all, dc), BF16),
            jax.ShapeDtypeStruct((t_all, ds), F32),
            jax.ShapeDtypeStruct((t_all, dx), F32),
            jax.ShapeDtypeStruct((t_all, LANES), F32),
            jax.ShapeDtypeStruct((bp, ka, dc), F32),
            jax.ShapeDtypeStruct((bp, kb, dx), F32),
            jax.ShapeDtypeStruct((tm, dc), F32),
            jax.ShapeDtypeStruct((tm, dc), F32),
            jax.ShapeDtypeStruct((tm, dx), F32),
        ] + zero_shapes,
        scratch_shapes=[pltpu.VMEM((npro, cw, d), BF16), pltpu.VMEM((2, SUBLANES + tm, cw), F32),
                        pltpu.VMEM((SUBLANES, dc), F32), pltpu.VMEM((SUBLANES, dx), F32)],
        compiler_params=_cparams("arbitrary"),
        name="inproj",
    )(*xargs, seq_mod, seq_mod, row_mod, row_mod, norm_w.reshape(1, d), w_in, w_dt, caw, cbw, cbb)


def _split_hi_lo(v):
    hi = v.astype(BF16)
    lo = (v - hi.astype(F32)).astype(BF16)
    return jnp.concatenate([hi, lo], axis=1)


def _diag_block(cb, mask, cum, cum_t, dt_t, xs, lane, g, hg, gw, p):
    hpl = LANES // p
    parts = []
    for slab in range(gw // LANES):
        lhs, rhs = [], []
        xslab = xs[:, g * gw + slab * LANES:g * gw + (slab + 1) * LANES]
        for j in range(hpl):
            h = g * hg + slab * hpl + j
            seg = cum[:, h:h + 1] - cum_t[h:h + 1, :]
            m = cb * jnp.where(mask, jnp.exp2(seg), 0.0) * dt_t[h:h + 1, :]
            lhs.append(m.astype(BF16))
            rhs.append(jnp.where((lane >= j * p) & (lane < (j + 1) * p), xslab, 0.0).astype(BF16))
        parts.append(_dot(jnp.concatenate(lhs, axis=1), jnp.concatenate(rhs, axis=0)))
    return jnp.concatenate(parts, axis=1)


def _gated_group_norm(y, xs_g, z_g, dsk_g, snorm_g):
    y = (y + dsk_g * xs_g) * _silu(z_g)
    return (y * lax.rsqrt(jnp.mean(y * y, axis=-1, keepdims=True) + EPS) * snorm_g).astype(BF16)


def _ssd_prompt_kernel(z_ref, xc_ref, dt_ref, dtb_ref, alog_ref, dsk_ref, snorm_ref, e_ref, yb_ref, nss_ref, st,
                       *, ds, n, p, hg, sub):
    c = pl.program_id(1)
    q = SSD_CHUNK
    gw = hg * p

    @pl.when(c == 0)
    def _():
        st[...] = jnp.zeros(st.shape, F32)

    row = lax.broadcasted_iota(jnp.int32, (q, LANES), 0)
    tril = lax.broadcasted_iota(jnp.int32, (q, q), 0) >= lax.broadcasted_iota(jnp.int32, (q, q), 1)
    lane = lax.broadcasted_iota(jnp.int32, (q, LANES), 1)
    neg_a = -jnp.exp(alog_ref[...]) * LOG2E
    for k in range(sub):
        rs = slice(k * q, (k + 1) * q)
        xs = xc_ref[rs, 0:ds]
        dt = _softplus(dt_ref[rs, :] + dtb_ref[...])
        cum = dt * neg_a
        step = 1
        while step < q:
            cum = cum + jnp.where(row >= step, pltpu.roll(cum, step, axis=0), 0.0)
            step *= 2
        cum_last = cum[q - 1:q, :]
        w_exp = _dot(_split_hi_lo(dt * jnp.exp2(cum_last - cum)), e_ref[...])
        ecum_exp = _dot(_split_hi_lo(jnp.exp2(cum)), e_ref[...])
        cum_t = cum.T
        dt_t = dt.T
        for g in range(SSD_GROUPS):
            gsl = slice(g * gw, (g + 1) * gw)
            bm = xc_ref[rs, ds + g * n:ds + (g + 1) * n]
            cm = xc_ref[rs, ds + SSD_GROUPS * n + g * n:ds + SSD_GROUPS * n + (g + 1) * n].astype(BF16)
            bm_t = bm.T.astype(BF16)
            y = _diag_block(_dot(cm, bm_t), tril, cum, cum_t, dt_t, xs, lane, g, hg, gw, p)
            s_prev = st[g]
            y = y + _dot(cm, s_prev.astype(BF16)) * ecum_exp[:, gsl]
            xw = (xs[:, gsl] * w_exp[:, gsl]).astype(BF16)
            st[g] = s_prev * ecum_exp[q - 1:q, gsl] + _dot(bm_t, xw)
            yb_ref[rs, gsl] = _gated_group_norm(y, xs[:, gsl], z_ref[rs, gsl], dsk_ref[:, gsl], snorm_ref[:, gsl])

    @pl.when(c == pl.num_programs(1) - 1)
    def _():
        for g in range(SSD_GROUPS):
            for slab in range(gw // LANES):
                r0 = g * gw + slab * LANES
                nss_ref[r0:r0 + LANES, :] = st[g, :, slab * LANES:(slab + 1) * LANES].T


def _ssd_prompt(z, xc, dtr, bsz, seq, params, n, p, hg):
    ds = z.shape[1]
    dx = xc.shape[1]
    sub = next(k for k in SSD_CHUNKS_PER_STEP if seq % (k * SSD_CHUNK) == 0)
    rows = sub * SSD_CHUNK
    nc = seq // rows
    row = lambda w: pl.BlockSpec((rows, w), lambda b, c: (b * nc + c, 0))
    return pl.pallas_call(
        functools.partial(_ssd_prompt_kernel, ds=ds, n=n, p=p, hg=hg, sub=sub),
        grid=(bsz, nc),
        in_specs=[row(ds), row(dx), row(LANES)] + [_resident(a.shape) for a in params],
        out_specs=[row(ds), pl.BlockSpec((None, ds, n), lambda b, c: (b, 0, 0))],
        out_shape=[jax.ShapeDtypeStruct((bsz * seq, ds), BF16), jax.ShapeDtypeStruct((bsz, ds, n), F32)],
        scratch_shapes=[pltpu.VMEM((SSD_GROUPS, n, hg * p), F32)],
        compiler_params=_cparams("arbitrary", "arbitrary"),
        name="ssd_prompt",
    )(z, xc, dtr, *params)


def _ssd_sample_kernel(gb_ref, u_ref, xbc_ref, z_ref, dt_ref, sa_ref, sb_ref, ss_hbm, caw_ref, cbw_ref, cbb_ref,
                       dtb_ref, alog_ref, dsk_ref, snorm_ref, e_ref, ya_ref, yb_ref, nca_ref, ncb_ref, nss_ref,
                       sbuf, sems, *, dc, ds, dx, n, p, hg, steps, bb, layer, nsteps):
    gw = hg * p
    rows = steps * bb
    ka = caw_ref.shape[0] - 1
    kb = cbw_ref.shape[0] - 1

    i = pl.program_id(0)
    nbuf = sbuf.shape[0]
    slot = i % nbuf

    def fetch(j, s):
        return pltpu.make_async_copy(ss_hbm.at[layer, pl.ds(j * bb, bb)], sbuf.at[s], sems.at[s])

    @pl.when(i == 0)
    def _():
        for k in range(min(nbuf, nsteps)):
            fetch(k, k).start()

    fetch(i, slot).wait()
    ss_ref = sbuf.at[slot]

    def conv(cur_ref, st_ref, w_ref, kprev, width):
        hist = [st_ref[:, j * width:(j + 1) * width] for j in range(kprev)] + [cur_ref[t] for t in range(steps)]
        outs = []
        for t in range(steps):
            acc = w_ref[kprev:kprev + 1, :] * hist[t + kprev]
            for j in range(kprev):
                acc = acc + w_ref[j:j + 1, :] * hist[t + j]
            outs.append(acc)
        return outs, hist[len(hist) - kprev:]

    v, new_a = conv(u_ref, sa_ref, caw_ref, ka, dc)
    for t in range(steps):
        ya_ref[t] = (gb_ref[t] * v[t]).astype(BF16)
    for j in range(ka):
        nca_ref[:, j * dc:(j + 1) * dc] = new_a[j]
    xcs, new_b = conv(xbc_ref, sb_ref, cbw_ref, kb, dx)
    for j in range(kb):
        ncb_ref[:, j * dx:(j + 1) * dx] = new_b[j]
    xc = _silu(jnp.concatenate(xcs, axis=0) + cbb_ref[...])
    xs = xc[:, 0:ds]

    dt = _softplus(jnp.concatenate([dt_ref[t] for t in range(steps)], axis=0) + dtb_ref[...])
    da = dt * (-jnp.exp(alog_ref[...]) * LOG2E)
    cums = [da[0:bb]]
    for t in range(1, steps):
        cums.append(cums[-1] + da[t * bb:(t + 1) * bb])
    cum = jnp.concatenate(cums, axis=0)
    cum_last = jnp.concatenate([cums[-1]] * steps, axis=0)
    w_exp = _dot(_split_hi_lo(dt * jnp.exp2(cum_last - cum)), e_ref[...])
    ecum_exp = _dot(_split_hi_lo(jnp.exp2(cum)), e_ref[...])

    def pad_t(a):
        a = jnp.concatenate([a, jnp.zeros((LANES - rows, LANES), F32)], axis=0) if rows < LANES else a
        return a.T[:, 0:rows]

    cum_t = pad_t(cum)
    dt_t = pad_t(dt)

    ri = lax.broadcasted_iota(jnp.int32, (rows, rows), 0)
    ci = lax.broadcasted_iota(jnp.int32, (rows, rows), 1)
    same = ((ri % bb) == (ci % bb)) & (ri >= ci)
    lane = lax.broadcasted_iota(jnp.int32, (rows, LANES), 1)
    rowid = lax.broadcasted_iota(jnp.int32, (rows, 1), 0) % bb
    seqlane = lax.broadcasted_iota(jnp.int32, (LANES, LANES), 1)
    nslab = gw // LANES
    for g in range(SSD_GROUPS):
        gsl = slice(g * gw, (g + 1) * gw)
        bm = xc[:, ds + g * n:ds + (g + 1) * n].astype(BF16)
        cm = xc[:, ds + SSD_GROUPS * n + g * n:ds + SSD_GROUPS * n + (g + 1) * n].astype(BF16)
        y_diag = _diag_block(_dot_nt(cm, bm), same, cum, cum_t, dt_t, xs, lane, g, hg, gw, p)
        xw = (xs[:, gsl] * w_exp[:, gsl]).astype(BF16)
        dec = ecum_exp[(steps - 1) * bb:steps * bb, gsl]
        dec = jnp.concatenate([dec, jnp.zeros((LANES - bb, gw), F32)], axis=0)
        dec_t = [dec[:, s * LANES:(s + 1) * LANES].T for s in range(nslab)]

        def per_seq(b, y_off, g=g, cm=cm, bm=bm, xw=xw, dec_t=dec_t):
            r0 = g * gw
            s0 = ss_ref[b, r0:r0 + gw, :]
            y_off = jnp.where(rowid == b, _dot_nt(cm, s0.astype(BF16)), y_off)
            upd = _dot_tn(jnp.where(rowid == b, xw, jnp.zeros_like(xw)), bm)
            for s in range(nslab):
                dcol = jnp.sum(jnp.where(seqlane == b, dec_t[s], 0.0), axis=1, keepdims=True)
                nss_ref[b, r0 + s * LANES:r0 + (s + 1) * LANES, :] = (
                    s0[s * LANES:(s + 1) * LANES, :] * dcol + upd[s * LANES:(s + 1) * LANES, :])
            return y_off

        y_off = lax.fori_loop(0, bb, per_seq, jnp.zeros((rows, gw), F32), unroll=min(SAMPLE_LOOP_UNROLL, bb))
        zg = jnp.concatenate([z_ref[t, :, gsl] for t in range(steps)], axis=0)
        yn = _gated_group_norm(y_diag + y_off * ecum_exp[:, gsl], xs[:, gsl], zg, dsk_ref[:, gsl], snorm_ref[:, gsl])
        for t in range(steps):
            yb_ref[t, :, gsl] = yn[t * bb:(t + 1) * bb]

    @pl.when(i + nbuf < nsteps)
    def _():
        fetch(i + nbuf, slot).start()


def _ssd_sample_aliased_kernel(*refs, **kw):
    _ssd_sample_kernel(*refs[1:], **kw)


def _ssd_sample(gb, u, xbc, z, dtr, blk0, steps, bs, layer, sa, sb, ss, prev_ss, params, n, p, hg, bb):
    dc, ds, dx = u.shape[2], z.shape[2], xbc.shape[2]
    ka, kb = params[0].shape[0] - 1, params[1].shape[0] - 1
    own = lambda w: pl.BlockSpec((steps, bb, w), lambda i: (0, i, 0))
    tok = lambda w: pl.BlockSpec((steps, bb, w), lambda i: (blk0, i, 0))
    state = pl.BlockSpec((None, bb, ds, n), lambda i: (layer, i, 0, 0))
    return pl.pallas_call(
        functools.partial(_ssd_sample_aliased_kernel, dc=dc, ds=ds, dx=dx, n=n, p=p, hg=hg, steps=steps, bb=bb,
                          layer=layer, nsteps=bs // bb),
        grid=(bs // bb,),
        in_specs=[pl.BlockSpec(memory_space=pl.ANY), own(dc), own(dc), own(dx), tok(ds), tok(LANES),
                  pl.BlockSpec((None, bb, ka * dc), lambda i: (layer, i, 0)),
                  pl.BlockSpec((None, bb, kb * dx), lambda i: (layer, i, 0)), pl.BlockSpec(memory_space=pl.ANY)]
        + [_resident(a.shape) for a in params],
        out_specs=[own(dc), own(ds), pl.BlockSpec((bb, ka * dc), lambda i: (i, 0)),
                   pl.BlockSpec((bb, kb * dx), lambda i: (i, 0)), state],
        out_shape=[jax.ShapeDtypeStruct((steps, bs, dc), BF16), jax.ShapeDtypeStruct((steps, bs, ds), BF16),
                   jax.ShapeDtypeStruct((bs, ka * dc), F32), jax.ShapeDtypeStruct((bs, kb * dx), F32),
                   jax.ShapeDtypeStruct(prev_ss.shape, F32)],
        input_output_aliases={0: 4},
        scratch_shapes=[pltpu.VMEM((SAMPLE_STATE_SLOTS, bb, ds, n), F32), pltpu.SemaphoreType.DMA((SAMPLE_STATE_SLOTS,))],
        compiler_params=_cparams("arbitrary"),
        name="ssd_sample",
    )(prev_ss, gb, u, xbc, z, dtr, sa, sb, ss, *params)


def _route(logits_t, n_exp, cnt_ref):
    ep, toks = logits_t.shape
    sub = lax.broadcasted_iota(jnp.int32, logits_t.shape, 0).astype(F32)
    valid = sub < n_exp
    logits_t = jnp.where(valid, logits_t, -jnp.inf)
    e = jnp.exp(logits_t - jnp.max(logits_t, axis=0, keepdims=True))
    prob = jnp.where(valid, e / jnp.sum(e, axis=0, keepdims=True), -1.0)
    big = float(LANES)
    m1 = jnp.max(prob, axis=0, keepdims=True)
    i1 = jnp.min(jnp.where(prob == m1, sub, big), axis=0, keepdims=True)
    rest = jnp.where(sub == i1, -1.0, prob)
    m2 = jnp.max(rest, axis=0, keepdims=True)
    i2 = jnp.min(jnp.where(rest == m2, sub, big), axis=0, keepdims=True)
    den = m1 + m2
    chosen = jnp.where((sub == i1) | (sub == i2), 1.0, 0.0)
    earlier = lax.broadcasted_iota(jnp.int32, (toks, toks), 0) < lax.broadcasted_iota(jnp.int32, (toks, toks), 1)
    rank = _dot(chosen.astype(BF16), jnp.where(earlier, 1.0, 0.0).astype(BF16)) + cnt_ref[:, 0:1]
    r1 = jnp.sum(jnp.where(sub == i1, rank, 0.0), axis=0, keepdims=True)
    r2 = jnp.sum(jnp.where(sub == i2, rank, 0.0), axis=0, keepdims=True)
    cnt_ref[...] = cnt_ref[...] + jnp.sum(chosen, axis=1, keepdims=True)
    rows = [i1, i2, m1 / den, m2 / den, r1, r2]
    compact = jnp.concatenate(rows + [jnp.zeros((SUBLANES - len(rows), toks), F32)], axis=0)
    wide = jnp.concatenate([compact, jnp.zeros((LANES - SUBLANES, toks), F32)], axis=0).T
    return wide, compact


def _outproj_kernel(yap_ref, yas_ref, ybp_ref, ybs_ref, wch_ref, *rest, geo, n_exp, kc, split_x):
    if split_x:
        xp_ref, xs_ref, *rest = rest
    else:
        xp_ref, *rest = rest
    (qg_ref, qsh_ref, qsc_ref, rg_ref, rsh_ref, rsc_ref, npost_ref, npre_ref, r_ref,
     x1_ref, h2_ref, sel_ref, selt_ref, cnt_ref, wbf) = rest
    i = pl.program_id(0)

    @pl.when(i < geo.pro)
    def _():
        wbf[pl.ds(pl.multiple_of(i * kc, kc), kc), :] = wch_ref[...].astype(BF16)
        cnt_ref[...] = jnp.zeros(cnt_ref.shape, F32)

    @pl.when(i >= geo.pro)
    def _():
        is_s = i - geo.pro == geo.nt - 1
        da = yap_ref.shape[1]
        nh = 2 if geo.reps % 2 == 0 else 1
        th = geo.tm // nh
        for hs in range(nh):
            rs = slice(hs * th, (hs + 1) * th)
            mod = lambda q_ref, r_ref: _mod(is_s, q_ref, r_ref, geo.reps // nh)
            mix = (_dot(jnp.where(is_s, yas_ref[rs, :], yap_ref[rs, :]), wbf[0:da, :])
                   + _dot(jnp.where(is_s, ybs_ref[rs, :], ybp_ref[rs, :]), wbf[da:, :]))
            x = jnp.where(is_s, xs_ref[rs, :], xp_ref[rs, :]) if split_x else xp_ref[rs, :]
            x1 = x + mod(qg_ref, rg_ref) * _rms(mix, npost_ref[...])
            x1_ref[rs, :] = x1
            h2 = _rms(x1, npre_ref[...]) * (1.0 + mod(qsc_ref, rsc_ref)) + mod(qsh_ref, rsh_ref)
            h2_ref[rs, :] = h2
            h_hi = h2.astype(BF16)
            h_lo = (h2 - h_hi.astype(F32)).astype(BF16)
            r = r_ref[...]
            ep = r.shape[0]
            r_hi = r.astype(BF16)
            r_lo = (r - r_hi.astype(F32)).astype(BF16)
            both = _dot_nt(jnp.concatenate([r_hi, r_lo], axis=0), h_hi)
            logits_t = both[0:ep, :] + (_dot_nt(r_hi, h_lo) + both[ep:, :])
            sel_ref[rs, :], selt_ref[:, rs] = _route(logits_t, n_exp, cnt_ref)


def _outproj(ya_p, ya_s, yb_p, yb_s, w_out, layer, x_p, x_s, seq_mod, row_mod, geo_args, npost, npre, router,
             n_exp):
    tm, nt, tps, reps, d = geo_args
    da, db = ya_p.shape[1], yb_p.shape[1]
    dm = da + db
    kc = _chunk(dm) // 2
    npro = dm // kc
    geo = _Rows(tm, nt, tps, reps, d, npro)
    t_all = tm * nt
    nrow = row_mod.shape[0]
    vec = pl.BlockSpec((1, d), lambda i: (0, 0))
    split_x = x_s is not None
    in_specs = [geo.prompt_rows(da), geo.sample_rows(da), geo.prompt_rows(db), geo.sample_rows(db),
                pl.BlockSpec((None, kc, d), lambda i: (layer, jnp.minimum(i, npro - 1), 0))]
    in_specs += [geo.prompt_rows(d), geo.sample_rows(d)] if split_x else [geo.rows(d)]
    in_specs += [geo.seq_mod(2), geo.seq_mod(3), geo.seq_mod(4),
                 geo.row_mod(2, nrow), geo.row_mod(3, nrow), geo.row_mod(4, nrow), vec, vec]
    args = [ya_p, ya_s, yb_p, yb_s, w_out] + ([x_p, x_s] if split_x else [x_p])
    in_specs.append(_resident(router.shape))
    args += [seq_mod, seq_mod, seq_mod, row_mod, row_mod, row_mod, npost.reshape(1, d), npre.reshape(1, d), router]
    cnt_shape = (router.shape[0], LANES)
    out_specs = [geo.rows(d), geo.rows(d), geo.rows(LANES), pl.BlockSpec((SUBLANES, tm), lambda i: (0, geo.tile(i))),
                 pl.BlockSpec(cnt_shape, lambda i: (0, 0))]
    out_shape = [jax.ShapeDtypeStruct((t_all, d), F32), jax.ShapeDtypeStruct((t_all, d), F32),
                 jax.ShapeDtypeStruct((t_all, LANES), F32), jax.ShapeDtypeStruct((SUBLANES, t_all), F32),
                 jax.ShapeDtypeStruct(cnt_shape, F32)]
    return pl.pallas_call(
        functools.partial(_outproj_kernel, geo=geo, n_exp=n_exp, kc=kc, split_x=split_x),
        grid=(npro + nt,),
        in_specs=in_specs,
        out_specs=out_specs,
        out_shape=out_shape,
        scratch_shapes=[pltpu.VMEM((dm, d), BF16)],
        compiler_params=_cparams("arbitrary"),
        name="outproj",
    )(*args)


def _write_split(is_s, val, outp_ref, outs_ref):
    @pl.when(jnp.logical_not(is_s))
    def _():
        outp_ref[...] = val

    @pl.when(is_s)
    def _():
        outs_ref[...] = val


def _out_rows(geo, d, split_out):
    t_all = geo.tm * geo.nt
    if split_out:
        return ([geo.prompt_rows(d), geo.sample_rows(d)],
                [jax.ShapeDtypeStruct((t_all - geo.tm, d), F32), jax.ShapeDtypeStruct((geo.tm, d), F32)])
    return [geo.rows(d)], [jax.ShapeDtypeStruct((t_all, d), F32)]


def _mix_ffn_kernel(yap_ref, yas_ref, ybp_ref, ybs_ref, wo_ref, wg_ref, wu_ref, wd_ref, *rest, geo, kc, cf, n_out,
                    n_ffn, split_x, split_out):
    if split_x:
        xp_ref, xs_ref, *rest = rest
    else:
        xp_ref, *rest = rest
    (qg1_ref, qsh_ref, qsc_ref, qg2_ref, rg1_ref, rsh_ref, rsc_ref, rg2_ref, npost_ref, npre_ref, nffn_ref,
     *rest) = rest
    outs, (wob, wgb, wub, wdb, act) = rest[:-5], rest[-5:]
    i = pl.program_id(0)

    @pl.when(i < n_out)
    def _():
        wob[pl.ds(pl.multiple_of(i * kc, kc), kc), :] = wo_ref[...].astype(BF16)

    @pl.when(i < n_ffn)
    def _():
        wgb[i] = wg_ref[...].astype(BF16)
        wub[i] = wu_ref[...].astype(BF16)
        wdb[pl.ds(pl.multiple_of(i * cf, cf), cf), :] = wd_ref[...].astype(BF16)

    @pl.when(i >= geo.pro)
    def _():
        is_s = i - geo.pro == geo.nt - 1
        da = yap_ref.shape[1]
        mod = lambda q_ref, r_ref: _mod(is_s, q_ref, r_ref, geo.reps)
        mix = (_dot(jnp.where(is_s, yas_ref[...], yap_ref[...]), wob[0:da, :])
               + _dot(jnp.where(is_s, ybs_ref[...], ybp_ref[...]), wob[da:, :]))
        x = jnp.where(is_s, xs_ref[...], xp_ref[...]) if split_x else xp_ref[...]
        x1 = x + mod(qg1_ref, rg1_ref) * _rms(mix, npost_ref[...])
        h = (_rms(x1, npre_ref[...]) * (1.0 + mod(qsc_ref, rsc_ref)) + mod(qsh_ref, rsh_ref)).astype(BF16)
        for c in range(n_ffn):
            act[:, c * cf:(c + 1) * cf] = (_silu(_dot(h, wgb[c])) * _dot(h, wub[c])).astype(BF16)
        f = _dot(act[...], wdb[...])
        val = x1 + mod(qg2_ref, rg2_ref) * _rms(f, nffn_ref[...])
        if split_out:
            _write_split(is_s, val, *outs)
        else:
            outs[0][...] = val


def _mix_ffn(ya_p, ya_s, yb_p, yb_s, w_out, layer, wg, wu, wd, j, x_p, x_s, seq_mod, row_mod, geo_args, npost, npre,
             nffn, split_out):
    tm, nt, tps, reps, d = geo_args
    da, db = ya_p.shape[1], yb_p.shape[1]
    dm = da + db
    f = wg.shape[2]
    kc = _chunk(dm) // 2
    cf = MXU_WIDTH if f % MXU_WIDTH == 0 else LANES
    n_out, n_ffn = dm // kc, f // cf
    geo = _Rows(tm, nt, tps, reps, d, max(n_out, n_ffn))
    nrow = row_mod.shape[0]
    vec = pl.BlockSpec((1, d), lambda i: (0, 0))
    split_x = x_s is not None
    out_specs, out_shape = _out_rows(geo, d, split_out)
    oc = lambda i: jnp.minimum(i, n_out - 1)
    fc = lambda i: jnp.minimum(i, n_ffn - 1)
    in_specs = [geo.prompt_rows(da), geo.sample_rows(da), geo.prompt_rows(db), geo.sample_rows(db),
                pl.BlockSpec((None, kc, d), lambda i: (layer, oc(i), 0)),
                pl.BlockSpec((None, d, cf), lambda i: (j, 0, fc(i))),
                pl.BlockSpec((None, d, cf), lambda i: (j, 0, fc(i))),
                pl.BlockSpec((None, cf, d), lambda i: (j, fc(i), 0))]
    in_specs += [geo.prompt_rows(d), geo.sample_rows(d)] if split_x else [geo.rows(d)]
    in_specs += [geo.seq_mod(k) for k in (2, 3, 4, 5)] + [geo.row_mod(k, nrow) for k in (2, 3, 4, 5)] + [vec] * 3
    args = [ya_p, ya_s, yb_p, yb_s, w_out, wg, wu, wd] + ([x_p, x_s] if split_x else [x_p])
    args += [seq_mod] * 4 + [row_mod] * 4 + [npost.reshape(1, d), npre.reshape(1, d), nffn.reshape(1, d)]
    return pl.pallas_call(
        functools.partial(_mix_ffn_kernel, geo=geo, kc=kc, cf=cf, n_out=n_out, n_ffn=n_ffn, split_x=split_x,
                          split_out=split_out),
        grid=(geo.pro + nt,),
        in_specs=in_specs,
        out_specs=out_specs,
        out_shape=out_shape,
        scratch_shapes=[pltpu.VMEM((dm, d), BF16), pltpu.VMEM((n_ffn, d, cf), BF16), pltpu.VMEM((n_ffn, d, cf), BF16),
                        pltpu.VMEM((f, d), BF16), pltpu.VMEM((tm, f), BF16)],
        compiler_params=_cparams("arbitrary"),
        name="mix_ffn",
    )(*args)


def _row_copy(src, src_row, dst, dst_row, sem):
    return pltpu.make_async_copy(src.at[pl.ds(src_row, 1)], dst.at[pl.ds(dst_row, 1)], sem)


def _moe_dispatch_kernel(pad_start_ref, pad_len_ref, nvalid_ref, pos_ref, h_ref, wg_ref, wu_ref, wd_ref,
                         xs_hbm, wgb_ref, wub_ref, wdb_ref, zbuf, hbuf, sems, *, tm, n_exp, nt):
    i = pl.program_id(0)
    ts = zbuf.shape[0]
    wgb_ref[...] = wg_ref[...].astype(BF16)
    wub_ref[...] = wu_ref[...].astype(BF16)
    wdb_ref[...] = wd_ref[...].astype(BF16)

    @pl.when(i == 0)
    def _():
        zbuf[...] = jnp.zeros(zbuf.shape, F32)
        for e in range(n_exp):
            def start(k, c, e=e):
                _row_copy(zbuf, 0, xs_hbm, pad_start_ref[e] + k, sems.at[2]).start()
                return c

            def wait(k, c, e=e):
                _row_copy(zbuf, 0, xs_hbm, pad_start_ref[e] + k, sems.at[2]).wait()
                return c

            lax.fori_loop(0, pad_len_ref[e], start, 0)
            lax.fori_loop(0, pad_len_ref[e], wait, 0)

        def tile_copy(j):
            return pltpu.make_async_copy(zbuf, xs_hbm.at[pl.ds(pl.multiple_of(j * ts, ts), ts)], sems.at[2])

        def start_tile(j, c):
            tile_copy(j).start()
            return c

        def wait_tile(j, c):
            tile_copy(j).wait()
            return c

        lax.fori_loop(nvalid_ref[0], xs_hbm.shape[0] // ts, start_tile, 0)
        lax.fori_loop(nvalid_ref[0], xs_hbm.shape[0] // ts, wait_tile, 0)

    def wait_slot(s):
        for _ in range(TOP_K):
            pltpu.make_async_copy(hbuf.at[s], xs_hbm.at[pl.ds(0, tm)], sems.at[s]).wait()

    for s in range(2):
        @pl.when(i % 2 == s)
        def _(s=s):
            @pl.when(i >= 2)
            def _():
                wait_slot(s)

            hbuf[s] = h_ref[...]
            for r in range(tm):
                _row_copy(hbuf.at[s], r, xs_hbm, pos_ref[0, r], sems.at[s]).start(priority=0)
                _row_copy(hbuf.at[s], r, xs_hbm, pos_ref[0, tm + r], sems.at[s]).start(priority=1)

    @pl.when(i == nt - 1)
    def _():
        wait_slot((nt - 1) % 2)
        if nt > 1:
            wait_slot(nt % 2)


def _moe_dispatch(h2, pos_tiles, pad_start, pad_len, n_valid, n_slots, tm, tm_slot, wg, wu, wd):
    t, d = h2.shape
    nt = t // tm
    n_exp = pad_start.shape[0]
    pack = 2 * SUBLANES
    ncast = next(k for k in range(nt, 0, -1)
                 if wg.shape[0] % (k * pack) == 0 and wd.shape[0] % (k * pack) == 0)
    wrows = lambda a: pl.BlockSpec((a.shape[0] // ncast, a.shape[1]), lambda i, *_: (jnp.minimum(i, ncast - 1), 0))
    grid_spec = pltpu.PrefetchScalarGridSpec(
        num_scalar_prefetch=3,
        grid=(nt,),
        in_specs=[pl.BlockSpec((None, 1, TOP_K * tm), lambda i, *_: (i, 0, 0), memory_space=pltpu.SMEM),
                  pl.BlockSpec((tm, d), lambda i, *_: (i, 0)), wrows(wg), wrows(wu), wrows(wd)],
        out_specs=[pl.BlockSpec(memory_space=pl.ANY), wrows(wg), wrows(wu), wrows(wd)],
        scratch_shapes=[pltpu.VMEM((tm_slot, d), F32), pltpu.VMEM((2, tm, d), F32), pltpu.SemaphoreType.DMA((3,))],
    )
    return pl.pallas_call(
        functools.partial(_moe_dispatch_kernel, tm=tm, n_exp=n_exp, nt=nt),
        grid_spec=grid_spec,
        out_shape=[jax.ShapeDtypeStruct((n_slots, d), F32)] + [jax.ShapeDtypeStruct(a.shape, BF16) for a in (wg, wu, wd)],
        compiler_params=_cparams("arbitrary"),
        name="moe_dispatch",
    )(pad_start, pad_len, n_valid, pos_tiles.reshape(nt, 1, TOP_K * tm), h2, wg, wu, wd)


def _moe_expert_kernel(texp_ref, nvalid_ref, x_ref, wg_ref, wu_ref, wd_ref, o_ref):
    i = pl.program_id(0)

    @pl.when(i < nvalid_ref[0])
    def _():
        h = x_ref[...].astype(BF16)
        act = (_silu(_dot(h, wg_ref[...])) * _dot(h, wu_ref[...])).astype(BF16)
        o_ref[...] = _dot(act, wd_ref[...])

    @pl.when(i >= nvalid_ref[0])
    def _():
        o_ref[...] = jnp.zeros(o_ref.shape, F32)


def _moe_experts(x_sorted, tile_expert, n_valid, wg, wu, wd, tm):
    n_tiles = tile_expert.shape[0]
    d = x_sorted.shape[1]
    n_exp, _, fe = wg.shape
    grid_spec = pltpu.PrefetchScalarGridSpec(
        num_scalar_prefetch=2,
        grid=(n_tiles,),
        in_specs=[
            pl.BlockSpec((tm, d), lambda i, te, nv: (jnp.minimum(i, nv[0] - 1), 0)),
            pl.BlockSpec((None, d, fe), lambda i, te, nv: (te[i], 0, 0)),
            pl.BlockSpec((None, d, fe), lambda i, te, nv: (te[i], 0, 0)),
            pl.BlockSpec((None, fe, d), lambda i, te, nv: (te[i], 0, 0)),
        ],
        out_specs=pl.BlockSpec((tm, d), lambda i, te, nv: (i, 0)),
    )
    return pl.pallas_call(
        _moe_expert_kernel,
        grid_spec=grid_spec,
        out_shape=jax.ShapeDtypeStruct((n_tiles * tm, d), F32),
        compiler_params=_cparams("arbitrary"),
        name="moe_experts",
    )(tile_expert, n_valid, x_sorted, wg, wu, wd)


def _start_row_gather(idx_ref, src_hbm, dst, sem, count):
    for r in range(count):
        _row_copy(src_hbm, idx_ref[0, r], dst, r, sem).start(priority=r % 2)


def _wait_row_gather(src_hbm, dst, sem):
    pltpu.make_async_copy(src_hbm.at[pl.ds(0, dst.shape[0])], dst, sem).wait()


def _moe_combine_kernel(pos0_ref, posnext_ref, ye_hbm, sel_ref, x1_ref, qg_ref, rg_ref, npost_ref, *rest,
                        geo, split_out):
    outs, (ybuf, sems) = rest[:-2], rest[-2:]
    i = pl.program_id(0)
    tm = geo.tm
    slot = i % 2

    @pl.when(i == 0)
    def _():
        _start_row_gather(pos0_ref, ye_hbm, ybuf.at[0], sems.at[0], 2 * tm)

    @pl.when(i + 1 < geo.nt)
    def _():
        _start_row_gather(posnext_ref, ye_hbm, ybuf.at[1 - slot], sems.at[1 - slot], 2 * tm)

    _wait_row_gather(ye_hbm, ybuf.at[slot], sems.at[slot])
    is_s = i == geo.nt - 1
    sel = sel_ref[...]
    lane = lax.broadcasted_iota(jnp.int32, sel.shape, 1)
    w1 = jnp.sum(jnp.where(lane == 2, sel, 0.0), axis=-1, keepdims=True)
    w2 = jnp.sum(jnp.where(lane == 3, sel, 0.0), axis=-1, keepdims=True)
    f = w1 * ybuf[slot, 0:tm, :] + w2 * ybuf[slot, tm:2 * tm, :]
    val = x1_ref[...] + _mod(is_s, qg_ref, rg_ref, geo.reps) * _rms(f, npost_ref[...])
    if split_out:
        _write_split(is_s, val, *outs)
    else:
        outs[0][...] = val


def _moe_combine(ye, pos, sel, x1, seq_mod, row_mod, geo_args, npost, split_out):
    tm, nt, tps, reps, d = geo_args
    geo = _Rows(tm, nt, tps, reps, d, 0)
    out_specs, out_shape = _out_rows(geo, d, split_out)
    pos3 = pos.reshape(nt, 1, 2 * tm)
    smem_blk = lambda fn: pl.BlockSpec((None, 1, 2 * tm), fn, memory_space=pltpu.SMEM)
    return pl.pallas_call(
        functools.partial(_moe_combine_kernel, geo=geo, split_out=split_out),
        grid=(nt,),
        in_specs=[smem_blk(lambda i: (0, 0, 0)),
                  smem_blk(lambda i: (jnp.minimum(i + 1, nt - 1), 0, 0)),
                  pl.BlockSpec(memory_space=pl.ANY),
                  geo.rows(LANES), geo.rows(d), geo.seq_mod(5), geo.row_mod(5, row_mod.shape[0]),
                  pl.BlockSpec((1, d), lambda i: (0, 0))],
        out_specs=out_specs,
        out_shape=out_shape,
        scratch_shapes=[pltpu.VMEM((2, 2 * tm, d), F32), pltpu.SemaphoreType.DMA((2,))],
        compiler_params=_cparams("arbitrary"),
        name="moe_combine",
    )(pos3, pos3, ye, sel, x1, seq_mod, row_mod, npost.reshape(1, d))


def _moe_tables(sel_t, counts, n_exp, tm_tok, tm_slot):
    t = sel_t.shape[1]
    counts = counts[:n_exp, 0].astype(jnp.int32)
    tiles_e = (counts + tm_slot - 1) // tm_slot
    tile_end = jnp.cumsum(tiles_e)
    slot_start = (tile_end - tiles_e) * tm_slot
    choice = sel_t[0:TOP_K].astype(jnp.int32)
    rank = sel_t[2 * TOP_K:3 * TOP_K].astype(jnp.int32)
    onehot = choice[:, :, None] == jnp.arange(n_exp, dtype=jnp.int32)[None, None, :]
    pos = jnp.sum(jnp.where(onehot, slot_start[None, None, :], 0), axis=-1) + rank
    n_tiles = (TOP_K * t + n_exp * (tm_slot - 1)) // tm_slot
    tile_expert = jnp.minimum(
        jnp.sum((jnp.arange(n_tiles, dtype=jnp.int32)[:, None] >= tile_end[None, :]).astype(jnp.int32), axis=1),
        n_exp - 1)
    pos_tiles = jnp.concatenate([pos[k].reshape(-1, tm_tok) for k in range(TOP_K)], axis=1)
    pad_start = slot_start + counts
    pad_len = tiles_e * tm_slot - counts
    return tile_expert, tile_end[-1:].astype(jnp.int32), pos_tiles, pad_start, pad_len, n_tiles * tm_slot


def _largest_tile(t, want):
    tm = min(want, t)
    while t % tm:
        tm //= 2
    return tm


def kernel(x_prompt, x_sample, c_prompt, c_sample, state_conva, state_convb, state_ssm, ada_w, ada_b, norm_pre_mix,
           norm_post_mix, norm_pre_ffn, norm_post_ffn, w_in, w_out, conva_w, convb_w, convb_b, dt_bias, a_log, d_skip,
           ssd_norm, ffd_w_gate, ffd_w_up, ffd_w_down, moe_router, moe_w_gate, moe_w_up, moe_w_down):
    bp, seq, d = x_prompt.shape
    bs, steps, _ = x_sample.shape
    depth = w_in.shape[0]
    dc = conva_w.shape[-1]
    dx = convb_w.shape[-1]
    ds = ssd_norm.shape[-1]
    heads = dt_bias.shape[-1]
    p = ds // heads
    n = (dx - ds) // (2 * SSD_GROUPS)
    hg = heads // SSD_GROUPS
    n_exp = moe_router.shape[-1]
    ka, kb = conva_w.shape[1] - 1, convb_w.shape[1] - 1
    tm = steps * bs
    t_p = bp * seq
    assert seq % SSD_CHUNK == 0 and LANES % p == 0 and (hg * p) % LANES == 0 and n == LANES
    assert heads <= LANES and n_exp <= LANES and dc % LANES == 0 and dx % LANES == 0 and d % LANES == 0
    assert w_in.shape[-1] == 3 * dc + ds + dx + heads and seq % tm == 0 and tm % SUBLANES == 0
    assert bs % SUBLANES == 0 and (t_p // bs) % steps == 0
    nt = t_p // tm + 1
    geo_args = (tm, nt, seq // tm, steps, d)

    w_in_t = jnp.swapaxes(w_in, 1, 2)
    w_dt = jnp.pad(w_in_t[:, 3 * dc + ds + dx:, :], ((0, 0), (0, LANES - heads), (0, 0))).astype(BF16)
    router_t = jnp.pad(jnp.swapaxes(moe_router, 1, 2), ((0, 0), (0, -n_exp % SUBLANES), (0, 0)))
    padh = lambda a: jnp.pad(a, ((0, 0), (0, LANES - heads))).reshape(depth, 1, LANES)
    dtb_p, alog_p = padh(dt_bias), padh(a_log)
    dsk_e = jnp.repeat(d_skip, p, axis=-1).reshape(depth, 1, ds)
    hot = (jnp.arange(LANES)[:, None] == (jnp.arange(ds)[None, :] // p)).astype(BF16)
    emat = jnp.concatenate([hot, hot], axis=0)

    mod = _adaln(jnp.concatenate([c_prompt, c_sample], axis=0), ada_w, ada_b)
    seq_mod = jnp.pad(mod[:, :bp], ((0, 0), (0, 1), (0, 0))).reshape(depth, bp + 1, 1, 6 * d)
    row_mod = mod[:, bp:]

    bb = _largest_tile(bs, SAMPLE_SEQS_PER_STEP)
    x_all = None
    xs_tm = x_sample.transpose(1, 0, 2).reshape(tm, d)
    xp2d = x_prompt.reshape(t_p, d)
    sa_all = state_conva.reshape(depth, bs, ka * dc)
    sb_all = state_convb.reshape(depth, bs, kb * dx)
    ss_all = state_ssm.reshape(depth, bs, ds, n)
    pa, pb, ps, sa_l, sb_l = [], [], [], [], []
    ss_new = None
    for i in range(depth):
        j = i // 2
        last = i == depth - 1
        x_in = (xp2d, xs_tm) if x_all is None else (x_all, None)
        conv_params = [conva_w[i], convb_w[i], convb_b[i].reshape(1, dx)]
        ya, z, xc, dtr, na, nb, gb_s, u_s, xbc_s, *zeros = _inproj(
            *x_in, seq_mod[i], row_mod[i], geo_args, norm_pre_mix[i], w_in_t, i, w_dt[i], *conv_params, bp, ds,
            zero_rows=depth * bs * ds * n // LANES if ss_new is None else 0)
        if zeros:
            ss_new = zeros[0].reshape(depth, bs, ds, n)
        params = [dtb_p[i], alog_p[i], dsk_e[i], ssd_norm[i].reshape(1, ds), emat]
        yb_p, ns = _ssd_prompt(z, xc, dtr, bp, seq, params, n, p, hg)
        r3 = lambda a: a.reshape(a.shape[0] // bs, bs, a.shape[-1])
        ya_s, yb_s, sna, snb, ss_new = _ssd_sample(
            r3(gb_s), r3(u_s), r3(xbc_s), r3(z), r3(dtr), t_p // (bs * steps), steps, bs, i, sa_all, sb_all, ss_all,
            ss_new, conv_params + params, n, p, hg, bb)
        pa.append(na)
        pb.append(nb)
        ps.append(ns.reshape(bp, heads, p, n))
        sa_l.append(sna.reshape(bs, ka, dc))
        sb_l.append(snb.reshape(bs, kb, dx))

        if i % 2 == 1:
            x1, h2, sel, sel_t, counts = _outproj(ya, ya_s.reshape(tm, dc), yb_p, yb_s.reshape(tm, ds), w_out, i, *x_in,
                                           seq_mod[i], row_mod[i], geo_args, norm_post_mix[i], norm_pre_ffn[i],
                                           router_t[j], n_exp)
            tile_expert, n_valid, pos_tiles, pad_start, pad_len, n_slots = _moe_tables(sel_t, counts, n_exp, tm, MOE_TILE)
            fe = moe_w_gate.shape[-1]
            x_sorted, wg_b, wu_b, wd_b = _moe_dispatch(
                h2, pos_tiles, pad_start, pad_len, n_valid, n_slots, tm, MOE_TILE, moe_w_gate[j].reshape(n_exp * d, fe),
                moe_w_up[j].reshape(n_exp * d, fe), moe_w_down[j].reshape(n_exp * fe, d))
            ye = _moe_experts(x_sorted, tile_expert, n_valid, wg_b.reshape(n_exp, d, fe), wu_b.reshape(n_exp, d, fe),
                              wd_b.reshape(n_exp, fe, d), MOE_TILE)
            out = _moe_combine(ye, pos_tiles, sel, x1, seq_mod[i], row_mod[i], geo_args, norm_post_ffn[i], last)
        else:
            out = _mix_ffn(ya, ya_s.reshape(tm, dc), yb_p, yb_s.reshape(tm, ds), w_out, i, ffd_w_gate, ffd_w_up,
                           ffd_w_down, j, *x_in, seq_mod[i], row_mod[i], geo_args, norm_post_mix[i], norm_pre_ffn[i],
                           norm_post_ffn[i], last)
        if last:
            y_p, y_s = out
        else:
            x_all = out[0]

    y_prompt = y_p.reshape(bp, seq, d)
    y_sample = y_s.reshape(steps, bs, d).transpose(1, 0, 2)
    return (y_prompt, y_sample, jnp.stack(pa), jnp.stack(pb), jnp.stack(ps),
            jnp.stack(sa_l), jnp.stack(sb_l), ss_new.reshape(depth, bs, heads, p, n))
```
